```python
import math
import jax, jax.numpy as jnp
from jax import lax
import numpy as np

D_MODEL = 1024
BATCH = 4
SEQ = 8192
DEPTH = 1

CHUNK = 64
QBLK = 128
A_QBLK = CHUNK
HEAD_DIM = 64
A_HEADS = 8
A_DIM = A_HEADS * HEAD_DIM
IDX_HEADS = 8
IDX_DIM = 64
TOPK_MAX = 256
B_HEADS = 8
B_DIM = B_HEADS * HEAD_DIM
C_HEADS = 4
C_HEAD_DIM = 128
C_DIM = C_HEADS * C_HEAD_DIM
N_MEM = 256
N_BRANCH = 3
D_FF = ((8 * D_MODEL // 3 + 255) // 256) * 256
REL_BUCKETS = 32
REL_MAX_DIST = 128
EPS = 1e-6
SPLIT_SIZES = (A_DIM, A_DIM, A_DIM, IDX_HEADS * IDX_DIM, IDX_DIM, IDX_HEADS,
               B_DIM, B_DIM, B_DIM, C_DIM, N_BRANCH * D_MODEL)
N_IN = sum(SPLIT_SIZES)

kernel_name = 'hybrid_dsa_stickbreak_memory_block'


def rmsnorm(x, g):
    xf = x.astype(jnp.float32)
    y = xf * lax.rsqrt(jnp.mean(xf * xf, axis=-1, keepdims=True) + EPS)
    return (y * g.astype(jnp.float32)).astype(x.dtype)


def rel_bucket(rel):
    nb = REL_BUCKETS // 2
    max_exact = nb // 2
    n = jnp.abs(rel)
    large = max_exact + (jnp.log(jnp.maximum(n, 1).astype(jnp.float32) / max_exact)
                         / math.log(REL_MAX_DIST / max_exact) * (nb - max_exact)).astype(jnp.int32)
    large = jnp.minimum(large, nb - 1)
    return jnp.where(rel > 0, nb, 0) + jnp.where(n < max_exact, n, large)


def dsa_attention(aq, ak, av, iq, ik, iw, rel_bias):
    bsz, seq = aq.shape[0], aq.shape[1]
    k_sel = min(TOPK_MAX, seq // 4)
    key_chunk = jnp.arange(seq) // CHUNK
    gather = jax.vmap(lambda t, j: t[j])

    def block(i):
        q0 = i * A_QBLK
        qpos = q0 + jnp.arange(A_QBLK)
        qchunk = qpos // CHUNK
        iq_b = lax.dynamic_slice_in_dim(iq, q0, A_QBLK, axis=1)
        iw_b = lax.dynamic_slice_in_dim(iw, q0, A_QBLK, axis=1).astype(jnp.float32)
        aq_b = lax.dynamic_slice_in_dim(aq, q0, A_QBLK, axis=1)
        s = jnp.einsum('bqhd,bkd->bqhk', iq_b, ik).astype(jnp.float32) * IDX_DIM ** -0.5
        score = jnp.einsum('bqh,bqhk->bqk', iw_b, jax.nn.relu(s))
        admissible = key_chunk[None, :] <= qchunk[:, None]
        score = jnp.where(admissible[None], score, -jnp.inf)
        _, idx = lax.top_k(score, k_sel)
        valid = (idx // CHUNK) <= qchunk[None, :, None]
        kg = gather(ak, idx)
        vg = gather(av, idx)
        logits = jnp.einsum('bqhd,bqkhd->bhqk', aq_b, kg).astype(jnp.float32) * HEAD_DIM ** -0.5
        bias = rel_bias[rel_bucket(idx - qpos[None, :, None])]
        logits = logits + jnp.transpose(bias, (0, 3, 1, 2)).astype(jnp.float32)
        logits = jnp.where(valid[:, None], logits, -jnp.inf)
        p = jax.nn.softmax(logits, axis=-1).astype(vg.dtype)
        return jnp.einsum('bhqk,bqkhd->bqhd', p, vg)

    out = lax.map(block, jnp.arange(seq // A_QBLK))
    return jnp.transpose(out, (1, 0, 2, 3, 4)).reshape(bsz, seq, A_DIM)


def stick_breaking_attention(bq, bk, bv):
    bsz, seq = bq.shape[0], bq.shape[1]
    kpos = jnp.arange(seq)

    def block(i):
        q0 = i * QBLK
        qpos = q0 + jnp.arange(QBLK)
        q_b = lax.dynamic_slice_in_dim(bq, q0, QBLK, axis=1)
        z = jnp.einsum('bqhd,bkhd->bhqk', q_b, bk).astype(jnp.float32) * HEAD_DIM ** -0.5
        causal = kpos[None, :] < qpos[:, None]
        log_beta = jax.nn.log_sigmoid(z)
        log_one_minus = jnp.where(causal, jax.nn.log_sigmoid(-z), 0.0)
        tail = lax.cumsum(log_one_minus, axis=log_one_minus.ndim - 1, reverse=True) - log_one_minus
        a = jnp.where(causal, jnp.exp(log_beta + tail), 0.0).astype(bv.dtype)
        return jnp.einsum('bhqk,bkhd->bqhd', a, bv)

    out = lax.map(block, jnp.arange(seq // QBLK))
    return jnp.transpose(out, (1, 0, 2, 3, 4)).reshape(bsz, seq, B_DIM)


def memory_attention(cq, mk, mv):
    bsz, seq = cq.shape[0], cq.shape[1]
    logits = jnp.einsum('bshd,bmhd->bhsm', cq, mk).astype(jnp.float32) * C_HEAD_DIM ** -0.5
    p = jax.nn.softmax(logits, axis=-1).astype(mv.dtype)
    return jnp.einsum('bhsm,bmhd->bshd', p, mv).reshape(bsz, seq, C_DIM)


def setup_inputs(seed: int = 0) -> dict:
    key = jax.random.key(seed)
    ks = jax.random.split(key, 20)
    f32 = jnp.float32

    def w(k, shape, fan_in):
        return jax.random.normal(k, shape, f32) * fan_in ** -0.5

    def gain(k, shape):
        return 1.0 + 0.02 * jax.random.normal(k, shape, f32)

    return {
        'x': jax.random.normal(ks[0], (BATCH, SEQ, D_MODEL), f32),
        'mem': jax.random.normal(ks[1], (BATCH, N_MEM, D_MODEL), f32),
        'rel_bias': 0.5 * jax.random.normal(ks[2], (REL_BUCKETS, A_HEADS), f32),
        'g_mix_pre': gain(ks[3], (DEPTH, D_MODEL)),
        'w_in': w(ks[4], (DEPTH, D_MODEL, N_IN), D_MODEL),
        'b_gate': 0.02 * jax.random.normal(ks[5], (DEPTH, N_BRANCH * D_MODEL), f32),
        'g_mem': gain(ks[6], (DEPTH, D_MODEL)),
        'w_mem_kv': w(ks[7], (DEPTH, D_MODEL, 2 * C_DIM), D_MODEL),
        'w_up_a': w(ks[8], (DEPTH, A_DIM, D_MODEL), A_DIM),
        'w_up_b': w(ks[9], (DEPTH, B_DIM, D_MODEL), B_DIM),
        'w_up_c': w(ks[10], (DEPTH, C_DIM, D_MODEL), C_DIM),
        'w_out': w(ks[11], (DEPTH, D_MODEL, D_MODEL), D_MODEL),
        'g_mix_post': gain(ks[12], (DEPTH, D_MODEL)),
        'g_ffn_pre': gain(ks[13], (DEPTH, D_MODEL)),
        'w_ffn_in': w(ks[14], (DEPTH, D_MODEL, 2 * D_FF), D_MODEL),
        'w_ffn_out': w(ks[15], (DEPTH, D_FF, D_MODEL), D_FF),
        'g_ffn_post': gain(ks[16], (DEPTH, D_MODEL)),
    }


def reference(x, mem, rel_bias, g_mix_pre, w_in, b_gate, g_mem, w_mem_kv, w_up_a, w_up_b,
              w_up_c, w_out, g_mix_post, g_ffn_pre, w_ffn_in, w_ffn_out, g_ffn_post):
    bsz, seq = x.shape[0], x.shape[1]
    split_points = [int(v) for v in np.cumsum(SPLIT_SIZES)[:-1]]
    for l in range(DEPTH):
        h = rmsnorm(x, g_mix_pre[l])
        proj = h @ w_in[l]
        aq, ak, av, iq, ik, iw, bq, bk, bv, cq, gate_logits = jnp.split(proj, split_points, axis=-1)
        aq = aq.reshape(bsz, seq, A_HEADS, HEAD_DIM)
        ak = ak.reshape(bsz, seq, A_HEADS, HEAD_DIM)
        av = av.reshape(bsz, seq, A_HEADS, HEAD_DIM)
        iq = iq.reshape(bsz, seq, IDX_HEADS, IDX_DIM)
        iw = iw * IDX_HEADS ** -0.5
        bq = bq.reshape(bsz, seq, B_HEADS, HEAD_DIM)
        bk = bk.reshape(bsz, seq, B_HEADS, HEAD_DIM)
        bv = bv.reshape(bsz, seq, B_HEADS, HEAD_DIM)
        cq = cq.reshape(bsz, seq, C_HEADS, C_HEAD_DIM)
        mkv = rmsnorm(mem, g_mem[l]) @ w_mem_kv[l]
        mk = mkv[..., :C_DIM].reshape(bsz, N_MEM, C_HEADS, C_HEAD_DIM)
        mv = mkv[..., C_DIM:].reshape(bsz, N_MEM, C_HEADS, C_HEAD_DIM)

        ya = dsa_attention(aq, ak, av, iq, ik, iw, rel_bias) @ w_up_a[l]
        yb = stick_breaking_attention(bq, bk, bv) @ w_up_b[l]
        yc = memory_attention(cq, mk, mv) @ w_up_c[l]
        gates = jax.nn.sigmoid((gate_logits + b_gate[l]).astype(jnp.float32)).astype(x.dtype)
        gates = gates.reshape(bsz, seq, N_BRANCH, D_MODEL)
        merged = gates[:, :, 0] * ya + gates[:, :, 1] * yb + gates[:, :, 2] * yc
        x = x + rmsnorm(merged @ w_out[l], g_mix_post[l])

        h = rmsnorm(x, g_ffn_pre[l])
        gu = h @ w_ffn_in[l]
        f = (jax.nn.silu(gu[..., :D_FF]) * gu[..., D_FF:]) @ w_ffn_out[l]
        x = x + rmsnorm(f, g_ffn_post[l])
    return x
```

```python
import functools

import numpy as np
import jax
import jax.numpy as jnp
from jax import lax
from jax.experimental import pallas as pl
from jax.experimental.pallas import tpu as pltpu

D_MODEL = 1024
CHUNK = 64
HEAD_DIM = 64
N_HEADS = 8
IDX_HEADS = 8
TOPK_MAX = 256
C_HEADS = 4
C_HEAD_DIM = 128
N_BRANCH = 3
REL_BUCKETS = 32
EPS = 1e-6

F32 = jnp.float32
BF16 = jnp.bfloat16
INT_MIN = -2 ** 31
NEG_BIG = -1e30

QB = 128
NK = 256
N_PAIR = N_HEADS // 2

COL_AQ, COL_AK, COL_AV, COL_IQ, COL_BQ, COL_BK, COL_BV, COL_CQ = range(8)
GATE_COL0 = 8 * 512
N_MAIN = GATE_COL0 + N_BRANCH * D_MODEL
N_SMALL = 384

VMEM_LIMIT = 56 * 1024 * 1024

_NT = (((1,), (1,)), ((), ()))


def _rms(x, g):
    return x * lax.rsqrt(jnp.mean(x * x, axis=-1, keepdims=True) + EPS) * g


def _proj_kernel(x_ref, g_ref, w_ref, b_ref, ws_ref, o_ref, ik_ref, iw_ref, h_ref, *,
                 first_gate_tile):
    j = pl.program_id(1)

    @pl.when(j == 0)
    def _():
        hb = _rms(x_ref[...], g_ref[...]).astype(BF16)
        h_ref[...] = hb
        small = jnp.dot(hb, ws_ref[...], preferred_element_type=F32)
        ik_ref[...] = small[:, :256].astype(BF16)
        iw_ref[...] = small[:, 256:]

    acc = jnp.dot(h_ref[...], w_ref[...], preferred_element_type=F32)

    @pl.when(j < first_gate_tile)
    def _():
        o_ref[...] = acc.astype(BF16)

    @pl.when(j >= first_gate_tile)
    def _():
        o_ref[...] = jax.nn.sigmoid(acc + b_ref[...]).astype(BF16)


def _project(x2, g, w_main, b_main, w_small):
    n = x2.shape[0]
    tm = min(1024, n)
    tn = 1024
    grid = (n // tm, N_MAIN // tn)
    return pl.pallas_call(
        functools.partial(_proj_kernel, first_gate_tile=GATE_COL0 // tn),
        name="in_proj",
        grid=grid,
        in_specs=[
            pl.BlockSpec((tm, D_MODEL), lambda i, j: (i, 0)),
            pl.BlockSpec((1, D_MODEL), lambda i, j: (0, 0)),
            pl.BlockSpec((D_MODEL, tn), lambda i, j: (0, j)),
            pl.BlockSpec((1, tn), lambda i, j: (0, j)),
            pl.BlockSpec((D_MODEL, N_SMALL), lambda i, j: (0, 0)),
        ],
        out_specs=[
            pl.BlockSpec((tm, tn), lambda i, j: (i, j)),
            pl.BlockSpec((tm, 256), lambda i, j: (i, 0)),
            pl.BlockSpec((tm, 128), lambda i, j: (i, 0)),
        ],
        out_shape=[
            jax.ShapeDtypeStruct((n, N_MAIN), BF16),
            jax.ShapeDtypeStruct((n, 256), BF16),
            jax.ShapeDtypeStruct((n, 128), F32),
        ],
        scratch_shapes=[pltpu.VMEM((tm, D_MODEL), BF16)],
        compiler_params=pltpu.CompilerParams(
            dimension_semantics=("arbitrary", "arbitrary"), vmem_limit_bytes=VMEM_LIMIT),
    )(x2, g, w_main, b_main, w_small)


def _memkv_kernel(x_ref, g_ref, w_ref, o_ref):
    hb = _rms(x_ref[...], g_ref[...]).astype(BF16)
    o_ref[...] = jnp.dot(hb, w_ref[...], preferred_element_type=F32).astype(BF16)


def _memkv(mem2, g, w):
    n = mem2.shape[0]
    tm = min(512, n)
    return pl.pallas_call(
        _memkv_kernel,
        name="mem_kv",
        grid=(n // tm,),
        in_specs=[
            pl.BlockSpec((tm, D_MODEL), lambda i: (i, 0)),
            pl.BlockSpec((1, D_MODEL), lambda i: (0, 0)),
            pl.BlockSpec((D_MODEL, w.shape[1]), lambda i: (0, 0)),
        ],
        out_specs=pl.BlockSpec((tm, w.shape[1]), lambda i: (i, 0)),
        out_shape=jax.ShapeDtypeStruct((n, w.shape[1]), BF16),
        compiler_params=pltpu.CompilerParams(
            dimension_semantics=("arbitrary",), vmem_limit_bytes=VMEM_LIMIT),
    )(mem2, g, w)


BIAS_OFFSETS = (0, -QB, -NK)
_LOG_BUCKET_STARTS = (12, 16, 23, 32, 46, 64, 91)
FAR_BUCKET = 15


def _bias_kernel(rb_ref, o_ref):
    row = lax.broadcasted_iota(jnp.int32, (QB, NK), 0)
    col = lax.broadcasted_iota(jnp.int32, (QB, NK), 1)
    for c, off in enumerate(BIAS_OFFSETS):
        rel = col - row + off
        n = jnp.abs(rel)
        large = jnp.full((QB, NK), 8, jnp.int32)
        for start in _LOG_BUCKET_STARTS:
            large = large + jnp.where(n >= start, 1, 0)
        bucket = jnp.where(rel > 0, REL_BUCKETS // 2, 0) + jnp.where(n < 8, n, large)
        for h in range(N_HEADS):
            val = jnp.full((QB, NK), rb_ref[0, h], F32)
            for b in range(1, REL_BUCKETS):
                val = jnp.where(bucket == b, rb_ref[b, h], val)
            o_ref[c, h] = val - rb_ref[FAR_BUCKET, h]


def _bias_tiles(rel_bias):
    return pl.pallas_call(
        _bias_kernel,
        name="rel_bias_tiles",
        in_specs=[pl.BlockSpec(memory_space=pltpu.SMEM)],
        out_specs=pl.BlockSpec(memory_space=pltpu.VMEM),
        out_shape=jax.ShapeDtypeStruct((len(BIAS_OFFSETS), N_HEADS, QB, NK), F32),
    )(rel_bias)


def _split_heads_into(qm_ref, q):
    lane = lax.broadcasted_iota(jnp.int32, (QB, 128), 1)
    for p in range(N_PAIR):
        qp = q[:, p * 128:(p + 1) * 128].astype(F32)
        qm_ref[p, :QB, :] = jnp.where(lane < HEAD_DIM, qp, 0.0).astype(BF16)
        qm_ref[p, QB:, :] = jnp.where(lane >= HEAD_DIM, qp, 0.0).astype(BF16)


def _merge_pair(o_even, o_odd):
    lane = lax.broadcasted_iota(jnp.int32, (QB, 128), 1)
    return jnp.where(lane < HEAD_DIM, o_even, o_odd)


def _dsa_kernel(aq_ref, iq_ref, iw_ref, ak_ref, av_ref, ik_ref, bias_ref, o_ref,
                key_ref, wb_ref, qm_ref, m_ref, l_ref, acc_ref, *, k_sel):
    i = pl.program_id(1)
    diag = i // 2
    nkb = diag + 1
    row = lax.broadcasted_iota(jnp.int32, (QB, NK), 0)
    col = lax.broadcasted_iota(jnp.int32, (QB, NK), 1)
    qchunk = (i * QB + row) // CHUNK

    iw = iw_ref[...] * (IDX_HEADS ** -0.5)
    for h in range(IDX_HEADS):
        wb_ref[h] = jnp.broadcast_to(iw[:, h:h + 1], (QB, NK))
    iq = iq_ref[...]

    def score_block(jb, carry):
        k0 = pl.multiple_of(jb * NK, NK)
        ik2 = ik_ref[pl.ds(k0, NK), :]
        acc = jnp.zeros((QB, NK), F32)
        for h in range(IDX_HEADS):
            p = h // 2
            ikh = ik2[:, :128] if h % 2 == 0 else ik2[:, 128:]
            s = lax.dot_general(iq[:, p * 128:(p + 1) * 128], ikh, _NT,
                                preferred_element_type=F32)
            acc = acc + wb_ref[h] * jnp.maximum(s, 0.0)
        bits = lax.bitcast_convert_type(acc, jnp.int32)
        key = bits ^ ((bits >> 31) & 0x7FFFFFFF)
        admissible = ((k0 + col) // CHUNK) <= qchunk
        key_ref[jb] = jnp.where(admissible, key, INT_MIN)
        return carry

    lax.fori_loop(0, nkb, score_block, 0)

    def count_ge(cand):
        candb = jnp.broadcast_to(cand, (QB, 128))

        def body(jb, c):
            kb = key_ref[jb]
            return (c + jnp.where(kb[:, :128] >= candb, 1, 0)
                    + jnp.where(kb[:, 128:] >= candb, 1, 0))

        c = lax.fori_loop(0, nkb, body, jnp.zeros((QB, 128), jnp.int32))
        return jnp.sum(c, axis=1, keepdims=True)

    zero = jnp.zeros((QB, 1), jnp.int32)
    thr = jnp.where(count_ge(zero) >= k_sel, zero, INT_MIN)

    def bit_step(it, thr):
        cand = thr + jnp.left_shift(jnp.int32(1), 30 - it)
        return jnp.where(count_ge(cand) >= k_sel, cand, thr)

    thr = lax.fori_loop(0, 31, bit_step, thr)
    thrb = jnp.broadcast_to(jnp.maximum(thr, INT_MIN + 1), (QB, 128))

    _split_heads_into(qm_ref, aq_ref[...])
    m_ref[...] = jnp.full(m_ref.shape, NEG_BIG, F32)
    l_ref[...] = jnp.zeros(l_ref.shape, F32)
    acc_ref[...] = jnp.zeros(acc_ref.shape, F32)

    def attn_block(jb, bias_idx):
        k0 = pl.multiple_of(jb * NK, NK)
        kb = key_ref[jb]
        maskadd = jnp.concatenate(
            [jnp.where(kb[:, :128] >= thrb, 0.0, NEG_BIG),
             jnp.where(kb[:, 128:] >= thrb, 0.0, NEG_BIG)], axis=1)
        for p in range(N_PAIR):
            kp = ak_ref[pl.ds(k0, NK), p * 128:(p + 1) * 128]
            vp = av_ref[pl.ds(k0, NK), p * 128:(p + 1) * 128]
            s2 = lax.dot_general(qm_ref[p], kp, _NT, preferred_element_type=F32)
            for e in range(2):
                h = 2 * p + e
                s = s2[e * QB:(e + 1) * QB] + maskadd
                if bias_idx is not None:
                    s = s + bias_ref[bias_idx, h]
                m_prev = m_ref[h]
                m_new = jnp.maximum(m_prev, jnp.max(s, axis=1, keepdims=True))
                alpha = jnp.exp(m_prev - m_new)
                pe = jnp.exp(s - m_new)
                l_ref[h] = alpha * l_ref[h] + jnp.sum(pe, axis=1, keepdims=True)
                pv = jnp.dot(pe.astype(BF16), vp, preferred_element_type=F32)
                acc_ref[h] = alpha * acc_ref[h] + pv
                m_ref[h] = m_new

    odd = i % 2
    has_prev = jnp.logical_and(odd == 0, i >= 2)
    n_far = diag - has_prev.astype(jnp.int32)

    def far_body(jb, carry):
        attn_block(jb, None)
        return carry

    lax.fori_loop(0, n_far, far_body, 0)

    @pl.when(has_prev)
    def _():
        attn_block(diag - 1, 2)

    attn_block(diag, odd)

    for p in range(N_PAIR):
        o_even = acc_ref[2 * p] / l_ref[2 * p]
        o_odd = acc_ref[2 * p + 1] / l_ref[2 * p + 1]
        o_ref[:, p * 128:(p + 1) * 128] = _merge_pair(o_even, o_odd).astype(BF16)


def _dsa(proj3, ik3, iw3, bias_tiles, k_sel):
    bsz, seq, _ = proj3.shape
    resident = dict(pipeline_mode=pl.Buffered(1))
    return pl.pallas_call(
        functools.partial(_dsa_kernel, k_sel=k_sel),
        name="dsa",
        grid=(bsz, seq // QB),
        in_specs=[
            pl.BlockSpec((None, QB, 512), lambda b, i: (b, i, COL_AQ)),
            pl.BlockSpec((None, QB, 512), lambda b, i: (b, i, COL_IQ)),
            pl.BlockSpec((None, QB, 128), lambda b, i: (b, i, 0)),
            pl.BlockSpec((None, seq, 512), lambda b, i: (b, 0, COL_AK), **resident),
            pl.BlockSpec((None, seq, 512), lambda b, i: (b, 0, COL_AV), **resident),
            pl.BlockSpec((None, seq, 256), lambda b, i: (b, 0, 0), **resident),
            pl.BlockSpec(bias_tiles.shape, lambda b, i: (0, 0, 0, 0), **resident),
        ],
        out_specs=pl.BlockSpec((None, QB, 512), lambda b, i: (b, i, 0)),
        out_shape=jax.ShapeDtypeStruct((bsz, seq, 512), BF16),
        scratch_shapes=[
            pltpu.VMEM((seq // NK, QB, NK), jnp.int32),
            pltpu.VMEM((IDX_HEADS, QB, NK), F32),
            pltpu.VMEM((N_PAIR, 2 * QB, 128), BF16),
            pltpu.VMEM((N_HEADS, QB, 1), F32),
            pltpu.VMEM((N_HEADS, QB, 1), F32),
            pltpu.VMEM((N_HEADS, QB, 128), F32),
        ],
        compiler_params=pltpu.CompilerParams(
            dimension_semantics=("arbitrary", "arbitrary"), vmem_limit_bytes=VMEM_LIMIT),
    )(proj3, proj3, iw3, proj3, proj3, ik3, bias_tiles)


SB_DEAD_MASS = 104.0


def _sb_kernel(q_ref, k_ref, v_ref, o_ref, qm_ref, u_ref, carry_ref, acc_ref):
    i = pl.program_id(1)
    diag = i // 2
    row = lax.broadcasted_iota(jnp.int32, (QB, NK), 0)
    col = lax.broadcasted_iota(jnp.int32, (QB, NK), 1)
    qpos = i * QB + row

    _split_heads_into(qm_ref, q_ref[...])
    kr = lax.broadcasted_iota(jnp.int32, (NK, NK), 0)
    kc = lax.broadcasted_iota(jnp.int32, (NK, NK), 1)
    u_ref[...] = jnp.where(kr > kc, 1.0, 0.0).astype(BF16)
    carry_ref[...] = jnp.zeros(carry_ref.shape, F32)
    acc_ref[...] = jnp.zeros(acc_ref.shape, F32)

    def block(jb, on_diagonal):
        k0 = pl.multiple_of(jb * NK, NK)
        if on_diagonal:
            causal = (k0 + col) < qpos
        for p in range(N_PAIR):
            kp = k_ref[pl.ds(k0, NK), p * 128:(p + 1) * 128]
            vp = v_ref[pl.ds(k0, NK), p * 128:(p + 1) * 128]
            z2 = lax.dot_general(qm_ref[p], kp, _NT, preferred_element_type=F32)
            for e in range(2):
                h = 2 * p + e
                z = z2[e * QB:(e + 1) * QB]
                sp = jnp.maximum(z, 0.0) + jnp.log(1.0 + jnp.exp(-jnp.abs(z)))
                spm = jnp.where(causal, sp, 0.0) if on_diagonal else sp
                hi = spm.astype(BF16)
                lo = (spm - hi.astype(F32)).astype(BF16)
                u = u_ref[...]
                later = (jnp.dot(hi, u, preferred_element_type=F32)
                         + jnp.dot(lo, u, preferred_element_type=F32))
                carry = carry_ref[h]
                a = jnp.exp(z - sp - later - carry)
                if on_diagonal:
                    a = jnp.where(causal, a, 0.0)
                acc_ref[h] += jnp.dot(a.astype(BF16), vp, preferred_element_type=F32)
                carry_ref[h] = carry + jnp.sum(spm, axis=1, keepdims=True)

    block(diag, True)

    def alive():
        return (jnp.min(carry_ref[...]) <= SB_DEAD_MASS).astype(jnp.int32)

    def cond(state):
        jb, go = state
        return jnp.logical_and(jb >= 0, go > 0)

    def body(state):
        jb, _ = state
        block(jb, False)
        return jb - 1, alive()

    lax.while_loop(cond, body, (diag - 1, alive()))

    for p in range(N_PAIR):
        o_ref[:, p * 128:(p + 1) * 128] = _merge_pair(
            acc_ref[2 * p], acc_ref[2 * p + 1]).astype(BF16)


def _stick_breaking(proj3):
    bsz, seq, _ = proj3.shape
    resident = dict(pipeline_mode=pl.Buffered(1))
    return pl.pallas_call(
        _sb_kernel,
        name="stick_breaking",
        grid=(bsz, seq // QB),
        in_specs=[
            pl.BlockSpec((None, QB, 512), lambda b, i: (b, i, COL_BQ)),
            pl.BlockSpec((None, seq, 512), lambda b, i: (b, 0, COL_BK), **resident),
            pl.BlockSpec((None, seq, 512), lambda b, i: (b, 0, COL_BV), **resident),
        ],
        out_specs=pl.BlockSpec((None, QB, 512), lambda b, i: (b, i, 0)),
        out_shape=jax.ShapeDtypeStruct((bsz, seq, 512), BF16),
        scratch_shapes=[
            pltpu.VMEM((N_PAIR, 2 * QB, 128), BF16),
            pltpu.VMEM((NK, NK), BF16),
            pltpu.VMEM((N_HEADS, QB, 1), F32),
            pltpu.VMEM((N_HEADS, QB, 128), F32),
        ],
        compiler_params=pltpu.CompilerParams(
            dimension_semantics=("arbitrary", "arbitrary"), vmem_limit_bytes=VMEM_LIMIT),
    )(proj3, proj3, proj3)


def _merge_kernel(x_ref, ya_ref, yb_ref, cq_ref, g0_ref, g1_ref, g2_ref, mk_ref, mv_ref,
                  wa_ref, wb_ref, wc_ref, wo_ref, gp_ref, o_ref):
    cq = cq_ref[...]
    heads = []
    for h in range(C_HEADS):
        sl = slice(h * C_HEAD_DIM, (h + 1) * C_HEAD_DIM)
        s = lax.dot_general(cq[:, sl], mk_ref[:, sl], _NT,
                            preferred_element_type=F32) * (C_HEAD_DIM ** -0.5)
        e = jnp.exp(s - jnp.max(s, axis=1, keepdims=True))
        p = e / jnp.sum(e, axis=1, keepdims=True)
        heads.append(jnp.dot(p.astype(BF16), mv_ref[:, sl], preferred_element_type=F32))
    yc_pre = jnp.concatenate(heads, axis=1).astype(BF16)
    ya = jnp.dot(ya_ref[...], wa_ref[...], preferred_element_type=F32)
    yb = jnp.dot(yb_ref[...], wb_ref[...], preferred_element_type=F32)
    yc = jnp.dot(yc_pre, wc_ref[...], preferred_element_type=F32)
    merged = (g0_ref[...].astype(F32) * ya + g1_ref[...].astype(F32) * yb
              + g2_ref[...].astype(F32) * yc)
    o = jnp.dot(merged.astype(BF16), wo_ref[...], preferred_element_type=F32)
    o_ref[...] = x_ref[...] + _rms(o, gp_ref[...])


def _merge(x2, ya2, yb2, proj2, mkv3, wa, wb, wc, wo, g_post, seq):
    n = x2.shape[0]
    tm = min(512, seq)
    per_batch = seq // tm
    n_mem = mkv3.shape[1]
    c_dim = C_HEADS * C_HEAD_DIM
    const = lambda t: (0, 0)
    return pl.pallas_call(
        _merge_kernel,
        name="merge",
        grid=(n // tm,),
        in_specs=[
            pl.BlockSpec((tm, D_MODEL), lambda t: (t, 0)),
            pl.BlockSpec((tm, 512), lambda t: (t, 0)),
            pl.BlockSpec((tm, 512), lambda t: (t, 0)),
            pl.BlockSpec((tm, 512), lambda t: (t, COL_CQ)),
            pl.BlockSpec((tm, D_MODEL), lambda t: (t, GATE_COL0 // D_MODEL)),
            pl.BlockSpec((tm, D_MODEL), lambda t: (t, GATE_COL0 // D_MODEL + 1)),
            pl.BlockSpec((tm, D_MODEL), lambda t: (t, GATE_COL0 // D_MODEL + 2)),
            pl.BlockSpec((None, n_mem, c_dim), lambda t: (t // per_batch, 0, 0)),
            pl.BlockSpec((None, n_mem, c_dim), lambda t: (t // per_batch, 0, 1)),
            pl.BlockSpec(wa.shape, const),
            pl.BlockSpec(wb.shape, const),
            pl.BlockSpec(wc.shape, const),
            pl.BlockSpec(wo.shape, const),
            pl.BlockSpec((1, D_MODEL), const),
        ],
        out_specs=pl.BlockSpec((tm, D_MODEL), lambda t: (t, 0)),
        out_shape=jax.ShapeDtypeStruct((n, D_MODEL), F32),
        compiler_params=pltpu.CompilerParams(
            dimension_semantics=("arbitrary",), vmem_limit_bytes=VMEM_LIMIT),
    )(x2, ya2, yb2, proj2, proj2, proj2, proj2, mkv3, mkv3, wa, wb, wc, wo, g_post)


def _ffn_kernel(x_ref, gpre_ref, wg_ref, wu_ref, wo_ref, gpost_ref, o_ref, h_ref, acc_ref):
    k = pl.program_id(1)

    @pl.when(k == 0)
    def _():
        h_ref[...] = _rms(x_ref[...], gpre_ref[...]).astype(BF16)
        acc_ref[...] = jnp.zeros(acc_ref.shape, F32)

    h = h_ref[...]
    g = jnp.dot(h, wg_ref[...], preferred_element_type=F32)
    u = jnp.dot(h, wu_ref[...], preferred_element_type=F32)
    act = (g * jax.nn.sigmoid(g) * u).astype(BF16)
    acc_ref[...] += jnp.dot(act, wo_ref[...], preferred_element_type=F32)

    @pl.when(k == pl.num_programs(1) - 1)
    def _():
        o_ref[...] = x_ref[...] + _rms(acc_ref[...], gpost_ref[...])


def _ffn(x2, g_pre, wg, wu, wo, g_post):
    n = x2.shape[0]
    d_ff = wg.shape[1]
    tm = min(512, n)
    tf = d_ff // 2
    return pl.pallas_call(
        _ffn_kernel,
        name="ffn",
        grid=(n // tm, d_ff // tf),
        in_specs=[
            pl.BlockSpec((tm, D_MODEL), lambda t, k: (t, 0)),
            pl.BlockSpec((1, D_MODEL), lambda t, k: (0, 0)),
            pl.BlockSpec((D_MODEL, tf), lambda t, k: (0, k)),
            pl.BlockSpec((D_MODEL, tf), lambda t, k: (0, k)),
            pl.BlockSpec((tf, D_MODEL), lambda t, k: (k, 0)),
            pl.BlockSpec((1, D_MODEL), lambda t, k: (0, 0)),
        ],
        out_specs=pl.BlockSpec((tm, D_MODEL), lambda t, k: (t, 0)),
        out_shape=jax.ShapeDtypeStruct((n, D_MODEL), F32),
        scratch_shapes=[pltpu.VMEM((tm, D_MODEL), BF16), pltpu.VMEM((tm, D_MODEL), F32)],
        compiler_params=pltpu.CompilerParams(
            dimension_semantics=("arbitrary", "arbitrary"), vmem_limit_bytes=VMEM_LIMIT),
    )(x2, g_pre, wg, wu, wo, g_post)


def _pack_w_in(w, b_gate):
    sizes = (512, 512, 512, IDX_HEADS * 64, 64, IDX_HEADS, 512, 512, 512, 512,
             N_BRANCH * D_MODEL)
    aq, ak, av, iq, ik, iw, bq, bk, bv, cq, gates = jnp.split(w, np.cumsum(sizes)[:-1], axis=1)
    scale = HEAD_DIM ** -0.5
    w_main = jnp.concatenate(
        [aq * scale, ak, av, iq * scale, bq * scale, bk, bv, cq, gates], axis=1).astype(BF16)
    z64 = jnp.zeros((D_MODEL, 64), F32)
    w_small = jnp.concatenate(
        [ik, z64, z64, ik, iw, jnp.zeros((D_MODEL, 128 - IDX_HEADS), F32)], axis=1).astype(BF16)
    b_main = jnp.concatenate([jnp.zeros((GATE_COL0,), F32), b_gate])[None, :]
    return w_main, w_small, b_main


def kernel(x, mem, rel_bias, g_mix_pre, w_in, b_gate, g_mem, w_mem_kv, w_up_a, w_up_b, w_up_c,
           w_out, g_mix_post, g_ffn_pre, w_ffn_in, w_ffn_out, g_ffn_post):
    bsz, seq, _ = x.shape
    n_mem = mem.shape[1]
    k_sel = min(TOPK_MAX, seq // 4)
    bias_tiles = _bias_tiles(rel_bias)
    x2 = x.reshape(bsz * seq, D_MODEL)
    for l in range(w_in.shape[0]):
        w_main, w_small, b_main = _pack_w_in(w_in[l], b_gate[l])
        proj2, ik2, iw2 = _project(x2, g_mix_pre[l][None, :], w_main, b_main, w_small)
        proj3 = proj2.reshape(bsz, seq, N_MAIN)
        mkv = _memkv(mem.reshape(bsz * n_mem, D_MODEL), g_mem[l][None, :],
                     w_mem_kv[l].astype(BF16))
        ya = _dsa(proj3, ik2.reshape(bsz, seq, 256), iw2.reshape(bsz, seq, 128),
                  bias_tiles, k_sel)
        yb = _stick_breaking(proj3)
        x2 = _merge(x2, ya.reshape(bsz * seq, 512), yb.reshape(bsz * seq, 512), proj2,
                    mkv.reshape(bsz, n_mem, 2 * C_HEADS * C_HEAD_DIM),
                    w_up_a[l].astype(BF16), w_up_b[l].astype(BF16), w_up_c[l].astype(BF16),
                    w_out[l].astype(BF16), g_mix_post[l][None, :], seq)
        d_ff = w_ffn_out.shape[1]
        w_ffn = w_ffn_in[l].astype(BF16)
        x2 = _ffn(x2, g_ffn_pre[l][None, :], w_ffn[:, :d_ff], w_ffn[:, d_ff:],
                  w_ffn_out[l].astype(BF16), g_ffn_post[l][None, :])
    return x2.reshape(bsz, seq, D_MODEL)
```

```python
import functools

import numpy as np
import jax
import jax.numpy as jnp
from jax import lax
from jax.experimental import pallas as pl
from jax.experimental.pallas import tpu as pltpu

D_MODEL = 1024
CHUNK = 64
HEAD_DIM = 64
N_HEADS = 8
IDX_HEADS = 8
TOPK_MAX = 256
C_HEADS = 4
C_HEAD_DIM = 128
N_BRANCH = 3
REL_BUCKETS = 32
EPS = 1e-6

F32 = jnp.float32
BF16 = jnp.bfloat16
INT_MIN = -2 ** 31
NEG_BIG = -1e30

QB = 128
NK = 256
N_PAIR = N_HEADS // 2
ONES_ROWS = 16

COL_AQ, COL_AK, COL_AV, COL_IQ, COL_BQ, COL_BK, COL_BV, COL_CQ = range(8)
GATE_COL0 = 8 * 512
N_MAIN = GATE_COL0 + N_BRANCH * D_MODEL
N_IK = 256
N_TRANS = 512 + 16

VMEM_LIMIT = 56 * 1024 * 1024

_NT = (((1,), (1,)), ((), ()))


def _rms(x, g):
    return x * lax.rsqrt(jnp.mean(x * x, axis=-1, keepdims=True) + EPS) * g


def _proj_kernel(x_ref, g_ref, w_ref, b_ref, wik_ref, wt_ref, o_ref, ik_ref, avt_ref, iwt_ref,
                 h_ref, *, first_gate_tile):
    j = pl.program_id(1)

    @pl.when(j == 0)
    def _():
        hb = _rms(x_ref[...], g_ref[...]).astype(BF16)
        h_ref[...] = hb
        ik_ref[...] = jnp.dot(hb, wik_ref[...], preferred_element_type=F32).astype(BF16)
        tr = lax.dot_general(wt_ref[...], hb, _NT, preferred_element_type=F32)
        for c in range(avt_ref.shape[0]):
            avt_ref[c] = tr[:512, c * NK:(c + 1) * NK].astype(BF16)
        iwt_ref[...] = tr[512:512 + IDX_HEADS, :]

    acc = jnp.dot(h_ref[...], w_ref[...], preferred_element_type=F32)

    @pl.when(j < first_gate_tile)
    def _():
        o_ref[...] = acc.astype(BF16)

    @pl.when(j >= first_gate_tile)
    def _():
        o_ref[...] = jax.nn.sigmoid(acc + b_ref[...]).astype(BF16)


def _project(x2, g, w_main, b_main, w_ik, w_trans):
    n = x2.shape[0]
    tm = min(1024, n)
    tn = 1024
    grid = (n // tm, N_MAIN // tn)
    return pl.pallas_call(
        functools.partial(_proj_kernel, first_gate_tile=GATE_COL0 // tn),
        name="in_proj",
        grid=grid,
        in_specs=[
            pl.BlockSpec((tm, D_MODEL), lambda i, j: (i, 0)),
            pl.BlockSpec((1, D_MODEL), lambda i, j: (0, 0)),
            pl.BlockSpec((D_MODEL, tn), lambda i, j: (0, j)),
            pl.BlockSpec((1, tn), lambda i, j: (0, j)),
            pl.BlockSpec((D_MODEL, N_IK), lambda i, j: (0, 0)),
            pl.BlockSpec((N_TRANS, D_MODEL), lambda i, j: (0, 0)),
        ],
        out_specs=[
            pl.BlockSpec((tm, tn), lambda i, j: (i, j)),
            pl.BlockSpec((tm, N_IK), lambda i, j: (i, 0)),
            pl.BlockSpec((tm // NK, 512, NK), lambda i, j: (i, 0, 0)),
            pl.BlockSpec((IDX_HEADS, tm), lambda i, j: (0, i)),
        ],
        out_shape=[
            jax.ShapeDtypeStruct((n, N_MAIN), BF16),
            jax.ShapeDtypeStruct((n, N_IK), BF16),
            jax.ShapeDtypeStruct((n // NK, 512, NK), BF16),
            jax.ShapeDtypeStruct((IDX_HEADS, n), F32),
        ],
        scratch_shapes=[pltpu.VMEM((tm, D_MODEL), BF16)],
        compiler_params=pltpu.CompilerParams(
            dimension_semantics=("arbitrary", "arbitrary"), vmem_limit_bytes=VMEM_LIMIT),
    )(x2, g, w_main, b_main, w_ik, w_trans)


def _memkv_kernel(x_ref, g_ref, w_ref, o_ref):
    hb = _rms(x_ref[...], g_ref[...]).astype(BF16)
    o_ref[...] = jnp.dot(hb, w_ref[...], preferred_element_type=F32).astype(BF16)


def _memkv(mem2, g, w):
    n = mem2.shape[0]
    tm = min(512, n)
    return pl.pallas_call(
        _memkv_kernel,
        name="mem_kv",
        grid=(n // tm,),
        in_specs=[
            pl.BlockSpec((tm, D_MODEL), lambda i: (i, 0)),
            pl.BlockSpec((1, D_MODEL), lambda i: (0, 0)),
            pl.BlockSpec((D_MODEL, w.shape[1]), lambda i: (0, 0)),
        ],
        out_specs=pl.BlockSpec((tm, w.shape[1]), lambda i: (i, 0)),
        out_shape=jax.ShapeDtypeStruct((n, w.shape[1]), BF16),
        compiler_params=pltpu.CompilerParams(
            dimension_semantics=("arbitrary",), vmem_limit_bytes=VMEM_LIMIT),
    )(mem2, g, w)


BIAS_OFFSETS = (0, -QB, -NK, -QB - NK)
_LOG_BUCKET_STARTS = (12, 16, 23, 32, 46, 64, 91)
FAR_BUCKET = 15


def _bias_kernel(rb_ref, o_ref):
    key = lax.broadcasted_iota(jnp.int32, (NK, QB), 0)
    qry = lax.broadcasted_iota(jnp.int32, (NK, QB), 1)
    for c, off in enumerate(BIAS_OFFSETS):
        rel = key - qry + off
        n = jnp.abs(rel)
        large = jnp.full((NK, QB), 8, jnp.int32)
        for start in _LOG_BUCKET_STARTS:
            large = large + jnp.where(n >= start, 1, 0)
        bucket = jnp.where(rel > 0, REL_BUCKETS // 2, 0) + jnp.where(n < 8, n, large)
        for h in range(N_HEADS):
            val = jnp.full((NK, QB), rb_ref[0, h], F32)
            for b in range(1, REL_BUCKETS):
                val = jnp.where(bucket == b, rb_ref[b, h], val)
            o_ref[c, h // 2, :, (h % 2) * QB:(h % 2 + 1) * QB] = val - rb_ref[FAR_BUCKET, h]


def _bias_tiles(rel_bias):
    return pl.pallas_call(
        _bias_kernel,
        name="rel_bias_tiles",
        in_specs=[pl.BlockSpec(memory_space=pltpu.SMEM)],
        out_specs=pl.BlockSpec(memory_space=pltpu.VMEM),
        out_shape=jax.ShapeDtypeStruct((len(BIAS_OFFSETS), N_PAIR, NK, 2 * QB), F32),
    )(rel_bias)


def _split_heads_into(qm_ref, q):
    lane = lax.broadcasted_iota(jnp.int32, (QB, 128), 1)
    for p in range(N_PAIR):
        qp = q[:, p * 128:(p + 1) * 128].astype(F32)
        qm_ref[p, :QB, :] = jnp.where(lane < HEAD_DIM, qp, 0.0).astype(BF16)
        qm_ref[p, QB:, :] = jnp.where(lane >= HEAD_DIM, qp, 0.0).astype(BF16)


def _merge_pair(o_even, o_odd):
    lane = lax.broadcasted_iota(jnp.int32, (QB, 128), 1)
    return jnp.where(lane < HEAD_DIM, o_even, o_odd)


CNT_ROWS = 32


def _dsa_kernel(aq_ref, iq_ref, iwt_ref, ak_ref, avt_ref, ik_ref, bias_ref, o_ref,
                key_ref, iqs_ref, qm_ref, m_ref, acc_ref, s0_ref, cm0_ref, s1_ref, cm1_ref, *,
                k_sel):
    i = pl.program_id(1)
    diag = i // 2
    nkb = diag + 1
    krow = lax.broadcasted_iota(jnp.int32, (NK, QB), 0)
    qcol = lax.broadcasted_iota(jnp.int32, (NK, QB), 1)
    qchunk = (i * QB + qcol) // CHUNK

    iwt = iwt_ref[...] * (IDX_HEADS ** -0.5)
    iq = iq_ref[...]
    for p in range(N_PAIR):
        iqs_ref[p * QB:(p + 1) * QB, :] = iq[:, p * 128:(p + 1) * 128]

    def score_block(jb, carry):
        k0 = pl.multiple_of(jb * NK, NK)
        ik2 = ik_ref[pl.ds(k0, NK), :]
        acc = jnp.zeros((NK, QB), F32)
        for half in range(2):
            iqh = iqs_ref[half * 2 * QB:(half + 1) * 2 * QB, :]
            s_even = lax.dot_general(ik2[:, :128], iqh, _NT, preferred_element_type=F32)
            s_odd = lax.dot_general(ik2[:, 128:], iqh, _NT, preferred_element_type=F32)
            for pp in range(2):
                h = 2 * (2 * half + pp)
                sl = slice(pp * QB, (pp + 1) * QB)
                acc = acc + iwt[h:h + 1, :] * jnp.maximum(s_even[:, sl], 0.0)
                acc = acc + iwt[h + 1:h + 2, :] * jnp.maximum(s_odd[:, sl], 0.0)
        bits = lax.bitcast_convert_type(acc, jnp.int32)
        key = bits ^ ((bits >> 31) & 0x7FFFFFFF)
        admissible = ((k0 + krow) // CHUNK) <= qchunk
        key_ref[jb] = jnp.where(admissible, key, INT_MIN)
        return carry

    lax.fori_loop(0, nkb, score_block, 0)

    def count_ge(cand):
        def body(jb, c):
            ge = jnp.where(key_ref[jb] >= cand, 1, 0)
            for r in range(NK // CNT_ROWS):
                c = c + ge[r * CNT_ROWS:(r + 1) * CNT_ROWS, :]
            return c

        c = lax.fori_loop(0, nkb, body, jnp.zeros((CNT_ROWS, QB), jnp.int32))
        return jnp.sum(c, axis=0, keepdims=True)

    zero = jnp.zeros((1, QB), jnp.int32)
    thr = jnp.where(count_ge(zero) >= k_sel, zero, INT_MIN)

    def bit_step(it, thr):
        cand = thr + jnp.left_shift(jnp.int32(1), 30 - it)
        return jnp.where(count_ge(cand) >= k_sel, cand, thr)

    thr = lax.fori_loop(0, 31, bit_step, thr)
    thr = jnp.maximum(thr, INT_MIN + 1)

    _split_heads_into(qm_ref, aq_ref[...])
    m_ref[...] = jnp.full(m_ref.shape, NEG_BIG, F32)
    acc_ref[...] = jnp.zeros(acc_ref.shape, F32)
    ones = jnp.ones((ONES_ROWS, NK), BF16)

    slots = ((s0_ref, cm0_ref), (s1_ref, cm1_ref))

    def logits_stage(jb, slot, bias_idx):
        s_ref, cm_ref = slots[slot]
        k0 = pl.multiple_of(jb * NK, NK)
        mask1 = jnp.where(key_ref[jb] >= thr, 0.0, NEG_BIG)
        mask2 = jnp.concatenate([mask1, mask1], axis=1)
        for p in range(N_PAIR):
            kp = ak_ref[pl.ds(k0, NK), p * 128:(p + 1) * 128]
            s2 = lax.dot_general(kp, qm_ref[p], _NT, preferred_element_type=F32) + mask2
            if bias_idx is not None:
                s2 = s2 + bias_ref[bias_idx, p]
            s_ref[p] = s2
            cm_ref[p] = jnp.max(s2, axis=0, keepdims=True)

    def softmax_stage(jb, slot):
        s_ref, cm_ref = slots[slot]
        for p in range(N_PAIR):
            m_prev = m_ref[p]
            m_new = jnp.maximum(m_prev, cm_ref[p])
            alpha = jnp.exp(m_prev - m_new)
            pe = jnp.exp(s_ref[p] - m_new).astype(BF16)
            vt = jnp.concatenate([avt_ref[jb, p * 128:(p + 1) * 128, :], ones], axis=0)
            acc_ref[p] = alpha * acc_ref[p] + jnp.dot(vt, pe, preferred_element_type=F32)
            m_ref[p] = m_new

    odd = i % 2
    odd_diag = diag % 2
    logits_stage(diag, 0, odd)

    @pl.when(diag >= 1)
    def _():
        logits_stage(diag - 1, 1, 2 + odd)

    softmax_stage(diag, 0)
    n_steps = diag - 1

    def two_steps(u, carry):
        b = diag - 1 - 2 * u
        logits_stage(b - 1, 0, None)
        softmax_stage(b, 1)
        logits_stage(b - 2, 1, None)
        softmax_stage(b - 1, 0)
        return carry

    lax.fori_loop(0, jnp.maximum(n_steps, 0) // 2, two_steps, 0)

    @pl.when(jnp.logical_and(n_steps >= 1, n_steps % 2 == 1))
    def _():
        logits_stage(0, 0, None)
        softmax_stage(1, 1)

    @pl.when(jnp.logical_and(diag >= 1, odd_diag == 1))
    def _():
        softmax_stage(0, 1)

    @pl.when(jnp.logical_and(diag >= 1, odd_diag == 0))
    def _():
        softmax_stage(0, 0)

    drow = lax.broadcasted_iota(jnp.int32, (128, QB), 0)
    for p in range(N_PAIR):
        a = acc_ref[p]
        o_even = a[0:128, :QB] / a[128:129, :QB]
        o_odd = a[0:128, QB:] / a[128:129, QB:]
        o_t = jnp.where(drow < HEAD_DIM, o_even, o_odd)
        o_ref[:, p * 128:(p + 1) * 128] = o_t.T.astype(BF16)


def _dsa(proj3, ik3, avt4, iwt, bias_tiles, k_sel):
    bsz, seq, _ = proj3.shape
    nq = seq // QB
    resident = dict(pipeline_mode=pl.Buffered(1))
    return pl.pallas_call(
        functools.partial(_dsa_kernel, k_sel=k_sel),
        name="dsa",
        grid=(bsz, nq),
        in_specs=[
            pl.BlockSpec((None, QB, 512), lambda b, i: (b, i, COL_AQ)),
            pl.BlockSpec((None, QB, 512), lambda b, i: (b, i, COL_IQ)),
            pl.BlockSpec((IDX_HEADS, QB), lambda b, i: (0, b * nq + i)),
            pl.BlockSpec((None, seq, 512), lambda b, i: (b, 0, COL_AK), **resident),
            pl.BlockSpec((None, seq // NK, 512, NK), lambda b, i: (b, 0, 0, 0), **resident),
            pl.BlockSpec((None, seq, N_IK), lambda b, i: (b, 0, 0), **resident),
            pl.BlockSpec(bias_tiles.shape, lambda b, i: (0, 0, 0, 0), **resident),
        ],
        out_specs=pl.BlockSpec((None, QB, 512), lambda b, i: (b, i, 0)),
        out_shape=jax.ShapeDtypeStruct((bsz, seq, 512), BF16),
        scratch_shapes=[
            pltpu.VMEM((seq // NK, NK, QB), jnp.int32),
            pltpu.VMEM((N_PAIR * QB, 128), BF16),
            pltpu.VMEM((N_PAIR, 2 * QB, 128), BF16),
            pltpu.VMEM((N_PAIR, 1, 2 * QB), F32),
            pltpu.VMEM((N_PAIR, 128 + ONES_ROWS, 2 * QB), F32),
            pltpu.VMEM((N_PAIR, NK, 2 * QB), F32),
            pltpu.VMEM((N_PAIR, 1, 2 * QB), F32),
            pltpu.VMEM((N_PAIR, NK, 2 * QB), F32),
            pltpu.VMEM((N_PAIR, 1, 2 * QB), F32),
        ],
        compiler_params=pltpu.CompilerParams(
            dimension_semantics=("arbitrary", "arbitrary"), vmem_limit_bytes=VMEM_LIMIT),
    )(proj3, proj3, iwt, proj3, avt4, ik3, bias_tiles)


SB_DEAD_MASS = 104.0


def _sb_kernel(q_ref, k_ref, v_ref, o_ref, qm_ref, u_ref, carry_ref, acc_ref):
    i = pl.program_id(1)
    diag = i // 2
    row = lax.broadcasted_iota(jnp.int32, (QB, NK), 0)
    col = lax.broadcasted_iota(jnp.int32, (QB, NK), 1)
    qpos = i * QB + row

    _split_heads_into(qm_ref, q_ref[...])
    kr = lax.broadcasted_iota(jnp.int32, (NK, NK), 0)
    kc = lax.broadcasted_iota(jnp.int32, (NK, NK), 1)
    u_ref[...] = jnp.where(kr > kc, 1.0, 0.0).astype(BF16)
    carry_ref[...] = jnp.zeros(carry_ref.shape, F32)
    acc_ref[...] = jnp.zeros(acc_ref.shape, F32)

    def block(jb, on_diagonal):
        k0 = pl.multiple_of(jb * NK, NK)
        if on_diagonal:
            causal = (k0 + col) < qpos
        for p in range(N_PAIR):
            kp = k_ref[pl.ds(k0, NK), p * 128:(p + 1) * 128]
            vp = v_ref[pl.ds(k0, NK), p * 128:(p + 1) * 128]
            z2 = lax.dot_general(qm_ref[p], kp, _NT, preferred_element_type=F32)
            for e in range(2):
                h = 2 * p + e
                z = z2[e * QB:(e + 1) * QB]
                sp = jnp.maximum(z, 0.0) + jnp.log(1.0 + jnp.exp(-jnp.abs(z)))
                spm = jnp.where(causal, sp, 0.0) if on_diagonal else sp
                hi = spm.astype(BF16)
                lo = (spm - hi.astype(F32)).astype(BF16)
                u = u_ref[...]
                later = (jnp.dot(hi, u, preferred_element_type=F32)
                         + jnp.dot(lo, u, preferred_element_type=F32))
                carry = carry_ref[h]
                a = jnp.exp(z - sp - later - carry)
                if on_diagonal:
                    a = jnp.where(causal, a, 0.0)
                acc_ref[h] += jnp.dot(a.astype(BF16), vp, preferred_element_type=F32)
                carry_ref[h] = carry + jnp.sum(spm, axis=1, keepdims=True)

    block(diag, True)

    def alive():
        return (jnp.min(carry_ref[...]) <= SB_DEAD_MASS).astype(jnp.int32)

    def cond(state):
        jb, go = state
        return jnp.logical_and(jb >= 0, go > 0)

    def body(state):
        jb, _ = state
        block(jb, False)
        return jb - 1, alive()

    lax.while_loop(cond, body, (diag - 1, alive()))

    for p in range(N_PAIR):
        o_ref[:, p * 128:(p + 1) * 128] = _merge_pair(
            acc_ref[2 * p], acc_ref[2 * p + 1]).astype(BF16)


def _stick_breaking(proj3):
    bsz, seq, _ = proj3.shape
    resident = dict(pipeline_mode=pl.Buffered(1))
    return pl.pallas_call(
        _sb_kernel,
        name="stick_breaking",
        grid=(bsz, seq // QB),
        in_specs=[
            pl.BlockSpec((None, QB, 512), lambda b, i: (b, i, COL_BQ)),
            pl.BlockSpec((None, seq, 512), lambda b, i: (b, 0, COL_BK), **resident),
            pl.BlockSpec((None, seq, 512), lambda b, i: (b, 0, COL_BV), **resident),
        ],
        out_specs=pl.BlockSpec((None, QB, 512), lambda b, i: (b, i, 0)),
        out_shape=jax.ShapeDtypeStruct((bsz, seq, 512), BF16),
        scratch_shapes=[
            pltpu.VMEM((N_PAIR, 2 * QB, 128), BF16),
            pltpu.VMEM((NK, NK), BF16),
            pltpu.VMEM((N_HEADS, QB, 1), F32),
            pltpu.VMEM((N_HEADS, QB, 128), F32),
        ],
        compiler_params=pltpu.CompilerParams(
            dimension_semantics=("arbitrary", "arbitrary"), vmem_limit_bytes=VMEM_LIMIT),
    )(proj3, proj3, proj3)


def _merge_kernel(x_ref, ya_ref, yb_ref, cq_ref, g0_ref, g1_ref, g2_ref, mk_ref, mv_ref,
                  wa_ref, wb_ref, wc_ref, wo_ref, gp_ref, o_ref):
    cq = cq_ref[...]
    heads = []
    for h in range(C_HEADS):
        sl = slice(h * C_HEAD_DIM, (h + 1) * C_HEAD_DIM)
        s = lax.dot_general(cq[:, sl], mk_ref[:, sl], _NT,
                            preferred_element_type=F32) * (C_HEAD_DIM ** -0.5)
        e = jnp.exp(s - jnp.max(s, axis=1, keepdims=True))
        p = e / jnp.sum(e, axis=1, keepdims=True)
        heads.append(jnp.dot(p.astype(BF16), mv_ref[:, sl], preferred_element_type=F32))
    yc_pre = jnp.concatenate(heads, axis=1).astype(BF16)
    ya = jnp.dot(ya_ref[...], wa_ref[...], preferred_element_type=F32)
    yb = jnp.dot(yb_ref[...], wb_ref[...], preferred_element_type=F32)
    yc = jnp.dot(yc_pre, wc_ref[...], preferred_element_type=F32)
    merged = (g0_ref[...].astype(F32) * ya + g1_ref[...].astype(F32) * yb
              + g2_ref[...].astype(F32) * yc)
    o = jnp.dot(merged.astype(BF16), wo_ref[...], preferred_element_type=F32)
    o_ref[...] = x_ref[...] + _rms(o, gp_ref[...])


def _merge(x2, ya2, yb2, proj2, mkv3, wa, wb, wc, wo, g_post, seq):
    n = x2.shape[0]
    tm = min(512, seq)
    per_batch = seq // tm
    n_mem = mkv3.shape[1]
    c_dim = C_HEADS * C_HEAD_DIM
    const = lambda t: (0, 0)
    return pl.pallas_call(
        _merge_kernel,
        name="merge",
        grid=(n // tm,),
        in_specs=[
            pl.BlockSpec((tm, D_MODEL), lambda t: (t, 0)),
            pl.BlockSpec((tm, 512), lambda t: (t, 0)),
            pl.BlockSpec((tm, 512), lambda t: (t, 0)),
            pl.BlockSpec((tm, 512), lambda t: (t, COL_CQ)),
            pl.BlockSpec((tm, D_MODEL), lambda t: (t, GATE_COL0 // D_MODEL)),
            pl.BlockSpec((tm, D_MODEL), lambda t: (t, GATE_COL0 // D_MODEL + 1)),
            pl.BlockSpec((tm, D_MODEL), lambda t: (t, GATE_COL0 // D_MODEL + 2)),
            pl.BlockSpec((None, n_mem, c_dim), lambda t: (t // per_batch, 0, 0)),
            pl.BlockSpec((None, n_mem, c_dim), lambda t: (t // per_batch, 0, 1)),
            pl.BlockSpec(wa.shape, const),
            pl.BlockSpec(wb.shape, const),
            pl.BlockSpec(wc.shape, const),
            pl.BlockSpec(wo.shape, const),
            pl.BlockSpec((1, D_MODEL), const),
        ],
        out_specs=pl.BlockSpec((tm, D_MODEL), lambda t: (t, 0)),
        out_shape=jax.ShapeDtypeStruct((n, D_MODEL), F32),
        compiler_params=pltpu.CompilerParams(
            dimension_semantics=("arbitrary",), vmem_limit_bytes=VMEM_LIMIT),
    )(x2, ya2, yb2, proj2, proj2, proj2, proj2, mkv3, mkv3, wa, wb, wc, wo, g_post)


def _ffn_kernel(x_ref, gpre_ref, wg_ref, wu_ref, wo_ref, gpost_ref, o_ref, h_ref, acc_ref):
    k = pl.program_id(1)

    @pl.when(k == 0)
    def _():
        h_ref[...] = _rms(x_ref[...], gpre_ref[...]).astype(BF16)
        acc_ref[...] = jnp.zeros(acc_ref.shape, F32)

    h = h_ref[...]
    g = jnp.dot(h, wg_ref[...], preferred_element_type=F32)
    u = jnp.dot(h, wu_ref[...], preferred_element_type=F32)
    act = (g * jax.nn.sigmoid(g) * u).astype(BF16)
    acc_ref[...] += jnp.dot(act, wo_ref[...], preferred_element_type=F32)

    @pl.when(k == pl.num_programs(1) - 1)
    def _():
        o_ref[...] = x_ref[...] + _rms(acc_ref[...], gpost_ref[...])


def _ffn(x2, g_pre, wg, wu, wo, g_post):
    n = x2.shape[0]
    d_ff = wg.shape[1]
    tm = min(512, n)
    tf = d_ff // 2
    return pl.pallas_call(
        _ffn_kernel,
        name="ffn",
        grid=(n // tm, d_ff // tf),
        in_specs=[
            pl.BlockSpec((tm, D_MODEL), lambda t, k: (t, 0)),
            pl.BlockSpec((1, D_MODEL), lambda t, k: (0, 0)),
            pl.BlockSpec((D_MODEL, tf), lambda t, k: (0, k)),
            pl.BlockSpec((D_MODEL, tf), lambda t, k: (0, k)),
            pl.BlockSpec((tf, D_MODEL), lambda t, k: (k, 0)),
            pl.BlockSpec((1, D_MODEL), lambda t, k: (0, 0)),
        ],
        out_specs=pl.BlockSpec((tm, D_MODEL), lambda t, k: (t, 0)),
        out_shape=jax.ShapeDtypeStruct((n, D_MODEL), F32),
        scratch_shapes=[pltpu.VMEM((tm, D_MODEL), BF16), pltpu.VMEM((tm, D_MODEL), F32)],
        compiler_params=pltpu.CompilerParams(
            dimension_semantics=("arbitrary", "arbitrary"), vmem_limit_bytes=VMEM_LIMIT),
    )(x2, g_pre, wg, wu, wo, g_post)


def _pack_w_in(w, b_gate):
    sizes = (512, 512, 512, IDX_HEADS * 64, 64, IDX_HEADS, 512, 512, 512, 512,
             N_BRANCH * D_MODEL)
    aq, ak, av, iq, ik, iw, bq, bk, bv, cq, gates = jnp.split(w, np.cumsum(sizes)[:-1], axis=1)
    scale = HEAD_DIM ** -0.5
    w_main = jnp.concatenate(
        [aq * scale, ak, av, iq * scale, bq * scale, bk, bv, cq, gates], axis=1).astype(BF16)
    z64 = jnp.zeros((D_MODEL, 64), F32)
    w_ik = jnp.concatenate([ik, z64, z64, ik], axis=1).astype(BF16)
    w_trans = jnp.concatenate(
        [av, iw, jnp.zeros((D_MODEL, N_TRANS - 512 - IDX_HEADS), F32)], axis=1).T.astype(BF16)
    b_main = jnp.concatenate([jnp.zeros((GATE_COL0,), F32), b_gate])[None, :]
    return w_main, w_ik, w_trans, b_main


def kernel(x, mem, rel_bias, g_mix_pre, w_in, b_gate, g_mem, w_mem_kv, w_up_a, w_up_b, w_up_c,
           w_out, g_mix_post, g_ffn_pre, w_ffn_in, w_ffn_out, g_ffn_post):
    bsz, seq, _ = x.shape
    n_mem = mem.shape[1]
    k_sel = min(TOPK_MAX, seq // 4)
    bias_tiles = _bias_tiles(rel_bias)
    x2 = x.reshape(bsz * seq, D_MODEL)
    for l in range(w_in.shape[0]):
        w_main, w_ik, w_trans, b_main = _pack_w_in(w_in[l], b_gate[l])
        proj2, ik2, avt, iwt = _project(x2, g_mix_pre[l][None, :], w_main, b_main, w_ik, w_trans)
        proj3 = proj2.reshape(bsz, seq, N_MAIN)
        mkv = _memkv(mem.reshape(bsz * n_mem, D_MODEL), g_mem[l][None, :],
                     w_mem_kv[l].astype(BF16))
        ya = _dsa(proj3, ik2.reshape(bsz, seq, N_IK), avt.reshape(bsz, seq // NK, 512, NK),
                  iwt, bias_tiles, k_sel)
        yb = _stick_breaking(proj3)
        x2 = _merge(x2, ya.reshape(bsz * seq, 512), yb.reshape(bsz * seq, 512), proj2,
                    mkv.reshape(bsz, n_mem, 2 * C_HEADS * C_HEAD_DIM),
                    w_up_a[l].astype(BF16), w_up_b[l].astype(BF16), w_up_c[l].astype(BF16),
                    w_out[l].astype(BF16), g_mix_post[l][None, :], seq)
        d_ff = w_ffn_out.shape[1]
        w_ffn = w_ffn_in[l].astype(BF16)
        x2 = _ffn(x2, g_ffn_pre[l][None, :], w_ffn[:, :d_ff], w_ffn[:, d_ff:],
                  w_ffn_out[l].astype(BF16), g_ffn_post[l][None, :])
    return x2.reshape(bsz, seq, D_MODEL)
```

```python
import functools

import numpy as np
import jax
import jax.numpy as jnp
from jax import lax
from jax.experimental import pallas as pl
from jax.experimental.pallas import tpu as pltpu

D_MODEL = 1024
CHUNK = 64
HEAD_DIM = 64
N_HEADS = 8
IDX_HEADS = 8
TOPK_MAX = 256
C_HEADS = 4
C_HEAD_DIM = 128
N_BRANCH = 3
REL_BUCKETS = 32
EPS = 1e-6

F32 = jnp.float32
BF16 = jnp.bfloat16
INT_MIN = -2 ** 31
NEG_BIG = -1e30

QB = 128
NK = 256
N_PAIR = N_HEADS // 2
ONES_ROWS = 16

COL_AQ, COL_AK, COL_AV, COL_IQ, COL_BQ, COL_BK, COL_BV, COL_CQ = range(8)
GATE_COL0 = 8 * 512
N_MAIN = GATE_COL0 + N_BRANCH * D_MODEL
N_IK = 256
N_TRANS = 512 + 16

VMEM_LIMIT = 56 * 1024 * 1024

_NT = (((1,), (1,)), ((), ()))


def _rms(x, g):
    return x * lax.rsqrt(jnp.mean(x * x, axis=-1, keepdims=True) + EPS) * g


def _proj_kernel(x_ref, g_ref, w_ref, b_ref, wik_ref, wt_ref, o_ref, ik_ref, avt_ref, iwt_ref,
                 h_ref, *, first_gate_tile):
    j = pl.program_id(1)

    @pl.when(j == 0)
    def _():
        hb = _rms(x_ref[...], g_ref[...]).astype(BF16)
        h_ref[...] = hb
        ik_ref[...] = jnp.dot(hb, wik_ref[...], preferred_element_type=F32).astype(BF16)
        tr = lax.dot_general(wt_ref[...], hb, _NT, preferred_element_type=F32)
        for c in range(avt_ref.shape[0]):
            avt_ref[c] = tr[:512, c * NK:(c + 1) * NK].astype(BF16)
        iwt_ref[...] = tr[512:512 + IDX_HEADS, :]

    acc = jnp.dot(h_ref[...], w_ref[...], preferred_element_type=F32)

    @pl.when(j < first_gate_tile)
    def _():
        o_ref[...] = acc.astype(BF16)

    @pl.when(j >= first_gate_tile)
    def _():
        o_ref[...] = jax.nn.sigmoid(acc + b_ref[...]).astype(BF16)


def _project(x2, g, w_main, b_main, w_ik, w_trans):
    n = x2.shape[0]
    tm = min(1024, n)
    tn = 1024
    grid = (n // tm, N_MAIN // tn)
    return pl.pallas_call(
        functools.partial(_proj_kernel, first_gate_tile=GATE_COL0 // tn),
        name="in_proj",
        grid=grid,
        in_specs=[
            pl.BlockSpec((tm, D_MODEL), lambda i, j: (i, 0)),
            pl.BlockSpec((1, D_MODEL), lambda i, j: (0, 0)),
            pl.BlockSpec((D_MODEL, tn), lambda i, j: (0, j)),
            pl.BlockSpec((1, tn), lambda i, j: (0, j)),
            pl.BlockSpec((D_MODEL, N_IK), lambda i, j: (0, 0)),
            pl.BlockSpec((N_TRANS, D_MODEL), lambda i, j: (0, 0)),
        ],
        out_specs=[
            pl.BlockSpec((tm, tn), lambda i, j: (i, j)),
            pl.BlockSpec((tm, N_IK), lambda i, j: (i, 0)),
            pl.BlockSpec((tm // NK, 512, NK), lambda i, j: (i, 0, 0)),
            pl.BlockSpec((IDX_HEADS, tm), lambda i, j: (0, i)),
        ],
        out_shape=[
            jax.ShapeDtypeStruct((n, N_MAIN), BF16),
            jax.ShapeDtypeStruct((n, N_IK), BF16),
            jax.ShapeDtypeStruct((n // NK, 512, NK), BF16),
            jax.ShapeDtypeStruct((IDX_HEADS, n), F32),
        ],
        scratch_shapes=[pltpu.VMEM((tm, D_MODEL), BF16)],
        compiler_params=pltpu.CompilerParams(
            dimension_semantics=("arbitrary", "arbitrary"), vmem_limit_bytes=VMEM_LIMIT),
    )(x2, g, w_main, b_main, w_ik, w_trans)


def _memkv_kernel(x_ref, g_ref, w_ref, o_ref):
    hb = _rms(x_ref[...], g_ref[...]).astype(BF16)
    o_ref[...] = jnp.dot(hb, w_ref[...], preferred_element_type=F32).astype(BF16)


def _memkv(mem2, g, w):
    n = mem2.shape[0]
    tm = min(512, n)
    return pl.pallas_call(
        _memkv_kernel,
        name="mem_kv",
        grid=(n // tm,),
        in_specs=[
            pl.BlockSpec((tm, D_MODEL), lambda i: (i, 0)),
            pl.BlockSpec((1, D_MODEL), lambda i: (0, 0)),
            pl.BlockSpec((D_MODEL, w.shape[1]), lambda i: (0, 0)),
        ],
        out_specs=pl.BlockSpec((tm, w.shape[1]), lambda i: (i, 0)),
        out_shape=jax.ShapeDtypeStruct((n, w.shape[1]), BF16),
        compiler_params=pltpu.CompilerParams(
            dimension_semantics=("arbitrary",), vmem_limit_bytes=VMEM_LIMIT),
    )(mem2, g, w)


BIAS_OFFSETS = (0, -QB, -NK, -QB - NK)
_LOG_BUCKET_STARTS = (12, 16, 23, 32, 46, 64, 91)
FAR_BUCKET = 15


def _bias_kernel(rb_ref, o_ref):
    key = lax.broadcasted_iota(jnp.int32, (NK, QB), 0)
    qry = lax.broadcasted_iota(jnp.int32, (NK, QB), 1)
    for c, off in enumerate(BIAS_OFFSETS):
        rel = key - qry + off
        n = jnp.abs(rel)
        large = jnp.full((NK, QB), 8, jnp.int32)
        for start in _LOG_BUCKET_STARTS:
            large = large + jnp.where(n >= start, 1, 0)
        bucket = jnp.where(rel > 0, REL_BUCKETS // 2, 0) + jnp.where(n < 8, n, large)
        for h in range(N_HEADS):
            val = jnp.full((NK, QB), rb_ref[0, h], F32)
            for b in range(1, REL_BUCKETS):
                val = jnp.where(bucket == b, rb_ref[b, h], val)
            o_ref[c, h // 2, :, (h % 2) * QB:(h % 2 + 1) * QB] = val - rb_ref[FAR_BUCKET, h]


def _bias_tiles(rel_bias):
    return pl.pallas_call(
        _bias_kernel,
        name="rel_bias_tiles",
        in_specs=[pl.BlockSpec(memory_space=pltpu.SMEM)],
        out_specs=pl.BlockSpec(memory_space=pltpu.VMEM),
        out_shape=jax.ShapeDtypeStruct((len(BIAS_OFFSETS), N_PAIR, NK, 2 * QB), F32),
    )(rel_bias)


def _split_heads_into(qm_ref, q):
    lane = lax.broadcasted_iota(jnp.int32, (QB, 128), 1)
    for p in range(N_PAIR):
        qp = q[:, p * 128:(p + 1) * 128].astype(F32)
        qm_ref[p, :QB, :] = jnp.where(lane < HEAD_DIM, qp, 0.0).astype(BF16)
        qm_ref[p, QB:, :] = jnp.where(lane >= HEAD_DIM, qp, 0.0).astype(BF16)


def _merge_pair(o_even, o_odd):
    lane = lax.broadcasted_iota(jnp.int32, (QB, 128), 1)
    return jnp.where(lane < HEAD_DIM, o_even, o_odd)


CNT_BLOCKS = 4
assert NK == 8 * 32


def _bit_transpose32(x):
    x = list(x)
    j, m = 16, 0x0000FFFF
    while j:
        k = 0
        while k < 32:
            t = (x[k] ^ lax.shift_right_logical(x[k + j], jnp.int32(j))) & m
            x[k] = x[k] ^ t
            x[k + j] = x[k + j] ^ (t << j)
            k = (k + j + 1) & ~j
        j >>= 1
        m ^= m << j
    return x


def _dsa_kernel(aq_ref, iq_ref, iwt_ref, ak_ref, avt_ref, ik_ref, bias_ref, o_ref,
                key_ref, plane_ref, alive_ref, iqs_ref, qm_ref, m_ref, acc_ref, s0_ref, cm0_ref, s1_ref, cm1_ref, *,
                k_sel):
    i = pl.program_id(1)
    diag = i // 2
    nkb = diag + 1
    krow = lax.broadcasted_iota(jnp.int32, (NK, QB), 0)
    qcol = lax.broadcasted_iota(jnp.int32, (NK, QB), 1)
    qchunk = (i * QB + qcol) // CHUNK

    iwt = iwt_ref[...] * (IDX_HEADS ** -0.5)
    iq = iq_ref[...]
    for p in range(N_PAIR):
        iqs_ref[p * QB:(p + 1) * QB, :] = iq[:, p * 128:(p + 1) * 128]

    def score_block(jb, carry):
        k0 = pl.multiple_of(jb * NK, NK)
        ik2 = ik_ref[pl.ds(k0, NK), :]
        acc = jnp.zeros((NK, QB), F32)
        for half in range(2):
            iqh = iqs_ref[half * 2 * QB:(half + 1) * 2 * QB, :]
            s_even = lax.dot_general(ik2[:, :128], iqh, _NT, preferred_element_type=F32)
            s_odd = lax.dot_general(ik2[:, 128:], iqh, _NT, preferred_element_type=F32)
            for pp in range(2):
                h = 2 * (2 * half + pp)
                sl = slice(pp * QB, (pp + 1) * QB)
                acc = acc + iwt[h:h + 1, :] * jnp.maximum(s_even[:, sl], 0.0)
                acc = acc + iwt[h + 1:h + 2, :] * jnp.maximum(s_odd[:, sl], 0.0)
        bits = lax.bitcast_convert_type(acc, jnp.int32)
        key = bits ^ ((bits >> 31) & 0x7FFFFFFF)
        admissible = ((k0 + krow) // CHUNK) <= qchunk
        key = jnp.where(admissible, key, INT_MIN)
        key_ref[jb] = key
        ukey = key ^ INT_MIN
        planes = _bit_transpose32([ukey[8 * r:8 * r + 8, :] for r in range(32)])
        for b in range(32):
            plane_ref[b, jb] = planes[31 - b]
        plane_ref[32, jb] = jnp.full((8, QB), -1, jnp.int32)
        alive_ref[jb] = jnp.full((8, QB), -1, jnp.int32)
        return carry

    lax.fori_loop(0, nkb, score_block, 0)

    n_groups = (nkb + CNT_BLOCKS - 1) // CNT_BLOCKS

    def pad_block(jb, carry):
        for b in range(33):
            plane_ref[b, jb] = jnp.zeros((8, QB), jnp.int32)
        alive_ref[jb] = jnp.zeros((8, QB), jnp.int32)
        return carry

    lax.fori_loop(nkb, n_groups * CNT_BLOCKS, pad_block, 0)

    def select_pass(it, state):
        took_prev, n_above, thr_u = state
        b = 31 - it
        take_prev = took_prev != 0

        def body(g, cnts):
            cnts = list(cnts)
            for u in range(CNT_BLOCKS):
                jb = g * CNT_BLOCKS + u
                alive = alive_ref[jb]
                with_prev = alive & plane_ref[b + 1, jb]
                alive = jnp.where(take_prev, with_prev, alive ^ with_prev)
                alive_ref[jb] = alive
                cnts[u] = cnts[u] + lax.population_count(alive & plane_ref[b, jb])
            return tuple(cnts)

        zeros = jnp.zeros((8, QB), jnp.int32)
        cnts = lax.fori_loop(0, n_groups, body, (zeros,) * CNT_BLOCKS)
        n_one = jnp.sum(sum(cnts[1:], cnts[0]), axis=0, keepdims=True)
        take = (n_above + n_one) >= k_sel
        n_above = jnp.where(take, n_above, n_above + n_one)
        thr_u = jnp.where(take, thr_u | jnp.left_shift(jnp.int32(1), b), thr_u)
        return take.astype(jnp.int32), n_above, thr_u

    row0 = jnp.zeros((1, QB), jnp.int32)
    _, _, thr_u = lax.fori_loop(0, 32, select_pass, (row0 + 1, row0, row0))
    thr = thr_u ^ INT_MIN
    thr = jnp.maximum(thr, INT_MIN + 1)

    _split_heads_into(qm_ref, aq_ref[...])
    m_ref[...] = jnp.full(m_ref.shape, NEG_BIG, F32)
    acc_ref[...] = jnp.zeros(acc_ref.shape, F32)
    ones = jnp.ones((ONES_ROWS, NK), BF16)

    slots = ((s0_ref, cm0_ref), (s1_ref, cm1_ref))

    def logits_stage(jb, slot, bias_idx):
        s_ref, cm_ref = slots[slot]
        k0 = pl.multiple_of(jb * NK, NK)
        mask1 = jnp.where(key_ref[jb] >= thr, 0.0, NEG_BIG)
        mask2 = jnp.concatenate([mask1, mask1], axis=1)
        for p in range(N_PAIR):
            kp = ak_ref[pl.ds(k0, NK), p * 128:(p + 1) * 128]
            s2 = lax.dot_general(kp, qm_ref[p], _NT, preferred_element_type=F32) + mask2
            if bias_idx is not None:
                s2 = s2 + bias_ref[bias_idx, p]
            s_ref[p] = s2
            cm_ref[p] = jnp.max(s2, axis=0, keepdims=True)

    def softmax_stage(jb, slot):
        s_ref, cm_ref = slots[slot]
        for p in range(N_PAIR):
            m_prev = m_ref[p]
            m_new = jnp.maximum(m_prev, cm_ref[p])
            alpha = jnp.exp(m_prev - m_new)
            pe = jnp.exp(s_ref[p] - m_new).astype(BF16)
            vt = jnp.concatenate([avt_ref[jb, p * 128:(p + 1) * 128, :], ones], axis=0)
            acc_ref[p] = alpha * acc_ref[p] + jnp.dot(vt, pe, preferred_element_type=F32)
            m_ref[p] = m_new

    odd = i % 2
    odd_diag = diag % 2
    logits_stage(diag, 0, odd)

    @pl.when(diag >= 1)
    def _():
        logits_stage(diag - 1, 1, 2 + odd)

    softmax_stage(diag, 0)
    n_steps = diag - 1

    def two_steps(u, carry):
        b = diag - 1 - 2 * u
        logits_stage(b - 1, 0, None)
        softmax_stage(b, 1)
        logits_stage(b - 2, 1, None)
        softmax_stage(b - 1, 0)
        return carry

    lax.fori_loop(0, jnp.maximum(n_steps, 0) // 2, two_steps, 0)

    @pl.when(jnp.logical_and(n_steps >= 1, n_steps % 2 == 1))
    def _():
        logits_stage(0, 0, None)
        softmax_stage(1, 1)

    @pl.when(jnp.logical_and(diag >= 1, odd_diag == 1))
    def _():
        softmax_stage(0, 1)

    @pl.when(jnp.logical_and(diag >= 1, odd_diag == 0))
    def _():
        softmax_stage(0, 0)

    drow = lax.broadcasted_iota(jnp.int32, (128, QB), 0)
    for p in range(N_PAIR):
        a = acc_ref[p]
        o_even = a[0:128, :QB] / a[128:129, :QB]
        o_odd = a[0:128, QB:] / a[128:129, QB:]
        o_t = jnp.where(drow < HEAD_DIM, o_even, o_odd)
        o_ref[:, p * 128:(p + 1) * 128] = o_t.T.astype(BF16)


def _dsa(proj3, ik3, avt4, iwt, bias_tiles, k_sel):
    bsz, seq, _ = proj3.shape
    nq = seq // QB
    assert seq % (NK * CNT_BLOCKS) == 0
    resident = dict(pipeline_mode=pl.Buffered(1))
    return pl.pallas_call(
        functools.partial(_dsa_kernel, k_sel=k_sel),
        name="dsa",
        grid=(bsz, nq),
        in_specs=[
            pl.BlockSpec((None, QB, 512), lambda b, i: (b, i, COL_AQ)),
            pl.BlockSpec((None, QB, 512), lambda b, i: (b, i, COL_IQ)),
            pl.BlockSpec((IDX_HEADS, QB), lambda b, i: (0, b * nq + i)),
            pl.BlockSpec((None, seq, 512), lambda b, i: (b, 0, COL_AK), **resident),
            pl.BlockSpec((None, seq // NK, 512, NK), lambda b, i: (b, 0, 0, 0), **resident),
            pl.BlockSpec((None, seq, N_IK), lambda b, i: (b, 0, 0), **resident),
            pl.BlockSpec(bias_tiles.shape, lambda b, i: (0, 0, 0, 0), **resident),
        ],
        out_specs=pl.BlockSpec((None, QB, 512), lambda b, i: (b, i, 0)),
        out_shape=jax.ShapeDtypeStruct((bsz, seq, 512), BF16),
        scratch_shapes=[
            pltpu.VMEM((seq // NK, NK, QB), jnp.int32),
            pltpu.VMEM((33, seq // NK, 8, QB), jnp.int32),
            pltpu.VMEM((seq // NK, 8, QB), jnp.int32),
            pltpu.VMEM((N_PAIR * QB, 128), BF16),
            pltpu.VMEM((N_PAIR, 2 * QB, 128), BF16),
            pltpu.VMEM((N_PAIR, 1, 2 * QB), F32),
            pltpu.VMEM((N_PAIR, 128 + ONES_ROWS, 2 * QB), F32),
            pltpu.VMEM((N_PAIR, NK, 2 * QB), F32),
            pltpu.VMEM((N_PAIR, 1, 2 * QB), F32),
            pltpu.VMEM((N_PAIR, NK, 2 * QB), F32),
            pltpu.VMEM((N_PAIR, 1, 2 * QB), F32),
        ],
        compiler_params=pltpu.CompilerParams(
            dimension_semantics=("arbitrary", "arbitrary"), vmem_limit_bytes=VMEM_LIMIT),
    )(proj3, proj3, iwt, proj3, avt4, ik3, bias_tiles)


SB_DEAD_MASS = 104.0


def _sb_kernel(q_ref, k_ref, v_ref, o_ref, qm_ref, u_ref, carry_ref, acc_ref):
    i = pl.program_id(1)
    diag = i // 2
    row = lax.broadcasted_iota(jnp.int32, (QB, NK), 0)
    col = lax.broadcasted_iota(jnp.int32, (QB, NK), 1)
    qpos = i * QB + row

    _split_heads_into(qm_ref, q_ref[...])
    kr = lax.broadcasted_iota(jnp.int32, (NK, NK), 0)
    kc = lax.broadcasted_iota(jnp.int32, (NK, NK), 1)
    u_ref[...] = jnp.where(kr > kc, 1.0, 0.0).astype(BF16)
    carry_ref[...] = jnp.zeros(carry_ref.shape, F32)
    acc_ref[...] = jnp.zeros(acc_ref.shape, F32)

    def block(jb, on_diagonal):
        k0 = pl.multiple_of(jb * NK, NK)
        if on_diagonal:
            causal = (k0 + col) < qpos
        for p in range(N_PAIR):
            kp = k_ref[pl.ds(k0, NK), p * 128:(p + 1) * 128]
            vp = v_ref[pl.ds(k0, NK), p * 128:(p + 1) * 128]
            z2 = lax.dot_general(qm_ref[p], kp, _NT, preferred_element_type=F32)
            for e in range(2):
                h = 2 * p + e
                z = z2[e * QB:(e + 1) * QB]
                sp = jnp.maximum(z, 0.0) + jnp.log(1.0 + jnp.exp(-jnp.abs(z)))
                spm = jnp.where(causal, sp, 0.0) if on_diagonal else sp
                hi = spm.astype(BF16)
                lo = (spm - hi.astype(F32)).astype(BF16)
                u = u_ref[...]
                later = (jnp.dot(hi, u, preferred_element_type=F32)
                         + jnp.dot(lo, u, preferred_element_type=F32))
                carry = carry_ref[h]
                a = jnp.exp(z - sp - later - carry)
                if on_diagonal:
                    a = jnp.where(causal, a, 0.0)
                acc_ref[h] += jnp.dot(a.astype(BF16), vp, preferred_element_type=F32)
                carry_ref[h] = carry + jnp.sum(spm, axis=1, keepdims=True)

    block(diag, True)

    def alive():
        return (jnp.min(carry_ref[...]) <= SB_DEAD_MASS).astype(jnp.int32)

    def cond(state):
        jb, go = state
        return jnp.logical_and(jb >= 0, go > 0)

    def body(state):
        jb, _ = state
        block(jb, False)
        return jb - 1, alive()

    lax.while_loop(cond, body, (diag - 1, alive()))

    for p in range(N_PAIR):
        o_ref[:, p * 128:(p + 1) * 128] = _merge_pair(
            acc_ref[2 * p], acc_ref[2 * p + 1]).astype(BF16)


def _stick_breaking(proj3):
    bsz, seq, _ = proj3.shape
    resident = dict(pipeline_mode=pl.Buffered(1))
    return pl.pallas_call(
        _sb_kernel,
        name="stick_breaking",
        grid=(bsz, seq // QB),
        in_specs=[
            pl.BlockSpec((None, QB, 512), lambda b, i: (b, i, COL_BQ)),
            pl.BlockSpec((None, seq, 512), lambda b, i: (b, 0, COL_BK), **resident),
            pl.BlockSpec((None, seq, 512), lambda b, i: (b, 0, COL_BV), **resident),
        ],
        out_specs=pl.BlockSpec((None, QB, 512), lambda b, i: (b, i, 0)),
        out_shape=jax.ShapeDtypeStruct((bsz, seq, 512), BF16),
        scratch_shapes=[
            pltpu.VMEM((N_PAIR, 2 * QB, 128), BF16),
            pltpu.VMEM((NK, NK), BF16),
            pltpu.VMEM((N_HEADS, QB, 1), F32),
            pltpu.VMEM((N_HEADS, QB, 128), F32),
        ],
        compiler_params=pltpu.CompilerParams(
            dimension_semantics=("arbitrary", "arbitrary"), vmem_limit_bytes=VMEM_LIMIT),
    )(proj3, proj3, proj3)


def _merge_kernel(x_ref, ya_ref, yb_ref, cq_ref, g0_ref, g1_ref, g2_ref, mk_ref, mv_ref,
                  wa_ref, wb_ref, wc_ref, wo_ref, gp_ref, o_ref):
    cq = cq_ref[...]
    heads = []
    for h in range(C_HEADS):
        sl = slice(h * C_HEAD_DIM, (h + 1) * C_HEAD_DIM)
        s = lax.dot_general(cq[:, sl], mk_ref[:, sl], _NT,
                            preferred_element_type=F32) * (C_HEAD_DIM ** -0.5)
        e = jnp.exp(s - jnp.max(s, axis=1, keepdims=True))
        p = e / jnp.sum(e, axis=1, keepdims=True)
        heads.append(jnp.dot(p.astype(BF16), mv_ref[:, sl], preferred_element_type=F32))
    yc_pre = jnp.concatenate(heads, axis=1).astype(BF16)
    ya = jnp.dot(ya_ref[...], wa_ref[...], preferred_element_type=F32)
    yb = jnp.dot(yb_ref[...], wb_ref[...], preferred_element_type=F32)
    yc = jnp.dot(yc_pre, wc_ref[...], preferred_element_type=F32)
    merged = (g0_ref[...].astype(F32) * ya + g1_ref[...].astype(F32) * yb
              + g2_ref[...].astype(F32) * yc)
    o = jnp.dot(merged.astype(BF16), wo_ref[...], preferred_element_type=F32)
    o_ref[...] = x_ref[...] + _rms(o, gp_ref[...])


def _merge(x2, ya2, yb2, proj2, mkv3, wa, wb, wc, wo, g_post, seq):
    n = x2.shape[0]
    tm = min(512, seq)
    per_batch = seq // tm
    n_mem = mkv3.shape[1]
    c_dim = C_HEADS * C_HEAD_DIM
    const = lambda t: (0, 0)
    return pl.pallas_call(
        _merge_kernel,
        name="merge",
        grid=(n // tm,),
        in_specs=[
            pl.BlockSpec((tm, D_MODEL), lambda t: (t, 0)),
            pl.BlockSpec((tm, 512), lambda t: (t, 0)),
            pl.BlockSpec((tm, 512), lambda t: (t, 0)),
            pl.BlockSpec((tm, 512), lambda t: (t, COL_CQ)),
            pl.BlockSpec((tm, D_MODEL), lambda t: (t, GATE_COL0 // D_MODEL)),
            pl.BlockSpec((tm, D_MODEL), lambda t: (t, GATE_COL0 // D_MODEL + 1)),
            pl.BlockSpec((tm, D_MODEL), lambda t: (t, GATE_COL0 // D_MODEL + 2)),
            pl.BlockSpec((None, n_mem, c_dim), lambda t: (t // per_batch, 0, 0)),
            pl.BlockSpec((None, n_mem, c_dim), lambda t: (t // per_batch, 0, 1)),
            pl.BlockSpec(wa.shape, const),
            pl.BlockSpec(wb.shape, const),
            pl.BlockSpec(wc.shape, const),
            pl.BlockSpec(wo.shape, const),
            pl.BlockSpec((1, D_MODEL), const),
        ],
        out_specs=pl.BlockSpec((tm, D_MODEL), lambda t: (t, 0)),
        out_shape=jax.ShapeDtypeStruct((n, D_MODEL), F32),
        compiler_params=pltpu.CompilerParams(
            dimension_semantics=("arbitrary",), vmem_limit_bytes=VMEM_LIMIT),
    )(x2, ya2, yb2, proj2, proj2, proj2, proj2, mkv3, mkv3, wa, wb, wc, wo, g_post)


def _ffn_kernel(x_ref, gpre_ref, wg_ref, wu_ref, wo_ref, gpost_ref, o_ref, h_ref, acc_ref):
    k = pl.program_id(1)

    @pl.when(k == 0)
    def _():
        h_ref[...] = _rms(x_ref[...], gpre_ref[...]).astype(BF16)
        acc_ref[...] = jnp.zeros(acc_ref.shape, F32)

    h = h_ref[...]
    g = jnp.dot(h, wg_ref[...], preferred_element_type=F32)
    u = jnp.dot(h, wu_ref[...], preferred_element_type=F32)
    act = (g * jax.nn.sigmoid(g) * u).astype(BF16)
    acc_ref[...] += jnp.dot(act, wo_ref[...], preferred_element_type=F32)

    @pl.when(k == pl.num_programs(1) - 1)
    def _():
        o_ref[...] = x_ref[...] + _rms(acc_ref[...], gpost_ref[...])


def _ffn(x2, g_pre, wg, wu, wo, g_post):
    n = x2.shape[0]
    d_ff = wg.shape[1]
    tm = min(512, n)
    tf = d_ff // 2
    return pl.pallas_call(
        _ffn_kernel,
        name="ffn",
        grid=(n // tm, d_ff // tf),
        in_specs=[
            pl.BlockSpec((tm, D_MODEL), lambda t, k: (t, 0)),
            pl.BlockSpec((1, D_MODEL), lambda t, k: (0, 0)),
            pl.BlockSpec((D_MODEL, tf), lambda t, k: (0, k)),
            pl.BlockSpec((D_MODEL, tf), lambda t, k: (0, k)),
            pl.BlockSpec((tf, D_MODEL), lambda t, k: (k, 0)),
            pl.BlockSpec((1, D_MODEL), lambda t, k: (0, 0)),
        ],
        out_specs=pl.BlockSpec((tm, D_MODEL), lambda t, k: (t, 0)),
        out_shape=jax.ShapeDtypeStruct((n, D_MODEL), F32),
        scratch_shapes=[pltpu.VMEM((tm, D_MODEL), BF16), pltpu.VMEM((tm, D_MODEL), F32)],
        compiler_params=pltpu.CompilerParams(
            dimension_semantics=("arbitrary", "arbitrary"), vmem_limit_bytes=VMEM_LIMIT),
    )(x2, g_pre, wg, wu, wo, g_post)


def _pack_w_in(w, b_gate):
    sizes = (512, 512, 512, IDX_HEADS * 64, 64, IDX_HEADS, 512, 512, 512, 512,
             N_BRANCH * D_MODEL)
    aq, ak, av, iq, ik, iw, bq, bk, bv, cq, gates = jnp.split(w, np.cumsum(sizes)[:-1], axis=1)
    scale = HEAD_DIM ** -0.5
    w_main = jnp.concatenate(
        [aq * scale, ak, av, iq * scale, bq * scale, bk, bv, cq, gates], axis=1).astype(BF16)
    z64 = jnp.zeros((D_MODEL, 64), F32)
    w_ik = jnp.concatenate([ik, z64, z64, ik], axis=1).astype(BF16)
    w_trans = jnp.concatenate(
        [av, iw, jnp.zeros((D_MODEL, N_TRANS - 512 - IDX_HEADS), F32)], axis=1).T.astype(BF16)
    b_main = jnp.concatenate([jnp.zeros((GATE_COL0,), F32), b_gate])[None, :]
    return w_main, w_ik, w_trans, b_main


def kernel(x, mem, rel_bias, g_mix_pre, w_in, b_gate, g_mem, w_mem_kv, w_up_a, w_up_b, w_up_c,
           w_out, g_mix_post, g_ffn_pre, w_ffn_in, w_ffn_out, g_ffn_post):
    bsz, seq, _ = x.shape
    n_mem = mem.shape[1]
    k_sel = min(TOPK_MAX, seq // 4)
    bias_tiles = _bias_tiles(rel_bias)
    x2 = x.reshape(bsz * seq, D_MODEL)
    for l in range(w_in.shape[0]):
        w_main, w_ik, w_trans, b_main = _pack_w_in(w_in[l], b_gate[l])
        proj2, ik2, avt, iwt = _project(x2, g_mix_pre[l][None, :], w_main, b_main, w_ik, w_trans)
        proj3 = proj2.reshape(bsz, seq, N_MAIN)
        mkv = _memkv(mem.reshape(bsz * n_mem, D_MODEL), g_mem[l][None, :],
                     w_mem_kv[l].astype(BF16))
        ya = _dsa(proj3, ik2.reshape(bsz, seq, N_IK), avt.reshape(bsz, seq // NK, 512, NK),
                  iwt, bias_tiles, k_sel)
        yb = _stick_breaking(proj3)
        x2 = _merge(x2, ya.reshape(bsz * seq, 512), yb.reshape(bsz * seq, 512), proj2,
                    mkv.reshape(bsz, n_mem, 2 * C_HEADS * C_HEAD_DIM),
                    w_up_a[l].astype(BF16), w_up_b[l].astype(BF16), w_up_c[l].astype(BF16),
                    w_out[l].astype(BF16), g_mix_post[l][None, :], seq)
        d_ff = w_ffn_out.shape[1]
        w_ffn = w_ffn_in[l].astype(BF16)
        x2 = _ffn(x2, g_ffn_pre[l][None, :], w_ffn[:, :d_ff], w_ffn[:, d_ff:],
                  w_ffn_out[l].astype(BF16), g_ffn_post[l][None, :])
    return x2.reshape(bsz, seq, D_MODEL)
```

```python
import functools

import numpy as np
import jax
import jax.numpy as jnp
from jax import lax
from jax.experimental import pallas as pl
from jax.experimental.pallas import tpu as pltpu

D_MODEL = 1024
CHUNK = 64
HEAD_DIM = 64
N_HEADS = 8
IDX_HEADS = 8
TOPK_MAX = 256
C_HEADS = 4
C_HEAD_DIM = 128
N_BRANCH = 3
REL_BUCKETS = 32
EPS = 1e-6

F32 = jnp.float32
BF16 = jnp.bfloat16
INT_MIN = -2 ** 31
NEG_BIG = -1e30

QB = 128
NK = 256
N_PAIR = N_HEADS // 2
ONES_ROWS = 16

COL_AQ, COL_AK, COL_AV, COL_IQ, COL_BQ, COL_BK, COL_BV, COL_CQ = range(8)
GATE_COL0 = 8 * 512
N_MAIN = GATE_COL0 + N_BRANCH * D_MODEL
N_IK = 256
N_TRANS = 512 + 16

VMEM_LIMIT = 56 * 1024 * 1024

_NT = (((1,), (1,)), ((), ()))


def _rms(x, g):
    return x * lax.rsqrt(jnp.mean(x * x, axis=-1, keepdims=True) + EPS) * g


def _proj_kernel(x_ref, g_ref, w_ref, b_ref, wik_ref, wt_ref, o_ref, ik_ref, avt_ref, iwt_ref,
                 h_ref, *, first_gate_tile):
    j = pl.program_id(1)

    @pl.when(j == 0)
    def _():
        hb = _rms(x_ref[...], g_ref[...]).astype(BF16)
        h_ref[...] = hb
        ik_ref[...] = jnp.dot(hb, wik_ref[...], preferred_element_type=F32).astype(BF16)
        tr = lax.dot_general(wt_ref[...], hb, _NT, preferred_element_type=F32)
        for c in range(avt_ref.shape[0]):
            avt_ref[c] = tr[:512, c * NK:(c + 1) * NK].astype(BF16)
        iwt_ref[...] = tr[512:512 + IDX_HEADS, :]

    acc = jnp.dot(h_ref[...], w_ref[...], preferred_element_type=F32)

    @pl.when(j < first_gate_tile)
    def _():
        o_ref[...] = acc.astype(BF16)

    @pl.when(j >= first_gate_tile)
    def _():
        o_ref[...] = jax.nn.sigmoid(acc + b_ref[...]).astype(BF16)


def _project(x2, g, w_main, b_main, w_ik, w_trans):
    n = x2.shape[0]
    tm = min(1024, n)
    tn = 1024
    grid = (n // tm, N_MAIN // tn)
    return pl.pallas_call(
        functools.partial(_proj_kernel, first_gate_tile=GATE_COL0 // tn),
        name="in_proj",
        grid=grid,
        in_specs=[
            pl.BlockSpec((tm, D_MODEL), lambda i, j: (i, 0)),
            pl.BlockSpec((1, D_MODEL), lambda i, j: (0, 0)),
            pl.BlockSpec((D_MODEL, tn), lambda i, j: (0, j)),
            pl.BlockSpec((1, tn), lambda i, j: (0, j)),
            pl.BlockSpec((D_MODEL, N_IK), lambda i, j: (0, 0)),
            pl.BlockSpec((N_TRANS, D_MODEL), lambda i, j: (0, 0)),
        ],
        out_specs=[
            pl.BlockSpec((tm, tn), lambda i, j: (i, j)),
            pl.BlockSpec((tm, N_IK), lambda i, j: (i, 0)),
            pl.BlockSpec((tm // NK, 512, NK), lambda i, j: (i, 0, 0)),
            pl.BlockSpec((IDX_HEADS, tm), lambda i, j: (0, i)),
        ],
        out_shape=[
            jax.ShapeDtypeStruct((n, N_MAIN), BF16),
            jax.ShapeDtypeStruct((n, N_IK), BF16),
            jax.ShapeDtypeStruct((n // NK, 512, NK), BF16),
            jax.ShapeDtypeStruct((IDX_HEADS, n), F32),
        ],
        scratch_shapes=[pltpu.VMEM((tm, D_MODEL), BF16)],
        compiler_params=pltpu.CompilerParams(
            dimension_semantics=("arbitrary", "arbitrary"), vmem_limit_bytes=VMEM_LIMIT),
    )(x2, g, w_main, b_main, w_ik, w_trans)


def _memkv_kernel(x_ref, g_ref, w_ref, o_ref):
    hb = _rms(x_ref[...], g_ref[...]).astype(BF16)
    o_ref[...] = jnp.dot(hb, w_ref[...], preferred_element_type=F32).astype(BF16)


def _memkv(mem2, g, w):
    n = mem2.shape[0]
    tm = min(512, n)
    return pl.pallas_call(
        _memkv_kernel,
        name="mem_kv",
        grid=(n // tm,),
        in_specs=[
            pl.BlockSpec((tm, D_MODEL), lambda i: (i, 0)),
            pl.BlockSpec((1, D_MODEL), lambda i: (0, 0)),
            pl.BlockSpec((D_MODEL, w.shape[1]), lambda i: (0, 0)),
        ],
        out_specs=pl.BlockSpec((tm, w.shape[1]), lambda i: (i, 0)),
        out_shape=jax.ShapeDtypeStruct((n, w.shape[1]), BF16),
        compiler_params=pltpu.CompilerParams(
            dimension_semantics=("arbitrary",), vmem_limit_bytes=VMEM_LIMIT),
    )(mem2, g, w)


BIAS_OFFSETS = (0, -QB, -NK, -QB - NK)
_LOG_BUCKET_STARTS = (12, 16, 23, 32, 46, 64, 91)
FAR_BUCKET = 15


def _bias_kernel(rb_ref, o_ref):
    key = lax.broadcasted_iota(jnp.int32, (NK, QB), 0)
    qry = lax.broadcasted_iota(jnp.int32, (NK, QB), 1)
    for c, off in enumerate(BIAS_OFFSETS):
        rel = key - qry + off
        n = jnp.abs(rel)
        large = jnp.full((NK, QB), 8, jnp.int32)
        for start in _LOG_BUCKET_STARTS:
            large = large + jnp.where(n >= start, 1, 0)
        bucket = jnp.where(rel > 0, REL_BUCKETS // 2, 0) + jnp.where(n < 8, n, large)
        for h in range(N_HEADS):
            val = jnp.full((NK, QB), rb_ref[0, h], F32)
            for b in range(1, REL_BUCKETS):
                val = jnp.where(bucket == b, rb_ref[b, h], val)
            o_ref[c, h // 2, :, (h % 2) * QB:(h % 2 + 1) * QB] = val - rb_ref[FAR_BUCKET, h]


def _bias_tiles(rel_bias):
    return pl.pallas_call(
        _bias_kernel,
        name="rel_bias_tiles",
        in_specs=[pl.BlockSpec(memory_space=pltpu.SMEM)],
        out_specs=pl.BlockSpec(memory_space=pltpu.VMEM),
        out_shape=jax.ShapeDtypeStruct((len(BIAS_OFFSETS), N_PAIR, NK, 2 * QB), F32),
    )(rel_bias)


def _split_heads_into(qm_ref, q):
    lane = lax.broadcasted_iota(jnp.int32, (QB, 128), 1)
    for p in range(N_PAIR):
        qp = q[:, p * 128:(p + 1) * 128].astype(F32)
        qm_ref[p, :QB, :] = jnp.where(lane < HEAD_DIM, qp, 0.0).astype(BF16)
        qm_ref[p, QB:, :] = jnp.where(lane >= HEAD_DIM, qp, 0.0).astype(BF16)


def _two_stage_pipeline(n, first, second):
    first(0, 0)

    def two(u, carry):
        j = 2 * u
        first(j + 1, 1)
        second(j, 0, False)
        first(j + 2, 0)
        second(j + 1, 1, False)
        return carry

    lax.fori_loop(0, (n - 1) // 2, two, 0)

    @pl.when(n % 2 == 0)
    def _():
        first(n - 1, 1)
        second(n - 2, 0, False)
        second(n - 1, 1, True)

    @pl.when(n % 2 == 1)
    def _():
        second(n - 1, 0, True)


def _merge_pair(o_even, o_odd):
    lane = lax.broadcasted_iota(jnp.int32, (QB, 128), 1)
    return jnp.where(lane < HEAD_DIM, o_even, o_odd)


CNT_BLOCKS = 4
KEY_ROWS = 64
assert NK == 8 * 32


def _bit_transpose32(load_row, tmp_ref, store_row):
    def swap(a, b, j, m):
        t = (a ^ lax.shift_right_logical(b, jnp.int32(j))) & m
        return a ^ t, b ^ (t << j)

    lower = []
    for k in range(16):
        a, b = swap(load_row(k), load_row(k + 16), 16, 0x0000FFFF)
        lower.append(a)
        tmp_ref[k] = b
    for base in (0, 16):
        x = lower if base == 0 else [tmp_ref[k] for k in range(16)]
        j, m = 8, 0x00FF00FF
        while j:
            k = 0
            while k < 16:
                x[k], x[k + j] = swap(x[k], x[k + j], j, m)
                k = (k + j + 1) & ~j
            j >>= 1
            m ^= m << j
        for i in range(16):
            store_row(base + i, x[i])


def _dsa_kernel(aq_ref, iq_ref, iwt_ref, ak_ref, avt_ref, ik_ref, bias_ref, o_ref,
                key_ref, plane_ref, alive_ref, iqs_ref, qm_ref, m_ref, acc_ref,
                s0_ref, cm0_ref, s1_ref, cm1_ref, raw0_ref, raw1_ref, tmp_ref, *,
                k_sel):
    i = pl.program_id(1)
    diag = i // 2
    nkb = diag + 1
    krow = lax.broadcasted_iota(jnp.int32, (KEY_ROWS, QB), 0)
    qcol = lax.broadcasted_iota(jnp.int32, (KEY_ROWS, QB), 1)
    qchunk = (i * QB + qcol) // CHUNK

    iwt = iwt_ref[...] * (IDX_HEADS ** -0.5)
    iq = iq_ref[...]
    for p in range(N_PAIR):
        iqs_ref[p * QB:(p + 1) * QB, :] = iq[:, p * 128:(p + 1) * 128]

    def dots_stage(jb, slot):
        raw_ref = (raw0_ref, raw1_ref)[slot]
        k0 = pl.multiple_of(jb * NK, NK)
        ik2 = ik_ref[pl.ds(k0, NK), :]
        for half in range(2):
            iqh = iqs_ref[half * 2 * QB:(half + 1) * 2 * QB, :]
            raw_ref[2 * half] = lax.dot_general(ik2[:, :128], iqh, _NT,
                                                preferred_element_type=F32)
            raw_ref[2 * half + 1] = lax.dot_general(ik2[:, 128:], iqh, _NT,
                                                    preferred_element_type=F32)

    def keys_stage(jb, slot, last):
        raw_ref = (raw0_ref, raw1_ref)[slot]
        k0 = pl.multiple_of(jb * NK, NK)
        for c in range(NK // KEY_ROWS):
            rows = slice(c * KEY_ROWS, (c + 1) * KEY_ROWS)
            acc = jnp.zeros((KEY_ROWS, QB), F32)
            for half in range(2):
                for pp in range(2):
                    h = 2 * (2 * half + pp)
                    sl = slice(pp * QB, (pp + 1) * QB)
                    acc = acc + iwt[h:h + 1, :] * jnp.maximum(raw_ref[2 * half, rows, sl], 0.0)
                    acc = acc + iwt[h + 1:h + 2, :] * jnp.maximum(
                        raw_ref[2 * half + 1, rows, sl], 0.0)
            bits = lax.bitcast_convert_type(acc, jnp.int32)
            key = bits ^ ((bits >> 31) & 0x7FFFFFFF)
            if last:
                admissible = ((k0 + c * KEY_ROWS + krow) // CHUNK) <= qchunk
                key = jnp.where(admissible, key, INT_MIN)
            key_ref[jb, rows, :] = key
        def load_row(r):
            return key_ref[jb, 8 * r:8 * r + 8, :] ^ INT_MIN

        def store_plane(i, v):
            plane_ref[31 - i, jb] = v

        _bit_transpose32(load_row, tmp_ref, store_plane)
        plane_ref[32, jb] = jnp.full((8, QB), -1, jnp.int32)
        alive_ref[jb] = jnp.full((8, QB), -1, jnp.int32)

    _two_stage_pipeline(nkb, dots_stage, keys_stage)

    n_groups = (nkb + CNT_BLOCKS - 1) // CNT_BLOCKS

    def pad_block(jb, carry):
        for b in range(33):
            plane_ref[b, jb] = jnp.zeros((8, QB), jnp.int32)
        alive_ref[jb] = jnp.zeros((8, QB), jnp.int32)
        return carry

    lax.fori_loop(nkb, n_groups * CNT_BLOCKS, pad_block, 0)

    def select_pass(it, state):
        took_prev, n_above, thr_u = state
        b = 31 - it
        take_prev = took_prev != 0

        def body(g, cnts):
            cnts = list(cnts)
            for u in range(CNT_BLOCKS):
                jb = g * CNT_BLOCKS + u
                alive = alive_ref[jb]
                with_prev = alive & plane_ref[b + 1, jb]
                alive = jnp.where(take_prev, with_prev, alive ^ with_prev)
                alive_ref[jb] = alive
                cnts[u] = cnts[u] + lax.population_count(alive & plane_ref[b, jb])
            return tuple(cnts)

        zeros = jnp.zeros((8, QB), jnp.int32)
        cnts = lax.fori_loop(0, n_groups, body, (zeros,) * CNT_BLOCKS)
        n_one = jnp.sum(sum(cnts[1:], cnts[0]), axis=0, keepdims=True)
        take = (n_above + n_one) >= k_sel
        n_above = jnp.where(take, n_above, n_above + n_one)
        thr_u = jnp.where(take, thr_u | jnp.left_shift(jnp.int32(1), b), thr_u)
        return take.astype(jnp.int32), n_above, thr_u

    row0 = jnp.zeros((1, QB), jnp.int32)
    _, _, thr_u = lax.fori_loop(0, 32, select_pass, (row0 + 1, row0, row0))
    thr = thr_u ^ INT_MIN
    thr = jnp.maximum(thr, INT_MIN + 1)

    _split_heads_into(qm_ref, aq_ref[...])
    m_ref[...] = jnp.full(m_ref.shape, NEG_BIG, F32)
    acc_ref[...] = jnp.zeros(acc_ref.shape, F32)
    ones = jnp.ones((ONES_ROWS, NK), BF16)

    slots = ((s0_ref, cm0_ref), (s1_ref, cm1_ref))

    def logits_stage(jb, slot, bias_idx):
        s_ref, cm_ref = slots[slot]
        k0 = pl.multiple_of(jb * NK, NK)
        mask1 = jnp.where(key_ref[jb] >= thr, 0.0, NEG_BIG)
        mask2 = jnp.concatenate([mask1, mask1], axis=1)
        for p in range(N_PAIR):
            kp = ak_ref[pl.ds(k0, NK), p * 128:(p + 1) * 128]
            s2 = lax.dot_general(kp, qm_ref[p], _NT, preferred_element_type=F32) + mask2
            if bias_idx is not None:
                s2 = s2 + bias_ref[bias_idx, p]
            s_ref[p] = s2
            cm_ref[p] = jnp.max(s2, axis=0, keepdims=True)

    def softmax_stage(jb, slot):
        s_ref, cm_ref = slots[slot]
        for p in range(N_PAIR):
            m_prev = m_ref[p]
            m_new = jnp.maximum(m_prev, cm_ref[p])
            alpha = jnp.exp(m_prev - m_new)
            pe = jnp.exp(s_ref[p] - m_new).astype(BF16)
            vt = jnp.concatenate([avt_ref[jb, p * 128:(p + 1) * 128, :], ones], axis=0)
            acc_ref[p] = alpha * acc_ref[p] + jnp.dot(vt, pe, preferred_element_type=F32)
            m_ref[p] = m_new

    odd = i % 2
    odd_diag = diag % 2
    logits_stage(diag, 0, odd)

    @pl.when(diag >= 1)
    def _():
        logits_stage(diag - 1, 1, 2 + odd)

    softmax_stage(diag, 0)
    n_steps = diag - 1

    def two_steps(u, carry):
        b = diag - 1 - 2 * u
        logits_stage(b - 1, 0, None)
        softmax_stage(b, 1)
        logits_stage(b - 2, 1, None)
        softmax_stage(b - 1, 0)
        return carry

    lax.fori_loop(0, jnp.maximum(n_steps, 0) // 2, two_steps, 0)

    @pl.when(jnp.logical_and(n_steps >= 1, n_steps % 2 == 1))
    def _():
        logits_stage(0, 0, None)
        softmax_stage(1, 1)

    @pl.when(jnp.logical_and(diag >= 1, odd_diag == 1))
    def _():
        softmax_stage(0, 1)

    @pl.when(jnp.logical_and(diag >= 1, odd_diag == 0))
    def _():
        softmax_stage(0, 0)

    drow = lax.broadcasted_iota(jnp.int32, (128, QB), 0)
    for p in range(N_PAIR):
        a = acc_ref[p]
        o_even = a[0:128, :QB] / a[128:129, :QB]
        o_odd = a[0:128, QB:] / a[128:129, QB:]
        o_t = jnp.where(drow < HEAD_DIM, o_even, o_odd)
        o_ref[:, p * 128:(p + 1) * 128] = o_t.T.astype(BF16)


def _dsa(proj3, ik3, avt4, iwt, bias_tiles, k_sel):
    bsz, seq, _ = proj3.shape
    nq = seq // QB
    assert seq % (NK * CNT_BLOCKS) == 0
    resident = dict(pipeline_mode=pl.Buffered(1))
    return pl.pallas_call(
        functools.partial(_dsa_kernel, k_sel=k_sel),
        name="dsa",
        grid=(bsz, nq),
        in_specs=[
            pl.BlockSpec((None, QB, 512), lambda b, i: (b, i, COL_AQ)),
            pl.BlockSpec((None, QB, 512), lambda b, i: (b, i, COL_IQ)),
            pl.BlockSpec((IDX_HEADS, QB), lambda b, i: (0, b * nq + i)),
            pl.BlockSpec((None, seq, 512), lambda b, i: (b, 0, COL_AK), **resident),
            pl.BlockSpec((None, seq // NK, 512, NK), lambda b, i: (b, 0, 0, 0), **resident),
            pl.BlockSpec((None, seq, N_IK), lambda b, i: (b, 0, 0), **resident),
            pl.BlockSpec(bias_tiles.shape, lambda b, i: (0, 0, 0, 0), **resident),
        ],
        out_specs=pl.BlockSpec((None, QB, 512), lambda b, i: (b, i, 0)),
        out_shape=jax.ShapeDtypeStruct((bsz, seq, 512), BF16),
        scratch_shapes=[
            pltpu.VMEM((seq // NK, NK, QB), jnp.int32),
            pltpu.VMEM((33, seq // NK, 8, QB), jnp.int32),
            pltpu.VMEM((seq // NK, 8, QB), jnp.int32),
            pltpu.VMEM((N_PAIR * QB, 128), BF16),
            pltpu.VMEM((N_PAIR, 2 * QB, 128), BF16),
            pltpu.VMEM((N_PAIR, 1, 2 * QB), F32),
            pltpu.VMEM((N_PAIR, 128 + ONES_ROWS, 2 * QB), F32),
            pltpu.VMEM((N_PAIR, NK, 2 * QB), F32),
            pltpu.VMEM((N_PAIR, 1, 2 * QB), F32),
            pltpu.VMEM((N_PAIR, NK, 2 * QB), F32),
            pltpu.VMEM((N_PAIR, 1, 2 * QB), F32),
            pltpu.VMEM((4, NK, 2 * QB), F32),
            pltpu.VMEM((4, NK, 2 * QB), F32),
            pltpu.VMEM((16, 8, QB), jnp.int32),
        ],
        compiler_params=pltpu.CompilerParams(
            dimension_semantics=("arbitrary", "arbitrary"), vmem_limit_bytes=VMEM_LIMIT),
    )(proj3, proj3, iwt, proj3, avt4, ik3, bias_tiles)


SB_DEAD_MASS = 104.0


def _sb_kernel(q_ref, k_ref, v_ref, o_ref, qm_ref, u_ref, carry_ref, acc_ref):
    i = pl.program_id(1)
    diag = i // 2
    row = lax.broadcasted_iota(jnp.int32, (QB, NK), 0)
    col = lax.broadcasted_iota(jnp.int32, (QB, NK), 1)
    qpos = i * QB + row

    _split_heads_into(qm_ref, q_ref[...])
    kr = lax.broadcasted_iota(jnp.int32, (NK, NK), 0)
    kc = lax.broadcasted_iota(jnp.int32, (NK, NK), 1)
    u_ref[...] = jnp.where(kr > kc, 1.0, 0.0).astype(BF16)
    carry_ref[...] = jnp.zeros(carry_ref.shape, F32)
    acc_ref[...] = jnp.zeros(acc_ref.shape, F32)

    def block(jb, on_diagonal):
        k0 = pl.multiple_of(jb * NK, NK)
        if on_diagonal:
            causal = (k0 + col) < qpos
        for p in range(N_PAIR):
            kp = k_ref[pl.ds(k0, NK), p * 128:(p + 1) * 128]
            vp = v_ref[pl.ds(k0, NK), p * 128:(p + 1) * 128]
            z2 = lax.dot_general(qm_ref[p], kp, _NT, preferred_element_type=F32)
            for e in range(2):
                h = 2 * p + e
                z = z2[e * QB:(e + 1) * QB]
                sp = jnp.maximum(z, 0.0) + jnp.log(1.0 + jnp.exp(-jnp.abs(z)))
                spm = jnp.where(causal, sp, 0.0) if on_diagonal else sp
                hi = spm.astype(BF16)
                lo = (spm - hi.astype(F32)).astype(BF16)
                u = u_ref[...]
                later = (jnp.dot(hi, u, preferred_element_type=F32)
                         + jnp.dot(lo, u, preferred_element_type=F32))
                carry = carry_ref[h]
                a = jnp.exp(z - sp - later - carry)
                if on_diagonal:
                    a = jnp.where(causal, a, 0.0)
                acc_ref[h] += jnp.dot(a.astype(BF16), vp, preferred_element_type=F32)
                carry_ref[h] = carry + jnp.sum(spm, axis=1, keepdims=True)

    block(diag, True)

    def alive():
        return (jnp.min(carry_ref[...]) <= SB_DEAD_MASS).astype(jnp.int32)

    def cond(state):
        jb, go = state
        return jnp.logical_and(jb >= 0, go > 0)

    def body(state):
        jb, _ = state
        block(jb, False)
        return jb - 1, alive()

    lax.while_loop(cond, body, (diag - 1, alive()))

    for p in range(N_PAIR):
        o_ref[:, p * 128:(p + 1) * 128] = _merge_pair(
            acc_ref[2 * p], acc_ref[2 * p + 1]).astype(BF16)


def _stick_breaking(proj3):
    bsz, seq, _ = proj3.shape
    resident = dict(pipeline_mode=pl.Buffered(1))
    return pl.pallas_call(
        _sb_kernel,
        name="stick_breaking",
        grid=(bsz, seq // QB),
        in_specs=[
            pl.BlockSpec((None, QB, 512), lambda b, i: (b, i, COL_BQ)),
            pl.BlockSpec((None, seq, 512), lambda b, i: (b, 0, COL_BK), **resident),
            pl.BlockSpec((None, seq, 512), lambda b, i: (b, 0, COL_BV), **resident),
        ],
        out_specs=pl.BlockSpec((None, QB, 512), lambda b, i: (b, i, 0)),
        out_shape=jax.ShapeDtypeStruct((bsz, seq, 512), BF16),
        scratch_shapes=[
            pltpu.VMEM((N_PAIR, 2 * QB, 128), BF16),
            pltpu.VMEM((NK, NK), BF16),
            pltpu.VMEM((N_HEADS, QB, 1), F32),
            pltpu.VMEM((N_HEADS, QB, 128), F32),
        ],
        compiler_params=pltpu.CompilerParams(
            dimension_semantics=("arbitrary", "arbitrary"), vmem_limit_bytes=VMEM_LIMIT),
    )(proj3, proj3, proj3)


def _merge_kernel(x_ref, ya_ref, yb_ref, cq_ref, g0_ref, g1_ref, g2_ref, mk_ref, mv_ref,
                  wa_ref, wb_ref, wc_ref, wo_ref, gp_ref, o_ref):
    cq = cq_ref[...]
    heads = []
    for h in range(C_HEADS):
        sl = slice(h * C_HEAD_DIM, (h + 1) * C_HEAD_DIM)
        s = lax.dot_general(cq[:, sl], mk_ref[:, sl], _NT,
                            preferred_element_type=F32) * (C_HEAD_DIM ** -0.5)
        e = jnp.exp(s - jnp.max(s, axis=1, keepdims=True))
        p = e / jnp.sum(e, axis=1, keepdims=True)
        heads.append(jnp.dot(p.astype(BF16), mv_ref[:, sl], preferred_element_type=F32))
    yc_pre = jnp.concatenate(heads, axis=1).astype(BF16)
    ya = jnp.dot(ya_ref[...], wa_ref[...], preferred_element_type=F32)
    yb = jnp.dot(yb_ref[...], wb_ref[...], preferred_element_type=F32)
    yc = jnp.dot(yc_pre, wc_ref[...], preferred_element_type=F32)
    merged = (g0_ref[...].astype(F32) * ya + g1_ref[...].astype(F32) * yb
              + g2_ref[...].astype(F32) * yc)
    o = jnp.dot(merged.astype(BF16), wo_ref[...], preferred_element_type=F32)
    o_ref[...] = x_ref[...] + _rms(o, gp_ref[...])


def _merge(x2, ya2, yb2, proj2, mkv3, wa, wb, wc, wo, g_post, seq):
    n = x2.shape[0]
    tm = min(512, seq)
    per_batch = seq // tm
    n_mem = mkv3.shape[1]
    c_dim = C_HEADS * C_HEAD_DIM
    const = lambda t: (0, 0)
    return pl.pallas_call(
        _merge_kernel,
        name="merge",
        grid=(n // tm,),
        in_specs=[
            pl.BlockSpec((tm, D_MODEL), lambda t: (t, 0)),
            pl.BlockSpec((tm, 512), lambda t: (t, 0)),
            pl.BlockSpec((tm, 512), lambda t: (t, 0)),
            pl.BlockSpec((tm, 512), lambda t: (t, COL_CQ)),
            pl.BlockSpec((tm, D_MODEL), lambda t: (t, GATE_COL0 // D_MODEL)),
            pl.BlockSpec((tm, D_MODEL), lambda t: (t, GATE_COL0 // D_MODEL + 1)),
            pl.BlockSpec((tm, D_MODEL), lambda t: (t, GATE_COL0 // D_MODEL + 2)),
            pl.BlockSpec((None, n_mem, c_dim), lambda t: (t // per_batch, 0, 0)),
            pl.BlockSpec((None, n_mem, c_dim), lambda t: (t // per_batch, 0, 1)),
            pl.BlockSpec(wa.shape, const),
            pl.BlockSpec(wb.shape, const),
            pl.BlockSpec(wc.shape, const),
            pl.BlockSpec(wo.shape, const),
            pl.BlockSpec((1, D_MODEL), const),
        ],
        out_specs=pl.BlockSpec((tm, D_MODEL), lambda t: (t, 0)),
        out_shape=jax.ShapeDtypeStruct((n, D_MODEL), F32),
        compiler_params=pltpu.CompilerParams(
            dimension_semantics=("arbitrary",), vmem_limit_bytes=VMEM_LIMIT),
    )(x2, ya2, yb2, proj2, proj2, proj2, proj2, mkv3, mkv3, wa, wb, wc, wo, g_post)


def _ffn_kernel(x_ref, gpre_ref, wg_ref, wu_ref, wo_ref, gpost_ref, o_ref, h_ref, acc_ref):
    k = pl.program_id(1)

    @pl.when(k == 0)
    def _():
        h_ref[...] = _rms(x_ref[...], gpre_ref[...]).astype(BF16)
        acc_ref[...] = jnp.zeros(acc_ref.shape, F32)

    h = h_ref[...]
    g = jnp.dot(h, wg_ref[...], preferred_element_type=F32)
    u = jnp.dot(h, wu_ref[...], preferred_element_type=F32)
    act = (g * jax.nn.sigmoid(g) * u).astype(BF16)
    acc_ref[...] += jnp.dot(act, wo_ref[...], preferred_element_type=F32)

    @pl.when(k == pl.num_programs(1) - 1)
    def _():
        o_ref[...] = x_ref[...] + _rms(acc_ref[...], gpost_ref[...])


def _ffn(x2, g_pre, wg, wu, wo, g_post):
    n = x2.shape[0]
    d_ff = wg.shape[1]
    tm = min(512, n)
    tf = d_ff // 2
    return pl.pallas_call(
        _ffn_kernel,
        name="ffn",
        grid=(n // tm, d_ff // tf),
        in_specs=[
            pl.BlockSpec((tm, D_MODEL), lambda t, k: (t, 0)),
            pl.BlockSpec((1, D_MODEL), lambda t, k: (0, 0)),
            pl.BlockSpec((D_MODEL, tf), lambda t, k: (0, k)),
            pl.BlockSpec((D_MODEL, tf), lambda t, k: (0, k)),
            pl.BlockSpec((tf, D_MODEL), lambda t, k: (k, 0)),
            pl.BlockSpec((1, D_MODEL), lambda t, k: (0, 0)),
        ],
        out_specs=pl.BlockSpec((tm, D_MODEL), lambda t, k: (t, 0)),
        out_shape=jax.ShapeDtypeStruct((n, D_MODEL), F32),
        scratch_shapes=[pltpu.VMEM((tm, D_MODEL), BF16), pltpu.VMEM((tm, D_MODEL), F32)],
        compiler_params=pltpu.CompilerParams(
            dimension_semantics=("arbitrary", "arbitrary"), vmem_limit_bytes=VMEM_LIMIT),
    )(x2, g_pre, wg, wu, wo, g_post)


def _pack_w_in(w, b_gate):
    sizes = (512, 512, 512, IDX_HEADS * 64, 64, IDX_HEADS, 512, 512, 512, 512,
             N_BRANCH * D_MODEL)
    aq, ak, av, iq, ik, iw, bq, bk, bv, cq, gates = jnp.split(w, np.cumsum(sizes)[:-1], axis=1)
    scale = HEAD_DIM ** -0.5
    w_main = jnp.concatenate(
        [aq * scale, ak, av, iq * scale, bq * scale, bk, bv, cq, gates], axis=1).astype(BF16)
    z64 = jnp.zeros((D_MODEL, 64), F32)
    w_ik = jnp.concatenate([ik, z64, z64, ik], axis=1).astype(BF16)
    w_trans = jnp.concatenate(
        [av, iw, jnp.zeros((D_MODEL, N_TRANS - 512 - IDX_HEADS), F32)], axis=1).T.astype(BF16)
    b_main = jnp.concatenate([jnp.zeros((GATE_COL0,), F32), b_gate])[None, :]
    return w_main, w_ik, w_trans, b_main


def kernel(x, mem, rel_bias, g_mix_pre, w_in, b_gate, g_mem, w_mem_kv, w_up_a, w_up_b, w_up_c,
           w_out, g_mix_post, g_ffn_pre, w_ffn_in, w_ffn_out, g_ffn_post):
    bsz, seq, _ = x.shape
    n_mem = mem.shape[1]
    k_sel = min(TOPK_MAX, seq // 4)
    bias_tiles = _bias_tiles(rel_bias)
    x2 = x.reshape(bsz * seq, D_MODEL)
    for l in range(w_in.shape[0]):
        w_main, w_ik, w_trans, b_main = _pack_w_in(w_in[l], b_gate[l])
        proj2, ik2, avt, iwt = _project(x2, g_mix_pre[l][None, :], w_main, b_main, w_ik, w_trans)
        proj3 = proj2.reshape(bsz, seq, N_MAIN)
        mkv = _memkv(mem.reshape(bsz * n_mem, D_MODEL), g_mem[l][None, :],
                     w_mem_kv[l].astype(BF16))
        ya = _dsa(proj3, ik2.reshape(bsz, seq, N_IK), avt.reshape(bsz, seq // NK, 512, NK),
                  iwt, bias_tiles, k_sel)
        yb = _stick_breaking(proj3)
        x2 = _merge(x2, ya.reshape(bsz * seq, 512), yb.reshape(bsz * seq, 512), proj2,
                    mkv.reshape(bsz, n_mem, 2 * C_HEADS * C_HEAD_DIM),
                    w_up_a[l].astype(BF16), w_up_b[l].astype(BF16), w_up_c[l].astype(BF16),
                    w_out[l].astype(BF16), g_mix_post[l][None, :], seq)
        d_ff = w_ffn_out.shape[1]
        w_ffn = w_ffn_in[l].astype(BF16)
        x2 = _ffn(x2, g_ffn_pre[l][None, :], w_ffn[:, :d_ff], w_ffn[:, d_ff:],
                  w_ffn_out[l].astype(BF16), g_ffn_post[l][None, :])
    return x2.reshape(bsz, seq, D_MODEL)
```

```python
import functools

import numpy as np
import jax
import jax.numpy as jnp
from jax import lax
from jax.experimental import pallas as pl
from jax.experimental.pallas import tpu as pltpu

D_MODEL = 1024
CHUNK = 64
HEAD_DIM = 64
N_HEADS = 8
IDX_HEADS = 8
TOPK_MAX = 256
C_HEADS = 4
C_HEAD_DIM = 128
N_BRANCH = 3
REL_BUCKETS = 32
EPS = 1e-6

F32 = jnp.float32
BF16 = jnp.bfloat16
INT_MIN = -2 ** 31
NEG_BIG = -1e30
LOG2E = 1.4426950408889634

QB = 128
NK = 256
N_PAIR = N_HEADS // 2
ONES_ROWS = 16

COL_AQ, COL_AK, COL_AV, COL_IQ, COL_BQ, COL_BK, COL_BV, COL_CQ = range(8)
GATE_COL0 = 8 * 512
N_MAIN = GATE_COL0 + N_BRANCH * D_MODEL
N_IK = 256
N_TRANS = 512 + 16

VMEM_LIMIT = 56 * 1024 * 1024

_NT = (((1,), (1,)), ((), ()))


def _rms(x, g):
    return x * lax.rsqrt(jnp.mean(x * x, axis=-1, keepdims=True) + EPS) * g


def _proj_kernel(x_ref, g_ref, w_ref, b_ref, wik_ref, wt_ref, o_ref, ik_ref, avt_ref, iwt_ref,
                 h_ref, *, first_gate_tile):
    j = pl.program_id(1)

    @pl.when(j == 0)
    def _():
        hb = _rms(x_ref[...], g_ref[...]).astype(BF16)
        h_ref[...] = hb
        ik_ref[...] = jnp.dot(hb, wik_ref[...], preferred_element_type=F32).astype(BF16)
        tr = lax.dot_general(wt_ref[...], hb, _NT, preferred_element_type=F32)
        for c in range(avt_ref.shape[0]):
            avt_ref[c] = tr[:512, c * NK:(c + 1) * NK].astype(BF16)
        iwt_ref[...] = tr[512:512 + IDX_HEADS, :]

    acc = jnp.dot(h_ref[...], w_ref[...], preferred_element_type=F32)

    @pl.when(j < first_gate_tile)
    def _():
        o_ref[...] = acc.astype(BF16)

    @pl.when(j >= first_gate_tile)
    def _():
        o_ref[...] = jax.nn.sigmoid(acc + b_ref[...]).astype(BF16)


def _project(x2, g, w_main, b_main, w_ik, w_trans):
    n = x2.shape[0]
    tm = min(1024, n)
    tn = 1024
    grid = (n // tm, N_MAIN // tn)
    return pl.pallas_call(
        functools.partial(_proj_kernel, first_gate_tile=GATE_COL0 // tn),
        name="in_proj",
        grid=grid,
        in_specs=[
            pl.BlockSpec((tm, D_MODEL), lambda i, j: (i, 0)),
            pl.BlockSpec((1, D_MODEL), lambda i, j: (0, 0)),
            pl.BlockSpec((D_MODEL, tn), lambda i, j: (0, j)),
            pl.BlockSpec((1, tn), lambda i, j: (0, j)),
            pl.BlockSpec((D_MODEL, N_IK), lambda i, j: (0, 0)),
            pl.BlockSpec((N_TRANS, D_MODEL), lambda i, j: (0, 0)),
        ],
        out_specs=[
            pl.BlockSpec((tm, tn), lambda i, j: (i, j)),
            pl.BlockSpec((tm, N_IK), lambda i, j: (i, 0)),
            pl.BlockSpec((tm // NK, 512, NK), lambda i, j: (i, 0, 0)),
            pl.BlockSpec((IDX_HEADS, tm), lambda i, j: (0, i)),
        ],
        out_shape=[
            jax.ShapeDtypeStruct((n, N_MAIN), BF16),
            jax.ShapeDtypeStruct((n, N_IK), BF16),
            jax.ShapeDtypeStruct((n // NK, 512, NK), BF16),
            jax.ShapeDtypeStruct((IDX_HEADS, n), F32),
        ],
        scratch_shapes=[pltpu.VMEM((tm, D_MODEL), BF16)],
        compiler_params=pltpu.CompilerParams(
            dimension_semantics=("arbitrary", "arbitrary"), vmem_limit_bytes=VMEM_LIMIT),
    )(x2, g, w_main, b_main, w_ik, w_trans)


def _memkv_kernel(x_ref, g_ref, w_ref, o_ref):
    hb = _rms(x_ref[...], g_ref[...]).astype(BF16)
    o_ref[...] = jnp.dot(hb, w_ref[...], preferred_element_type=F32).astype(BF16)


def _memkv(mem2, g, w):
    n = mem2.shape[0]
    tm = min(512, n)
    return pl.pallas_call(
        _memkv_kernel,
        name="mem_kv",
        grid=(n // tm,),
        in_specs=[
            pl.BlockSpec((tm, D_MODEL), lambda i: (i, 0)),
            pl.BlockSpec((1, D_MODEL), lambda i: (0, 0)),
            pl.BlockSpec((D_MODEL, w.shape[1]), lambda i: (0, 0)),
        ],
        out_specs=pl.BlockSpec((tm, w.shape[1]), lambda i: (i, 0)),
        out_shape=jax.ShapeDtypeStruct((n, w.shape[1]), BF16),
        compiler_params=pltpu.CompilerParams(
            dimension_semantics=("arbitrary",), vmem_limit_bytes=VMEM_LIMIT),
    )(mem2, g, w)


BIAS_OFFSETS = (0, -QB, -NK, -QB - NK)
_LOG_BUCKET_STARTS = (12, 16, 23, 32, 46, 64, 91)
FAR_BUCKET = 15


def _bias_kernel(rb_ref, o_ref):
    key = lax.broadcasted_iota(jnp.int32, (NK, QB), 0)
    qry = lax.broadcasted_iota(jnp.int32, (NK, QB), 1)
    for c, off in enumerate(BIAS_OFFSETS):
        rel = key - qry + off
        n = jnp.abs(rel)
        large = jnp.full((NK, QB), 8, jnp.int32)
        for start in _LOG_BUCKET_STARTS:
            large = large + jnp.where(n >= start, 1, 0)
        bucket = jnp.where(rel > 0, REL_BUCKETS // 2, 0) + jnp.where(n < 8, n, large)
        for h in range(N_HEADS):
            val = jnp.full((NK, QB), rb_ref[0, h], F32)
            for b in range(1, REL_BUCKETS):
                val = jnp.where(bucket == b, rb_ref[b, h], val)
            o_ref[c, h // 2, :, (h % 2) * QB:(h % 2 + 1) * QB] = (
                val - rb_ref[FAR_BUCKET, h]) * LOG2E


def _bias_tiles(rel_bias):
    return pl.pallas_call(
        _bias_kernel,
        name="rel_bias_tiles",
        in_specs=[pl.BlockSpec(memory_space=pltpu.SMEM)],
        out_specs=pl.BlockSpec(memory_space=pltpu.VMEM),
        out_shape=jax.ShapeDtypeStruct((len(BIAS_OFFSETS), N_PAIR, NK, 2 * QB), F32),
    )(rel_bias)


def _split_heads_into(qm_ref, q):
    lane = lax.broadcasted_iota(jnp.int32, (QB, 128), 1)
    for p in range(N_PAIR):
        qp = q[:, p * 128:(p + 1) * 128].astype(F32)
        qm_ref[p, :QB, :] = jnp.where(lane < HEAD_DIM, qp, 0.0).astype(BF16)
        qm_ref[p, QB:, :] = jnp.where(lane >= HEAD_DIM, qp, 0.0).astype(BF16)


def _two_stage_pipeline(n, first, second):
    first(0, 0)

    def two(u, carry):
        j = 2 * u
        first(j + 1, 1)
        second(j, 0, False)
        first(j + 2, 0)
        second(j + 1, 1, False)
        return carry

    lax.fori_loop(0, (n - 1) // 2, two, 0)

    @pl.when(n % 2 == 0)
    def _():
        first(n - 1, 1)
        second(n - 2, 0, False)
        second(n - 1, 1, True)

    @pl.when(n % 2 == 1)
    def _():
        second(n - 1, 0, True)


def _merge_pair(o_even, o_odd):
    lane = lax.broadcasted_iota(jnp.int32, (QB, 128), 1)
    return jnp.where(lane < HEAD_DIM, o_even, o_odd)


CNT_BLOCKS = 4
KEY_ROWS = 64
assert NK == 8 * 32


def _bit_transpose32(load_row, tmp_ref, store_row):
    def swap(a, b, j, m):
        t = (a ^ lax.shift_right_logical(b, jnp.int32(j))) & m
        return a ^ t, b ^ (t << j)

    lower = []
    for k in range(16):
        a, b = swap(load_row(k), load_row(k + 16), 16, 0x0000FFFF)
        lower.append(a)
        tmp_ref[k] = b
    for base in (0, 16):
        x = lower if base == 0 else [tmp_ref[k] for k in range(16)]
        j, m = 8, 0x00FF00FF
        while j:
            k = 0
            while k < 16:
                x[k], x[k + j] = swap(x[k], x[k + j], j, m)
                k = (k + j + 1) & ~j
            j >>= 1
            m ^= m << j
        for i in range(16):
            store_row(base + i, x[i])


def _dsa_kernel(aq_ref, iq_ref, iwt_ref, ak_ref, avt_ref, ik_ref, bias_ref, o_ref,
                key_ref, plane_ref, alive_ref, iqs_ref, qm_ref, m_ref, acc_ref,
                s0_ref, cm0_ref, s1_ref, cm1_ref, raw0_ref, raw1_ref, tmp_ref, *,
                k_sel):
    i = pl.program_id(1)
    diag = i // 2
    nkb = diag + 1
    krow = lax.broadcasted_iota(jnp.int32, (KEY_ROWS, QB), 0)
    qcol = lax.broadcasted_iota(jnp.int32, (KEY_ROWS, QB), 1)
    qchunk = (i * QB + qcol) // CHUNK

    iwt = iwt_ref[...] * (IDX_HEADS ** -0.5)
    iq = iq_ref[...]
    for p in range(N_PAIR):
        iqs_ref[p * QB:(p + 1) * QB, :] = iq[:, p * 128:(p + 1) * 128]

    def dots_stage(jb, slot):
        raw_ref = (raw0_ref, raw1_ref)[slot]
        k0 = pl.multiple_of(jb * NK, NK)
        ik2 = ik_ref[pl.ds(k0, NK), :]
        for half in range(2):
            iqh = iqs_ref[half * 2 * QB:(half + 1) * 2 * QB, :]
            raw_ref[2 * half] = lax.dot_general(ik2[:, :128], iqh, _NT,
                                                preferred_element_type=F32)
            raw_ref[2 * half + 1] = lax.dot_general(ik2[:, 128:], iqh, _NT,
                                                    preferred_element_type=F32)

    def keys_stage(jb, slot, last):
        raw_ref = (raw0_ref, raw1_ref)[slot]
        k0 = pl.multiple_of(jb * NK, NK)
        for c in range(NK // KEY_ROWS):
            rows = slice(c * KEY_ROWS, (c + 1) * KEY_ROWS)
            acc = jnp.zeros((KEY_ROWS, QB), F32)
            for half in range(2):
                for pp in range(2):
                    h = 2 * (2 * half + pp)
                    sl = slice(pp * QB, (pp + 1) * QB)
                    acc = acc + iwt[h:h + 1, :] * jnp.maximum(raw_ref[2 * half, rows, sl], 0.0)
                    acc = acc + iwt[h + 1:h + 2, :] * jnp.maximum(
                        raw_ref[2 * half + 1, rows, sl], 0.0)
            bits = lax.bitcast_convert_type(acc, jnp.int32)
            key = bits ^ ((bits >> 31) & 0x7FFFFFFF)
            if last:
                admissible = ((k0 + c * KEY_ROWS + krow) // CHUNK) <= qchunk
                key = jnp.where(admissible, key, INT_MIN)
            key_ref[jb, rows, :] = key
        def load_row(r):
            return key_ref[jb, 8 * r:8 * r + 8, :] ^ INT_MIN

        def store_plane(i, v):
            plane_ref[31 - i, jb] = v

        _bit_transpose32(load_row, tmp_ref, store_plane)
        plane_ref[32, jb] = jnp.full((8, QB), -1, jnp.int32)
        alive_ref[jb] = jnp.full((8, QB), -1, jnp.int32)

    _two_stage_pipeline(nkb, dots_stage, keys_stage)

    n_groups = (nkb + CNT_BLOCKS - 1) // CNT_BLOCKS

    def pad_block(jb, carry):
        for b in range(33):
            plane_ref[b, jb] = jnp.zeros((8, QB), jnp.int32)
        alive_ref[jb] = jnp.zeros((8, QB), jnp.int32)
        return carry

    lax.fori_loop(nkb, n_groups * CNT_BLOCKS, pad_block, 0)

    def select_pass(it, state):
        took_prev, n_above, thr_u = state
        b = 31 - it
        take_prev = took_prev != 0

        def body(g, cnts):
            cnts = list(cnts)
            for u in range(CNT_BLOCKS):
                jb = g * CNT_BLOCKS + u
                alive = alive_ref[jb]
                with_prev = alive & plane_ref[b + 1, jb]
                alive = jnp.where(take_prev, with_prev, alive ^ with_prev)
                alive_ref[jb] = alive
                cnts[u] = cnts[u] + lax.population_count(alive & plane_ref[b, jb])
            return tuple(cnts)

        zeros = jnp.zeros((8, QB), jnp.int32)
        cnts = lax.fori_loop(0, n_groups, body, (zeros,) * CNT_BLOCKS)
        n_one = jnp.sum(sum(cnts[1:], cnts[0]), axis=0, keepdims=True)
        take = (n_above + n_one) >= k_sel
        n_above = jnp.where(take, n_above, n_above + n_one)
        thr_u = jnp.where(take, thr_u | jnp.left_shift(jnp.int32(1), b), thr_u)
        return take.astype(jnp.int32), n_above, thr_u

    row0 = jnp.zeros((1, QB), jnp.int32)
    _, _, thr_u = lax.fori_loop(0, 32, select_pass, (row0 + 1, row0, row0))
    thr = thr_u ^ INT_MIN
    thr = jnp.maximum(thr, INT_MIN + 1)

    _split_heads_into(qm_ref, aq_ref[...])
    m_ref[...] = jnp.full(m_ref.shape, NEG_BIG, F32)
    acc_ref[...] = jnp.zeros(acc_ref.shape, F32)
    ones = jnp.ones((ONES_ROWS, NK), BF16)

    slots = ((s0_ref, cm0_ref), (s1_ref, cm1_ref))

    def logits_stage(jb, slot, bias_idx):
        s_ref, cm_ref = slots[slot]
        k0 = pl.multiple_of(jb * NK, NK)
        mask1 = jnp.where(key_ref[jb] >= thr, 0.0, NEG_BIG)
        mask2 = jnp.concatenate([mask1, mask1], axis=1)
        for p in range(N_PAIR):
            kp = ak_ref[pl.ds(k0, NK), p * 128:(p + 1) * 128]
            s2 = lax.dot_general(kp, qm_ref[p], _NT, preferred_element_type=F32) + mask2
            if bias_idx is not None:
                s2 = s2 + bias_ref[bias_idx, p]
            s_ref[p] = s2
            cm_ref[p] = jnp.max(s2, axis=0, keepdims=True)

    def softmax_stage(jb, slot):
        s_ref, cm_ref = slots[slot]
        for p in range(N_PAIR):
            m_prev = m_ref[p]
            m_new = jnp.maximum(m_prev, cm_ref[p])
            alpha = jnp.exp2(m_prev - m_new)
            pe = jnp.exp2(s_ref[p] - m_new).astype(BF16)
            vt = jnp.concatenate([avt_ref[jb, p * 128:(p + 1) * 128, :], ones], axis=0)
            acc_ref[p] = alpha * acc_ref[p] + jnp.dot(vt, pe, preferred_element_type=F32)
            m_ref[p] = m_new

    odd = i % 2
    odd_diag = diag % 2
    logits_stage(diag, 0, odd)

    @pl.when(diag >= 1)
    def _():
        logits_stage(diag - 1, 1, 2 + odd)

    softmax_stage(diag, 0)
    n_steps = diag - 1

    def two_steps(u, carry):
        b = diag - 1 - 2 * u
        logits_stage(b - 1, 0, None)
        softmax_stage(b, 1)
        logits_stage(b - 2, 1, None)
        softmax_stage(b - 1, 0)
        return carry

    lax.fori_loop(0, jnp.maximum(n_steps, 0) // 2, two_steps, 0)

    @pl.when(jnp.logical_and(n_steps >= 1, n_steps % 2 == 1))
    def _():
        logits_stage(0, 0, None)
        softmax_stage(1, 1)

    @pl.when(jnp.logical_and(diag >= 1, odd_diag == 1))
    def _():
        softmax_stage(0, 1)

    @pl.when(jnp.logical_and(diag >= 1, odd_diag == 0))
    def _():
        softmax_stage(0, 0)

    drow = lax.broadcasted_iota(jnp.int32, (128, QB), 0)
    for p in range(N_PAIR):
        a = acc_ref[p]
        o_even = a[0:128, :QB] / a[128:129, :QB]
        o_odd = a[0:128, QB:] / a[128:129, QB:]
        o_t = jnp.where(drow < HEAD_DIM, o_even, o_odd)
        o_ref[:, p * 128:(p + 1) * 128] = o_t.T.astype(BF16)


def _dsa(proj3, ik3, avt4, iwt, bias_tiles, k_sel):
    bsz, seq, _ = proj3.shape
    nq = seq // QB
    assert seq % (NK * CNT_BLOCKS) == 0
    resident = dict(pipeline_mode=pl.Buffered(1))
    return pl.pallas_call(
        functools.partial(_dsa_kernel, k_sel=k_sel),
        name="dsa",
        grid=(bsz, nq),
        in_specs=[
            pl.BlockSpec((None, QB, 512), lambda b, i: (b, i, COL_AQ)),
            pl.BlockSpec((None, QB, 512), lambda b, i: (b, i, COL_IQ)),
            pl.BlockSpec((IDX_HEADS, QB), lambda b, i: (0, b * nq + i)),
            pl.BlockSpec((None, seq, 512), lambda b, i: (b, 0, COL_AK), **resident),
            pl.BlockSpec((None, seq // NK, 512, NK), lambda b, i: (b, 0, 0, 0), **resident),
            pl.BlockSpec((None, seq, N_IK), lambda b, i: (b, 0, 0), **resident),
            pl.BlockSpec(bias_tiles.shape, lambda b, i: (0, 0, 0, 0), **resident),
        ],
        out_specs=pl.BlockSpec((None, QB, 512), lambda b, i: (b, i, 0)),
        out_shape=jax.ShapeDtypeStruct((bsz, seq, 512), BF16),
        scratch_shapes=[
            pltpu.VMEM((seq // NK, NK, QB), jnp.int32),
            pltpu.VMEM((33, seq // NK, 8, QB), jnp.int32),
            pltpu.VMEM((seq // NK, 8, QB), jnp.int32),
            pltpu.VMEM((N_PAIR * QB, 128), BF16),
            pltpu.VMEM((N_PAIR, 2 * QB, 128), BF16),
            pltpu.VMEM((N_PAIR, 1, 2 * QB), F32),
            pltpu.VMEM((N_PAIR, 128 + ONES_ROWS, 2 * QB), F32),
            pltpu.VMEM((N_PAIR, NK, 2 * QB), F32),
            pltpu.VMEM((N_PAIR, 1, 2 * QB), F32),
            pltpu.VMEM((N_PAIR, NK, 2 * QB), F32),
            pltpu.VMEM((N_PAIR, 1, 2 * QB), F32),
            pltpu.VMEM((4, NK, 2 * QB), F32),
            pltpu.VMEM((4, NK, 2 * QB), F32),
            pltpu.VMEM((16, 8, QB), jnp.int32),
        ],
        compiler_params=pltpu.CompilerParams(
            dimension_semantics=("arbitrary", "arbitrary"), vmem_limit_bytes=VMEM_LIMIT),
    )(proj3, proj3, iwt, proj3, avt4, ik3, bias_tiles)


SB_DEAD_MASS = 104.0 * LOG2E


def _sb_kernel(q_ref, k_ref, v_ref, o_ref, qm_ref, uu_ref, carry_ref, acc_ref, z_ref, sp_ref,
               later_ref):
    i = pl.program_id(1)
    diag = i // 2

    @pl.when(jnp.logical_and(pl.program_id(0) == 0, i == 0))
    def _():
        kr = lax.broadcasted_iota(jnp.int32, (2 * NK, NK), 0) % NK
        kc = lax.broadcasted_iota(jnp.int32, (2 * NK, NK), 1)
        uu_ref[...] = jnp.where(kr > kc, 1.0, 0.0).astype(BF16)

    _split_heads_into(qm_ref, q_ref[...])
    carry_ref[...] = jnp.zeros(carry_ref.shape, F32)
    acc_ref[...] = jnp.zeros(acc_ref.shape, F32)

    def block(jb, on_diagonal):
        k0 = pl.multiple_of(jb * NK, NK)
        if on_diagonal:
            row = lax.broadcasted_iota(jnp.int32, (2 * QB, NK), 0)
            col = lax.broadcasted_iota(jnp.int32, (2 * QB, NK), 1)
            causal = (k0 + col) < (i * QB + row % QB)
        for p in range(N_PAIR):
            kp = k_ref[pl.ds(k0, NK), p * 128:(p + 1) * 128]
            z_ref[p] = lax.dot_general(qm_ref[p], kp, _NT, preferred_element_type=F32)
        for p in range(N_PAIR):
            z = z_ref[p]
            sp = jnp.maximum(z, 0.0) + jnp.log2(1.0 + jnp.exp2(-jnp.abs(z)))
            if on_diagonal:
                sp = jnp.where(causal, sp, 0.0)
            sp_ref[p] = sp
            hi = sp.astype(BF16)
            lo = (sp - hi.astype(F32)).astype(BF16)
            later_ref[p] = jnp.dot(jnp.concatenate([hi, lo], axis=1), uu_ref[...],
                                   preferred_element_type=F32)
        for p in range(N_PAIR):
            vp = v_ref[pl.ds(k0, NK), p * 128:(p + 1) * 128]
            carry = carry_ref[p]
            sp = sp_ref[p]
            a = jnp.exp2(z_ref[p] - sp - later_ref[p] - carry)
            if on_diagonal:
                a = jnp.where(causal, a, 0.0)
            acc_ref[p] += jnp.dot(a.astype(BF16), vp, preferred_element_type=F32)
            carry_ref[p] = carry + jnp.sum(sp, axis=1, keepdims=True)

    block(diag, True)

    def alive():
        return (jnp.min(carry_ref[...]) <= SB_DEAD_MASS).astype(jnp.int32)

    def cond(state):
        jb, go = state
        return jnp.logical_and(jb >= 0, go > 0)

    def body(state):
        jb, _ = state
        block(jb, False)
        return jb - 1, alive()

    lax.while_loop(cond, body, (diag - 1, alive()))

    for p in range(N_PAIR):
        o_ref[:, p * 128:(p + 1) * 128] = _merge_pair(
            acc_ref[p, :QB, :], acc_ref[p, QB:, :]).astype(BF16)


def _stick_breaking(proj3):
    bsz, seq, _ = proj3.shape
    resident = dict(pipeline_mode=pl.Buffered(1))
    return pl.pallas_call(
        _sb_kernel,
        name="stick_breaking",
        grid=(bsz, seq // QB),
        in_specs=[
            pl.BlockSpec((None, QB, 512), lambda b, i: (b, i, COL_BQ)),
            pl.BlockSpec((None, seq, 512), lambda b, i: (b, 0, COL_BK), **resident),
            pl.BlockSpec((None, seq, 512), lambda b, i: (b, 0, COL_BV), **resident),
        ],
        out_specs=pl.BlockSpec((None, QB, 512), lambda b, i: (b, i, 0)),
        out_shape=jax.ShapeDtypeStruct((bsz, seq, 512), BF16),
        scratch_shapes=[
            pltpu.VMEM((N_PAIR, 2 * QB, 128), BF16),
            pltpu.VMEM((2 * NK, NK), BF16),
            pltpu.VMEM((N_PAIR, 2 * QB, 1), F32),
            pltpu.VMEM((N_PAIR, 2 * QB, 128), F32),
            pltpu.VMEM((N_PAIR, 2 * QB, NK), F32),
            pltpu.VMEM((N_PAIR, 2 * QB, NK), F32),
            pltpu.VMEM((N_PAIR, 2 * QB, NK), F32),
        ],
        compiler_params=pltpu.CompilerParams(
            dimension_semantics=("arbitrary", "arbitrary"), vmem_limit_bytes=VMEM_LIMIT),
    )(proj3, proj3, proj3)


def _merge_kernel(x_ref, ya_ref, yb_ref, cq_ref, g0_ref, g1_ref, g2_ref, mk_ref, mv_ref,
                  wa_ref, wb_ref, wc_ref, wo_ref, gp_ref, o_ref):
    cq = cq_ref[...]
    heads = []
    for h in range(C_HEADS):
        sl = slice(h * C_HEAD_DIM, (h + 1) * C_HEAD_DIM)
        s = lax.dot_general(cq[:, sl], mk_ref[:, sl], _NT,
                            preferred_element_type=F32) * (C_HEAD_DIM ** -0.5)
        e = jnp.exp(s - jnp.max(s, axis=1, keepdims=True))
        p = e / jnp.sum(e, axis=1, keepdims=True)
        heads.append(jnp.dot(p.astype(BF16), mv_ref[:, sl], preferred_element_type=F32))
    yc_pre = jnp.concatenate(heads, axis=1).astype(BF16)
    ya = jnp.dot(ya_ref[...], wa_ref[...], preferred_element_type=F32)
    yb = jnp.dot(yb_ref[...], wb_ref[...], preferred_element_type=F32)
    yc = jnp.dot(yc_pre, wc_ref[...], preferred_element_type=F32)
    merged = (g0_ref[...].astype(F32) * ya + g1_ref[...].astype(F32) * yb
              + g2_ref[...].astype(F32) * yc)
    o = jnp.dot(merged.astype(BF16), wo_ref[...], preferred_element_type=F32)
    o_ref[...] = x_ref[...] + _rms(o, gp_ref[...])


def _merge(x2, ya2, yb2, proj2, mkv3, wa, wb, wc, wo, g_post, seq):
    n = x2.shape[0]
    tm = min(512, seq)
    per_batch = seq // tm
    n_mem = mkv3.shape[1]
    c_dim = C_HEADS * C_HEAD_DIM
    const = lambda t: (0, 0)
    return pl.pallas_call(
        _merge_kernel,
        name="merge",
        grid=(n // tm,),
        in_specs=[
            pl.BlockSpec((tm, D_MODEL), lambda t: (t, 0)),
            pl.BlockSpec((tm, 512), lambda t: (t, 0)),
            pl.BlockSpec((tm, 512), lambda t: (t, 0)),
            pl.BlockSpec((tm, 512), lambda t: (t, COL_CQ)),
            pl.BlockSpec((tm, D_MODEL), lambda t: (t, GATE_COL0 // D_MODEL)),
            pl.BlockSpec((tm, D_MODEL), lambda t: (t, GATE_COL0 // D_MODEL + 1)),
            pl.BlockSpec((tm, D_MODEL), lambda t: (t, GATE_COL0 // D_MODEL + 2)),
            pl.BlockSpec((None, n_mem, c_dim), lambda t: (t // per_batch, 0, 0)),
            pl.BlockSpec((None, n_mem, c_dim), lambda t: (t // per_batch, 0, 1)),
            pl.BlockSpec(wa.shape, const),
            pl.BlockSpec(wb.shape, const),
            pl.BlockSpec(wc.shape, const),
            pl.BlockSpec(wo.shape, const),
            pl.BlockSpec((1, D_MODEL), const),
        ],
        out_specs=pl.BlockSpec((tm, D_MODEL), lambda t: (t, 0)),
        out_shape=jax.ShapeDtypeStruct((n, D_MODEL), F32),
        compiler_params=pltpu.CompilerParams(
            dimension_semantics=("arbitrary",), vmem_limit_bytes=VMEM_LIMIT),
    )(x2, ya2, yb2, proj2, proj2, proj2, proj2, mkv3, mkv3, wa, wb, wc, wo, g_post)


def _ffn_kernel(x_ref, gpre_ref, wg_ref, wu_ref, wo_ref, gpost_ref, o_ref, h_ref, acc_ref):
    k = pl.program_id(1)

    @pl.when(k == 0)
    def _():
        h_ref[...] = _rms(x_ref[...], gpre_ref[...]).astype(BF16)
        acc_ref[...] = jnp.zeros(acc_ref.shape, F32)

    h = h_ref[...]
    g = jnp.dot(h, wg_ref[...], preferred_element_type=F32)
    u = jnp.dot(h, wu_ref[...], preferred_element_type=F32)
    act = (g * jax.nn.sigmoid(g) * u).astype(BF16)
    acc_ref[...] += jnp.dot(act, wo_ref[...], preferred_element_type=F32)

    @pl.when(k == pl.num_programs(1) - 1)
    def _():
        o_ref[...] = x_ref[...] + _rms(acc_ref[...], gpost_ref[...])


def _ffn(x2, g_pre, wg, wu, wo, g_post):
    n = x2.shape[0]
    d_ff = wg.shape[1]
    tm = min(512, n)
    tf = d_ff // 2
    return pl.pallas_call(
        _ffn_kernel,
        name="ffn",
        grid=(n // tm, d_ff // tf),
        in_specs=[
            pl.BlockSpec((tm, D_MODEL), lambda t, k: (t, 0)),
            pl.BlockSpec((1, D_MODEL), lambda t, k: (0, 0)),
            pl.BlockSpec((D_MODEL, tf), lambda t, k: (0, k)),
            pl.BlockSpec((D_MODEL, tf), lambda t, k: (0, k)),
            pl.BlockSpec((tf, D_MODEL), lambda t, k: (k, 0)),
            pl.BlockSpec((1, D_MODEL), lambda t, k: (0, 0)),
        ],
        out_specs=pl.BlockSpec((tm, D_MODEL), lambda t, k: (t, 0)),
        out_shape=jax.ShapeDtypeStruct((n, D_MODEL), F32),
        scratch_shapes=[pltpu.VMEM((tm, D_MODEL), BF16), pltpu.VMEM((tm, D_MODEL), F32)],
        compiler_params=pltpu.CompilerParams(
            dimension_semantics=("arbitrary", "arbitrary"), vmem_limit_bytes=VMEM_LIMIT),
    )(x2, g_pre, wg, wu, wo, g_post)


def _pack_w_in(w, b_gate):
    sizes = (512, 512, 512, IDX_HEADS * 64, 64, IDX_HEADS, 512, 512, 512, 512,
             N_BRANCH * D_MODEL)
    aq, ak, av, iq, ik, iw, bq, bk, bv, cq, gates = jnp.split(w, np.cumsum(sizes)[:-1], axis=1)
    scale = HEAD_DIM ** -0.5
    scale2 = scale * LOG2E
    w_main = jnp.concatenate(
        [aq * scale2, ak, av, iq * scale, bq * scale2, bk, bv, cq, gates], axis=1).astype(BF16)
    z64 = jnp.zeros((D_MODEL, 64), F32)
    w_ik = jnp.concatenate([ik, z64, z64, ik], axis=1).astype(BF16)
    w_trans = jnp.concatenate(
        [av, iw, jnp.zeros((D_MODEL, N_TRANS - 512 - IDX_HEADS), F32)], axis=1).T.astype(BF16)
    b_main = jnp.concatenate([jnp.zeros((GATE_COL0,), F32), b_gate])[None, :]
    return w_main, w_ik, w_trans, b_main


def kernel(x, mem, rel_bias, g_mix_pre, w_in, b_gate, g_mem, w_mem_kv, w_up_a, w_up_b, w_up_c,
           w_out, g_mix_post, g_ffn_pre, w_ffn_in, w_ffn_out, g_ffn_post):
    bsz, seq, _ = x.shape
    n_mem = mem.shape[1]
    k_sel = min(TOPK_MAX, seq // 4)
    bias_tiles = _bias_tiles(rel_bias)
    x2 = x.reshape(bsz * seq, D_MODEL)
    for l in range(w_in.shape[0]):
        w_main, w_ik, w_trans, b_main = _pack_w_in(w_in[l], b_gate[l])
        proj2, ik2, avt, iwt = _project(x2, g_mix_pre[l][None, :], w_main, b_main, w_ik, w_trans)
        proj3 = proj2.reshape(bsz, seq, N_MAIN)
        mkv = _memkv(mem.reshape(bsz * n_mem, D_MODEL), g_mem[l][None, :],
                     w_mem_kv[l].astype(BF16))
        ya = _dsa(proj3, ik2.reshape(bsz, seq, N_IK), avt.reshape(bsz, seq // NK, 512, NK),
                  iwt, bias_tiles, k_sel)
        yb = _stick_breaking(proj3)
        x2 = _merge(x2, ya.reshape(bsz * seq, 512), yb.reshape(bsz * seq, 512), proj2,
                    mkv.reshape(bsz, n_mem, 2 * C_HEADS * C_HEAD_DIM),
                    w_up_a[l].astype(BF16), w_up_b[l].astype(BF16), w_up_c[l].astype(BF16),
                    w_out[l].astype(BF16), g_mix_post[l][None, :], seq)
        d_ff = w_ffn_out.shape[1]
        w_ffn = w_ffn_in[l].astype(BF16)
        x2 = _ffn(x2, g_ffn_pre[l][None, :], w_ffn[:, :d_ff], w_ffn[:, d_ff:],
                  w_ffn_out[l].astype(BF16), g_ffn_post[l][None, :])
    return x2.reshape(bsz, seq, D_MODEL)
```

```python
import functools

import numpy as np
import jax
import jax.numpy as jnp
from jax import lax
from jax.experimental import pallas as pl
from jax.experimental.pallas import tpu as pltpu

D_MODEL = 1024
CHUNK = 64
HEAD_DIM = 64
N_HEADS = 8
IDX_HEADS = 8
TOPK_MAX = 256
C_HEADS = 4
C_HEAD_DIM = 128
N_BRANCH = 3
REL_BUCKETS = 32
EPS = 1e-6

F32 = jnp.float32
BF16 = jnp.bfloat16
INT_MIN = -2 ** 31
NEG_BIG = -1e30
LOG2E = 1.4426950408889634

QB = 128
NK = 256
N_PAIR = N_HEADS // 2
ONES_ROWS = 16

N_GATE = N_BRANCH * D_MODEL
COL_AQ, COL_AK, COL_IQ, COL_BQ, COL_BK, COL_BV, COL_CQ = range(N_GATE // 512, N_GATE // 512 + 7)
N_MAIN = N_GATE + 7 * 512
N_IK = 256
N_TRANS = 512 + 16

VMEM_LIMIT = 56 * 1024 * 1024

_NT = (((1,), (1,)), ((), ()))


def _rms(x, g):
    return x * lax.rsqrt(jnp.mean(x * x, axis=-1, keepdims=True) + EPS) * g


def _proj_kernel(x_ref, g_ref, w_ref, b_ref, wik_ref, wt_ref, o_ref, ik_ref, avt_ref, iwt_ref,
                 h_ref, *, n_gate_tiles):
    j = pl.program_id(1)

    @pl.when(j == 0)
    def _():
        hb = _rms(x_ref[...], g_ref[...]).astype(BF16)
        h_ref[...] = hb
        ik_ref[...] = jnp.dot(hb, wik_ref[...], preferred_element_type=F32).astype(BF16)
        tr = lax.dot_general(wt_ref[...], hb, _NT, preferred_element_type=F32)
        for c in range(avt_ref.shape[0]):
            avt_ref[c] = tr[:512, c * NK:(c + 1) * NK].astype(BF16)
        iwt_ref[...] = tr[512:512 + IDX_HEADS, :]

    acc = jnp.dot(h_ref[...], w_ref[...], preferred_element_type=F32)

    @pl.when(j >= n_gate_tiles)
    def _():
        o_ref[...] = acc.astype(BF16)

    @pl.when(j < n_gate_tiles)
    def _():
        o_ref[...] = (0.5 + 0.5 * jnp.tanh(0.5 * (acc + b_ref[...]))).astype(BF16)


def _project(x2, g, w_main, b_main, w_ik, w_trans):
    n = x2.shape[0]
    tm = min(2048, n)
    tn = 512
    grid = (n // tm, N_MAIN // tn)
    return pl.pallas_call(
        functools.partial(_proj_kernel, n_gate_tiles=N_GATE // tn),
        name="in_proj",
        grid=grid,
        in_specs=[
            pl.BlockSpec((tm, D_MODEL), lambda i, j: (i, 0)),
            pl.BlockSpec((1, D_MODEL), lambda i, j: (0, 0)),
            pl.BlockSpec((D_MODEL, tn), lambda i, j: (0, j)),
            pl.BlockSpec((1, tn), lambda i, j: (0, j)),
            pl.BlockSpec((D_MODEL, N_IK), lambda i, j: (0, 0)),
            pl.BlockSpec((N_TRANS, D_MODEL), lambda i, j: (0, 0)),
        ],
        out_specs=[
            pl.BlockSpec((tm, tn), lambda i, j: (i, j)),
            pl.BlockSpec((tm, N_IK), lambda i, j: (i, 0)),
            pl.BlockSpec((tm // NK, 512, NK), lambda i, j: (i, 0, 0)),
            pl.BlockSpec((IDX_HEADS, tm), lambda i, j: (0, i)),
        ],
        out_shape=[
            jax.ShapeDtypeStruct((n, N_MAIN), BF16),
            jax.ShapeDtypeStruct((n, N_IK), BF16),
            jax.ShapeDtypeStruct((n // NK, 512, NK), BF16),
            jax.ShapeDtypeStruct((IDX_HEADS, n), F32),
        ],
        scratch_shapes=[pltpu.VMEM((tm, D_MODEL), BF16)],
        compiler_params=pltpu.CompilerParams(
            dimension_semantics=("arbitrary", "arbitrary"), vmem_limit_bytes=VMEM_LIMIT),
    )(x2, g, w_main, b_main, w_ik, w_trans)


def _memkv_kernel(x_ref, g_ref, w_ref, o_ref):
    hb = _rms(x_ref[...], g_ref[...]).astype(BF16)
    o_ref[...] = jnp.dot(hb, w_ref[...], preferred_element_type=F32).astype(BF16)


def _memkv(mem2, g, w):
    n = mem2.shape[0]
    tm = min(512, n)
    return pl.pallas_call(
        _memkv_kernel,
        name="mem_kv",
        grid=(n // tm,),
        in_specs=[
            pl.BlockSpec((tm, D_MODEL), lambda i: (i, 0)),
            pl.BlockSpec((1, D_MODEL), lambda i: (0, 0)),
            pl.BlockSpec((D_MODEL, w.shape[1]), lambda i: (0, 0)),
        ],
        out_specs=pl.BlockSpec((tm, w.shape[1]), lambda i: (i, 0)),
        out_shape=jax.ShapeDtypeStruct((n, w.shape[1]), BF16),
        compiler_params=pltpu.CompilerParams(
            dimension_semantics=("arbitrary",), vmem_limit_bytes=VMEM_LIMIT),
    )(mem2, g, w)


BIAS_OFFSETS = (0, -QB, -NK, -QB - NK)
_LOG_BUCKET_STARTS = (12, 16, 23, 32, 46, 64, 91)
FAR_BUCKET = 15


def _bias_kernel(rb_ref, o_ref):
    key = lax.broadcasted_iota(jnp.int32, (NK, QB), 0)
    qry = lax.broadcasted_iota(jnp.int32, (NK, QB), 1)
    for c, off in enumerate(BIAS_OFFSETS):
        rel = key - qry + off
        n = jnp.abs(rel)
        large = jnp.full((NK, QB), 8, jnp.int32)
        for start in _LOG_BUCKET_STARTS:
            large = large + jnp.where(n >= start, 1, 0)
        bucket = jnp.where(rel > 0, REL_BUCKETS // 2, 0) + jnp.where(n < 8, n, large)
        for h in range(N_HEADS):
            val = jnp.full((NK, QB), rb_ref[0, h], F32)
            for b in range(1, REL_BUCKETS):
                val = jnp.where(bucket == b, rb_ref[b, h], val)
            o_ref[c, h // 2, :, (h % 2) * QB:(h % 2 + 1) * QB] = (
                val - rb_ref[FAR_BUCKET, h]) * LOG2E


def _bias_tiles(rel_bias):
    return pl.pallas_call(
        _bias_kernel,
        name="rel_bias_tiles",
        in_specs=[pl.BlockSpec(memory_space=pltpu.SMEM)],
        out_specs=pl.BlockSpec(memory_space=pltpu.VMEM),
        out_shape=jax.ShapeDtypeStruct((len(BIAS_OFFSETS), N_PAIR, NK, 2 * QB), F32),
    )(rel_bias)


def _split_heads_into(qm_ref, q):
    lane = lax.broadcasted_iota(jnp.int32, (QB, 128), 1)
    for p in range(N_PAIR):
        qp = q[:, p * 128:(p + 1) * 128].astype(F32)
        qm_ref[p, :QB, :] = jnp.where(lane < HEAD_DIM, qp, 0.0).astype(BF16)
        qm_ref[p, QB:, :] = jnp.where(lane >= HEAD_DIM, qp, 0.0).astype(BF16)


def _two_stage_pipeline(n, first, second):
    first(0, 0)

    def two(u, carry):
        j = 2 * u
        first(j + 1, 1)
        second(j, 0, False)
        first(j + 2, 0)
        second(j + 1, 1, False)
        return carry

    lax.fori_loop(0, (n - 1) // 2, two, 0)

    @pl.when(n % 2 == 0)
    def _():
        first(n - 1, 1)
        second(n - 2, 0, False)
        second(n - 1, 1, True)

    @pl.when(n % 2 == 1)
    def _():
        second(n - 1, 0, True)


def _merge_pair(o_even, o_odd):
    lane = lax.broadcasted_iota(jnp.int32, (QB, 128), 1)
    return jnp.where(lane < HEAD_DIM, o_even, o_odd)


CNT_BLOCKS = 4
KEY_ROWS = 64
assert NK == 8 * 32


def _bit_transpose32(load_row, tmp_ref, store_row):
    def swap(a, b, j, m):
        t = (a ^ lax.shift_right_logical(b, jnp.int32(j))) & m
        return a ^ t, b ^ (t << j)

    lower = []
    for k in range(16):
        a, b = swap(load_row(k), load_row(k + 16), 16, 0x0000FFFF)
        lower.append(a)
        tmp_ref[k] = b
    for base in (0, 16):
        x = lower if base == 0 else [tmp_ref[k] for k in range(16)]
        j, m = 8, 0x00FF00FF
        while j:
            k = 0
            while k < 16:
                x[k], x[k + j] = swap(x[k], x[k + j], j, m)
                k = (k + j + 1) & ~j
            j >>= 1
            m ^= m << j
        for i in range(16):
            store_row(base + i, x[i])


def _dsa_kernel(aq_ref, iq_ref, iwt_ref, ak_ref, avt_ref, ik_ref, bias_ref, o_ref,
                key_ref, plane_ref, alive_ref, iqs_ref, qm_ref, m_ref, acc_ref,
                s0_ref, cm0_ref, s1_ref, cm1_ref, raw0_ref, raw1_ref, tmp_ref, *,
                k_sel):
    i = pl.program_id(1)
    diag = i // 2
    nkb = diag + 1
    krow = lax.broadcasted_iota(jnp.int32, (KEY_ROWS, QB), 0)
    qcol = lax.broadcasted_iota(jnp.int32, (KEY_ROWS, QB), 1)
    qchunk = (i * QB + qcol) // CHUNK

    iwt = iwt_ref[...] * (IDX_HEADS ** -0.5)
    iq = iq_ref[...]
    for p in range(N_PAIR):
        iqs_ref[p * QB:(p + 1) * QB, :] = iq[:, p * 128:(p + 1) * 128]

    def dots_stage(jb, slot):
        raw_ref = (raw0_ref, raw1_ref)[slot]
        k0 = pl.multiple_of(jb * NK, NK)
        ik2 = ik_ref[pl.ds(k0, NK), :]
        for half in range(2):
            iqh = iqs_ref[half * 2 * QB:(half + 1) * 2 * QB, :]
            raw_ref[2 * half] = lax.dot_general(ik2[:, :128], iqh, _NT,
                                                preferred_element_type=F32)
            raw_ref[2 * half + 1] = lax.dot_general(ik2[:, 128:], iqh, _NT,
                                                    preferred_element_type=F32)

    def keys_stage(jb, slot, last):
        raw_ref = (raw0_ref, raw1_ref)[slot]
        k0 = pl.multiple_of(jb * NK, NK)
        for c in range(NK // KEY_ROWS):
            rows = slice(c * KEY_ROWS, (c + 1) * KEY_ROWS)
            acc = jnp.zeros((KEY_ROWS, QB), F32)
            for half in range(2):
                for pp in range(2):
                    h = 2 * (2 * half + pp)
                    sl = slice(pp * QB, (pp + 1) * QB)
                    acc = acc + iwt[h:h + 1, :] * jnp.maximum(raw_ref[2 * half, rows, sl], 0.0)
                    acc = acc + iwt[h + 1:h + 2, :] * jnp.maximum(
                        raw_ref[2 * half + 1, rows, sl], 0.0)
            bits = lax.bitcast_convert_type(acc, jnp.int32)
            key = bits ^ ((bits >> 31) & 0x7FFFFFFF)
            if last:
                admissible = ((k0 + c * KEY_ROWS + krow) // CHUNK) <= qchunk
                key = jnp.where(admissible, key, INT_MIN)
            key_ref[jb, rows, :] = key
        def load_row(r):
            return key_ref[jb, 8 * r:8 * r + 8, :] ^ INT_MIN

        def store_plane(i, v):
            plane_ref[31 - i, jb] = v

        _bit_transpose32(load_row, tmp_ref, store_plane)
        plane_ref[32, jb] = jnp.full((8, QB), -1, jnp.int32)
        alive_ref[jb] = jnp.full((8, QB), -1, jnp.int32)

    _two_stage_pipeline(nkb, dots_stage, keys_stage)

    n_groups = (nkb + CNT_BLOCKS - 1) // CNT_BLOCKS

    def pad_block(jb, carry):
        for b in range(33):
            plane_ref[b, jb] = jnp.zeros((8, QB), jnp.int32)
        alive_ref[jb] = jnp.zeros((8, QB), jnp.int32)
        return carry

    lax.fori_loop(nkb, n_groups * CNT_BLOCKS, pad_block, 0)

    def select_pass(it, state):
        took_prev, n_above, thr_u = state
        b = 31 - it
        take_prev = took_prev != 0

        def body(g, cnts):
            cnts = list(cnts)
            for u in range(CNT_BLOCKS):
                jb = g * CNT_BLOCKS + u
                alive = alive_ref[jb]
                with_prev = alive & plane_ref[b + 1, jb]
                alive = jnp.where(take_prev, with_prev, alive ^ with_prev)
                alive_ref[jb] = alive
                cnts[u] = cnts[u] + lax.population_count(alive & plane_ref[b, jb])
            return tuple(cnts)

        zeros = jnp.zeros((8, QB), jnp.int32)
        cnts = lax.fori_loop(0, n_groups, body, (zeros,) * CNT_BLOCKS)
        n_one = jnp.sum(sum(cnts[1:], cnts[0]), axis=0, keepdims=True)
        take = (n_above + n_one) >= k_sel
        n_above = jnp.where(take, n_above, n_above + n_one)
        thr_u = jnp.where(take, thr_u | jnp.left_shift(jnp.int32(1), b), thr_u)
        return take.astype(jnp.int32), n_above, thr_u

    row0 = jnp.zeros((1, QB), jnp.int32)
    _, _, thr_u = lax.fori_loop(0, 32, select_pass, (row0 + 1, row0, row0))
    thr = thr_u ^ INT_MIN
    thr = jnp.maximum(thr, INT_MIN + 1)

    _split_heads_into(qm_ref, aq_ref[...])
    m_ref[...] = jnp.full(m_ref.shape, NEG_BIG, F32)
    acc_ref[...] = jnp.zeros(acc_ref.shape, F32)
    ones = jnp.ones((ONES_ROWS, NK), BF16)

    slots = ((s0_ref, cm0_ref), (s1_ref, cm1_ref))

    def logits_stage(jb, slot, bias_idx):
        s_ref, cm_ref = slots[slot]
        k0 = pl.multiple_of(jb * NK, NK)
        mask1 = jnp.where(key_ref[jb] >= thr, 0.0, NEG_BIG)
        mask2 = jnp.concatenate([mask1, mask1], axis=1)
        for p in range(N_PAIR):
            kp = ak_ref[pl.ds(k0, NK), p * 128:(p + 1) * 128]
            s2 = lax.dot_general(kp, qm_ref[p], _NT, preferred_element_type=F32) + mask2
            if bias_idx is not None:
                s2 = s2 + bias_ref[bias_idx, p]
            s_ref[p] = s2
            cm_ref[p] = jnp.max(s2, axis=0, keepdims=True)

    def softmax_stage(jb, slot):
        s_ref, cm_ref = slots[slot]
        for p in range(N_PAIR):
            m_prev = m_ref[p]
            m_new = jnp.maximum(m_prev, cm_ref[p])
            alpha = jnp.exp2(m_prev - m_new)
            pe = jnp.exp2(s_ref[p] - m_new).astype(BF16)
            vt = jnp.concatenate([avt_ref[jb, p * 128:(p + 1) * 128, :], ones], axis=0)
            acc_ref[p] = alpha * acc_ref[p] + jnp.dot(vt, pe, preferred_element_type=F32)
            m_ref[p] = m_new

    odd = i % 2
    odd_diag = diag % 2
    logits_stage(diag, 0, odd)
    logits_stage(jnp.maximum(diag - 1, 0), 1, 2 + odd)
    softmax_stage(diag, 0)
    n_steps = diag - 1

    def two_steps(u, carry):
        b = diag - 1 - 2 * u
        logits_stage(b - 1, 0, None)
        softmax_stage(b, 1)
        logits_stage(b - 2, 1, None)
        softmax_stage(b - 1, 0)
        return carry

    lax.fori_loop(0, jnp.maximum(n_steps, 0) // 2, two_steps, 0)

    @pl.when(jnp.logical_and(n_steps >= 1, n_steps % 2 == 1))
    def _():
        logits_stage(0, 0, None)
        softmax_stage(1, 1)

    @pl.when(jnp.logical_and(diag >= 1, odd_diag == 1))
    def _():
        softmax_stage(0, 1)

    @pl.when(jnp.logical_and(diag >= 1, odd_diag == 0))
    def _():
        softmax_stage(0, 0)

    drow = lax.broadcasted_iota(jnp.int32, (128, QB), 0)
    for p in range(N_PAIR):
        a = acc_ref[p]
        o_even = a[0:128, :QB] / a[128:129, :QB]
        o_odd = a[0:128, QB:] / a[128:129, QB:]
        o_t = jnp.where(drow < HEAD_DIM, o_even, o_odd)
        o_ref[:, p * 128:(p + 1) * 128] = o_t.T.astype(BF16)


def _dsa(proj3, ik3, avt4, iwt, bias_tiles, k_sel):
    bsz, seq, _ = proj3.shape
    nq = seq // QB
    assert seq % (NK * CNT_BLOCKS) == 0
    resident = dict(pipeline_mode=pl.Buffered(1))
    return pl.pallas_call(
        functools.partial(_dsa_kernel, k_sel=k_sel),
        name="dsa",
        grid=(bsz, nq),
        in_specs=[
            pl.BlockSpec((None, QB, 512), lambda b, i: (b, i, COL_AQ)),
            pl.BlockSpec((None, QB, 512), lambda b, i: (b, i, COL_IQ)),
            pl.BlockSpec((IDX_HEADS, QB), lambda b, i: (0, b * nq + i)),
            pl.BlockSpec((None, seq, 512), lambda b, i: (b, 0, COL_AK), **resident),
            pl.BlockSpec((None, seq // NK, 512, NK), lambda b, i: (b, 0, 0, 0), **resident),
            pl.BlockSpec((None, seq, N_IK), lambda b, i: (b, 0, 0), **resident),
            pl.BlockSpec(bias_tiles.shape, lambda b, i: (0, 0, 0, 0), **resident),
        ],
        out_specs=pl.BlockSpec((None, QB, 512), lambda b, i: (b, i, 0)),
        out_shape=jax.ShapeDtypeStruct((bsz, seq, 512), BF16),
        scratch_shapes=[
            pltpu.VMEM((seq // NK, NK, QB), jnp.int32),
            pltpu.VMEM((33, seq // NK, 8, QB), jnp.int32),
            pltpu.VMEM((seq // NK, 8, QB), jnp.int32),
            pltpu.VMEM((N_PAIR * QB, 128), BF16),
            pltpu.VMEM((N_PAIR, 2 * QB, 128), BF16),
            pltpu.VMEM((N_PAIR, 1, 2 * QB), F32),
            pltpu.VMEM((N_PAIR, 128 + ONES_ROWS, 2 * QB), F32),
            pltpu.VMEM((N_PAIR, NK, 2 * QB), F32),
            pltpu.VMEM((N_PAIR, 1, 2 * QB), F32),
            pltpu.VMEM((N_PAIR, NK, 2 * QB), F32),
            pltpu.VMEM((N_PAIR, 1, 2 * QB), F32),
            pltpu.VMEM((4, NK, 2 * QB), F32),
            pltpu.VMEM((4, NK, 2 * QB), F32),
            pltpu.VMEM((16, 8, QB), jnp.int32),
        ],
        compiler_params=pltpu.CompilerParams(
            dimension_semantics=("arbitrary", "arbitrary"), vmem_limit_bytes=VMEM_LIMIT),
    )(proj3, proj3, iwt, proj3, avt4, ik3, bias_tiles)


SB_DEAD_MASS = 104.0 * LOG2E


def _sb_kernel(q_ref, k_ref, v_ref, o_ref, qm_ref, uu_ref, carry_ref, acc_ref, z_ref, sp_ref,
               later_ref):
    i = pl.program_id(1)
    diag = i // 2

    @pl.when(jnp.logical_and(pl.program_id(0) == 0, i == 0))
    def _():
        kr = lax.broadcasted_iota(jnp.int32, (2 * NK, NK), 0) % NK
        kc = lax.broadcasted_iota(jnp.int32, (2 * NK, NK), 1)
        uu_ref[...] = jnp.where(kr > kc, 1.0, 0.0).astype(BF16)

    _split_heads_into(qm_ref, q_ref[...])
    carry_ref[...] = jnp.zeros(carry_ref.shape, F32)
    acc_ref[...] = jnp.zeros(acc_ref.shape, F32)

    def block(jb, on_diagonal):
        k0 = pl.multiple_of(jb * NK, NK)
        if on_diagonal:
            row = lax.broadcasted_iota(jnp.int32, (2 * QB, NK), 0)
            col = lax.broadcasted_iota(jnp.int32, (2 * QB, NK), 1)
            causal = (k0 + col) < (i * QB + row % QB)
        for p in range(N_PAIR):
            kp = k_ref[pl.ds(k0, NK), p * 128:(p + 1) * 128]
            z_ref[p] = lax.dot_general(qm_ref[p], kp, _NT, preferred_element_type=F32)
        for p in range(N_PAIR):
            z = z_ref[p]
            neg_abs = lax.bitcast_convert_type(
                lax.bitcast_convert_type(z, jnp.int32) | INT_MIN, F32)
            sp = jnp.maximum(z, 0.0) + jnp.log(1.0 + jnp.exp2(neg_abs)) * LOG2E
            if on_diagonal:
                sp = jnp.where(causal, sp, 0.0)
            sp_ref[p] = sp
            hi = sp.astype(BF16)
            lo = (sp - hi.astype(F32)).astype(BF16)
            later_ref[p] = jnp.dot(jnp.concatenate([hi, lo], axis=1), uu_ref[...],
                                   preferred_element_type=F32)
        for p in range(N_PAIR):
            vp = v_ref[pl.ds(k0, NK), p * 128:(p + 1) * 128]
            carry = carry_ref[p]
            sp = sp_ref[p]
            a = jnp.exp2(z_ref[p] - sp - later_ref[p] - carry)
            if on_diagonal:
                a = jnp.where(causal, a, 0.0)
            acc_ref[p] += jnp.dot(a.astype(BF16), vp, preferred_element_type=F32)
            carry_ref[p] = carry + jnp.sum(sp, axis=1, keepdims=True)

    block(diag, True)

    def alive():
        return (jnp.min(carry_ref[...]) <= SB_DEAD_MASS).astype(jnp.int32)

    def cond(state):
        jb, go = state
        return jnp.logical_and(jb >= 0, go > 0)

    def body(state):
        jb, _ = state
        block(jb, False)
        return jb - 1, alive()

    lax.while_loop(cond, body, (diag - 1, alive()))

    for p in range(N_PAIR):
        o_ref[:, p * 128:(p + 1) * 128] = _merge_pair(
            acc_ref[p, :QB, :], acc_ref[p, QB:, :]).astype(BF16)


def _stick_breaking(proj3):
    bsz, seq, _ = proj3.shape
    resident = dict(pipeline_mode=pl.Buffered(1))
    return pl.pallas_call(
        _sb_kernel,
        name="stick_breaking",
        grid=(bsz, seq // QB),
        in_specs=[
            pl.BlockSpec((None, QB, 512), lambda b, i: (b, i, COL_BQ)),
            pl.BlockSpec((None, seq, 512), lambda b, i: (b, 0, COL_BK), **resident),
            pl.BlockSpec((None, seq, 512), lambda b, i: (b, 0, COL_BV), **resident),
        ],
        out_specs=pl.BlockSpec((None, QB, 512), lambda b, i: (b, i, 0)),
        out_shape=jax.ShapeDtypeStruct((bsz, seq, 512), BF16),
        scratch_shapes=[
            pltpu.VMEM((N_PAIR, 2 * QB, 128), BF16),
            pltpu.VMEM((2 * NK, NK), BF16),
            pltpu.VMEM((N_PAIR, 2 * QB, 1), F32),
            pltpu.VMEM((N_PAIR, 2 * QB, 128), F32),
            pltpu.VMEM((N_PAIR, 2 * QB, NK), F32),
            pltpu.VMEM((N_PAIR, 2 * QB, NK), F32),
            pltpu.VMEM((N_PAIR, 2 * QB, NK), F32),
        ],
        compiler_params=pltpu.CompilerParams(
            dimension_semantics=("arbitrary", "arbitrary"), vmem_limit_bytes=VMEM_LIMIT),
    )(proj3, proj3, proj3)


def _merge_kernel(x_ref, ya_ref, yb_ref, cq_ref, g0_ref, g1_ref, g2_ref, mk_ref, mv_ref,
                  wa_ref, wb_ref, wc_ref, wo_ref, gp_ref, o_ref):
    cq = cq_ref[...]
    heads = []
    for h in range(C_HEADS):
        sl = slice(h * C_HEAD_DIM, (h + 1) * C_HEAD_DIM)
        s = lax.dot_general(cq[:, sl], mk_ref[:, sl], _NT,
                            preferred_element_type=F32) * (C_HEAD_DIM ** -0.5)
        e = jnp.exp(s - jnp.max(s, axis=1, keepdims=True))
        p = e / jnp.sum(e, axis=1, keepdims=True)
        heads.append(jnp.dot(p.astype(BF16), mv_ref[:, sl], preferred_element_type=F32))
    yc_pre = jnp.concatenate(heads, axis=1).astype(BF16)
    ya = jnp.dot(ya_ref[...], wa_ref[...], preferred_element_type=F32)
    yb = jnp.dot(yb_ref[...], wb_ref[...], preferred_element_type=F32)
    yc = jnp.dot(yc_pre, wc_ref[...], preferred_element_type=F32)
    merged = (g0_ref[...].astype(F32) * ya + g1_ref[...].astype(F32) * yb
              + g2_ref[...].astype(F32) * yc)
    o = jnp.dot(merged.astype(BF16), wo_ref[...], preferred_element_type=F32)
    o_ref[...] = x_ref[...] + _rms(o, gp_ref[...])


def _merge(x2, ya2, yb2, proj2, mkv3, wa, wb, wc, wo, g_post, seq):
    n = x2.shape[0]
    tm = min(512, seq)
    per_batch = seq // tm
    n_mem = mkv3.shape[1]
    c_dim = C_HEADS * C_HEAD_DIM
    const = lambda t: (0, 0)
    return pl.pallas_call(
        _merge_kernel,
        name="merge",
        grid=(n // tm,),
        in_specs=[
            pl.BlockSpec((tm, D_MODEL), lambda t: (t, 0)),
            pl.BlockSpec((tm, 512), lambda t: (t, 0)),
            pl.BlockSpec((tm, 512), lambda t: (t, 0)),
            pl.BlockSpec((tm, 512), lambda t: (t, COL_CQ)),
            pl.BlockSpec((tm, D_MODEL), lambda t: (t, 0)),
            pl.BlockSpec((tm, D_MODEL), lambda t: (t, 1)),
            pl.BlockSpec((tm, D_MODEL), lambda t: (t, 2)),
            pl.BlockSpec((None, n_mem, c_dim), lambda t: (t // per_batch, 0, 0)),
            pl.BlockSpec((None, n_mem, c_dim), lambda t: (t // per_batch, 0, 1)),
            pl.BlockSpec(wa.shape, const),
            pl.BlockSpec(wb.shape, const),
            pl.BlockSpec(wc.shape, const),
            pl.BlockSpec(wo.shape, const),
            pl.BlockSpec((1, D_MODEL), const),
        ],
        out_specs=pl.BlockSpec((tm, D_MODEL), lambda t: (t, 0)),
        out_shape=jax.ShapeDtypeStruct((n, D_MODEL), F32),
        compiler_params=pltpu.CompilerParams(
            dimension_semantics=("arbitrary",), vmem_limit_bytes=VMEM_LIMIT),
    )(x2, ya2, yb2, proj2, proj2, proj2, proj2, mkv3, mkv3, wa, wb, wc, wo, g_post)


def _ffn_kernel(x_ref, gpre_ref, wg_ref, wu_ref, wo_ref, gpost_ref, o_ref, h_ref, acc_ref):
    k = pl.program_id(1)

    @pl.when(k == 0)
    def _():
        h_ref[...] = _rms(x_ref[...], gpre_ref[...]).astype(BF16)
        acc_ref[...] = jnp.zeros(acc_ref.shape, F32)

    h = h_ref[...]
    g = jnp.dot(h, wg_ref[...], preferred_element_type=F32)
    u = jnp.dot(h, wu_ref[...], preferred_element_type=F32)
    act = (g * jax.nn.sigmoid(g) * u).astype(BF16)
    acc_ref[...] += jnp.dot(act, wo_ref[...], preferred_element_type=F32)

    @pl.when(k == pl.num_programs(1) - 1)
    def _():
        o_ref[...] = x_ref[...] + _rms(acc_ref[...], gpost_ref[...])


def _ffn(x2, g_pre, wg, wu, wo, g_post):
    n = x2.shape[0]
    d_ff = wg.shape[1]
    tm = min(512, n)
    tf = d_ff // 2
    return pl.pallas_call(
        _ffn_kernel,
        name="ffn",
        grid=(n // tm, d_ff // tf),
        in_specs=[
            pl.BlockSpec((tm, D_MODEL), lambda t, k: (t, 0)),
            pl.BlockSpec((1, D_MODEL), lambda t, k: (0, 0)),
            pl.BlockSpec((D_MODEL, tf), lambda t, k: (0, k)),
            pl.BlockSpec((D_MODEL, tf), lambda t, k: (0, k)),
            pl.BlockSpec((tf, D_MODEL), lambda t, k: (k, 0)),
            pl.BlockSpec((1, D_MODEL), lambda t, k: (0, 0)),
        ],
        out_specs=pl.BlockSpec((tm, D_MODEL), lambda t, k: (t, 0)),
        out_shape=jax.ShapeDtypeStruct((n, D_MODEL), F32),
        scratch_shapes=[pltpu.VMEM((tm, D_MODEL), BF16), pltpu.VMEM((tm, D_MODEL), F32)],
        compiler_params=pltpu.CompilerParams(
            dimension_semantics=("arbitrary", "arbitrary"), vmem_limit_bytes=VMEM_LIMIT),
    )(x2, g_pre, wg, wu, wo, g_post)


def _pack_w_in(w, b_gate):
    sizes = (512, 512, 512, IDX_HEADS * 64, 64, IDX_HEADS, 512, 512, 512, 512,
             N_BRANCH * D_MODEL)
    aq, ak, av, iq, ik, iw, bq, bk, bv, cq, gates = jnp.split(w, np.cumsum(sizes)[:-1], axis=1)
    scale = HEAD_DIM ** -0.5
    scale2 = scale * LOG2E
    w_main = jnp.concatenate(
        [gates, aq * scale2, ak, iq * scale, bq * scale2, bk, bv, cq], axis=1).astype(BF16)
    z64 = jnp.zeros((D_MODEL, 64), F32)
    w_ik = jnp.concatenate([ik, z64, z64, ik], axis=1).astype(BF16)
    w_trans = jnp.concatenate(
        [av, iw, jnp.zeros((D_MODEL, N_TRANS - 512 - IDX_HEADS), F32)], axis=1).T.astype(BF16)
    b_main = jnp.concatenate([b_gate, jnp.zeros((N_MAIN - N_GATE,), F32)])[None, :]
    return w_main, w_ik, w_trans, b_main


def kernel(x, mem, rel_bias, g_mix_pre, w_in, b_gate, g_mem, w_mem_kv, w_up_a, w_up_b, w_up_c,
           w_out, g_mix_post, g_ffn_pre, w_ffn_in, w_ffn_out, g_ffn_post):
    bsz, seq, _ = x.shape
    n_mem = mem.shape[1]
    k_sel = min(TOPK_MAX, seq // 4)
    bias_tiles = _bias_tiles(rel_bias)
    x2 = x.reshape(bsz * seq, D_MODEL)
    for l in range(w_in.shape[0]):
        w_main, w_ik, w_trans, b_main = _pack_w_in(w_in[l], b_gate[l])
        proj2, ik2, avt, iwt = _project(x2, g_mix_pre[l][None, :], w_main, b_main, w_ik, w_trans)
        proj3 = proj2.reshape(bsz, seq, N_MAIN)
        mkv = _memkv(mem.reshape(bsz * n_mem, D_MODEL), g_mem[l][None, :],
                     w_mem_kv[l].astype(BF16))
        ya = _dsa(proj3, ik2.reshape(bsz, seq, N_IK), avt.reshape(bsz, seq // NK, 512, NK),
                  iwt, bias_tiles, k_sel)
        yb = _stick_breaking(proj3)
        x2 = _merge(x2, ya.reshape(bsz * seq, 512), yb.reshape(bsz * seq, 512), proj2,
                    mkv.reshape(bsz, n_mem, 2 * C_HEADS * C_HEAD_DIM),
                    w_up_a[l].astype(BF16), w_up_b[l].astype(BF16), w_up_c[l].astype(BF16),
                    w_out[l].astype(BF16), g_mix_post[l][None, :], seq)
        d_ff = w_ffn_out.shape[1]
        w_ffn = w_ffn_in[l].astype(BF16)
        x2 = _ffn(x2, g_ffn_pre[l][None, :], w_ffn[:, :d_ff], w_ffn[:, d_ff:],
                  w_ffn_out[l].astype(BF16), g_ffn_post[l][None, :])
    return x2.reshape(bsz, seq, D_MODEL)
```

```python
import functools

import numpy as np
import jax
import jax.numpy as jnp
from jax import lax
from jax.experimental import pallas as pl
from jax.experimental.pallas import tpu as pltpu

D_MODEL = 1024
CHUNK = 64
HEAD_DIM = 64
N_HEADS = 8
IDX_HEADS = 8
TOPK_MAX = 256
C_HEADS = 4
C_HEAD_DIM = 128
N_BRANCH = 3
REL_BUCKETS = 32
EPS = 1e-6

F32 = jnp.float32
BF16 = jnp.bfloat16
INT_MIN = -2 ** 31
NEG_BIG = -1e30
LOG2E = 1.4426950408889634

QB = 128
NK = 256
N_PAIR = N_HEADS // 2
ONES_ROWS = 16

N_GATE = N_BRANCH * D_MODEL
COL_AQ, COL_AK, COL_IQ, COL_BQ, COL_BK, COL_BV, COL_CQ = range(N_GATE // 512, N_GATE // 512 + 7)
N_MAIN = N_GATE + 7 * 512
N_IK = 256
N_TRANS = 512 + 16

VMEM_LIMIT = 56 * 1024 * 1024

_NT = (((1,), (1,)), ((), ()))


def _rms(x, g):
    return x * lax.rsqrt(jnp.mean(x * x, axis=-1, keepdims=True) + EPS) * g


def _proj_kernel(x_ref, g_ref, w_ref, b_ref, wik_ref, wt_ref, o_ref, ik_ref, avt_ref, iwt_ref,
                 h_ref, *, n_gate_tiles):
    j = pl.program_id(1)

    @pl.when(j == 0)
    def _():
        hb = _rms(x_ref[...], g_ref[...]).astype(BF16)
        h_ref[...] = hb
        ik_ref[...] = jnp.dot(hb, wik_ref[...], preferred_element_type=F32).astype(BF16)
        tr = lax.dot_general(wt_ref[...], hb, _NT, preferred_element_type=F32)
        for c in range(avt_ref.shape[0]):
            avt_ref[c] = tr[:512, c * NK:(c + 1) * NK].astype(BF16)
        iwt_ref[...] = tr[512:512 + IDX_HEADS, :]

    acc = jnp.dot(h_ref[...], w_ref[...], preferred_element_type=F32)

    @pl.when(j >= n_gate_tiles)
    def _():
        o_ref[...] = acc.astype(BF16)

    @pl.when(j < n_gate_tiles)
    def _():
        o_ref[...] = (0.5 + 0.5 * jnp.tanh(0.5 * (acc + b_ref[...]))).astype(BF16)


def _project(x2, g, w_main, b_main, w_ik, w_trans):
    n = x2.shape[0]
    tm = min(2048, n)
    tn = 512
    grid = (n // tm, N_MAIN // tn)
    return pl.pallas_call(
        functools.partial(_proj_kernel, n_gate_tiles=N_GATE // tn),
        name="in_proj",
        grid=grid,
        in_specs=[
            pl.BlockSpec((tm, D_MODEL), lambda i, j: (i, 0)),
            pl.BlockSpec((1, D_MODEL), lambda i, j: (0, 0)),
            pl.BlockSpec((D_MODEL, tn), lambda i, j: (0, j)),
            pl.BlockSpec((1, tn), lambda i, j: (0, j)),
            pl.BlockSpec((D_MODEL, N_IK), lambda i, j: (0, 0)),
            pl.BlockSpec((N_TRANS, D_MODEL), lambda i, j: (0, 0)),
        ],
        out_specs=[
            pl.BlockSpec((tm, tn), lambda i, j: (i, j)),
            pl.BlockSpec((tm, N_IK), lambda i, j: (i, 0)),
            pl.BlockSpec((tm // NK, 512, NK), lambda i, j: (i, 0, 0)),
            pl.BlockSpec((IDX_HEADS, tm), lambda i, j: (0, i)),
        ],
        out_shape=[
            jax.ShapeDtypeStruct((n, N_MAIN), BF16),
            jax.ShapeDtypeStruct((n, N_IK), BF16),
            jax.ShapeDtypeStruct((n // NK, 512, NK), BF16),
            jax.ShapeDtypeStruct((IDX_HEADS, n), F32),
        ],
        scratch_shapes=[pltpu.VMEM((tm, D_MODEL), BF16)],
        compiler_params=pltpu.CompilerParams(
            dimension_semantics=("arbitrary", "arbitrary"), vmem_limit_bytes=VMEM_LIMIT),
    )(x2, g, w_main, b_main, w_ik, w_trans)


def _memkv_kernel(x_ref, g_ref, w_ref, o_ref):
    hb = _rms(x_ref[...], g_ref[...]).astype(BF16)
    o_ref[...] = jnp.dot(hb, w_ref[...], preferred_element_type=F32).astype(BF16)


def _memkv(mem2, g, w):
    n = mem2.shape[0]
    tm = min(512, n)
    return pl.pallas_call(
        _memkv_kernel,
        name="mem_kv",
        grid=(n // tm,),
        in_specs=[
            pl.BlockSpec((tm, D_MODEL), lambda i: (i, 0)),
            pl.BlockSpec((1, D_MODEL), lambda i: (0, 0)),
            pl.BlockSpec((D_MODEL, w.shape[1]), lambda i: (0, 0)),
        ],
        out_specs=pl.BlockSpec((tm, w.shape[1]), lambda i: (i, 0)),
        out_shape=jax.ShapeDtypeStruct((n, w.shape[1]), BF16),
        compiler_params=pltpu.CompilerParams(
            dimension_semantics=("arbitrary",), vmem_limit_bytes=VMEM_LIMIT),
    )(mem2, g, w)


BIAS_OFFSETS = (0, -QB, -NK, -QB - NK)
_LOG_BUCKET_STARTS = (12, 16, 23, 32, 46, 64, 91)
FAR_BUCKET = 15


def _bias_kernel(rb_ref, o_ref):
    key = lax.broadcasted_iota(jnp.int32, (NK, QB), 0)
    qry = lax.broadcasted_iota(jnp.int32, (NK, QB), 1)
    for c, off in enumerate(BIAS_OFFSETS):
        rel = key - qry + off
        n = jnp.abs(rel)
        large = jnp.full((NK, QB), 8, jnp.int32)
        for start in _LOG_BUCKET_STARTS:
            large = large + jnp.where(n >= start, 1, 0)
        bucket = jnp.where(rel > 0, REL_BUCKETS // 2, 0) + jnp.where(n < 8, n, large)
        for h in range(N_HEADS):
            val = jnp.full((NK, QB), rb_ref[0, h], F32)
            for b in range(1, REL_BUCKETS):
                val = jnp.where(bucket == b, rb_ref[b, h], val)
            o_ref[c, h // 2, :, (h % 2) * QB:(h % 2 + 1) * QB] = (
                val - rb_ref[FAR_BUCKET, h]) * LOG2E


def _bias_tiles(rel_bias):
    return pl.pallas_call(
        _bias_kernel,
        name="rel_bias_tiles",
        in_specs=[pl.BlockSpec(memory_space=pltpu.SMEM)],
        out_specs=pl.BlockSpec(memory_space=pltpu.VMEM),
        out_shape=jax.ShapeDtypeStruct((len(BIAS_OFFSETS), N_PAIR, NK, 2 * QB), F32),
    )(rel_bias)


def _split_heads_into(qm_ref, q):
    lane = lax.broadcasted_iota(jnp.int32, (QB, 128), 1)
    for p in range(N_PAIR):
        qp = q[:, p * 128:(p + 1) * 128].astype(F32)
        qm_ref[p, :QB, :] = jnp.where(lane < HEAD_DIM, qp, 0.0).astype(BF16)
        qm_ref[p, QB:, :] = jnp.where(lane >= HEAD_DIM, qp, 0.0).astype(BF16)


def _two_stage_pipeline(n, first, second):
    first(0, 0)

    def two(u, carry):
        j = 2 * u
        first(j + 1, 1)
        second(j, 0, False)
        first(j + 2, 0)
        second(j + 1, 1, False)
        return carry

    lax.fori_loop(0, (n - 1) // 2, two, 0)

    @pl.when(n % 2 == 0)
    def _():
        first(n - 1, 1)
        second(n - 2, 0, False)
        second(n - 1, 1, True)

    @pl.when(n % 2 == 1)
    def _():
        second(n - 1, 0, True)


def _merge_pair(o_even, o_odd):
    lane = lax.broadcasted_iota(jnp.int32, (QB, 128), 1)
    return jnp.where(lane < HEAD_DIM, o_even, o_odd)


CNT_BLOCKS = 4
KEY_ROWS = 64
assert NK == 8 * 32


def _bit_transpose32(load_row, tmp_ref, store_row):
    def swap(a, b, j, m):
        t = (a ^ lax.shift_right_logical(b, jnp.int32(j))) & m
        return a ^ t, b ^ (t << j)

    lower = []
    for k in range(16):
        a, b = swap(load_row(k), load_row(k + 16), 16, 0x0000FFFF)
        lower.append(a)
        tmp_ref[k] = b
    for base in (0, 16):
        x = lower if base == 0 else [tmp_ref[k] for k in range(16)]
        j, m = 8, 0x00FF00FF
        while j:
            k = 0
            while k < 16:
                x[k], x[k + j] = swap(x[k], x[k + j], j, m)
                k = (k + j + 1) & ~j
            j >>= 1
            m ^= m << j
        for i in range(16):
            store_row(base + i, x[i])


def _dsa_kernel(aq_ref, iq_ref, iwt_ref, ak_ref, avt_ref, ik_ref, bias_ref, o_ref,
                key_ref, plane_ref, alive_ref, iqs_ref, qm_ref, m_ref, acc_ref,
                s0_ref, cm0_ref, s1_ref, cm1_ref, raw0_ref, raw1_ref, tmp_ref, *,
                k_sel):
    i = pl.program_id(1)
    diag = i // 2
    nkb = diag + 1
    krow = lax.broadcasted_iota(jnp.int32, (KEY_ROWS, QB), 0)
    qcol = lax.broadcasted_iota(jnp.int32, (KEY_ROWS, QB), 1)
    qchunk = (i * QB + qcol) // CHUNK

    iwt = iwt_ref[...] * (IDX_HEADS ** -0.5)
    iq = iq_ref[...]
    for p in range(N_PAIR):
        iqs_ref[p * QB:(p + 1) * QB, :] = iq[:, p * 128:(p + 1) * 128]

    def dots_stage(jb, slot):
        raw_ref = (raw0_ref, raw1_ref)[slot]
        k0 = pl.multiple_of(jb * NK, NK)
        ik2 = ik_ref[pl.ds(k0, NK), :]
        for half in range(2):
            iqh = iqs_ref[half * 2 * QB:(half + 1) * 2 * QB, :]
            raw_ref[2 * half] = lax.dot_general(ik2[:, :128], iqh, _NT,
                                                preferred_element_type=F32)
            raw_ref[2 * half + 1] = lax.dot_general(ik2[:, 128:], iqh, _NT,
                                                    preferred_element_type=F32)

    def keys_stage(jb, slot, last):
        raw_ref = (raw0_ref, raw1_ref)[slot]
        k0 = pl.multiple_of(jb * NK, NK)
        for c in range(NK // KEY_ROWS):
            rows = slice(c * KEY_ROWS, (c + 1) * KEY_ROWS)
            acc = jnp.zeros((KEY_ROWS, QB), F32)
            for half in range(2):
                for pp in range(2):
                    h = 2 * (2 * half + pp)
                    sl = slice(pp * QB, (pp + 1) * QB)
                    acc = acc + iwt[h:h + 1, :] * jnp.maximum(raw_ref[2 * half, rows, sl], 0.0)
                    acc = acc + iwt[h + 1:h + 2, :] * jnp.maximum(
                        raw_ref[2 * half + 1, rows, sl], 0.0)
            bits = lax.bitcast_convert_type(acc, jnp.int32)
            key = bits ^ ((bits >> 31) & 0x7FFFFFFF)
            if last:
                admissible = ((k0 + c * KEY_ROWS + krow) // CHUNK) <= qchunk
                key = jnp.where(admissible, key, INT_MIN)
            key_ref[jb, rows, :] = key
        def load_row(r):
            return key_ref[jb, 8 * r:8 * r + 8, :] ^ INT_MIN

        def store_plane(i, v):
            plane_ref[31 - i, jb] = v

        _bit_transpose32(load_row, tmp_ref, store_plane)
        plane_ref[32, jb] = jnp.full((8, QB), -1, jnp.int32)
        alive_ref[jb] = jnp.full((8, QB), -1, jnp.int32)

    _two_stage_pipeline(nkb, dots_stage, keys_stage)

    n_groups = (nkb + CNT_BLOCKS - 1) // CNT_BLOCKS

    def pad_block(jb, carry):
        for b in range(33):
            plane_ref[b, jb] = jnp.zeros((8, QB), jnp.int32)
        alive_ref[jb] = jnp.zeros((8, QB), jnp.int32)
        return carry

    lax.fori_loop(nkb, n_groups * CNT_BLOCKS, pad_block, 0)

    def select_pass(it, state):
        took_prev, n_above, thr_u = state
        b = 31 - it
        take_prev = took_prev != 0

        def body(g, cnts):
            cnts = list(cnts)
            for u in range(CNT_BLOCKS):
                jb = g * CNT_BLOCKS + u
                alive = alive_ref[jb]
                with_prev = alive & plane_ref[b + 1, jb]
                alive = jnp.where(take_prev, with_prev, alive ^ with_prev)
                alive_ref[jb] = alive
                cnts[u] = cnts[u] + lax.population_count(alive & plane_ref[b, jb])
            return tuple(cnts)

        zeros = jnp.zeros((8, QB), jnp.int32)
        cnts = lax.fori_loop(0, n_groups, body, (zeros,) * CNT_BLOCKS)
        n_one = jnp.sum(sum(cnts[1:], cnts[0]), axis=0, keepdims=True)
        take = (n_above + n_one) >= k_sel
        n_above = jnp.where(take, n_above, n_above + n_one)
        thr_u = jnp.where(take, thr_u | jnp.left_shift(jnp.int32(1), b), thr_u)
        return take.astype(jnp.int32), n_above, thr_u

    row0 = jnp.zeros((1, QB), jnp.int32)
    _, _, thr_u = lax.fori_loop(0, 32, select_pass, (row0 + 1, row0, row0))
    thr = thr_u ^ INT_MIN
    thr = jnp.maximum(thr, INT_MIN + 1)

    _split_heads_into(qm_ref, aq_ref[...])
    m_ref[...] = jnp.full(m_ref.shape, NEG_BIG, F32)
    acc_ref[...] = jnp.zeros(acc_ref.shape, F32)
    ones = jnp.ones((ONES_ROWS, NK), BF16)

    slots = ((s0_ref, cm0_ref), (s1_ref, cm1_ref))

    def logits_stage(jb, slot, bias_idx):
        s_ref, cm_ref = slots[slot]
        k0 = pl.multiple_of(jb * NK, NK)
        mask1 = jnp.where(key_ref[jb] >= thr, 0.0, NEG_BIG).astype(BF16)
        mask2 = jnp.concatenate([mask1, mask1], axis=1)
        for p in range(N_PAIR):
            kp = ak_ref[pl.ds(k0, NK), p * 128:(p + 1) * 128]
            s2 = lax.dot_general(kp, qm_ref[p], _NT, preferred_element_type=F32)
            if bias_idx is not None:
                s2 = s2 + bias_ref[bias_idx, p]
            sb = s2.astype(BF16) + mask2
            s_ref[p] = sb
            cm_ref[p] = jnp.max(sb, axis=0, keepdims=True).astype(F32)

    def softmax_stage(jb, slot):
        s_ref, cm_ref = slots[slot]
        for p in range(N_PAIR):
            m_prev = m_ref[p]
            m_new = jnp.maximum(m_prev, cm_ref[p])
            alpha = jnp.exp2(m_prev - m_new)
            pe = jnp.exp2(s_ref[p] - m_new.astype(BF16))
            vt = jnp.concatenate([avt_ref[jb, p * 128:(p + 1) * 128, :], ones], axis=0)
            acc_ref[p] = alpha * acc_ref[p] + jnp.dot(vt, pe, preferred_element_type=F32)
            m_ref[p] = m_new

    odd = i % 2
    odd_diag = diag % 2
    logits_stage(diag, 0, odd)
    logits_stage(jnp.maximum(diag - 1, 0), 1, 2 + odd)
    softmax_stage(diag, 0)
    n_steps = diag - 1

    def two_steps(u, carry):
        b = diag - 1 - 2 * u
        logits_stage(b - 1, 0, None)
        softmax_stage(b, 1)
        logits_stage(b - 2, 1, None)
        softmax_stage(b - 1, 0)
        return carry

    lax.fori_loop(0, jnp.maximum(n_steps, 0) // 2, two_steps, 0)

    @pl.when(jnp.logical_and(n_steps >= 1, n_steps % 2 == 1))
    def _():
        logits_stage(0, 0, None)
        softmax_stage(1, 1)

    @pl.when(jnp.logical_and(diag >= 1, odd_diag == 1))
    def _():
        softmax_stage(0, 1)

    @pl.when(jnp.logical_and(diag >= 1, odd_diag == 0))
    def _():
        softmax_stage(0, 0)

    drow = lax.broadcasted_iota(jnp.int32, (128, QB), 0)
    for p in range(N_PAIR):
        a = acc_ref[p]
        o_even = a[0:128, :QB] / a[128:129, :QB]
        o_odd = a[0:128, QB:] / a[128:129, QB:]
        o_t = jnp.where(drow < HEAD_DIM, o_even, o_odd)
        o_ref[:, p * 128:(p + 1) * 128] = o_t.T.astype(BF16)


def _dsa(proj3, ik3, avt4, iwt, bias_tiles, k_sel):
    bsz, seq, _ = proj3.shape
    nq = seq // QB
    assert seq % (NK * CNT_BLOCKS) == 0
    resident = dict(pipeline_mode=pl.Buffered(1))
    return pl.pallas_call(
        functools.partial(_dsa_kernel, k_sel=k_sel),
        name="dsa",
        grid=(bsz, nq),
        in_specs=[
            pl.BlockSpec((None, QB, 512), lambda b, i: (b, i, COL_AQ)),
            pl.BlockSpec((None, QB, 512), lambda b, i: (b, i, COL_IQ)),
            pl.BlockSpec((IDX_HEADS, QB), lambda b, i: (0, b * nq + i)),
            pl.BlockSpec((None, seq, 512), lambda b, i: (b, 0, COL_AK), **resident),
            pl.BlockSpec((None, seq // NK, 512, NK), lambda b, i: (b, 0, 0, 0), **resident),
            pl.BlockSpec((None, seq, N_IK), lambda b, i: (b, 0, 0), **resident),
            pl.BlockSpec(bias_tiles.shape, lambda b, i: (0, 0, 0, 0), **resident),
        ],
        out_specs=pl.BlockSpec((None, QB, 512), lambda b, i: (b, i, 0)),
        out_shape=jax.ShapeDtypeStruct((bsz, seq, 512), BF16),
        scratch_shapes=[
            pltpu.VMEM((seq // NK, NK, QB), jnp.int32),
            pltpu.VMEM((33, seq // NK, 8, QB), jnp.int32),
            pltpu.VMEM((seq // NK, 8, QB), jnp.int32),
            pltpu.VMEM((N_PAIR * QB, 128), BF16),
            pltpu.VMEM((N_PAIR, 2 * QB, 128), BF16),
            pltpu.VMEM((N_PAIR, 1, 2 * QB), F32),
            pltpu.VMEM((N_PAIR, 128 + ONES_ROWS, 2 * QB), F32),
            pltpu.VMEM((N_PAIR, NK, 2 * QB), BF16),
            pltpu.VMEM((N_PAIR, 1, 2 * QB), F32),
            pltpu.VMEM((N_PAIR, NK, 2 * QB), BF16),
            pltpu.VMEM((N_PAIR, 1, 2 * QB), F32),
            pltpu.VMEM((4, NK, 2 * QB), F32),
            pltpu.VMEM((4, NK, 2 * QB), F32),
            pltpu.VMEM((16, 8, QB), jnp.int32),
        ],
        compiler_params=pltpu.CompilerParams(
            dimension_semantics=("arbitrary", "arbitrary"), vmem_limit_bytes=VMEM_LIMIT),
    )(proj3, proj3, iwt, proj3, avt4, ik3, bias_tiles)


SB_DEAD_MASS = 104.0 * LOG2E


def _sb_kernel(q_ref, k_ref, v_ref, o_ref, qm_ref, uu_ref, carry_ref, acc_ref, z_ref, sp_ref,
               later_ref):
    i = pl.program_id(1)
    diag = i // 2

    @pl.when(jnp.logical_and(pl.program_id(0) == 0, i == 0))
    def _():
        kr = lax.broadcasted_iota(jnp.int32, (2 * NK, NK), 0) % NK
        kc = lax.broadcasted_iota(jnp.int32, (2 * NK, NK), 1)
        uu_ref[...] = jnp.where(kr > kc, 1.0, 0.0).astype(BF16)

    _split_heads_into(qm_ref, q_ref[...])
    carry_ref[...] = jnp.zeros(carry_ref.shape, F32)
    acc_ref[...] = jnp.zeros(acc_ref.shape, F32)

    def block(jb, on_diagonal):
        k0 = pl.multiple_of(jb * NK, NK)
        if on_diagonal:
            row = lax.broadcasted_iota(jnp.int32, (2 * QB, NK), 0)
            col = lax.broadcasted_iota(jnp.int32, (2 * QB, NK), 1)
            causal = (k0 + col) < (i * QB + row % QB)
        for p in range(N_PAIR):
            kp = k_ref[pl.ds(k0, NK), p * 128:(p + 1) * 128]
            z_ref[p] = lax.dot_general(qm_ref[p], kp, _NT, preferred_element_type=F32)
        for p in range(N_PAIR):
            z = z_ref[p]
            neg_abs = lax.bitcast_convert_type(
                lax.bitcast_convert_type(z, jnp.int32) | INT_MIN, F32)
            sp = jnp.maximum(z, 0.0) + jnp.log(1.0 + jnp.exp2(neg_abs)) * LOG2E
            if on_diagonal:
                sp = jnp.where(causal, sp, 0.0)
            sp_ref[p] = sp
            hi = sp.astype(BF16)
            lo = (sp - hi.astype(F32)).astype(BF16)
            later_ref[p] = jnp.dot(jnp.concatenate([hi, lo], axis=1), uu_ref[...],
                                   preferred_element_type=F32)
        for p in range(N_PAIR):
            vp = v_ref[pl.ds(k0, NK), p * 128:(p + 1) * 128]
            carry = carry_ref[p]
            sp = sp_ref[p]
            a = jnp.exp2(z_ref[p] - sp - later_ref[p] - carry)
            if on_diagonal:
                a = jnp.where(causal, a, 0.0)
            acc_ref[p] += jnp.dot(a.astype(BF16), vp, preferred_element_type=F32)
            carry_ref[p] = carry + jnp.sum(sp, axis=1, keepdims=True)

    block(diag, True)

    def alive():
        return (jnp.min(carry_ref[...]) <= SB_DEAD_MASS).astype(jnp.int32)

    def cond(state):
        jb, go = state
        return jnp.logical_and(jb >= 0, go > 0)

    def body(state):
        jb, _ = state
        block(jb, False)
        return jb - 1, alive()

    lax.while_loop(cond, body, (diag - 1, alive()))

    for p in range(N_PAIR):
        o_ref[:, p * 128:(p + 1) * 128] = _merge_pair(
            acc_ref[p, :QB, :], acc_ref[p, QB:, :]).astype(BF16)


def _stick_breaking(proj3):
    bsz, seq, _ = proj3.shape
    resident = dict(pipeline_mode=pl.Buffered(1))
    return pl.pallas_call(
        _sb_kernel,
        name="stick_breaking",
        grid=(bsz, seq // QB),
        in_specs=[
            pl.BlockSpec((None, QB, 512), lambda b, i: (b, i, COL_BQ)),
            pl.BlockSpec((None, seq, 512), lambda b, i: (b, 0, COL_BK), **resident),
            pl.BlockSpec((None, seq, 512), lambda b, i: (b, 0, COL_BV), **resident),
        ],
        out_specs=pl.BlockSpec((None, QB, 512), lambda b, i: (b, i, 0)),
        out_shape=jax.ShapeDtypeStruct((bsz, seq, 512), BF16),
        scratch_shapes=[
            pltpu.VMEM((N_PAIR, 2 * QB, 128), BF16),
            pltpu.VMEM((2 * NK, NK), BF16),
            pltpu.VMEM((N_PAIR, 2 * QB, 1), F32),
            pltpu.VMEM((N_PAIR, 2 * QB, 128), F32),
            pltpu.VMEM((N_PAIR, 2 * QB, NK), F32),
            pltpu.VMEM((N_PAIR, 2 * QB, NK), F32),
            pltpu.VMEM((N_PAIR, 2 * QB, NK), F32),
        ],
        compiler_params=pltpu.CompilerParams(
            dimension_semantics=("arbitrary", "arbitrary"), vmem_limit_bytes=VMEM_LIMIT),
    )(proj3, proj3, proj3)


def _merge_kernel(x_ref, ya_ref, yb_ref, cq_ref, g0_ref, g1_ref, g2_ref, mk_ref, mv_ref,
                  wa_ref, wb_ref, wc_ref, wo_ref, gp_ref, o_ref):
    cq = cq_ref[...]
    heads = []
    for h in range(C_HEADS):
        sl = slice(h * C_HEAD_DIM, (h + 1) * C_HEAD_DIM)
        s = lax.dot_general(cq[:, sl], mk_ref[:, sl], _NT,
                            preferred_element_type=F32) * (C_HEAD_DIM ** -0.5)
        e = jnp.exp(s - jnp.max(s, axis=1, keepdims=True))
        p = e / jnp.sum(e, axis=1, keepdims=True)
        heads.append(jnp.dot(p.astype(BF16), mv_ref[:, sl], preferred_element_type=F32))
    yc_pre = jnp.concatenate(heads, axis=1).astype(BF16)
    ya = jnp.dot(ya_ref[...], wa_ref[...], preferred_element_type=F32)
    yb = jnp.dot(yb_ref[...], wb_ref[...], preferred_element_type=F32)
    yc = jnp.dot(yc_pre, wc_ref[...], preferred_element_type=F32)
    merged = (g0_ref[...].astype(F32) * ya + g1_ref[...].astype(F32) * yb
              + g2_ref[...].astype(F32) * yc)
    o = jnp.dot(merged.astype(BF16), wo_ref[...], preferred_element_type=F32)
    o_ref[...] = x_ref[...] + _rms(o, gp_ref[...])


def _merge(x2, ya2, yb2, proj2, mkv3, wa, wb, wc, wo, g_post, seq):
    n = x2.shape[0]
    tm = min(512, seq)
    per_batch = seq // tm
    n_mem = mkv3.shape[1]
    c_dim = C_HEADS * C_HEAD_DIM
    const = lambda t: (0, 0)
    return pl.pallas_call(
        _merge_kernel,
        name="merge",
        grid=(n // tm,),
        in_specs=[
            pl.BlockSpec((tm, D_MODEL), lambda t: (t, 0)),
            pl.BlockSpec((tm, 512), lambda t: (t, 0)),
            pl.BlockSpec((tm, 512), lambda t: (t, 0)),
            pl.BlockSpec((tm, 512), lambda t: (t, COL_CQ)),
            pl.BlockSpec((tm, D_MODEL), lambda t: (t, 0)),
            pl.BlockSpec((tm, D_MODEL), lambda t: (t, 1)),
            pl.BlockSpec((tm, D_MODEL), lambda t: (t, 2)),
            pl.BlockSpec((None, n_mem, c_dim), lambda t: (t // per_batch, 0, 0)),
            pl.BlockSpec((None, n_mem, c_dim), lambda t: (t // per_batch, 0, 1)),
            pl.BlockSpec(wa.shape, const),
            pl.BlockSpec(wb.shape, const),
            pl.BlockSpec(wc.shape, const),
            pl.BlockSpec(wo.shape, const),
            pl.BlockSpec((1, D_MODEL), const),
        ],
        out_specs=pl.BlockSpec((tm, D_MODEL), lambda t: (t, 0)),
        out_shape=jax.ShapeDtypeStruct((n, D_MODEL), F32),
        compiler_params=pltpu.CompilerParams(
            dimension_semantics=("arbitrary",), vmem_limit_bytes=VMEM_LIMIT),
    )(x2, ya2, yb2, proj2, proj2, proj2, proj2, mkv3, mkv3, wa, wb, wc, wo, g_post)


def _ffn_kernel(x_ref, gpre_ref, wg_ref, wu_ref, wo_ref, gpost_ref, o_ref, h_ref, acc_ref):
    k = pl.program_id(1)

    @pl.when(k == 0)
    def _():
        h_ref[...] = _rms(x_ref[...], gpre_ref[...]).astype(BF16)
        acc_ref[...] = jnp.zeros(acc_ref.shape, F32)

    h = h_ref[...]
    g = jnp.dot(h, wg_ref[...], preferred_element_type=F32)
    u = jnp.dot(h, wu_ref[...], preferred_element_type=F32)
    act = (g * jax.nn.sigmoid(g) * u).astype(BF16)
    acc_ref[...] += jnp.dot(act, wo_ref[...], preferred_element_type=F32)

    @pl.when(k == pl.num_programs(1) - 1)
    def _():
        o_ref[...] = x_ref[...] + _rms(acc_ref[...], gpost_ref[...])


def _ffn(x2, g_pre, wg, wu, wo, g_post):
    n = x2.shape[0]
    d_ff = wg.shape[1]
    tm = min(512, n)
    tf = d_ff // 2
    return pl.pallas_call(
        _ffn_kernel,
        name="ffn",
        grid=(n // tm, d_ff // tf),
        in_specs=[
            pl.BlockSpec((tm, D_MODEL), lambda t, k: (t, 0)),
            pl.BlockSpec((1, D_MODEL), lambda t, k: (0, 0)),
            pl.BlockSpec((D_MODEL, tf), lambda t, k: (0, k)),
            pl.BlockSpec((D_MODEL, tf), lambda t, k: (0, k)),
            pl.BlockSpec((tf, D_MODEL), lambda t, k: (k, 0)),
            pl.BlockSpec((1, D_MODEL), lambda t, k: (0, 0)),
        ],
        out_specs=pl.BlockSpec((tm, D_MODEL), lambda t, k: (t, 0)),
        out_shape=jax.ShapeDtypeStruct((n, D_MODEL), F32),
        scratch_shapes=[pltpu.VMEM((tm, D_MODEL), BF16), pltpu.VMEM((tm, D_MODEL), F32)],
        compiler_params=pltpu.CompilerParams(
            dimension_semantics=("arbitrary", "arbitrary"), vmem_limit_bytes=VMEM_LIMIT),
    )(x2, g_pre, wg, wu, wo, g_post)


def _pack_w_in(w, b_gate):
    sizes = (512, 512, 512, IDX_HEADS * 64, 64, IDX_HEADS, 512, 512, 512, 512,
             N_BRANCH * D_MODEL)
    aq, ak, av, iq, ik, iw, bq, bk, bv, cq, gates = jnp.split(w, np.cumsum(sizes)[:-1], axis=1)
    scale = HEAD_DIM ** -0.5
    scale2 = scale * LOG2E
    w_main = jnp.concatenate(
        [gates, aq * scale2, ak, iq * scale, bq * scale2, bk, bv, cq], axis=1).astype(BF16)
    z64 = jnp.zeros((D_MODEL, 64), F32)
    w_ik = jnp.concatenate([ik, z64, z64, ik], axis=1).astype(BF16)
    w_trans = jnp.concatenate(
        [av, iw, jnp.zeros((D_MODEL, N_TRANS - 512 - IDX_HEADS), F32)], axis=1).T.astype(BF16)
    b_main = jnp.concatenate([b_gate, jnp.zeros((N_MAIN - N_GATE,), F32)])[None, :]
    return w_main, w_ik, w_trans, b_main


def kernel(x, mem, rel_bias, g_mix_pre, w_in, b_gate, g_mem, w_mem_kv, w_up_a, w_up_b, w_up_c,
           w_out, g_mix_post, g_ffn_pre, w_ffn_in, w_ffn_out, g_ffn_post):
    bsz, seq, _ = x.shape
    n_mem = mem.shape[1]
    k_sel = min(TOPK_MAX, seq // 4)
    bias_tiles = _bias_tiles(rel_bias)
    x2 = x.reshape(bsz * seq, D_MODEL)
    for l in range(w_in.shape[0]):
        w_main, w_ik, w_trans, b_main = _pack_w_in(w_in[l], b_gate[l])
        proj2, ik2, avt, iwt = _project(x2, g_mix_pre[l][None, :], w_main, b_main, w_ik, w_trans)
        proj3 = proj2.reshape(bsz, seq, N_MAIN)
        mkv = _memkv(mem.reshape(bsz * n_mem, D_MODEL), g_mem[l][None, :],
                     w_mem_kv[l].astype(BF16))
        ya = _dsa(proj3, ik2.reshape(bsz, seq, N_IK), avt.reshape(bsz, seq // NK, 512, NK),
                  iwt, bias_tiles, k_sel)
        yb = _stick_breaking(proj3)
        x2 = _merge(x2, ya.reshape(bsz * seq, 512), yb.reshape(bsz * seq, 512), proj2,
                    mkv.reshape(bsz, n_mem, 2 * C_HEADS * C_HEAD_DIM),
                    w_up_a[l].astype(BF16), w_up_b[l].astype(BF16), w_up_c[l].astype(BF16),
                    w_out[l].astype(BF16), g_mix_post[l][None, :], seq)
        d_ff = w_ffn_out.shape[1]
        w_ffn = w_ffn_in[l].astype(BF16)
        x2 = _ffn(x2, g_ffn_pre[l][None, :], w_ffn[:, :d_ff], w_ffn[:, d_ff:],
                  w_ffn_out[l].astype(BF16), g_ffn_post[l][None, :])
    return x2.reshape(bsz, seq, D_MODEL)
```

```python
import functools

import numpy as np
import jax
import jax.numpy as jnp
from jax import lax
from jax.experimental import pallas as pl
from jax.experimental.pallas import tpu as pltpu

D_MODEL = 1024
CHUNK = 64
HEAD_DIM = 64
N_HEADS = 8
IDX_HEADS = 8
TOPK_MAX = 256
C_HEADS = 4
C_HEAD_DIM = 128
N_BRANCH = 3
REL_BUCKETS = 32
EPS = 1e-6

F32 = jnp.float32
BF16 = jnp.bfloat16
INT_MIN = -2 ** 31
NEG_BIG = -1e30
LOG2E = 1.4426950408889634

QB = 128
DQ = 256
NK = 256
N_PAIR = N_HEADS // 2
ONES_ROWS = 16

N_GATE = N_BRANCH * D_MODEL
COL_AQ, COL_AK, COL_IQ, COL_BQ, COL_BK, COL_BV, COL_CQ = range(N_GATE // 512, N_GATE // 512 + 7)
N_MAIN = N_GATE + 7 * 512
N_IK = 256
N_TRANS = 512 + 16

VMEM_LIMIT = 56 * 1024 * 1024

_NT = (((1,), (1,)), ((), ()))


def _rms(x, g):
    return x * lax.rsqrt(jnp.mean(x * x, axis=-1, keepdims=True) + EPS) * g


def _proj_kernel(x_ref, g_ref, w_ref, b_ref, wik_ref, wt_ref, o_ref, ik_ref, avt_ref, iwt_ref,
                 h_ref, *, n_gate_tiles):
    j = pl.program_id(1)

    @pl.when(j == 0)
    def _():
        hb = _rms(x_ref[...], g_ref[...]).astype(BF16)
        h_ref[...] = hb
        ik_ref[...] = jnp.dot(hb, wik_ref[...], preferred_element_type=F32).astype(BF16)
        tr = lax.dot_general(wt_ref[...], hb, _NT, preferred_element_type=F32)
        for c in range(avt_ref.shape[0]):
            avt_ref[c] = tr[:512, c * NK:(c + 1) * NK].astype(BF16)
        iwt_ref[...] = tr[512:512 + IDX_HEADS, :]

    acc = jnp.dot(h_ref[...], w_ref[...], preferred_element_type=F32)

    @pl.when(j >= n_gate_tiles)
    def _():
        o_ref[...] = acc.astype(BF16)

    @pl.when(j < n_gate_tiles)
    def _():
        o_ref[...] = (0.5 + 0.5 * jnp.tanh(0.5 * (acc + b_ref[...]))).astype(BF16)


def _project(x2, g, w_main, b_main, w_ik, w_trans):
    n = x2.shape[0]
    tm = min(2048, n)
    tn = 512
    grid = (n // tm, N_MAIN // tn)
    return pl.pallas_call(
        functools.partial(_proj_kernel, n_gate_tiles=N_GATE // tn),
        name="in_proj",
        grid=grid,
        in_specs=[
            pl.BlockSpec((tm, D_MODEL), lambda i, j: (i, 0)),
            pl.BlockSpec((1, D_MODEL), lambda i, j: (0, 0)),
            pl.BlockSpec((D_MODEL, tn), lambda i, j: (0, j)),
            pl.BlockSpec((1, tn), lambda i, j: (0, j)),
            pl.BlockSpec((D_MODEL, N_IK), lambda i, j: (0, 0)),
            pl.BlockSpec((N_TRANS, D_MODEL), lambda i, j: (0, 0)),
        ],
        out_specs=[
            pl.BlockSpec((tm, tn), lambda i, j: (i, j)),
            pl.BlockSpec((tm, N_IK), lambda i, j: (i, 0)),
            pl.BlockSpec((tm // NK, 512, NK), lambda i, j: (i, 0, 0)),
            pl.BlockSpec((IDX_HEADS, tm), lambda i, j: (0, i)),
        ],
        out_shape=[
            jax.ShapeDtypeStruct((n, N_MAIN), BF16),
            jax.ShapeDtypeStruct((n, N_IK), BF16),
            jax.ShapeDtypeStruct((n // NK, 512, NK), BF16),
            jax.ShapeDtypeStruct((IDX_HEADS, n), F32),
        ],
        scratch_shapes=[pltpu.VMEM((tm, D_MODEL), BF16)],
        compiler_params=pltpu.CompilerParams(
            dimension_semantics=("arbitrary", "arbitrary"), vmem_limit_bytes=VMEM_LIMIT),
    )(x2, g, w_main, b_main, w_ik, w_trans)


def _memkv_kernel(x_ref, g_ref, w_ref, o_ref):
    hb = _rms(x_ref[...], g_ref[...]).astype(BF16)
    o_ref[...] = jnp.dot(hb, w_ref[...], preferred_element_type=F32).astype(BF16)


def _memkv(mem2, g, w):
    n = mem2.shape[0]
    tm = min(512, n)
    return pl.pallas_call(
        _memkv_kernel,
        name="mem_kv",
        grid=(n // tm,),
        in_specs=[
            pl.BlockSpec((tm, D_MODEL), lambda i: (i, 0)),
            pl.BlockSpec((1, D_MODEL), lambda i: (0, 0)),
            pl.BlockSpec((D_MODEL, w.shape[1]), lambda i: (0, 0)),
        ],
        out_specs=pl.BlockSpec((tm, w.shape[1]), lambda i: (i, 0)),
        out_shape=jax.ShapeDtypeStruct((n, w.shape[1]), BF16),
        compiler_params=pltpu.CompilerParams(
            dimension_semantics=("arbitrary",), vmem_limit_bytes=VMEM_LIMIT),
    )(mem2, g, w)


BIAS_OFFSETS = (0, -NK)
_LOG_BUCKET_STARTS = (12, 16, 23, 32, 46, 64, 91)
FAR_BUCKET = 15
assert DQ == NK


def _bias_kernel(rb_ref, o_ref):
    key = lax.broadcasted_iota(jnp.int32, (NK, DQ), 0)
    qry = lax.broadcasted_iota(jnp.int32, (NK, DQ), 1)
    for c, off in enumerate(BIAS_OFFSETS):
        rel = key - qry + off
        n = jnp.abs(rel)
        large = jnp.full((NK, DQ), 8, jnp.int32)
        for start in _LOG_BUCKET_STARTS:
            large = large + jnp.where(n >= start, 1, 0)
        bucket = jnp.where(rel > 0, REL_BUCKETS // 2, 0) + jnp.where(n < 8, n, large)
        for h in range(N_HEADS):
            val = jnp.full((NK, DQ), rb_ref[0, h], F32)
            for b in range(1, REL_BUCKETS):
                val = jnp.where(bucket == b, rb_ref[b, h], val)
            o_ref[c, h] = (val - rb_ref[FAR_BUCKET, h]) * LOG2E


def _bias_tiles(rel_bias):
    return pl.pallas_call(
        _bias_kernel,
        name="rel_bias_tiles",
        in_specs=[pl.BlockSpec(memory_space=pltpu.SMEM)],
        out_specs=pl.BlockSpec(memory_space=pltpu.VMEM),
        out_shape=jax.ShapeDtypeStruct((len(BIAS_OFFSETS), N_HEADS, NK, DQ), F32),
    )(rel_bias)


def _split_heads_into(qm_ref, q):
    lane = lax.broadcasted_iota(jnp.int32, (QB, 128), 1)
    for p in range(N_PAIR):
        qp = q[:, p * 128:(p + 1) * 128].astype(F32)
        qm_ref[p, :QB, :] = jnp.where(lane < HEAD_DIM, qp, 0.0).astype(BF16)
        qm_ref[p, QB:, :] = jnp.where(lane >= HEAD_DIM, qp, 0.0).astype(BF16)


def _two_stage_pipeline(n, first, second):
    first(0, 0)

    def two(u, carry):
        j = 2 * u
        first(j + 1, 1)
        second(j, 0, False)
        first(j + 2, 0)
        second(j + 1, 1, False)
        return carry

    lax.fori_loop(0, (n - 1) // 2, two, 0)

    @pl.when(n % 2 == 0)
    def _():
        first(n - 1, 1)
        second(n - 2, 0, False)
        second(n - 1, 1, True)

    @pl.when(n % 2 == 1)
    def _():
        second(n - 1, 0, True)


def _merge_pair(o_even, o_odd):
    lane = lax.broadcasted_iota(jnp.int32, (QB, 128), 1)
    return jnp.where(lane < HEAD_DIM, o_even, o_odd)


CNT_BLOCKS = 4
KEY_ROWS = 32
assert NK == 8 * 32


def _bit_transpose32(load_row, tmp_ref, store_row):
    def swap(a, b, j, m):
        t = (a ^ lax.shift_right_logical(b, jnp.int32(j))) & m
        return a ^ t, b ^ (t << j)

    lower = []
    for k in range(16):
        a, b = swap(load_row(k), load_row(k + 16), 16, 0x0000FFFF)
        lower.append(a)
        tmp_ref[k] = b
    for base in (0, 16):
        x = lower if base == 0 else [tmp_ref[k] for k in range(16)]
        j, m = 8, 0x00FF00FF
        while j:
            k = 0
            while k < 16:
                x[k], x[k + j] = swap(x[k], x[k + j], j, m)
                k = (k + j + 1) & ~j
            j >>= 1
            m ^= m << j
        for i in range(16):
            store_row(base + i, x[i])


def _dsa_kernel(aq_ref, iq_ref, iwt_ref, ak_ref, avt_ref, ik_ref, bias_ref, o_ref,
                key_ref, plane_ref, alive_ref, qm_ref, m_ref, acc_ref,
                s0_ref, cm0_ref, s1_ref, cm1_ref, raw0_ref, raw1_ref, tmp_ref, *,
                k_sel):
    diag = pl.program_id(1)
    nkb = diag + 1
    krow = lax.broadcasted_iota(jnp.int32, (KEY_ROWS, DQ), 0)
    qcol = lax.broadcasted_iota(jnp.int32, (KEY_ROWS, DQ), 1)
    qchunk = (diag * DQ + qcol) // CHUNK

    iwt = iwt_ref[...] * (IDX_HEADS ** -0.5)

    def dots_stage(jb, slot):
        raw_ref = (raw0_ref, raw1_ref)[slot]
        k0 = pl.multiple_of(jb * NK, NK)
        for h in range(IDX_HEADS):
            ikh = ik_ref[pl.ds(k0, NK), (h % 2) * 128:(h % 2 + 1) * 128]
            iqp = iq_ref[:, (h // 2) * 128:(h // 2 + 1) * 128]
            raw_ref[h] = lax.dot_general(ikh, iqp, _NT, preferred_element_type=F32)

    def keys_stage(jb, slot, last):
        raw_ref = (raw0_ref, raw1_ref)[slot]
        k0 = pl.multiple_of(jb * NK, NK)
        for c in range(NK // KEY_ROWS):
            rows = slice(c * KEY_ROWS, (c + 1) * KEY_ROWS)
            acc = jnp.zeros((KEY_ROWS, DQ), F32)
            for h in range(IDX_HEADS):
                acc = acc + iwt[h:h + 1, :] * jnp.maximum(raw_ref[h, rows, :], 0.0)
            bits = lax.bitcast_convert_type(acc, jnp.int32)
            key = bits ^ ((bits >> 31) & 0x7FFFFFFF)
            if last:
                admissible = ((k0 + c * KEY_ROWS + krow) // CHUNK) <= qchunk
                key = jnp.where(admissible, key, INT_MIN)
            key_ref[jb, rows, :] = key
        for half in range(DQ // 128):
            lanes = slice(half * 128, (half + 1) * 128)

            def load_row(r):
                return key_ref[jb, 8 * r:8 * r + 8, lanes] ^ INT_MIN

            def store_plane(i, v):
                plane_ref[31 - i, jb, :, lanes] = v

            _bit_transpose32(load_row, tmp_ref, store_plane)
        plane_ref[32, jb] = jnp.full((8, DQ), -1, jnp.int32)
        alive_ref[jb] = jnp.full((8, DQ), -1, jnp.int32)

    _two_stage_pipeline(nkb, dots_stage, keys_stage)

    n_groups = (nkb + CNT_BLOCKS - 1) // CNT_BLOCKS

    def pad_block(jb, carry):
        for b in range(33):
            plane_ref[b, jb] = jnp.zeros((8, DQ), jnp.int32)
        alive_ref[jb] = jnp.zeros((8, DQ), jnp.int32)
        return carry

    lax.fori_loop(nkb, n_groups * CNT_BLOCKS, pad_block, 0)

    def select_pass(it, state):
        took_prev, n_above, thr_u = state
        b = 31 - it
        take_prev = took_prev != 0

        def body(g, cnts):
            cnts = list(cnts)
            for u in range(CNT_BLOCKS):
                jb = g * CNT_BLOCKS + u
                alive = alive_ref[jb]
                with_prev = alive & plane_ref[b + 1, jb]
                alive = jnp.where(take_prev, with_prev, alive ^ with_prev)
                alive_ref[jb] = alive
                cnts[u] = cnts[u] + lax.population_count(alive & plane_ref[b, jb])
            return tuple(cnts)

        zeros = jnp.zeros((8, DQ), jnp.int32)
        cnts = lax.fori_loop(0, n_groups, body, (zeros,) * CNT_BLOCKS)
        n_one = jnp.sum(sum(cnts[1:], cnts[0]), axis=0, keepdims=True)
        take = (n_above + n_one) >= k_sel
        n_above = jnp.where(take, n_above, n_above + n_one)
        thr_u = jnp.where(take, thr_u | jnp.left_shift(jnp.int32(1), b), thr_u)
        return take.astype(jnp.int32), n_above, thr_u

    row0 = jnp.zeros((1, DQ), jnp.int32)
    _, _, thr_u = lax.fori_loop(0, 32, select_pass, (row0 + 1, row0, row0))
    thr = thr_u ^ INT_MIN
    thr = jnp.maximum(thr, INT_MIN + 1)

    lane = lax.broadcasted_iota(jnp.int32, (DQ, 128), 1)
    for p in range(N_PAIR):
        qp = aq_ref[:, p * 128:(p + 1) * 128].astype(F32)
        qm_ref[2 * p] = jnp.where(lane < HEAD_DIM, qp, 0.0).astype(BF16)
        qm_ref[2 * p + 1] = jnp.where(lane >= HEAD_DIM, qp, 0.0).astype(BF16)
    m_ref[...] = jnp.full(m_ref.shape, NEG_BIG, F32)
    acc_ref[...] = jnp.zeros(acc_ref.shape, F32)
    ones = jnp.ones((ONES_ROWS, NK), BF16)

    slots = ((s0_ref, cm0_ref), (s1_ref, cm1_ref))

    def logits_stage(jb, slot, bias_idx):
        s_ref, cm_ref = slots[slot]
        k0 = pl.multiple_of(jb * NK, NK)
        mask = jnp.where(key_ref[jb] >= thr, 0.0, NEG_BIG).astype(BF16)
        for h in range(N_HEADS):
            kp = ak_ref[pl.ds(k0, NK), (h // 2) * 128:(h // 2 + 1) * 128]
            s = lax.dot_general(kp, qm_ref[h], _NT, preferred_element_type=F32)
            if bias_idx is not None:
                s = s + bias_ref[bias_idx, h]
            sb = s.astype(BF16) + mask
            s_ref[h] = sb
            cm_ref[h] = jnp.max(sb, axis=0, keepdims=True).astype(F32)

    def softmax_stage(jb, slot):
        s_ref, cm_ref = slots[slot]
        for h in range(N_HEADS):
            m_prev = m_ref[h]
            m_new = jnp.maximum(m_prev, cm_ref[h])
            alpha = jnp.exp2(m_prev - m_new)
            pe = jnp.exp2(s_ref[h] - m_new.astype(BF16))
            vt = jnp.concatenate([avt_ref[jb, h * HEAD_DIM:(h + 1) * HEAD_DIM, :], ones], axis=0)
            acc_ref[h] = alpha * acc_ref[h] + jnp.dot(vt, pe, preferred_element_type=F32)
            m_ref[h] = m_new

    odd_diag = diag % 2
    logits_stage(diag, 0, 0)
    logits_stage(jnp.maximum(diag - 1, 0), 1, 1)
    softmax_stage(diag, 0)
    n_steps = diag - 1

    def two_steps(u, carry):
        b = diag - 1 - 2 * u
        logits_stage(b - 1, 0, None)
        softmax_stage(b, 1)
        logits_stage(b - 2, 1, None)
        softmax_stage(b - 1, 0)
        return carry

    lax.fori_loop(0, jnp.maximum(n_steps, 0) // 2, two_steps, 0)

    @pl.when(jnp.logical_and(n_steps >= 1, n_steps % 2 == 1))
    def _():
        logits_stage(0, 0, None)
        softmax_stage(1, 1)

    @pl.when(jnp.logical_and(diag >= 1, odd_diag == 1))
    def _():
        softmax_stage(0, 1)

    @pl.when(jnp.logical_and(diag >= 1, odd_diag == 0))
    def _():
        softmax_stage(0, 0)

    for p in range(N_PAIR):
        halves = []
        for h in (2 * p, 2 * p + 1):
            a = acc_ref[h]
            halves.append(a[:HEAD_DIM, :] / a[HEAD_DIM:HEAD_DIM + 1, :])
        o_ref[:, p * 128:(p + 1) * 128] = jnp.concatenate(halves, axis=0).T.astype(BF16)


def _dsa(proj3, ik3, avt4, iwt, bias_tiles, k_sel):
    bsz, seq, _ = proj3.shape
    nq = seq // DQ
    assert seq % (NK * CNT_BLOCKS) == 0
    resident = dict(pipeline_mode=pl.Buffered(1))
    return pl.pallas_call(
        functools.partial(_dsa_kernel, k_sel=k_sel),
        name="dsa",
        grid=(bsz, nq),
        in_specs=[
            pl.BlockSpec((None, DQ, 512), lambda b, i: (b, i, COL_AQ)),
            pl.BlockSpec((None, DQ, 512), lambda b, i: (b, i, COL_IQ)),
            pl.BlockSpec((IDX_HEADS, DQ), lambda b, i: (0, b * nq + i)),
            pl.BlockSpec((None, seq, 512), lambda b, i: (b, 0, COL_AK), **resident),
            pl.BlockSpec((None, seq // NK, 512, NK), lambda b, i: (b, 0, 0, 0), **resident),
            pl.BlockSpec((None, seq, N_IK), lambda b, i: (b, 0, 0), **resident),
            pl.BlockSpec(bias_tiles.shape, lambda b, i: (0, 0, 0, 0), **resident),
        ],
        out_specs=pl.BlockSpec((None, DQ, 512), lambda b, i: (b, i, 0)),
        out_shape=jax.ShapeDtypeStruct((bsz, seq, 512), BF16),
        scratch_shapes=[
            pltpu.VMEM((seq // NK, NK, DQ), jnp.int32),
            pltpu.VMEM((33, seq // NK, 8, DQ), jnp.int32),
            pltpu.VMEM((seq // NK, 8, DQ), jnp.int32),
            pltpu.VMEM((N_HEADS, DQ, 128), BF16),
            pltpu.VMEM((N_HEADS, 1, DQ), F32),
            pltpu.VMEM((N_HEADS, HEAD_DIM + ONES_ROWS, DQ), F32),
            pltpu.VMEM((N_HEADS, NK, DQ), BF16),
            pltpu.VMEM((N_HEADS, 1, DQ), F32),
            pltpu.VMEM((N_HEADS, NK, DQ), BF16),
            pltpu.VMEM((N_HEADS, 1, DQ), F32),
            pltpu.VMEM((IDX_HEADS, NK, DQ), F32),
            pltpu.VMEM((IDX_HEADS, NK, DQ), F32),
            pltpu.VMEM((16, 8, 128), jnp.int32),
        ],
        compiler_params=pltpu.CompilerParams(
            dimension_semantics=("arbitrary", "arbitrary"), vmem_limit_bytes=VMEM_LIMIT),
    )(proj3, proj3, iwt, proj3, avt4, ik3, bias_tiles)


SB_DEAD_MASS = 104.0 * LOG2E


def _sb_kernel(q_ref, k_ref, v_ref, o_ref, qm_ref, uu_ref, carry_ref, acc_ref, z_ref, sp_ref,
               later_ref):
    i = pl.program_id(1)
    diag = i // 2

    @pl.when(jnp.logical_and(pl.program_id(0) == 0, i == 0))
    def _():
        kr = lax.broadcasted_iota(jnp.int32, (2 * NK, NK), 0) % NK
        kc = lax.broadcasted_iota(jnp.int32, (2 * NK, NK), 1)
        uu_ref[...] = jnp.where(kr > kc, 1.0, 0.0).astype(BF16)

    _split_heads_into(qm_ref, q_ref[...])
    carry_ref[...] = jnp.zeros(carry_ref.shape, F32)
    acc_ref[...] = jnp.zeros(acc_ref.shape, F32)

    def block(jb, on_diagonal):
        k0 = pl.multiple_of(jb * NK, NK)
        if on_diagonal:
            row = lax.broadcasted_iota(jnp.int32, (2 * QB, NK), 0)
            col = lax.broadcasted_iota(jnp.int32, (2 * QB, NK), 1)
            causal = (k0 + col) < (i * QB + row % QB)
        for p in range(N_PAIR):
            kp = k_ref[pl.ds(k0, NK), p * 128:(p + 1) * 128]
            z_ref[p] = lax.dot_general(qm_ref[p], kp, _NT, preferred_element_type=F32)
        for p in range(N_PAIR):
            z = z_ref[p]
            neg_abs = lax.bitcast_convert_type(
                lax.bitcast_convert_type(z, jnp.int32) | INT_MIN, F32)
            sp = jnp.maximum(z, 0.0) + jnp.log(1.0 + jnp.exp2(neg_abs)) * LOG2E
            if on_diagonal:
                sp = jnp.where(causal, sp, 0.0)
            sp_ref[p] = sp
            hi = sp.astype(BF16)
            lo = (sp - hi.astype(F32)).astype(BF16)
            later_ref[p] = jnp.dot(jnp.concatenate([hi, lo], axis=1), uu_ref[...],
                                   preferred_element_type=F32)
        for p in range(N_PAIR):
            vp = v_ref[pl.ds(k0, NK), p * 128:(p + 1) * 128]
            carry = carry_ref[p]
            sp = sp_ref[p]
            a = jnp.exp2(z_ref[p] - sp - later_ref[p] - carry)
            if on_diagonal:
                a = jnp.where(causal, a, 0.0)
            acc_ref[p] += jnp.dot(a.astype(BF16), vp, preferred_element_type=F32)
            carry_ref[p] = carry + jnp.sum(sp, axis=1, keepdims=True)

    block(diag, True)

    def alive():
        return (jnp.min(carry_ref[...]) <= SB_DEAD_MASS).astype(jnp.int32)

    def cond(state):
        jb, go = state
        return jnp.logical_and(jb >= 0, go > 0)

    def body(state):
        jb, _ = state
        block(jb, False)
        return jb - 1, alive()

    lax.while_loop(cond, body, (diag - 1, alive()))

    for p in range(N_PAIR):
        o_ref[:, p * 128:(p + 1) * 128] = _merge_pair(
            acc_ref[p, :QB, :], acc_ref[p, QB:, :]).astype(BF16)


def _stick_breaking(proj3):
    bsz, seq, _ = proj3.shape
    resident = dict(pipeline_mode=pl.Buffered(1))
    return pl.pallas_call(
        _sb_kernel,
        name="stick_breaking",
        grid=(bsz, seq // QB),
        in_specs=[
            pl.BlockSpec((None, QB, 512), lambda b, i: (b, i, COL_BQ)),
            pl.BlockSpec((None, seq, 512), lambda b, i: (b, 0, COL_BK), **resident),
            pl.BlockSpec((None, seq, 512), lambda b, i: (b, 0, COL_BV), **resident),
        ],
        out_specs=pl.BlockSpec((None, QB, 512), lambda b, i: (b, i, 0)),
        out_shape=jax.ShapeDtypeStruct((bsz, seq, 512), BF16),
        scratch_shapes=[
            pltpu.VMEM((N_PAIR, 2 * QB, 128), BF16),
            pltpu.VMEM((2 * NK, NK), BF16),
            pltpu.VMEM((N_PAIR, 2 * QB, 1), F32),
            pltpu.VMEM((N_PAIR, 2 * QB, 128), F32),
            pltpu.VMEM((N_PAIR, 2 * QB, NK), F32),
            pltpu.VMEM((N_PAIR, 2 * QB, NK), F32),
            pltpu.VMEM((N_PAIR, 2 * QB, NK), F32),
        ],
        compiler_params=pltpu.CompilerParams(
            dimension_semantics=("arbitrary", "arbitrary"), vmem_limit_bytes=VMEM_LIMIT),
    )(proj3, proj3, proj3)


def _merge_kernel(x_ref, ya_ref, yb_ref, cq_ref, g0_ref, g1_ref, g2_ref, mk_ref, mv_ref,
                  wa_ref, wb_ref, wc_ref, wo_ref, gp_ref, o_ref):
    cq = cq_ref[...]
    heads = []
    for h in range(C_HEADS):
        sl = slice(h * C_HEAD_DIM, (h + 1) * C_HEAD_DIM)
        s = lax.dot_general(cq[:, sl], mk_ref[:, sl], _NT,
                            preferred_element_type=F32) * (C_HEAD_DIM ** -0.5)
        e = jnp.exp(s - jnp.max(s, axis=1, keepdims=True))
        p = e / jnp.sum(e, axis=1, keepdims=True)
        heads.append(jnp.dot(p.astype(BF16), mv_ref[:, sl], preferred_element_type=F32))
    yc_pre = jnp.concatenate(heads, axis=1).astype(BF16)
    ya = jnp.dot(ya_ref[...], wa_ref[...], preferred_element_type=F32)
    yb = jnp.dot(yb_ref[...], wb_ref[...], preferred_element_type=F32)
    yc = jnp.dot(yc_pre, wc_ref[...], preferred_element_type=F32)
    merged = (g0_ref[...].astype(F32) * ya + g1_ref[...].astype(F32) * yb
              + g2_ref[...].astype(F32) * yc)
    o = jnp.dot(merged.astype(BF16), wo_ref[...], preferred_element_type=F32)
    o_ref[...] = x_ref[...] + _rms(o, gp_ref[...])


def _merge(x2, ya2, yb2, proj2, mkv3, wa, wb, wc, wo, g_post, seq):
    n = x2.shape[0]
    tm = min(512, seq)
    per_batch = seq // tm
    n_mem = mkv3.shape[1]
    c_dim = C_HEADS * C_HEAD_DIM
    const = lambda t: (0, 0)
    return pl.pallas_call(
        _merge_kernel,
        name="merge",
        grid=(n // tm,),
        in_specs=[
            pl.BlockSpec((tm, D_MODEL), lambda t: (t, 0)),
            pl.BlockSpec((tm, 512), lambda t: (t, 0)),
            pl.BlockSpec((tm, 512), lambda t: (t, 0)),
            pl.BlockSpec((tm, 512), lambda t: (t, COL_CQ)),
            pl.BlockSpec((tm, D_MODEL), lambda t: (t, 0)),
            pl.BlockSpec((tm, D_MODEL), lambda t: (t, 1)),
            pl.BlockSpec((tm, D_MODEL), lambda t: (t, 2)),
            pl.BlockSpec((None, n_mem, c_dim), lambda t: (t // per_batch, 0, 0)),
            pl.BlockSpec((None, n_mem, c_dim), lambda t: (t // per_batch, 0, 1)),
            pl.BlockSpec(wa.shape, const),
            pl.BlockSpec(wb.shape, const),
            pl.BlockSpec(wc.shape, const),
            pl.BlockSpec(wo.shape, const),
            pl.BlockSpec((1, D_MODEL), const),
        ],
        out_specs=pl.BlockSpec((tm, D_MODEL), lambda t: (t, 0)),
        out_shape=jax.ShapeDtypeStruct((n, D_MODEL), F32),
        compiler_params=pltpu.CompilerParams(
            dimension_semantics=("arbitrary",), vmem_limit_bytes=VMEM_LIMIT),
    )(x2, ya2, yb2, proj2, proj2, proj2, proj2, mkv3, mkv3, wa, wb, wc, wo, g_post)


def _ffn_kernel(x_ref, gpre_ref, wg_ref, wu_ref, wo_ref, gpost_ref, o_ref, h_ref, acc_ref):
    k = pl.program_id(1)

    @pl.when(k == 0)
    def _():
        h_ref[...] = _rms(x_ref[...], gpre_ref[...]).astype(BF16)
        acc_ref[...] = jnp.zeros(acc_ref.shape, F32)

    h = h_ref[...]
    g = jnp.dot(h, wg_ref[...], preferred_element_type=F32)
    u = jnp.dot(h, wu_ref[...], preferred_element_type=F32)
    act = (g * jax.nn.sigmoid(g) * u).astype(BF16)
    acc_ref[...] += jnp.dot(act, wo_ref[...], preferred_element_type=F32)

    @pl.when(k == pl.num_programs(1) - 1)
    def _():
        o_ref[...] = x_ref[...] + _rms(acc_ref[...], gpost_ref[...])


def _ffn(x2, g_pre, wg, wu, wo, g_post):
    n = x2.shape[0]
    d_ff = wg.shape[1]
    tm = min(512, n)
    tf = d_ff // 2
    return pl.pallas_call(
        _ffn_kernel,
        name="ffn",
        grid=(n // tm, d_ff // tf),
        in_specs=[
            pl.BlockSpec((tm, D_MODEL), lambda t, k: (t, 0)),
            pl.BlockSpec((1, D_MODEL), lambda t, k: (0, 0)),
            pl.BlockSpec((D_MODEL, tf), lambda t, k: (0, k)),
            pl.BlockSpec((D_MODEL, tf), lambda t, k: (0, k)),
            pl.BlockSpec((tf, D_MODEL), lambda t, k: (k, 0)),
            pl.BlockSpec((1, D_MODEL), lambda t, k: (0, 0)),
        ],
        out_specs=pl.BlockSpec((tm, D_MODEL), lambda t, k: (t, 0)),
        out_shape=jax.ShapeDtypeStruct((n, D_MODEL), F32),
        scratch_shapes=[pltpu.VMEM((tm, D_MODEL), BF16), pltpu.VMEM((tm, D_MODEL), F32)],
        compiler_params=pltpu.CompilerParams(
            dimension_semantics=("arbitrary", "arbitrary"), vmem_limit_bytes=VMEM_LIMIT),
    )(x2, g_pre, wg, wu, wo, g_post)


def _pack_w_in(w, b_gate):
    sizes = (512, 512, 512, IDX_HEADS * 64, 64, IDX_HEADS, 512, 512, 512, 512,
             N_BRANCH * D_MODEL)
    aq, ak, av, iq, ik, iw, bq, bk, bv, cq, gates = jnp.split(w, np.cumsum(sizes)[:-1], axis=1)
    scale = HEAD_DIM ** -0.5
    scale2 = scale * LOG2E
    w_main = jnp.concatenate(
        [gates, aq * scale2, ak, iq * scale, bq * scale2, bk, bv, cq], axis=1).astype(BF16)
    z64 = jnp.zeros((D_MODEL, 64), F32)
    w_ik = jnp.concatenate([ik, z64, z64, ik], axis=1).astype(BF16)
    w_trans = jnp.concatenate(
        [av, iw, jnp.zeros((D_MODEL, N_TRANS - 512 - IDX_HEADS), F32)], axis=1).T.astype(BF16)
    b_main = jnp.concatenate([b_gate, jnp.zeros((N_MAIN - N_GATE,), F32)])[None, :]
    return w_main, w_ik, w_trans, b_main


def kernel(x, mem, rel_bias, g_mix_pre, w_in, b_gate, g_mem, w_mem_kv, w_up_a, w_up_b, w_up_c,
           w_out, g_mix_post, g_ffn_pre, w_ffn_in, w_ffn_out, g_ffn_post):
    bsz, seq, _ = x.shape
    n_mem = mem.shape[1]
    k_sel = min(TOPK_MAX, seq // 4)
    bias_tiles = _bias_tiles(rel_bias)
    x2 = x.reshape(bsz * seq, D_MODEL)
    for l in range(w_in.shape[0]):
        w_main, w_ik, w_trans, b_main = _pack_w_in(w_in[l], b_gate[l])
        proj2, ik2, avt, iwt = _project(x2, g_mix_pre[l][None, :], w_main, b_main, w_ik, w_trans)
        proj3 = proj2.reshape(bsz, seq, N_MAIN)
        mkv = _memkv(mem.reshape(bsz * n_mem, D_MODEL), g_mem[l][None, :],
                     w_mem_kv[l].astype(BF16))
        ya = _dsa(proj3, ik2.reshape(bsz, seq, N_IK), avt.reshape(bsz, seq // NK, 512, NK),
                  iwt, bias_tiles, k_sel)
        yb = _stick_breaking(proj3)
        x2 = _merge(x2, ya.reshape(bsz * seq, 512), yb.reshape(bsz * seq, 512), proj2,
                    mkv.reshape(bsz, n_mem, 2 * C_HEADS * C_HEAD_DIM),
                    w_up_a[l].astype(BF16), w_up_b[l].astype(BF16), w_up_c[l].astype(BF16),
                    w_out[l].astype(BF16), g_mix_post[l][None, :], seq)
        d_ff = w_ffn_out.shape[1]
        w_ffn = w_ffn_in[l].astype(BF16)
        x2 = _ffn(x2, g_ffn_pre[l][None, :], w_ffn[:, :d_ff], w_ffn[:, d_ff:],
                  w_ffn_out[l].astype(BF16), g_ffn_post[l][None, :])
    return x2.reshape(bsz, seq, D_MODEL)
```

```python
import functools

import numpy as np
import jax
import jax.numpy as jnp
from jax import lax
from jax.experimental import pallas as pl
from jax.experimental.pallas import tpu as pltpu

D_MODEL = 1024
CHUNK = 64
HEAD_DIM = 64
N_HEADS = 8
IDX_HEADS = 8
TOPK_MAX = 256
C_HEADS = 4
C_HEAD_DIM = 128
N_BRANCH = 3
REL_BUCKETS = 32
EPS = 1e-6

F32 = jnp.float32
BF16 = jnp.bfloat16
INT_MIN = -2 ** 31
NEG_BIG = -1e30
LOG2E = 1.4426950408889634

QB = 256
DQ = 256
NK = 256
N_PAIR = N_HEADS // 2
ONES_ROWS = 16

N_GATE = N_BRANCH * D_MODEL
COL_AQ, COL_AK, COL_IQ, COL_BQ, COL_BK, COL_BV, COL_CQ = range(N_GATE // 512, N_GATE // 512 + 7)
N_MAIN = N_GATE + 7 * 512
N_IK = 256
N_TRANS = 512 + 16

VMEM_LIMIT = 56 * 1024 * 1024

_NT = (((1,), (1,)), ((), ()))


def _rms(x, g):
    return x * lax.rsqrt(jnp.mean(x * x, axis=-1, keepdims=True) + EPS) * g


def _proj_kernel(x_ref, g_ref, w_ref, b_ref, wik_ref, wt_ref, o_ref, ik_ref, avt_ref, iwt_ref,
                 h_ref, *, n_gate_tiles):
    j = pl.program_id(1)

    @pl.when(j == 0)
    def _():
        hb = _rms(x_ref[...], g_ref[...]).astype(BF16)
        h_ref[...] = hb
        ik_ref[...] = jnp.dot(hb, wik_ref[...], preferred_element_type=F32).astype(BF16)
        tr = lax.dot_general(wt_ref[...], hb, _NT, preferred_element_type=F32)
        for c in range(avt_ref.shape[0]):
            avt_ref[c] = tr[:512, c * NK:(c + 1) * NK].astype(BF16)
        iwt_ref[...] = tr[512:512 + IDX_HEADS, :]

    acc = jnp.dot(h_ref[...], w_ref[...], preferred_element_type=F32)

    @pl.when(j >= n_gate_tiles)
    def _():
        o_ref[...] = acc.astype(BF16)

    @pl.when(j < n_gate_tiles)
    def _():
        o_ref[...] = (0.5 + 0.5 * jnp.tanh(0.5 * (acc + b_ref[...]))).astype(BF16)


def _project(x2, g, w_main, b_main, w_ik, w_trans):
    n = x2.shape[0]
    tm = min(2048, n)
    tn = 512
    grid = (n // tm, N_MAIN // tn)
    return pl.pallas_call(
        functools.partial(_proj_kernel, n_gate_tiles=N_GATE // tn),
        name="in_proj",
        grid=grid,
        in_specs=[
            pl.BlockSpec((tm, D_MODEL), lambda i, j: (i, 0)),
            pl.BlockSpec((1, D_MODEL), lambda i, j: (0, 0)),
            pl.BlockSpec((D_MODEL, tn), lambda i, j: (0, j)),
            pl.BlockSpec((1, tn), lambda i, j: (0, j)),
            pl.BlockSpec((D_MODEL, N_IK), lambda i, j: (0, 0)),
            pl.BlockSpec((N_TRANS, D_MODEL), lambda i, j: (0, 0)),
        ],
        out_specs=[
            pl.BlockSpec((tm, tn), lambda i, j: (i, j)),
            pl.BlockSpec((tm, N_IK), lambda i, j: (i, 0)),
            pl.BlockSpec((tm // NK, 512, NK), lambda i, j: (i, 0, 0)),
            pl.BlockSpec((IDX_HEADS, tm), lambda i, j: (0, i)),
        ],
        out_shape=[
            jax.ShapeDtypeStruct((n, N_MAIN), BF16),
            jax.ShapeDtypeStruct((n, N_IK), BF16),
            jax.ShapeDtypeStruct((n // NK, 512, NK), BF16),
            jax.ShapeDtypeStruct((IDX_HEADS, n), F32),
        ],
        scratch_shapes=[pltpu.VMEM((tm, D_MODEL), BF16)],
        compiler_params=pltpu.CompilerParams(
            dimension_semantics=("arbitrary", "arbitrary"), vmem_limit_bytes=VMEM_LIMIT),
    )(x2, g, w_main, b_main, w_ik, w_trans)


def _memkv_kernel(x_ref, g_ref, w_ref, o_ref):
    hb = _rms(x_ref[...], g_ref[...]).astype(BF16)
    o_ref[...] = jnp.dot(hb, w_ref[...], preferred_element_type=F32).astype(BF16)


def _memkv(mem2, g, w):
    n = mem2.shape[0]
    tm = min(512, n)
    return pl.pallas_call(
        _memkv_kernel,
        name="mem_kv",
        grid=(n // tm,),
        in_specs=[
            pl.BlockSpec((tm, D_MODEL), lambda i: (i, 0)),
            pl.BlockSpec((1, D_MODEL), lambda i: (0, 0)),
            pl.BlockSpec((D_MODEL, w.shape[1]), lambda i: (0, 0)),
        ],
        out_specs=pl.BlockSpec((tm, w.shape[1]), lambda i: (i, 0)),
        out_shape=jax.ShapeDtypeStruct((n, w.shape[1]), BF16),
        compiler_params=pltpu.CompilerParams(
            dimension_semantics=("arbitrary",), vmem_limit_bytes=VMEM_LIMIT),
    )(mem2, g, w)


BIAS_OFFSETS = (0, -NK)
_LOG_BUCKET_STARTS = (12, 16, 23, 32, 46, 64, 91)
FAR_BUCKET = 15
assert DQ == NK


def _bias_kernel(rb_ref, o_ref):
    key = lax.broadcasted_iota(jnp.int32, (NK, DQ), 0)
    qry = lax.broadcasted_iota(jnp.int32, (NK, DQ), 1)
    for c, off in enumerate(BIAS_OFFSETS):
        rel = key - qry + off
        n = jnp.abs(rel)
        large = jnp.full((NK, DQ), 8, jnp.int32)
        for start in _LOG_BUCKET_STARTS:
            large = large + jnp.where(n >= start, 1, 0)
        bucket = jnp.where(rel > 0, REL_BUCKETS // 2, 0) + jnp.where(n < 8, n, large)
        for h in range(N_HEADS):
            val = jnp.full((NK, DQ), rb_ref[0, h], F32)
            for b in range(1, REL_BUCKETS):
                val = jnp.where(bucket == b, rb_ref[b, h], val)
            o_ref[c, h] = (val - rb_ref[FAR_BUCKET, h]) * LOG2E


def _bias_tiles(rel_bias):
    return pl.pallas_call(
        _bias_kernel,
        name="rel_bias_tiles",
        in_specs=[pl.BlockSpec(memory_space=pltpu.SMEM)],
        out_specs=pl.BlockSpec(memory_space=pltpu.VMEM),
        out_shape=jax.ShapeDtypeStruct((len(BIAS_OFFSETS), N_HEADS, NK, DQ), F32),
    )(rel_bias)


def _split_heads_into(qm_ref, q):
    lane = lax.broadcasted_iota(jnp.int32, (QB, 128), 1)
    for p in range(N_PAIR):
        qp = q[:, p * 128:(p + 1) * 128].astype(F32)
        qm_ref[p, :QB, :] = jnp.where(lane < HEAD_DIM, qp, 0.0).astype(BF16)
        qm_ref[p, QB:, :] = jnp.where(lane >= HEAD_DIM, qp, 0.0).astype(BF16)


def _two_stage_pipeline(n, first, second):
    first(0, 0)

    def two(u, carry):
        j = 2 * u
        first(j + 1, 1)
        second(j, 0, False)
        first(j + 2, 0)
        second(j + 1, 1, False)
        return carry

    lax.fori_loop(0, (n - 1) // 2, two, 0)

    @pl.when(n % 2 == 0)
    def _():
        first(n - 1, 1)
        second(n - 2, 0, False)
        second(n - 1, 1, True)

    @pl.when(n % 2 == 1)
    def _():
        second(n - 1, 0, True)


def _merge_pair(o_even, o_odd):
    lane = lax.broadcasted_iota(jnp.int32, (QB, 128), 1)
    return jnp.where(lane < HEAD_DIM, o_even, o_odd)


CNT_BLOCKS = 4
KEY_ROWS = 32
assert NK == 8 * 32


def _bit_transpose32(load_row, tmp_ref, store_row):
    def swap(a, b, j, m):
        t = (a ^ lax.shift_right_logical(b, jnp.int32(j))) & m
        return a ^ t, b ^ (t << j)

    lower = []
    for k in range(16):
        a, b = swap(load_row(k), load_row(k + 16), 16, 0x0000FFFF)
        lower.append(a)
        tmp_ref[k] = b
    for base in (0, 16):
        x = lower if base == 0 else [tmp_ref[k] for k in range(16)]
        j, m = 8, 0x00FF00FF
        while j:
            k = 0
            while k < 16:
                x[k], x[k + j] = swap(x[k], x[k + j], j, m)
                k = (k + j + 1) & ~j
            j >>= 1
            m ^= m << j
        for i in range(16):
            store_row(base + i, x[i])


def _dsa_kernel(aq_ref, iq_ref, iwt_ref, ak_ref, avt_ref, ik_ref, bias_ref, o_ref,
                key_ref, plane_ref, alive_ref, qm_ref, m_ref, acc_ref,
                s0_ref, cm0_ref, s1_ref, cm1_ref, raw0_ref, raw1_ref, tmp_ref, *,
                k_sel):
    diag = pl.program_id(1)
    nkb = diag + 1
    krow = lax.broadcasted_iota(jnp.int32, (KEY_ROWS, DQ), 0)
    qcol = lax.broadcasted_iota(jnp.int32, (KEY_ROWS, DQ), 1)
    qchunk = (diag * DQ + qcol) // CHUNK

    iwt = iwt_ref[...] * (IDX_HEADS ** -0.5)

    def dots_stage(jb, slot):
        raw_ref = (raw0_ref, raw1_ref)[slot]
        k0 = pl.multiple_of(jb * NK, NK)
        for h in range(IDX_HEADS):
            ikh = ik_ref[pl.ds(k0, NK), (h % 2) * 128:(h % 2 + 1) * 128]
            iqp = iq_ref[:, (h // 2) * 128:(h // 2 + 1) * 128]
            raw_ref[h] = lax.dot_general(ikh, iqp, _NT, preferred_element_type=F32)

    def keys_stage(jb, slot, last):
        raw_ref = (raw0_ref, raw1_ref)[slot]
        k0 = pl.multiple_of(jb * NK, NK)
        for c in range(NK // KEY_ROWS):
            rows = slice(c * KEY_ROWS, (c + 1) * KEY_ROWS)
            acc = jnp.zeros((KEY_ROWS, DQ), F32)
            for h in range(IDX_HEADS):
                acc = acc + iwt[h:h + 1, :] * jnp.maximum(raw_ref[h, rows, :], 0.0)
            bits = lax.bitcast_convert_type(acc, jnp.int32)
            key = bits ^ ((bits >> 31) & 0x7FFFFFFF)
            if last:
                admissible = ((k0 + c * KEY_ROWS + krow) // CHUNK) <= qchunk
                key = jnp.where(admissible, key, INT_MIN)
            key_ref[jb, rows, :] = key
        for half in range(DQ // 128):
            lanes = slice(half * 128, (half + 1) * 128)

            def load_row(r):
                return key_ref[jb, 8 * r:8 * r + 8, lanes] ^ INT_MIN

            def store_plane(i, v):
                plane_ref[31 - i, jb, :, lanes] = v

            _bit_transpose32(load_row, tmp_ref, store_plane)
        plane_ref[32, jb] = jnp.full((8, DQ), -1, jnp.int32)
        alive_ref[jb] = jnp.full((8, DQ), -1, jnp.int32)

    _two_stage_pipeline(nkb, dots_stage, keys_stage)

    n_groups = (nkb + CNT_BLOCKS - 1) // CNT_BLOCKS

    def pad_block(jb, carry):
        for b in range(33):
            plane_ref[b, jb] = jnp.zeros((8, DQ), jnp.int32)
        alive_ref[jb] = jnp.zeros((8, DQ), jnp.int32)
        return carry

    lax.fori_loop(nkb, n_groups * CNT_BLOCKS, pad_block, 0)

    def select_pass(it, state):
        took_prev, n_above, thr_u = state
        b = 31 - it
        take_prev = took_prev != 0

        def body(g, cnts):
            cnts = list(cnts)
            for u in range(CNT_BLOCKS):
                jb = g * CNT_BLOCKS + u
                alive = alive_ref[jb]
                with_prev = alive & plane_ref[b + 1, jb]
                alive = jnp.where(take_prev, with_prev, alive ^ with_prev)
                alive_ref[jb] = alive
                cnts[u] = cnts[u] + lax.population_count(alive & plane_ref[b, jb])
            return tuple(cnts)

        zeros = jnp.zeros((8, DQ), jnp.int32)
        cnts = lax.fori_loop(0, n_groups, body, (zeros,) * CNT_BLOCKS)
        n_one = jnp.sum(sum(cnts[1:], cnts[0]), axis=0, keepdims=True)
        take = (n_above + n_one) >= k_sel
        n_above = jnp.where(take, n_above, n_above + n_one)
        thr_u = jnp.where(take, thr_u | jnp.left_shift(jnp.int32(1), b), thr_u)
        return take.astype(jnp.int32), n_above, thr_u

    row0 = jnp.zeros((1, DQ), jnp.int32)
    _, _, thr_u = lax.fori_loop(0, 32, select_pass, (row0 + 1, row0, row0))
    thr = thr_u ^ INT_MIN
    thr = jnp.maximum(thr, INT_MIN + 1)

    lane = lax.broadcasted_iota(jnp.int32, (DQ, 128), 1)
    for p in range(N_PAIR):
        qp = aq_ref[:, p * 128:(p + 1) * 128].astype(F32)
        qm_ref[2 * p] = jnp.where(lane < HEAD_DIM, qp, 0.0).astype(BF16)
        qm_ref[2 * p + 1] = jnp.where(lane >= HEAD_DIM, qp, 0.0).astype(BF16)
    m_ref[...] = jnp.full(m_ref.shape, NEG_BIG, F32)
    acc_ref[...] = jnp.zeros(acc_ref.shape, F32)
    ones = jnp.ones((ONES_ROWS, NK), BF16)

    slots = ((s0_ref, cm0_ref), (s1_ref, cm1_ref))

    def logits_stage(jb, slot, bias_idx):
        s_ref, cm_ref = slots[slot]
        k0 = pl.multiple_of(jb * NK, NK)
        mask = jnp.where(key_ref[jb] >= thr, 0.0, NEG_BIG).astype(BF16)
        for h in range(N_HEADS):
            kp = ak_ref[pl.ds(k0, NK), (h // 2) * 128:(h // 2 + 1) * 128]
            s = lax.dot_general(kp, qm_ref[h], _NT, preferred_element_type=F32)
            if bias_idx is not None:
                s = s + bias_ref[bias_idx, h]
            sb = s.astype(BF16) + mask
            s_ref[h] = sb
            cm_ref[h] = jnp.max(sb, axis=0, keepdims=True).astype(F32)

    def softmax_stage(jb, slot):
        s_ref, cm_ref = slots[slot]
        for h in range(N_HEADS):
            m_prev = m_ref[h]
            m_new = jnp.maximum(m_prev, cm_ref[h])
            alpha = jnp.exp2(m_prev - m_new)
            pe = jnp.exp2(s_ref[h] - m_new.astype(BF16))
            vt = jnp.concatenate([avt_ref[jb, h * HEAD_DIM:(h + 1) * HEAD_DIM, :], ones], axis=0)
            acc_ref[h] = alpha * acc_ref[h] + jnp.dot(vt, pe, preferred_element_type=F32)
            m_ref[h] = m_new

    odd_diag = diag % 2
    logits_stage(diag, 0, 0)
    logits_stage(jnp.maximum(diag - 1, 0), 1, 1)
    softmax_stage(diag, 0)
    n_steps = diag - 1

    def two_steps(u, carry):
        b = diag - 1 - 2 * u
        logits_stage(b - 1, 0, None)
        softmax_stage(b, 1)
        logits_stage(b - 2, 1, None)
        softmax_stage(b - 1, 0)
        return carry

    lax.fori_loop(0, jnp.maximum(n_steps, 0) // 2, two_steps, 0)

    @pl.when(jnp.logical_and(n_steps >= 1, n_steps % 2 == 1))
    def _():
        logits_stage(0, 0, None)
        softmax_stage(1, 1)

    @pl.when(jnp.logical_and(diag >= 1, odd_diag == 1))
    def _():
        softmax_stage(0, 1)

    @pl.when(jnp.logical_and(diag >= 1, odd_diag == 0))
    def _():
        softmax_stage(0, 0)

    for p in range(N_PAIR):
        halves = []
        for h in (2 * p, 2 * p + 1):
            a = acc_ref[h]
            halves.append(a[:HEAD_DIM, :] / a[HEAD_DIM:HEAD_DIM + 1, :])
        o_ref[:, p * 128:(p + 1) * 128] = jnp.concatenate(halves, axis=0).T.astype(BF16)


def _dsa(proj3, ik3, avt4, iwt, bias_tiles, k_sel):
    bsz, seq, _ = proj3.shape
    nq = seq // DQ
    assert seq % (NK * CNT_BLOCKS) == 0
    resident = dict(pipeline_mode=pl.Buffered(1))
    return pl.pallas_call(
        functools.partial(_dsa_kernel, k_sel=k_sel),
        name="dsa",
        grid=(bsz, nq),
        in_specs=[
            pl.BlockSpec((None, DQ, 512), lambda b, i: (b, i, COL_AQ)),
            pl.BlockSpec((None, DQ, 512), lambda b, i: (b, i, COL_IQ)),
            pl.BlockSpec((IDX_HEADS, DQ), lambda b, i: (0, b * nq + i)),
            pl.BlockSpec((None, seq, 512), lambda b, i: (b, 0, COL_AK), **resident),
            pl.BlockSpec((None, seq // NK, 512, NK), lambda b, i: (b, 0, 0, 0), **resident),
            pl.BlockSpec((None, seq, N_IK), lambda b, i: (b, 0, 0), **resident),
            pl.BlockSpec(bias_tiles.shape, lambda b, i: (0, 0, 0, 0), **resident),
        ],
        out_specs=pl.BlockSpec((None, DQ, 512), lambda b, i: (b, i, 0)),
        out_shape=jax.ShapeDtypeStruct((bsz, seq, 512), BF16),
        scratch_shapes=[
            pltpu.VMEM((seq // NK, NK, DQ), jnp.int32),
            pltpu.VMEM((33, seq // NK, 8, DQ), jnp.int32),
            pltpu.VMEM((seq // NK, 8, DQ), jnp.int32),
            pltpu.VMEM((N_HEADS, DQ, 128), BF16),
            pltpu.VMEM((N_HEADS, 1, DQ), F32),
            pltpu.VMEM((N_HEADS, HEAD_DIM + ONES_ROWS, DQ), F32),
            pltpu.VMEM((N_HEADS, NK, DQ), BF16),
            pltpu.VMEM((N_HEADS, 1, DQ), F32),
            pltpu.VMEM((N_HEADS, NK, DQ), BF16),
            pltpu.VMEM((N_HEADS, 1, DQ), F32),
            pltpu.VMEM((IDX_HEADS, NK, DQ), F32),
            pltpu.VMEM((IDX_HEADS, NK, DQ), F32),
            pltpu.VMEM((16, 8, 128), jnp.int32),
        ],
        compiler_params=pltpu.CompilerParams(
            dimension_semantics=("arbitrary", "arbitrary"), vmem_limit_bytes=VMEM_LIMIT),
    )(proj3, proj3, iwt, proj3, avt4, ik3, bias_tiles)


SB_DEAD_MASS = 104.0 * LOG2E


def _sb_kernel(q_ref, k_ref, v_ref, o_ref, qm_ref, uu_ref, carry_ref, acc_ref, z_ref, sp_ref,
               later_ref):
    i = pl.program_id(1)
    diag = (i * QB + QB - 1) // NK

    @pl.when(jnp.logical_and(pl.program_id(0) == 0, i == 0))
    def _():
        kr = lax.broadcasted_iota(jnp.int32, (2 * NK, NK), 0) % NK
        kc = lax.broadcasted_iota(jnp.int32, (2 * NK, NK), 1)
        uu_ref[...] = jnp.where(kr > kc, 1.0, 0.0).astype(BF16)

    _split_heads_into(qm_ref, q_ref[...])
    carry_ref[...] = jnp.zeros(carry_ref.shape, F32)
    acc_ref[...] = jnp.zeros(acc_ref.shape, F32)

    def block(jb, on_diagonal):
        k0 = pl.multiple_of(jb * NK, NK)
        if on_diagonal:
            row = lax.broadcasted_iota(jnp.int32, (2 * QB, NK), 0)
            col = lax.broadcasted_iota(jnp.int32, (2 * QB, NK), 1)
            causal = (k0 + col) < (i * QB + row % QB)
        for p in range(N_PAIR):
            kp = k_ref[pl.ds(k0, NK), p * 128:(p + 1) * 128]
            z_ref[p] = lax.dot_general(qm_ref[p], kp, _NT, preferred_element_type=F32)
        for p in range(N_PAIR):
            z = z_ref[p]
            neg_abs = lax.bitcast_convert_type(
                lax.bitcast_convert_type(z, jnp.int32) | INT_MIN, F32)
            sp = jnp.maximum(z, 0.0) + jnp.log(1.0 + jnp.exp2(neg_abs)) * LOG2E
            if on_diagonal:
                sp = jnp.where(causal, sp, 0.0)
            sp_ref[p] = sp
            hi = sp.astype(BF16)
            lo = (sp - hi.astype(F32)).astype(BF16)
            later_ref[p] = jnp.dot(jnp.concatenate([hi, lo], axis=1), uu_ref[...],
                                   preferred_element_type=F32)
        for p in range(N_PAIR):
            vp = v_ref[pl.ds(k0, NK), p * 128:(p + 1) * 128]
            carry = carry_ref[p]
            sp = sp_ref[p]
            a = jnp.exp2(z_ref[p] - sp - later_ref[p] - carry)
            if on_diagonal:
                a = jnp.where(causal, a, 0.0)
            acc_ref[p] += jnp.dot(a.astype(BF16), vp, preferred_element_type=F32)
            carry_ref[p] = carry + jnp.sum(sp, axis=1, keepdims=True)

    block(diag, True)

    def alive():
        return (jnp.min(carry_ref[...]) <= SB_DEAD_MASS).astype(jnp.int32)

    def cond(state):
        jb, go = state
        return jnp.logical_and(jb >= 0, go > 0)

    def body(state):
        jb, _ = state
        block(jb, False)
        return jb - 1, alive()

    lax.while_loop(cond, body, (diag - 1, alive()))

    for p in range(N_PAIR):
        o_ref[:, p * 128:(p + 1) * 128] = _merge_pair(
            acc_ref[p, :QB, :], acc_ref[p, QB:, :]).astype(BF16)


def _stick_breaking(proj3):
    bsz, seq, _ = proj3.shape
    resident = dict(pipeline_mode=pl.Buffered(1))
    return pl.pallas_call(
        _sb_kernel,
        name="stick_breaking",
        grid=(bsz, seq // QB),
        in_specs=[
            pl.BlockSpec((None, QB, 512), lambda b, i: (b, i, COL_BQ)),
            pl.BlockSpec((None, seq, 512), lambda b, i: (b, 0, COL_BK), **resident),
            pl.BlockSpec((None, seq, 512), lambda b, i: (b, 0, COL_BV), **resident),
        ],
        out_specs=pl.BlockSpec((None, QB, 512), lambda b, i: (b, i, 0)),
        out_shape=jax.ShapeDtypeStruct((bsz, seq, 512), BF16),
        scratch_shapes=[
            pltpu.VMEM((N_PAIR, 2 * QB, 128), BF16),
            pltpu.VMEM((2 * NK, NK), BF16),
            pltpu.VMEM((N_PAIR, 2 * QB, 1), F32),
            pltpu.VMEM((N_PAIR, 2 * QB, 128), F32),
            pltpu.VMEM((N_PAIR, 2 * QB, NK), F32),
            pltpu.VMEM((N_PAIR, 2 * QB, NK), F32),
            pltpu.VMEM((N_PAIR, 2 * QB, NK), F32),
        ],
        compiler_params=pltpu.CompilerParams(
            dimension_semantics=("arbitrary", "arbitrary"), vmem_limit_bytes=VMEM_LIMIT),
    )(proj3, proj3, proj3)


def _merge_kernel(x_ref, ya_ref, yb_ref, cq_ref, g0_ref, g1_ref, g2_ref, mk_ref, mv_ref,
                  wa_ref, wb_ref, wc_ref, wo_ref, gp_ref, o_ref):
    cq = cq_ref[...]
    heads = []
    for h in range(C_HEADS):
        sl = slice(h * C_HEAD_DIM, (h + 1) * C_HEAD_DIM)
        s = lax.dot_general(cq[:, sl], mk_ref[:, sl], _NT,
                            preferred_element_type=F32) * (C_HEAD_DIM ** -0.5)
        e = jnp.exp(s - jnp.max(s, axis=1, keepdims=True))
        p = e / jnp.sum(e, axis=1, keepdims=True)
        heads.append(jnp.dot(p.astype(BF16), mv_ref[:, sl], preferred_element_type=F32))
    yc_pre = jnp.concatenate(heads, axis=1).astype(BF16)
    ya = jnp.dot(ya_ref[...], wa_ref[...], preferred_element_type=F32)
    yb = jnp.dot(yb_ref[...], wb_ref[...], preferred_element_type=F32)
    yc = jnp.dot(yc_pre, wc_ref[...], preferred_element_type=F32)
    merged = (g0_ref[...].astype(F32) * ya + g1_ref[...].astype(F32) * yb
              + g2_ref[...].astype(F32) * yc)
    o = jnp.dot(merged.astype(BF16), wo_ref[...], preferred_element_type=F32)
    o_ref[...] = x_ref[...] + _rms(o, gp_ref[...])


def _merge(x2, ya2, yb2, proj2, mkv3, wa, wb, wc, wo, g_post, seq):
    n = x2.shape[0]
    tm = min(512, seq)
    per_batch = seq // tm
    n_mem = mkv3.shape[1]
    c_dim = C_HEADS * C_HEAD_DIM
    const = lambda t: (0, 0)
    return pl.pallas_call(
        _merge_kernel,
        name="merge",
        grid=(n // tm,),
        in_specs=[
            pl.BlockSpec((tm, D_MODEL), lambda t: (t, 0)),
            pl.BlockSpec((tm, 512), lambda t: (t, 0)),
            pl.BlockSpec((tm, 512), lambda t: (t, 0)),
            pl.BlockSpec((tm, 512), lambda t: (t, COL_CQ)),
            pl.BlockSpec((tm, D_MODEL), lambda t: (t, 0)),
            pl.BlockSpec((tm, D_MODEL), lambda t: (t, 1)),
            pl.BlockSpec((tm, D_MODEL), lambda t: (t, 2)),
            pl.BlockSpec((None, n_mem, c_dim), lambda t: (t // per_batch, 0, 0)),
            pl.BlockSpec((None, n_mem, c_dim), lambda t: (t // per_batch, 0, 1)),
            pl.BlockSpec(wa.shape, const),
            pl.BlockSpec(wb.shape, const),
            pl.BlockSpec(wc.shape, const),
            pl.BlockSpec(wo.shape, const),
            pl.BlockSpec((1, D_MODEL), const),
        ],
        out_specs=pl.BlockSpec((tm, D_MODEL), lambda t: (t, 0)),
        out_shape=jax.ShapeDtypeStruct((n, D_MODEL), F32),
        compiler_params=pltpu.CompilerParams(
            dimension_semantics=("arbitrary",), vmem_limit_bytes=VMEM_LIMIT),
    )(x2, ya2, yb2, proj2, proj2, proj2, proj2, mkv3, mkv3, wa, wb, wc, wo, g_post)


def _ffn_kernel(x_ref, gpre_ref, wg_ref, wu_ref, wo_ref, gpost_ref, o_ref, h_ref, acc_ref):
    k = pl.program_id(1)

    @pl.when(k == 0)
    def _():
        h_ref[...] = _rms(x_ref[...], gpre_ref[...]).astype(BF16)
        acc_ref[...] = jnp.zeros(acc_ref.shape, F32)

    h = h_ref[...]
    g = jnp.dot(h, wg_ref[...], preferred_element_type=F32)
    u = jnp.dot(h, wu_ref[...], preferred_element_type=F32)
    act = (g * jax.nn.sigmoid(g) * u).astype(BF16)
    acc_ref[...] += jnp.dot(act, wo_ref[...], preferred_element_type=F32)

    @pl.when(k == pl.num_programs(1) - 1)
    def _():
        o_ref[...] = x_ref[...] + _rms(acc_ref[...], gpost_ref[...])


def _ffn(x2, g_pre, wg, wu, wo, g_post):
    n = x2.shape[0]
    d_ff = wg.shape[1]
    tm = min(512, n)
    tf = d_ff // 2
    return pl.pallas_call(
        _ffn_kernel,
        name="ffn",
        grid=(n // tm, d_ff // tf),
        in_specs=[
            pl.BlockSpec((tm, D_MODEL), lambda t, k: (t, 0)),
            pl.BlockSpec((1, D_MODEL), lambda t, k: (0, 0)),
            pl.BlockSpec((D_MODEL, tf), lambda t, k: (0, k)),
            pl.BlockSpec((D_MODEL, tf), lambda t, k: (0, k)),
            pl.BlockSpec((tf, D_MODEL), lambda t, k: (k, 0)),
            pl.BlockSpec((1, D_MODEL), lambda t, k: (0, 0)),
        ],
        out_specs=pl.BlockSpec((tm, D_MODEL), lambda t, k: (t, 0)),
        out_shape=jax.ShapeDtypeStruct((n, D_MODEL), F32),
        scratch_shapes=[pltpu.VMEM((tm, D_MODEL), BF16), pltpu.VMEM((tm, D_MODEL), F32)],
        compiler_params=pltpu.CompilerParams(
            dimension_semantics=("arbitrary", "arbitrary"), vmem_limit_bytes=VMEM_LIMIT),
    )(x2, g_pre, wg, wu, wo, g_post)


def _pack_w_in(w, b_gate):
    sizes = (512, 512, 512, IDX_HEADS * 64, 64, IDX_HEADS, 512, 512, 512, 512,
             N_BRANCH * D_MODEL)
    aq, ak, av, iq, ik, iw, bq, bk, bv, cq, gates = jnp.split(w, np.cumsum(sizes)[:-1], axis=1)
    scale = HEAD_DIM ** -0.5
    scale2 = scale * LOG2E
    w_main = jnp.concatenate(
        [gates, aq * scale2, ak, iq * scale, bq * scale2, bk, bv, cq], axis=1).astype(BF16)
    z64 = jnp.zeros((D_MODEL, 64), F32)
    w_ik = jnp.concatenate([ik, z64, z64, ik], axis=1).astype(BF16)
    w_trans = jnp.concatenate(
        [av, iw, jnp.zeros((D_MODEL, N_TRANS - 512 - IDX_HEADS), F32)], axis=1).T.astype(BF16)
    b_main = jnp.concatenate([b_gate, jnp.zeros((N_MAIN - N_GATE,), F32)])[None, :]
    return w_main, w_ik, w_trans, b_main


def kernel(x, mem, rel_bias, g_mix_pre, w_in, b_gate, g_mem, w_mem_kv, w_up_a, w_up_b, w_up_c,
           w_out, g_mix_post, g_ffn_pre, w_ffn_in, w_ffn_out, g_ffn_post):
    bsz, seq, _ = x.shape
    n_mem = mem.shape[1]
    k_sel = min(TOPK_MAX, seq // 4)
    bias_tiles = _bias_tiles(rel_bias)
    x2 = x.reshape(bsz * seq, D_MODEL)
    for l in range(w_in.shape[0]):
        w_main, w_ik, w_trans, b_main = _pack_w_in(w_in[l], b_gate[l])
        proj2, ik2, avt, iwt = _project(x2, g_mix_pre[l][None, :], w_main, b_main, w_ik, w_trans)
        proj3 = proj2.reshape(bsz, seq, N_MAIN)
        mkv = _memkv(mem.reshape(bsz * n_mem, D_MODEL), g_mem[l][None, :],
                     w_mem_kv[l].astype(BF16))
        ya = _dsa(proj3, ik2.reshape(bsz, seq, N_IK), avt.reshape(bsz, seq // NK, 512, NK),
                  iwt, bias_tiles, k_sel)
        yb = _stick_breaking(proj3)
        x2 = _merge(x2, ya.reshape(bsz * seq, 512), yb.reshape(bsz * seq, 512), proj2,
                    mkv.reshape(bsz, n_mem, 2 * C_HEADS * C_HEAD_DIM),
                    w_up_a[l].astype(BF16), w_up_b[l].astype(BF16), w_up_c[l].astype(BF16),
                    w_out[l].astype(BF16), g_mix_post[l][None, :], seq)
        d_ff = w_ffn_out.shape[1]
        w_ffn = w_ffn_in[l].astype(BF16)
        x2 = _ffn(x2, g_ffn_pre[l][None, :], w_ffn[:, :d_ff], w_ffn[:, d_ff:],
                  w_ffn_out[l].astype(BF16), g_ffn_post[l][None, :])
    return x2.reshape(bsz, seq, D_MODEL)
```

```python
import functools

import numpy as np
import jax
import jax.numpy as jnp
from jax import lax
from jax.experimental import pallas as pl
from jax.experimental.pallas import tpu as pltpu

D_MODEL = 1024
CHUNK = 64
HEAD_DIM = 64
N_HEADS = 8
IDX_HEADS = 8
TOPK_MAX = 256
C_HEADS = 4
C_HEAD_DIM = 128
N_BRANCH = 3
REL_BUCKETS = 32
EPS = 1e-6

F32 = jnp.float32
BF16 = jnp.bfloat16
INT_MIN = -2 ** 31
NEG_BIG = -1e30
LOG2E = 1.4426950408889634

QB = 256
DQ = 256
NK = 256
N_PAIR = N_HEADS // 2
ONES_ROWS = 16

N_GATE = N_BRANCH * D_MODEL
COL_AQ, COL_AK, COL_IQ, COL_BQ, COL_BK, COL_BV, COL_CQ = range(N_GATE // 512, N_GATE // 512 + 7)
N_MAIN = N_GATE + 7 * 512
N_IK = 256
N_TRANS = 512 + 16

VMEM_LIMIT = 56 * 1024 * 1024

_NT = (((1,), (1,)), ((), ()))


def _rms(x, g):
    return x * lax.rsqrt(jnp.mean(x * x, axis=-1, keepdims=True) + EPS) * g


def _proj_kernel(x_ref, g_ref, w_ref, b_ref, wik_ref, wt_ref, o_ref, ik_ref, avt_ref, iwt_ref,
                 h_ref, *, n_gate_tiles):
    j = pl.program_id(1)

    @pl.when(j == 0)
    def _():
        hb = _rms(x_ref[...], g_ref[...]).astype(BF16)
        h_ref[...] = hb
        ik_ref[...] = jnp.dot(hb, wik_ref[...], preferred_element_type=F32).astype(BF16)
        tr = lax.dot_general(wt_ref[...], hb, _NT, preferred_element_type=F32)
        for c in range(avt_ref.shape[0]):
            avt_ref[c] = tr[:512, c * NK:(c + 1) * NK].astype(BF16)
        iwt_ref[...] = tr[512:512 + IDX_HEADS, :]

    acc = jnp.dot(h_ref[...], w_ref[...], preferred_element_type=F32)

    @pl.when(j >= n_gate_tiles)
    def _():
        o_ref[...] = acc.astype(BF16)

    @pl.when(j < n_gate_tiles)
    def _():
        o_ref[...] = (0.5 + 0.5 * jnp.tanh(0.5 * (acc + b_ref[...]))).astype(BF16)


def _project(x2, g, w_main, b_main, w_ik, w_trans):
    n = x2.shape[0]
    tm = min(2048, n)
    tn = 512
    grid = (n // tm, N_MAIN // tn)
    return pl.pallas_call(
        functools.partial(_proj_kernel, n_gate_tiles=N_GATE // tn),
        name="in_proj",
        grid=grid,
        in_specs=[
            pl.BlockSpec((tm, D_MODEL), lambda i, j: (i, 0)),
            pl.BlockSpec((1, D_MODEL), lambda i, j: (0, 0)),
            pl.BlockSpec((D_MODEL, tn), lambda i, j: (0, j)),
            pl.BlockSpec((1, tn), lambda i, j: (0, j)),
            pl.BlockSpec((D_MODEL, N_IK), lambda i, j: (0, 0)),
            pl.BlockSpec((N_TRANS, D_MODEL), lambda i, j: (0, 0)),
        ],
        out_specs=[
            pl.BlockSpec((tm, tn), lambda i, j: (i, j)),
            pl.BlockSpec((tm, N_IK), lambda i, j: (i, 0)),
            pl.BlockSpec((tm // NK, 512, NK), lambda i, j: (i, 0, 0)),
            pl.BlockSpec((IDX_HEADS, tm), lambda i, j: (0, i)),
        ],
        out_shape=[
            jax.ShapeDtypeStruct((n, N_MAIN), BF16),
            jax.ShapeDtypeStruct((n, N_IK), BF16),
            jax.ShapeDtypeStruct((n // NK, 512, NK), BF16),
            jax.ShapeDtypeStruct((IDX_HEADS, n), F32),
        ],
        scratch_shapes=[pltpu.VMEM((tm, D_MODEL), BF16)],
        compiler_params=pltpu.CompilerParams(
            dimension_semantics=("arbitrary", "arbitrary"), vmem_limit_bytes=VMEM_LIMIT),
    )(x2, g, w_main, b_main, w_ik, w_trans)


def _memkv_kernel(x_ref, g_ref, w_ref, o_ref):
    hb = _rms(x_ref[...], g_ref[...]).astype(BF16)
    o_ref[...] = jnp.dot(hb, w_ref[...], preferred_element_type=F32).astype(BF16)


def _memkv(mem2, g, w):
    n = mem2.shape[0]
    tm = min(512, n)
    return pl.pallas_call(
        _memkv_kernel,
        name="mem_kv",
        grid=(n // tm,),
        in_specs=[
            pl.BlockSpec((tm, D_MODEL), lambda i: (i, 0)),
            pl.BlockSpec((1, D_MODEL), lambda i: (0, 0)),
            pl.BlockSpec((D_MODEL, w.shape[1]), lambda i: (0, 0)),
        ],
        out_specs=pl.BlockSpec((tm, w.shape[1]), lambda i: (i, 0)),
        out_shape=jax.ShapeDtypeStruct((n, w.shape[1]), BF16),
        compiler_params=pltpu.CompilerParams(
            dimension_semantics=("arbitrary",), vmem_limit_bytes=VMEM_LIMIT),
    )(mem2, g, w)


BIAS_OFFSETS = (0, -NK)
_LOG_BUCKET_STARTS = (12, 16, 23, 32, 46, 64, 91)
FAR_BUCKET = 15
assert DQ == NK


def _bias_kernel(rb_ref, o_ref):
    key = lax.broadcasted_iota(jnp.int32, (NK, DQ), 0)
    qry = lax.broadcasted_iota(jnp.int32, (NK, DQ), 1)
    for c, off in enumerate(BIAS_OFFSETS):
        rel = key - qry + off
        n = jnp.abs(rel)
        large = jnp.full((NK, DQ), 8, jnp.int32)
        for start in _LOG_BUCKET_STARTS:
            large = large + jnp.where(n >= start, 1, 0)
        bucket = jnp.where(rel > 0, REL_BUCKETS // 2, 0) + jnp.where(n < 8, n, large)
        for h in range(N_HEADS):
            val = jnp.full((NK, DQ), rb_ref[0, h], F32)
            for b in range(1, REL_BUCKETS):
                val = jnp.where(bucket == b, rb_ref[b, h], val)
            o_ref[c, h] = (val - rb_ref[FAR_BUCKET, h]) * LOG2E


def _bias_tiles(rel_bias):
    return pl.pallas_call(
        _bias_kernel,
        name="rel_bias_tiles",
        in_specs=[pl.BlockSpec(memory_space=pltpu.SMEM)],
        out_specs=pl.BlockSpec(memory_space=pltpu.VMEM),
        out_shape=jax.ShapeDtypeStruct((len(BIAS_OFFSETS), N_HEADS, NK, DQ), F32),
    )(rel_bias)


def _split_heads_into(qm_ref, q):
    lane = lax.broadcasted_iota(jnp.int32, (QB, 128), 1)
    for p in range(N_PAIR):
        qp = q[:, p * 128:(p + 1) * 128].astype(F32)
        qm_ref[p, :QB, :] = jnp.where(lane < HEAD_DIM, qp, 0.0).astype(BF16)
        qm_ref[p, QB:, :] = jnp.where(lane >= HEAD_DIM, qp, 0.0).astype(BF16)


def _two_stage_pipeline(n, first, second):
    first(0, 0)

    def two(u, carry):
        j = 2 * u
        first(j + 1, 1)
        second(j, 0, False)
        first(j + 2, 0)
        second(j + 1, 1, False)
        return carry

    lax.fori_loop(0, (n - 1) // 2, two, 0)

    @pl.when(n % 2 == 0)
    def _():
        first(n - 1, 1)
        second(n - 2, 0, False)
        second(n - 1, 1, True)

    @pl.when(n % 2 == 1)
    def _():
        second(n - 1, 0, True)


def _merge_pair(o_even, o_odd):
    lane = lax.broadcasted_iota(jnp.int32, (QB, 128), 1)
    return jnp.where(lane < HEAD_DIM, o_even, o_odd)


CNT_BLOCKS = 4
KEY_ROWS = 32
assert NK == 8 * 32


def _bit_transpose32(load_row, tmp_ref, store_row):
    def swap(a, b, j, m):
        t = (a ^ lax.shift_right_logical(b, jnp.int32(j))) & m
        return a ^ t, b ^ (t << j)

    lower = []
    for k in range(16):
        a, b = swap(load_row(k), load_row(k + 16), 16, 0x0000FFFF)
        lower.append(a)
        tmp_ref[k] = b
    for base in (0, 16):
        x = lower if base == 0 else [tmp_ref[k] for k in range(16)]
        j, m = 8, 0x00FF00FF
        while j:
            k = 0
            while k < 16:
                x[k], x[k + j] = swap(x[k], x[k + j], j, m)
                k = (k + j + 1) & ~j
            j >>= 1
            m ^= m << j
        for i in range(16):
            store_row(base + i, x[i])


def _demote_extra_ties(key_ref, alive_ref, kept_ref, thr, need, nkb, n_groups):
    n_blocks = key_ref.shape[0]
    n_padded = n_groups * CNT_BLOCKS
    sub = lax.broadcasted_iota(jnp.int32, (8, DQ), 0)
    zero_masks = []
    for c in reversed(range(max(n_blocks - 1, 1).bit_length())):
        zero_masks.append(
            lambda jb, c=c: jnp.where(((jb >> c) & 1) == 0, jnp.int32(-1), jnp.int32(0)))
    for word in (0xFFFF0000, 0xFF00FF00, 0xF0F0F0F0, 0xCCCCCCCC, 0xAAAAAAAA):
        zero_masks.append(lambda jb, word=word: jnp.int32(word - (1 << 32)))
    for c in (2, 1, 0):
        zero_masks.append(lambda jb, c=c: jnp.where(((sub >> c) & 1) == 0, -1, 0))

    def clear(jb, carry):
        kept_ref[jb] = jnp.zeros((8, DQ), jnp.int32)
        return carry

    lax.fori_loop(0, n_padded, clear, 0)
    remaining = need
    for zero_mask in zero_masks:
        def count(jb, cnt):
            return cnt + lax.population_count(alive_ref[jb] & zero_mask(jb))

        n_zero = jnp.sum(lax.fori_loop(0, n_padded, count, jnp.zeros((8, DQ), jnp.int32)),
                         axis=0, keepdims=True)
        take_zero = n_zero >= remaining

        def update(jb, carry):
            alive = alive_ref[jb]
            lows = alive & zero_mask(jb)
            kept_ref[jb] = kept_ref[jb] | jnp.where(take_zero, 0, lows)
            alive_ref[jb] = jnp.where(take_zero, lows, alive ^ lows)
            return carry

        lax.fori_loop(0, n_padded, update, 0)
        remaining = jnp.where(take_zero, remaining, remaining - n_zero)

    def demote(jb, carry):
        kept = kept_ref[jb] | alive_ref[jb]
        for r in range(32):
            rows = slice(8 * r, 8 * r + 8)
            key = key_ref[jb, rows, :]
            dropped = (lax.shift_right_logical(kept, jnp.int32(31 - r)) & 1) == 0
            key_ref[jb, rows, :] = jnp.where(jnp.logical_and(key == thr, dropped), key - 1, key)
        return carry

    lax.fori_loop(0, nkb, demote, 0)


def _dsa_kernel(aq_ref, iq_ref, iwt_ref, ak_ref, avt_ref, ik_ref, bias_ref, o_ref,
                key_ref, plane_ref, alive_ref, kept_ref, qm_ref, m_ref, acc_ref,
                s0_ref, cm0_ref, s1_ref, cm1_ref, raw0_ref, raw1_ref, tmp_ref, *,
                k_sel):
    diag = pl.program_id(1)
    nkb = diag + 1
    krow = lax.broadcasted_iota(jnp.int32, (KEY_ROWS, DQ), 0)
    qcol = lax.broadcasted_iota(jnp.int32, (KEY_ROWS, DQ), 1)
    qchunk = (diag * DQ + qcol) // CHUNK

    iwt = iwt_ref[...] * (IDX_HEADS ** -0.5)

    def dots_stage(jb, slot):
        raw_ref = (raw0_ref, raw1_ref)[slot]
        k0 = pl.multiple_of(jb * NK, NK)
        for h in range(IDX_HEADS):
            ikh = ik_ref[pl.ds(k0, NK), (h % 2) * 128:(h % 2 + 1) * 128]
            iqp = iq_ref[:, (h // 2) * 128:(h // 2 + 1) * 128]
            raw_ref[h] = lax.dot_general(ikh, iqp, _NT, preferred_element_type=F32)

    def keys_stage(jb, slot, last):
        raw_ref = (raw0_ref, raw1_ref)[slot]
        k0 = pl.multiple_of(jb * NK, NK)
        for c in range(NK // KEY_ROWS):
            rows = slice(c * KEY_ROWS, (c + 1) * KEY_ROWS)
            acc = jnp.zeros((KEY_ROWS, DQ), F32)
            for h in range(IDX_HEADS):
                acc = acc + iwt[h:h + 1, :] * jnp.maximum(raw_ref[h, rows, :], 0.0)
            bits = lax.bitcast_convert_type(acc, jnp.int32)
            key = bits ^ ((bits >> 31) & 0x7FFFFFFF)
            key = jnp.where(key == -1, 0, key)
            if last:
                admissible = ((k0 + c * KEY_ROWS + krow) // CHUNK) <= qchunk
                key = jnp.where(admissible, key, INT_MIN)
            key_ref[jb, rows, :] = key
        for half in range(DQ // 128):
            lanes = slice(half * 128, (half + 1) * 128)

            def load_row(r):
                return key_ref[jb, 8 * r:8 * r + 8, lanes] ^ INT_MIN

            def store_plane(i, v):
                plane_ref[31 - i, jb, :, lanes] = v

            _bit_transpose32(load_row, tmp_ref, store_plane)
        plane_ref[32, jb] = jnp.full((8, DQ), -1, jnp.int32)
        alive_ref[jb] = jnp.full((8, DQ), -1, jnp.int32)

    _two_stage_pipeline(nkb, dots_stage, keys_stage)

    n_groups = (nkb + CNT_BLOCKS - 1) // CNT_BLOCKS

    def pad_block(jb, carry):
        for b in range(33):
            plane_ref[b, jb] = jnp.zeros((8, DQ), jnp.int32)
        alive_ref[jb] = jnp.zeros((8, DQ), jnp.int32)
        return carry

    lax.fori_loop(nkb, n_groups * CNT_BLOCKS, pad_block, 0)

    def select_pass(it, state):
        took_prev, n_above, thr_u = state
        b = 31 - it
        take_prev = took_prev != 0

        def body(g, cnts):
            cnts = list(cnts)
            for u in range(CNT_BLOCKS):
                jb = g * CNT_BLOCKS + u
                alive = alive_ref[jb]
                with_prev = alive & plane_ref[b + 1, jb]
                alive = jnp.where(take_prev, with_prev, alive ^ with_prev)
                alive_ref[jb] = alive
                cnts[u] = cnts[u] + lax.population_count(alive & plane_ref[b, jb])
            return tuple(cnts)

        zeros = jnp.zeros((8, DQ), jnp.int32)
        cnts = lax.fori_loop(0, n_groups, body, (zeros,) * CNT_BLOCKS)
        n_one = jnp.sum(sum(cnts[1:], cnts[0]), axis=0, keepdims=True)
        take = (n_above + n_one) >= k_sel
        n_above = jnp.where(take, n_above, n_above + n_one)
        thr_u = jnp.where(take, thr_u | jnp.left_shift(jnp.int32(1), b), thr_u)
        return take.astype(jnp.int32), n_above, thr_u

    row0 = jnp.zeros((1, DQ), jnp.int32)
    took_last, n_above, thr_u = lax.fori_loop(0, 32, select_pass, (row0 + 1, row0, row0))
    thr = thr_u ^ INT_MIN
    thr = jnp.maximum(thr, INT_MIN + 1)

    def settle(g, cnts):
        cnts = list(cnts)
        for u in range(CNT_BLOCKS):
            jb = g * CNT_BLOCKS + u
            alive = alive_ref[jb]
            with_last = alive & plane_ref[0, jb]
            alive = jnp.where(took_last != 0, with_last, alive ^ with_last)
            alive_ref[jb] = alive
            cnts[u] = cnts[u] + lax.population_count(alive)
        return tuple(cnts)

    zeros = jnp.zeros((8, DQ), jnp.int32)
    cnts = lax.fori_loop(0, n_groups, settle, (zeros,) * CNT_BLOCKS)
    n_tied = jnp.sum(sum(cnts[1:], cnts[0]), axis=0, keepdims=True)
    need = k_sel - n_above
    extra = jnp.logical_and(n_tied > need, thr_u != 0)

    @pl.when(jnp.max(extra.astype(jnp.int32)) > 0)
    def _():
        _demote_extra_ties(key_ref, alive_ref, kept_ref, thr, need, nkb, n_groups)

    lane = lax.broadcasted_iota(jnp.int32, (DQ, 128), 1)
    for p in range(N_PAIR):
        qp = aq_ref[:, p * 128:(p + 1) * 128].astype(F32)
        qm_ref[2 * p] = jnp.where(lane < HEAD_DIM, qp, 0.0).astype(BF16)
        qm_ref[2 * p + 1] = jnp.where(lane >= HEAD_DIM, qp, 0.0).astype(BF16)
    m_ref[...] = jnp.full(m_ref.shape, NEG_BIG, F32)
    acc_ref[...] = jnp.zeros(acc_ref.shape, F32)
    ones = jnp.ones((ONES_ROWS, NK), BF16)

    slots = ((s0_ref, cm0_ref), (s1_ref, cm1_ref))

    def logits_stage(jb, slot, bias_idx):
        s_ref, cm_ref = slots[slot]
        k0 = pl.multiple_of(jb * NK, NK)
        mask = jnp.where(key_ref[jb] >= thr, 0.0, NEG_BIG).astype(BF16)
        for h in range(N_HEADS):
            kp = ak_ref[pl.ds(k0, NK), (h // 2) * 128:(h // 2 + 1) * 128]
            s = lax.dot_general(kp, qm_ref[h], _NT, preferred_element_type=F32)
            if bias_idx is not None:
                s = s + bias_ref[bias_idx, h]
            sb = s.astype(BF16) + mask
            s_ref[h] = sb
            cm_ref[h] = jnp.max(sb, axis=0, keepdims=True).astype(F32)

    def softmax_stage(jb, slot):
        s_ref, cm_ref = slots[slot]
        for h in range(N_HEADS):
            m_prev = m_ref[h]
            m_new = jnp.maximum(m_prev, cm_ref[h])
            alpha = jnp.exp2(m_prev - m_new)
            pe = jnp.exp2(s_ref[h] - m_new.astype(BF16))
            vt = jnp.concatenate([avt_ref[jb, h * HEAD_DIM:(h + 1) * HEAD_DIM, :], ones], axis=0)
            acc_ref[h] = alpha * acc_ref[h] + jnp.dot(vt, pe, preferred_element_type=F32)
            m_ref[h] = m_new

    odd_diag = diag % 2
    logits_stage(diag, 0, 0)
    logits_stage(jnp.maximum(diag - 1, 0), 1, 1)
    softmax_stage(diag, 0)
    n_steps = diag - 1

    def two_steps(u, carry):
        b = diag - 1 - 2 * u
        logits_stage(b - 1, 0, None)
        softmax_stage(b, 1)
        logits_stage(b - 2, 1, None)
        softmax_stage(b - 1, 0)
        return carry

    lax.fori_loop(0, jnp.maximum(n_steps, 0) // 2, two_steps, 0)

    @pl.when(jnp.logical_and(n_steps >= 1, n_steps % 2 == 1))
    def _():
        logits_stage(0, 0, None)
        softmax_stage(1, 1)

    @pl.when(jnp.logical_and(diag >= 1, odd_diag == 1))
    def _():
        softmax_stage(0, 1)

    @pl.when(jnp.logical_and(diag >= 1, odd_diag == 0))
    def _():
        softmax_stage(0, 0)

    for p in range(N_PAIR):
        halves = []
        for h in (2 * p, 2 * p + 1):
            a = acc_ref[h]
            halves.append(a[:HEAD_DIM, :] / a[HEAD_DIM:HEAD_DIM + 1, :])
        o_ref[:, p * 128:(p + 1) * 128] = jnp.concatenate(halves, axis=0).T.astype(BF16)


def _dsa(proj3, ik3, avt4, iwt, bias_tiles, k_sel):
    bsz, seq, _ = proj3.shape
    nq = seq // DQ
    assert seq % (NK * CNT_BLOCKS) == 0
    resident = dict(pipeline_mode=pl.Buffered(1))
    return pl.pallas_call(
        functools.partial(_dsa_kernel, k_sel=k_sel),
        name="dsa",
        grid=(bsz, nq),
        in_specs=[
            pl.BlockSpec((None, DQ, 512), lambda b, i: (b, i, COL_AQ)),
            pl.BlockSpec((None, DQ, 512), lambda b, i: (b, i, COL_IQ)),
            pl.BlockSpec((IDX_HEADS, DQ), lambda b, i: (0, b * nq + i)),
            pl.BlockSpec((None, seq, 512), lambda b, i: (b, 0, COL_AK), **resident),
            pl.BlockSpec((None, seq // NK, 512, NK), lambda b, i: (b, 0, 0, 0), **resident),
            pl.BlockSpec((None, seq, N_IK), lambda b, i: (b, 0, 0), **resident),
            pl.BlockSpec(bias_tiles.shape, lambda b, i: (0, 0, 0, 0), **resident),
        ],
        out_specs=pl.BlockSpec((None, DQ, 512), lambda b, i: (b, i, 0)),
        out_shape=jax.ShapeDtypeStruct((bsz, seq, 512), BF16),
        scratch_shapes=[
            pltpu.VMEM((seq // NK, NK, DQ), jnp.int32),
            pltpu.VMEM((33, seq // NK, 8, DQ), jnp.int32),
            pltpu.VMEM((seq // NK, 8, DQ), jnp.int32),
            pltpu.VMEM((seq // NK, 8, DQ), jnp.int32),
            pltpu.VMEM((N_HEADS, DQ, 128), BF16),
            pltpu.VMEM((N_HEADS, 1, DQ), F32),
            pltpu.VMEM((N_HEADS, HEAD_DIM + ONES_ROWS, DQ), F32),
            pltpu.VMEM((N_HEADS, NK, DQ), BF16),
            pltpu.VMEM((N_HEADS, 1, DQ), F32),
            pltpu.VMEM((N_HEADS, NK, DQ), BF16),
            pltpu.VMEM((N_HEADS, 1, DQ), F32),
            pltpu.VMEM((IDX_HEADS, NK, DQ), F32),
            pltpu.VMEM((IDX_HEADS, NK, DQ), F32),
            pltpu.VMEM((16, 8, 128), jnp.int32),
        ],
        compiler_params=pltpu.CompilerParams(
            dimension_semantics=("arbitrary", "arbitrary"), vmem_limit_bytes=VMEM_LIMIT),
    )(proj3, proj3, iwt, proj3, avt4, ik3, bias_tiles)


SB_DEAD_MASS = 104.0 * LOG2E


def _sb_kernel(q_ref, k_ref, v_ref, o_ref, qm_ref, uu_ref, carry_ref, acc_ref, z_ref, sp_ref,
               later_ref):
    i = pl.program_id(1)
    diag = (i * QB + QB - 1) // NK

    @pl.when(jnp.logical_and(pl.program_id(0) == 0, i == 0))
    def _():
        kr = lax.broadcasted_iota(jnp.int32, (2 * NK, NK), 0) % NK
        kc = lax.broadcasted_iota(jnp.int32, (2 * NK, NK), 1)
        uu_ref[...] = jnp.where(kr > kc, 1.0, 0.0).astype(BF16)

    _split_heads_into(qm_ref, q_ref[...])
    carry_ref[...] = jnp.zeros(carry_ref.shape, F32)
    acc_ref[...] = jnp.zeros(acc_ref.shape, F32)

    def block(jb, on_diagonal):
        k0 = pl.multiple_of(jb * NK, NK)
        if on_diagonal:
            row = lax.broadcasted_iota(jnp.int32, (2 * QB, NK), 0)
            col = lax.broadcasted_iota(jnp.int32, (2 * QB, NK), 1)
            causal = (k0 + col) < (i * QB + row % QB)
        for p in range(N_PAIR):
            kp = k_ref[pl.ds(k0, NK), p * 128:(p + 1) * 128]
            z_ref[p] = lax.dot_general(qm_ref[p], kp, _NT, preferred_element_type=F32)
        for p in range(N_PAIR):
            z = z_ref[p]
            neg_abs = lax.bitcast_convert_type(
                lax.bitcast_convert_type(z, jnp.int32) | INT_MIN, F32)
            sp = jnp.maximum(z, 0.0) + jnp.log(1.0 + jnp.exp2(neg_abs)) * LOG2E
            if on_diagonal:
                sp = jnp.where(causal, sp, 0.0)
            sp_ref[p] = sp
            hi = sp.astype(BF16)
            lo = (sp - hi.astype(F32)).astype(BF16)
            later_ref[p] = jnp.dot(jnp.concatenate([hi, lo], axis=1), uu_ref[...],
                                   preferred_element_type=F32)
        for p in range(N_PAIR):
            vp = v_ref[pl.ds(k0, NK), p * 128:(p + 1) * 128]
            carry = carry_ref[p]
            sp = sp_ref[p]
            a = jnp.exp2(z_ref[p] - sp - later_ref[p] - carry)
            if on_diagonal:
                a = jnp.where(causal, a, 0.0)
            acc_ref[p] += jnp.dot(a.astype(BF16), vp, preferred_element_type=F32)
            carry_ref[p] = carry + jnp.sum(sp, axis=1, keepdims=True)

    block(diag, True)

    def alive():
        return (jnp.min(carry_ref[...]) <= SB_DEAD_MASS).astype(jnp.int32)

    def cond(state):
        jb, go = state
        return jnp.logical_and(jb >= 0, go > 0)

    def body(state):
        jb, _ = state
        block(jb, False)
        return jb - 1, alive()

    lax.while_loop(cond, body, (diag - 1, alive()))

    for p in range(N_PAIR):
        o_ref[:, p * 128:(p + 1) * 128] = _merge_pair(
            acc_ref[p, :QB, :], acc_ref[p, QB:, :]).astype(BF16)


def _stick_breaking(proj3):
    bsz, seq, _ = proj3.shape
    resident = dict(pipeline_mode=pl.Buffered(1))
    return pl.pallas_call(
        _sb_kernel,
        name="stick_breaking",
        grid=(bsz, seq // QB),
        in_specs=[
            pl.BlockSpec((None, QB, 512), lambda b, i: (b, i, COL_BQ)),
            pl.BlockSpec((None, seq, 512), lambda b, i: (b, 0, COL_BK), **resident),
            pl.BlockSpec((None, seq, 512), lambda b, i: (b, 0, COL_BV), **resident),
        ],
        out_specs=pl.BlockSpec((None, QB, 512), lambda b, i: (b, i, 0)),
        out_shape=jax.ShapeDtypeStruct((bsz, seq, 512), BF16),
        scratch_shapes=[
            pltpu.VMEM((N_PAIR, 2 * QB, 128), BF16),
            pltpu.VMEM((2 * NK, NK), BF16),
            pltpu.VMEM((N_PAIR, 2 * QB, 1), F32),
            pltpu.VMEM((N_PAIR, 2 * QB, 128), F32),
            pltpu.VMEM((N_PAIR, 2 * QB, NK), F32),
            pltpu.VMEM((N_PAIR, 2 * QB, NK), F32),
            pltpu.VMEM((N_PAIR, 2 * QB, NK), F32),
        ],
        compiler_params=pltpu.CompilerParams(
            dimension_semantics=("arbitrary", "arbitrary"), vmem_limit_bytes=VMEM_LIMIT),
    )(proj3, proj3, proj3)


def _merge_kernel(x_ref, ya_ref, yb_ref, cq_ref, g0_ref, g1_ref, g2_ref, mk_ref, mv_ref,
                  wa_ref, wb_ref, wc_ref, wo_ref, gp_ref, o_ref):
    cq = cq_ref[...]
    heads = []
    for h in range(C_HEADS):
        sl = slice(h * C_HEAD_DIM, (h + 1) * C_HEAD_DIM)
        s = lax.dot_general(cq[:, sl], mk_ref[:, sl], _NT,
                            preferred_element_type=F32) * (C_HEAD_DIM ** -0.5)
        e = jnp.exp(s - jnp.max(s, axis=1, keepdims=True))
        p = e / jnp.sum(e, axis=1, keepdims=True)
        heads.append(jnp.dot(p.astype(BF16), mv_ref[:, sl], preferred_element_type=F32))
    yc_pre = jnp.concatenate(heads, axis=1).astype(BF16)
    ya = jnp.dot(ya_ref[...], wa_ref[...], preferred_element_type=F32)
    yb = jnp.dot(yb_ref[...], wb_ref[...], preferred_element_type=F32)
    yc = jnp.dot(yc_pre, wc_ref[...], preferred_element_type=F32)
    merged = (g0_ref[...].astype(F32) * ya + g1_ref[...].astype(F32) * yb
              + g2_ref[...].astype(F32) * yc)
    o = jnp.dot(merged.astype(BF16), wo_ref[...], preferred_element_type=F32)
    o_ref[...] = x_ref[...] + _rms(o, gp_ref[...])


def _merge(x2, ya2, yb2, proj2, mkv3, wa, wb, wc, wo, g_post, seq):
    n = x2.shape[0]
    tm = min(512, seq)
    per_batch = seq // tm
    n_mem = mkv3.shape[1]
    c_dim = C_HEADS * C_HEAD_DIM
    const = lambda t: (0, 0)
    return pl.pallas_call(
        _merge_kernel,
        name="merge",
        grid=(n // tm,),
        in_specs=[
            pl.BlockSpec((tm, D_MODEL), lambda t: (t, 0)),
            pl.BlockSpec((tm, 512), lambda t: (t, 0)),
            pl.BlockSpec((tm, 512), lambda t: (t, 0)),
            pl.BlockSpec((tm, 512), lambda t: (t, COL_CQ)),
            pl.BlockSpec((tm, D_MODEL), lambda t: (t, 0)),
            pl.BlockSpec((tm, D_MODEL), lambda t: (t, 1)),
            pl.BlockSpec((tm, D_MODEL), lambda t: (t, 2)),
            pl.BlockSpec((None, n_mem, c_dim), lambda t: (t // per_batch, 0, 0)),
            pl.BlockSpec((None, n_mem, c_dim), lambda t: (t // per_batch, 0, 1)),
            pl.BlockSpec(wa.shape, const),
            pl.BlockSpec(wb.shape, const),
            pl.BlockSpec(wc.shape, const),
            pl.BlockSpec(wo.shape, const),
            pl.BlockSpec((1, D_MODEL), const),
        ],
        out_specs=pl.BlockSpec((tm, D_MODEL), lambda t: (t, 0)),
        out_shape=jax.ShapeDtypeStruct((n, D_MODEL), F32),
        compiler_params=pltpu.CompilerParams(
            dimension_semantics=("arbitrary",), vmem_limit_bytes=VMEM_LIMIT),
    )(x2, ya2, yb2, proj2, proj2, proj2, proj2, mkv3, mkv3, wa, wb, wc, wo, g_post)


def _ffn_kernel(x_ref, gpre_ref, wg_ref, wu_ref, wo_ref, gpost_ref, o_ref, h_ref, acc_ref):
    k = pl.program_id(1)

    @pl.when(k == 0)
    def _():
        h_ref[...] = _rms(x_ref[...], gpre_ref[...]).astype(BF16)
        acc_ref[...] = jnp.zeros(acc_ref.shape, F32)

    h = h_ref[...]
    g = jnp.dot(h, wg_ref[...], preferred_element_type=F32)
    u = jnp.dot(h, wu_ref[...], preferred_element_type=F32)
    act = (g * jax.nn.sigmoid(g) * u).astype(BF16)
    acc_ref[...] += jnp.dot(act, wo_ref[...], preferred_element_type=F32)

    @pl.when(k == pl.num_programs(1) - 1)
    def _():
        o_ref[...] = x_ref[...] + _rms(acc_ref[...], gpost_ref[...])


def _ffn(x2, g_pre, wg, wu, wo, g_post):
    n = x2.shape[0]
    d_ff = wg.shape[1]
    tm = min(512, n)
    tf = d_ff // 2
    return pl.pallas_call(
        _ffn_kernel,
        name="ffn",
        grid=(n // tm, d_ff // tf),
        in_specs=[
            pl.BlockSpec((tm, D_MODEL), lambda t, k: (t, 0)),
            pl.BlockSpec((1, D_MODEL), lambda t, k: (0, 0)),
            pl.BlockSpec((D_MODEL, tf), lambda t, k: (0, k)),
            pl.BlockSpec((D_MODEL, tf), lambda t, k: (0, k)),
            pl.BlockSpec((tf, D_MODEL), lambda t, k: (k, 0)),
            pl.BlockSpec((1, D_MODEL), lambda t, k: (0, 0)),
        ],
        out_specs=pl.BlockSpec((tm, D_MODEL), lambda t, k: (t, 0)),
        out_shape=jax.ShapeDtypeStruct((n, D_MODEL), F32),
        scratch_shapes=[pltpu.VMEM((tm, D_MODEL), BF16), pltpu.VMEM((tm, D_MODEL), F32)],
        compiler_params=pltpu.CompilerParams(
            dimension_semantics=("arbitrary", "arbitrary"), vmem_limit_bytes=VMEM_LIMIT),
    )(x2, g_pre, wg, wu, wo, g_post)


def _pack_w_in(w, b_gate):
    sizes = (512, 512, 512, IDX_HEADS * 64, 64, IDX_HEADS, 512, 512, 512, 512,
             N_BRANCH * D_MODEL)
    aq, ak, av, iq, ik, iw, bq, bk, bv, cq, gates = jnp.split(w, np.cumsum(sizes)[:-1], axis=1)
    scale = HEAD_DIM ** -0.5
    scale2 = scale * LOG2E
    w_main = jnp.concatenate(
        [gates, aq * scale2, ak, iq * scale, bq * scale2, bk, bv, cq], axis=1).astype(BF16)
    z64 = jnp.zeros((D_MODEL, 64), F32)
    w_ik = jnp.concatenate([ik, z64, z64, ik], axis=1).astype(BF16)
    w_trans = jnp.concatenate(
        [av, iw, jnp.zeros((D_MODEL, N_TRANS - 512 - IDX_HEADS), F32)], axis=1).T.astype(BF16)
    b_main = jnp.concatenate([b_gate, jnp.zeros((N_MAIN - N_GATE,), F32)])[None, :]
    return w_main, w_ik, w_trans, b_main


def kernel(x, mem, rel_bias, g_mix_pre, w_in, b_gate, g_mem, w_mem_kv, w_up_a, w_up_b, w_up_c,
           w_out, g_mix_post, g_ffn_pre, w_ffn_in, w_ffn_out, g_ffn_post):
    bsz, seq, _ = x.shape
    n_mem = mem.shape[1]
    k_sel = min(TOPK_MAX, seq // 4)
    bias_tiles = _bias_tiles(rel_bias)
    x2 = x.reshape(bsz * seq, D_MODEL)
    for l in range(w_in.shape[0]):
        w_main, w_ik, w_trans, b_main = _pack_w_in(w_in[l], b_gate[l])
        proj2, ik2, avt, iwt = _project(x2, g_mix_pre[l][None, :], w_main, b_main, w_ik, w_trans)
        proj3 = proj2.reshape(bsz, seq, N_MAIN)
        mkv = _memkv(mem.reshape(bsz * n_mem, D_MODEL), g_mem[l][None, :],
                     w_mem_kv[l].astype(BF16))
        ya = _dsa(proj3, ik2.reshape(bsz, seq, N_IK), avt.reshape(bsz, seq // NK, 512, NK),
                  iwt, bias_tiles, k_sel)
        yb = _stick_breaking(proj3)
        x2 = _merge(x2, ya.reshape(bsz * seq, 512), yb.reshape(bsz * seq, 512), proj2,
                    mkv.reshape(bsz, n_mem, 2 * C_HEADS * C_HEAD_DIM),
                    w_up_a[l].astype(BF16), w_up_b[l].astype(BF16), w_up_c[l].astype(BF16),
                    w_out[l].astype(BF16), g_mix_post[l][None, :], seq)
        d_ff = w_ffn_out.shape[1]
        w_ffn = w_ffn_in[l].astype(BF16)
        x2 = _ffn(x2, g_ffn_pre[l][None, :], w_ffn[:, :d_ff], w_ffn[:, d_ff:],
                  w_ffn_out[l].astype(BF16), g_ffn_post[l][None, :])
    return x2.reshape(bsz, seq, D_MODEL)
```

```python
import functools

import numpy as np
import jax
import jax.numpy as jnp
from jax import lax
from jax.experimental import pallas as pl
from jax.experimental.pallas import tpu as pltpu

D_MODEL = 1024
CHUNK = 64
HEAD_DIM = 64
N_HEADS = 8
IDX_HEADS = 8
TOPK_MAX = 256
C_HEADS = 4
C_HEAD_DIM = 128
N_BRANCH = 3
REL_BUCKETS = 32
EPS = 1e-6

F32 = jnp.float32
BF16 = jnp.bfloat16
INT_MIN = -2 ** 31
NEG_BIG = -1e30
LOG2E = 1.4426950408889634

QB = 256
DQ = 256
NK = 256
N_PAIR = N_HEADS // 2
ONES_ROWS = 16

N_GATE = N_BRANCH * D_MODEL
COL_AQ, COL_AK, COL_IQ, COL_BQ, COL_BK, COL_BV, COL_CQ = range(N_GATE // 512, N_GATE // 512 + 7)
N_MAIN = N_GATE + 7 * 512
N_IK = 256
N_TRANS = 512 + 16

VMEM_LIMIT = 56 * 1024 * 1024

_NT = (((1,), (1,)), ((), ()))


def _rms(x, g):
    return x * lax.rsqrt(jnp.mean(x * x, axis=-1, keepdims=True) + EPS) * g


def _proj_kernel(x_ref, g_ref, w_ref, b_ref, wik_ref, wt_ref, o_ref, ik_ref, avt_ref, iwt_ref,
                 h_ref, *, n_gate_tiles):
    j = pl.program_id(1)

    @pl.when(j == 0)
    def _():
        hb = _rms(x_ref[...], g_ref[...]).astype(BF16)
        h_ref[...] = hb
        ik_ref[...] = jnp.dot(hb, wik_ref[...], preferred_element_type=F32).astype(BF16)
        tr = lax.dot_general(wt_ref[...], hb, _NT, preferred_element_type=F32)
        for c in range(avt_ref.shape[0]):
            avt_ref[c] = tr[:512, c * NK:(c + 1) * NK].astype(BF16)
        iwt_ref[...] = tr[512:512 + IDX_HEADS, :]

    acc = jnp.dot(h_ref[...], w_ref[...], preferred_element_type=F32)

    @pl.when(j >= n_gate_tiles)
    def _():
        o_ref[...] = acc.astype(BF16)

    @pl.when(j < n_gate_tiles)
    def _():
        o_ref[...] = (0.5 + 0.5 * jnp.tanh(0.5 * (acc + b_ref[...]))).astype(BF16)


def _project(x2, g, w_main, b_main, w_ik, w_trans):
    n = x2.shape[0]
    tm = min(2048, n)
    tn = 512
    grid = (n // tm, N_MAIN // tn)
    return pl.pallas_call(
        functools.partial(_proj_kernel, n_gate_tiles=N_GATE // tn),
        name="in_proj",
        grid=grid,
        in_specs=[
            pl.BlockSpec((tm, D_MODEL), lambda i, j: (i, 0)),
            pl.BlockSpec((1, D_MODEL), lambda i, j: (0, 0)),
            pl.BlockSpec((D_MODEL, tn), lambda i, j: (0, j)),
            pl.BlockSpec((1, tn), lambda i, j: (0, j)),
            pl.BlockSpec((D_MODEL, N_IK), lambda i, j: (0, 0)),
            pl.BlockSpec((N_TRANS, D_MODEL), lambda i, j: (0, 0)),
        ],
        out_specs=[
            pl.BlockSpec((tm, tn), lambda i, j: (i, j)),
            pl.BlockSpec((tm, N_IK), lambda i, j: (i, 0)),
            pl.BlockSpec((tm // NK, 512, NK), lambda i, j: (i, 0, 0)),
            pl.BlockSpec((IDX_HEADS, tm), lambda i, j: (0, i)),
        ],
        out_shape=[
            jax.ShapeDtypeStruct((n, N_MAIN), BF16),
            jax.ShapeDtypeStruct((n, N_IK), BF16),
            jax.ShapeDtypeStruct((n // NK, 512, NK), BF16),
            jax.ShapeDtypeStruct((IDX_HEADS, n), F32),
        ],
        scratch_shapes=[pltpu.VMEM((tm, D_MODEL), BF16)],
        compiler_params=pltpu.CompilerParams(
            dimension_semantics=("arbitrary", "arbitrary"), vmem_limit_bytes=VMEM_LIMIT),
    )(x2, g, w_main, b_main, w_ik, w_trans)


def _memkv_kernel(x_ref, g_ref, w_ref, o_ref):
    hb = _rms(x_ref[...], g_ref[...]).astype(BF16)
    o_ref[...] = jnp.dot(hb, w_ref[...], preferred_element_type=F32).astype(BF16)


def _memkv(mem2, g, w):
    n = mem2.shape[0]
    tm = min(512, n)
    return pl.pallas_call(
        _memkv_kernel,
        name="mem_kv",
        grid=(n // tm,),
        in_specs=[
            pl.BlockSpec((tm, D_MODEL), lambda i: (i, 0)),
            pl.BlockSpec((1, D_MODEL), lambda i: (0, 0)),
            pl.BlockSpec((D_MODEL, w.shape[1]), lambda i: (0, 0)),
        ],
        out_specs=pl.BlockSpec((tm, w.shape[1]), lambda i: (i, 0)),
        out_shape=jax.ShapeDtypeStruct((n, w.shape[1]), BF16),
        compiler_params=pltpu.CompilerParams(
            dimension_semantics=("arbitrary",), vmem_limit_bytes=VMEM_LIMIT),
    )(mem2, g, w)


BIAS_OFFSETS = (0, -NK)
_LOG_BUCKET_STARTS = (12, 16, 23, 32, 46, 64, 91)
FAR_BUCKET = 15
assert DQ == NK


def _bias_kernel(rb_ref, o_ref):
    key = lax.broadcasted_iota(jnp.int32, (NK, DQ), 0)
    qry = lax.broadcasted_iota(jnp.int32, (NK, DQ), 1)
    for c, off in enumerate(BIAS_OFFSETS):
        rel = key - qry + off
        n = jnp.abs(rel)
        large = jnp.full((NK, DQ), 8, jnp.int32)
        for start in _LOG_BUCKET_STARTS:
            large = large + jnp.where(n >= start, 1, 0)
        bucket = jnp.where(rel > 0, REL_BUCKETS // 2, 0) + jnp.where(n < 8, n, large)
        for h in range(N_HEADS):
            val = jnp.full((NK, DQ), rb_ref[0, h], F32)
            for b in range(1, REL_BUCKETS):
                val = jnp.where(bucket == b, rb_ref[b, h], val)
            o_ref[c, h] = (val - rb_ref[FAR_BUCKET, h]) * LOG2E


def _bias_tiles(rel_bias):
    return pl.pallas_call(
        _bias_kernel,
        name="rel_bias_tiles",
        in_specs=[pl.BlockSpec(memory_space=pltpu.SMEM)],
        out_specs=pl.BlockSpec(memory_space=pltpu.VMEM),
        out_shape=jax.ShapeDtypeStruct((len(BIAS_OFFSETS), N_HEADS, NK, DQ), F32),
    )(rel_bias)


def _split_heads_into(qm_ref, q):
    lane = lax.broadcasted_iota(jnp.int32, (QB, 128), 1)
    for p in range(N_PAIR):
        qp = q[:, p * 128:(p + 1) * 128].astype(F32)
        qm_ref[p, :QB, :] = jnp.where(lane < HEAD_DIM, qp, 0.0).astype(BF16)
        qm_ref[p, QB:, :] = jnp.where(lane >= HEAD_DIM, qp, 0.0).astype(BF16)


def _two_stage_pipeline(n, first, second):
    first(0, 0)

    def two(u, carry):
        j = 2 * u
        first(j + 1, 1)
        second(j, 0, False)
        first(j + 2, 0)
        second(j + 1, 1, False)
        return carry

    lax.fori_loop(0, (n - 1) // 2, two, 0)

    @pl.when(n % 2 == 0)
    def _():
        first(n - 1, 1)
        second(n - 2, 0, False)
        second(n - 1, 1, True)

    @pl.when(n % 2 == 1)
    def _():
        second(n - 1, 0, True)


def _merge_pair(o_even, o_odd):
    lane = lax.broadcasted_iota(jnp.int32, (QB, 128), 1)
    return jnp.where(lane < HEAD_DIM, o_even, o_odd)


CNT_BLOCKS = 4
KEY_ROWS = 32
assert NK == 8 * 32


def _bit_transpose32(load_row, tmp_ref, store_row):
    def swap(a, b, j, m):
        t = (a ^ lax.shift_right_logical(b, jnp.int32(j))) & m
        return a ^ t, b ^ (t << j)

    lower = []
    for k in range(16):
        a, b = swap(load_row(k), load_row(k + 16), 16, 0x0000FFFF)
        lower.append(a)
        tmp_ref[k] = b
    for base in (0, 16):
        x = lower if base == 0 else [tmp_ref[k] for k in range(16)]
        j, m = 8, 0x00FF00FF
        while j:
            k = 0
            while k < 16:
                x[k], x[k + j] = swap(x[k], x[k + j], j, m)
                k = (k + j + 1) & ~j
            j >>= 1
            m ^= m << j
        for i in range(16):
            store_row(base + i, x[i])


def _demote_extra_ties(key_ref, alive_ref, kept_ref, thr, need, nkb, n_groups):
    n_blocks = key_ref.shape[0]
    sub =lax.broadcasted_iota(jnp.int32, (8, DQ), 0)
    zero_masks = []
    for c in reversed(range(max(n_blocks - 1, 1).bit_length())):
        zero_masks.append(
            lambda jb, c=c: jnp.where(((jb >> c) & 1) == 0, jnp.int32(-1), jnp.int32(0)))
    for word in (0xFFFF0000, 0xFF00FF00, 0xF0F0F0F0, 0xCCCCCCCC, 0xAAAAAAAA):
        zero_masks.append(lambda jb, word=word: jnp.int32(word - (1 << 32)))
    for c in (2, 1, 0):
        zero_masks.append(lambda jb, c=c: jnp.where(((sub >> c) & 1) == 0, -1, 0))

    remaining = need
    took_zero = None
    for p, zero_mask in enumerate(zero_masks):
        prev_mask = zero_masks[p - 1] if p else None

        def body(g, cnts, zero_mask=zero_mask, prev_mask=prev_mask, took_zero=took_zero):
            cnts = list(cnts)
            for u in range(CNT_BLOCKS):
                jb = g * CNT_BLOCKS + u
                alive = alive_ref[jb]
                if prev_mask is None:
                    kept_ref[jb] = jnp.zeros((8, DQ), jnp.int32)
                else:
                    lows = alive & prev_mask(jb)
                    kept_ref[jb] = kept_ref[jb] | jnp.where(took_zero, 0, lows)
                    alive = jnp.where(took_zero, lows, alive ^ lows)
                    alive_ref[jb] = alive
                cnts[u] = cnts[u] + lax.population_count(alive & zero_mask(jb))
            return tuple(cnts)

        zeros = jnp.zeros((8, DQ), jnp.int32)
        cnts = lax.fori_loop(0, n_groups, body, (zeros,) * CNT_BLOCKS)
        n_zero = jnp.sum(sum(cnts[1:], cnts[0]), axis=0, keepdims=True)
        took_zero = n_zero >= remaining
        remaining = jnp.where(took_zero, remaining, remaining - n_zero)

    last_mask = zero_masks[-1]

    def demote(jb, carry):
        alive = alive_ref[jb]
        kept = kept_ref[jb] | jnp.where(took_zero, alive & last_mask(jb), alive)
        for r in range(32):
            rows = slice(8 * r, 8 * r + 8)
            key = key_ref[jb, rows, :]
            dropped = (lax.shift_right_logical(kept, jnp.int32(31 - r)) & 1) == 0
            key_ref[jb, rows, :] = jnp.where(jnp.logical_and(key == thr, dropped), key - 1, key)
        return carry

    lax.fori_loop(0, nkb, demote, 0)


def _dsa_kernel(aq_ref, iq_ref, iwt_ref, ak_ref, avt_ref, ik_ref, bias_ref, o_ref,
                key_ref, plane_ref, alive_ref, kept_ref, qm_ref, m_ref, acc_ref,
                s0_ref, cm0_ref, s1_ref, cm1_ref, raw0_ref, raw1_ref, tmp_ref, *,
                k_sel):
    diag = pl.program_id(1)
    nkb = diag + 1
    krow = lax.broadcasted_iota(jnp.int32, (KEY_ROWS, DQ), 0)
    qcol = lax.broadcasted_iota(jnp.int32, (KEY_ROWS, DQ), 1)
    qchunk = (diag * DQ + qcol) // CHUNK

    iwt = iwt_ref[...] * (IDX_HEADS ** -0.5)

    def dots_stage(jb, slot):
        raw_ref = (raw0_ref, raw1_ref)[slot]
        k0 = pl.multiple_of(jb * NK, NK)
        for h in range(IDX_HEADS):
            ikh = ik_ref[pl.ds(k0, NK), (h % 2) * 128:(h % 2 + 1) * 128]
            iqp = iq_ref[:, (h // 2) * 128:(h // 2 + 1) * 128]
            raw_ref[h] = lax.dot_general(ikh, iqp, _NT, preferred_element_type=F32)

    def keys_stage(jb, slot, last):
        raw_ref = (raw0_ref, raw1_ref)[slot]
        k0 = pl.multiple_of(jb * NK, NK)
        for c in range(NK // KEY_ROWS):
            rows = slice(c * KEY_ROWS, (c + 1) * KEY_ROWS)
            acc = jnp.zeros((KEY_ROWS, DQ), F32)
            for h in range(IDX_HEADS):
                acc = acc + iwt[h:h + 1, :] * jnp.maximum(raw_ref[h, rows, :], 0.0)
            bits = lax.bitcast_convert_type(acc, jnp.int32)
            key = bits ^ ((bits >> 31) & 0x7FFFFFFF)
            key = jnp.where(key == -1, 0, key)
            if last:
                admissible = ((k0 + c * KEY_ROWS + krow) // CHUNK) <= qchunk
                key = jnp.where(admissible, key, INT_MIN)
            key_ref[jb, rows, :] = key
        for half in range(DQ // 128):
            lanes = slice(half * 128, (half + 1) * 128)

            def load_row(r):
                return key_ref[jb, 8 * r:8 * r + 8, lanes] ^ INT_MIN

            def store_plane(i, v):
                plane_ref[31 - i, jb, :, lanes] = v

            _bit_transpose32(load_row, tmp_ref, store_plane)
        plane_ref[32, jb] = jnp.full((8, DQ), -1, jnp.int32)
        alive_ref[jb] = jnp.full((8, DQ), -1, jnp.int32)

    _two_stage_pipeline(nkb, dots_stage, keys_stage)

    n_groups = (nkb + CNT_BLOCKS - 1) // CNT_BLOCKS

    def pad_block(jb, carry):
        for b in range(33):
            plane_ref[b, jb] = jnp.zeros((8, DQ), jnp.int32)
        alive_ref[jb] = jnp.zeros((8, DQ), jnp.int32)
        return carry

    lax.fori_loop(nkb, n_groups * CNT_BLOCKS, pad_block, 0)

    def select_pass(it, state):
        took_prev, n_above, thr_u = state
        b = 31 - it
        take_prev = took_prev != 0

        def body(g, cnts):
            cnts = list(cnts)
            for u in range(CNT_BLOCKS):
                jb = g * CNT_BLOCKS + u
                alive = alive_ref[jb]
                with_prev = alive & plane_ref[b + 1, jb]
                alive = jnp.where(take_prev, with_prev, alive ^ with_prev)
                alive_ref[jb] = alive
                cnts[u] = cnts[u] + lax.population_count(alive & plane_ref[b, jb])
            return tuple(cnts)

        zeros = jnp.zeros((8, DQ), jnp.int32)
        cnts = lax.fori_loop(0, n_groups, body, (zeros,) * CNT_BLOCKS)
        n_one = jnp.sum(sum(cnts[1:], cnts[0]), axis=0, keepdims=True)
        take = (n_above + n_one) >= k_sel
        n_above = jnp.where(take, n_above, n_above + n_one)
        thr_u = jnp.where(take, thr_u | jnp.left_shift(jnp.int32(1), b), thr_u)
        return take.astype(jnp.int32), n_above, thr_u

    row0 = jnp.zeros((1, DQ), jnp.int32)
    took_last, n_above, thr_u = lax.fori_loop(0, 32, select_pass, (row0 + 1, row0, row0))
    thr = thr_u ^ INT_MIN
    thr = jnp.maximum(thr, INT_MIN + 1)

    def settle(g, cnts):
        cnts = list(cnts)
        for u in range(CNT_BLOCKS):
            jb = g * CNT_BLOCKS + u
            alive = alive_ref[jb]
            with_last = alive & plane_ref[0, jb]
            alive = jnp.where(took_last != 0, with_last, alive ^ with_last)
            alive_ref[jb] = alive
            cnts[u] = cnts[u] + lax.population_count(alive)
        return tuple(cnts)

    zeros = jnp.zeros((8, DQ), jnp.int32)
    cnts = lax.fori_loop(0, n_groups, settle, (zeros,) * CNT_BLOCKS)
    n_tied = jnp.sum(sum(cnts[1:], cnts[0]), axis=0, keepdims=True)
    need = k_sel - n_above
    extra = jnp.logical_and(n_tied > need, thr_u != 0)

    @pl.when(jnp.max(extra.astype(jnp.int32)) > 0)
    def _():
        _demote_extra_ties(key_ref, alive_ref, kept_ref, thr, need, nkb, n_groups)

    lane = lax.broadcasted_iota(jnp.int32, (DQ, 128), 1)
    for p in range(N_PAIR):
        qp = aq_ref[:, p * 128:(p + 1) * 128].astype(F32)
        qm_ref[2 * p] = jnp.where(lane < HEAD_DIM, qp, 0.0).astype(BF16)
        qm_ref[2 * p + 1] = jnp.where(lane >= HEAD_DIM, qp, 0.0).astype(BF16)
    m_ref[...] = jnp.full(m_ref.shape, NEG_BIG, F32)
    acc_ref[...] = jnp.zeros(acc_ref.shape, F32)
    ones = jnp.ones((ONES_ROWS, NK), BF16)

    slots = ((s0_ref, cm0_ref), (s1_ref, cm1_ref))

    def logits_stage(jb, slot, bias_idx):
        s_ref, cm_ref = slots[slot]
        k0 = pl.multiple_of(jb * NK, NK)
        mask = jnp.where(key_ref[jb] >= thr, 0.0, NEG_BIG).astype(BF16)
        for h in range(N_HEADS):
            kp = ak_ref[pl.ds(k0, NK), (h // 2) * 128:(h // 2 + 1) * 128]
            s = lax.dot_general(kp, qm_ref[h], _NT, preferred_element_type=F32)
            if bias_idx is not None:
                s = s + bias_ref[bias_idx, h]
            sb = s.astype(BF16) + mask
            s_ref[h] = sb
            cm_ref[h] = jnp.max(sb, axis=0, keepdims=True).astype(F32)

    def softmax_stage(jb, slot):
        s_ref, cm_ref = slots[slot]
        for h in range(N_HEADS):
            m_prev = m_ref[h]
            m_new = jnp.maximum(m_prev, cm_ref[h])
            alpha = jnp.exp2(m_prev - m_new)
            pe = jnp.exp2(s_ref[h] - m_new.astype(BF16))
            vt = jnp.concatenate([avt_ref[jb, h * HEAD_DIM:(h + 1) * HEAD_DIM, :], ones], axis=0)
            acc_ref[h] = alpha * acc_ref[h] + jnp.dot(vt, pe, preferred_element_type=F32)
            m_ref[h] = m_new

    odd_diag = diag % 2
    logits_stage(diag, 0, 0)
    logits_stage(jnp.maximum(diag - 1, 0), 1, 1)
    softmax_stage(diag, 0)
    n_steps = diag - 1

    def two_steps(u, carry):
        b = diag - 1 - 2 * u
        logits_stage(b - 1, 0, None)
        softmax_stage(b, 1)
        logits_stage(b - 2, 1, None)
        softmax_stage(b - 1, 0)
        return carry

    lax.fori_loop(0, jnp.maximum(n_steps, 0) // 2, two_steps, 0)

    @pl.when(jnp.logical_and(n_steps >= 1, n_steps % 2 == 1))
    def _():
        logits_stage(0, 0, None)
        softmax_stage(1, 1)

    @pl.when(jnp.logical_and(diag >= 1, odd_diag == 1))
    def _():
        softmax_stage(0, 1)

    @pl.when(jnp.logical_and(diag >= 1, odd_diag == 0))
    def _():
        softmax_stage(0, 0)

    for p in range(N_PAIR):
        halves = []
        for h in (2 * p, 2 * p + 1):
            a = acc_ref[h]
            halves.append(a[:HEAD_DIM, :] / a[HEAD_DIM:HEAD_DIM + 1, :])
        o_ref[:, p * 128:(p + 1) * 128] = jnp.concatenate(halves, axis=0).T.astype(BF16)


def _dsa(proj3, ik3, avt4, iwt, bias_tiles, k_sel):
    bsz, seq, _ = proj3.shape
    nq = seq // DQ
    assert seq % (NK * CNT_BLOCKS) == 0
    resident = dict(pipeline_mode=pl.Buffered(1))
    return pl.pallas_call(
        functools.partial(_dsa_kernel, k_sel=k_sel),
        name="dsa",
        grid=(bsz, nq),
        in_specs=[
            pl.BlockSpec((None, DQ, 512), lambda b, i: (b, i, COL_AQ)),
            pl.BlockSpec((None, DQ, 512), lambda b, i: (b, i, COL_IQ)),
            pl.BlockSpec((IDX_HEADS, DQ), lambda b, i: (0, b * nq + i)),
            pl.BlockSpec((None, seq, 512), lambda b, i: (b, 0, COL_AK), **resident),
            pl.BlockSpec((None, seq // NK, 512, NK), lambda b, i: (b, 0, 0, 0), **resident),
            pl.BlockSpec((None, seq, N_IK), lambda b, i: (b, 0, 0), **resident),
            pl.BlockSpec(bias_tiles.shape, lambda b, i: (0, 0, 0, 0), **resident),
        ],
        out_specs=pl.BlockSpec((None, DQ, 512), lambda b, i: (b, i, 0)),
        out_shape=jax.ShapeDtypeStruct((bsz, seq, 512), BF16),
        scratch_shapes=[
            pltpu.VMEM((seq // NK, NK, DQ), jnp.int32),
            pltpu.VMEM((33, seq // NK, 8, DQ), jnp.int32),
            pltpu.VMEM((seq // NK, 8, DQ), jnp.int32),
            pltpu.VMEM((seq // NK, 8, DQ), jnp.int32),
            pltpu.VMEM((N_HEADS, DQ, 128), BF16),
            pltpu.VMEM((N_HEADS, 1, DQ), F32),
            pltpu.VMEM((N_HEADS, HEAD_DIM + ONES_ROWS, DQ), F32),
            pltpu.VMEM((N_HEADS, NK, DQ), BF16),
            pltpu.VMEM((N_HEADS, 1, DQ), F32),
            pltpu.VMEM((N_HEADS, NK, DQ), BF16),
            pltpu.VMEM((N_HEADS, 1, DQ), F32),
            pltpu.VMEM((IDX_HEADS, NK, DQ), F32),
            pltpu.VMEM((IDX_HEADS, NK, DQ), F32),
            pltpu.VMEM((16, 8, 128), jnp.int32),
        ],
        compiler_params=pltpu.CompilerParams(
            dimension_semantics=("arbitrary", "arbitrary"), vmem_limit_bytes=VMEM_LIMIT),
    )(proj3, proj3, iwt, proj3, avt4, ik3, bias_tiles)


SB_DEAD_MASS = 104.0 * LOG2E


def _sb_kernel(q_ref, k_ref, v_ref, o_ref, qm_ref, uu_ref, carry_ref, acc_ref, z_ref, sp_ref,
               later_ref):
    i = pl.program_id(1)
    diag = (i * QB + QB - 1) // NK

    @pl.when(jnp.logical_and(pl.program_id(0) == 0, i == 0))
    def _():
        kr = lax.broadcasted_iota(jnp.int32, (2 * NK, NK), 0) % NK
        kc = lax.broadcasted_iota(jnp.int32, (2 * NK, NK), 1)
        uu_ref[...] = jnp.where(kr > kc, 1.0, 0.0).astype(BF16)

    _split_heads_into(qm_ref, q_ref[...])
    carry_ref[...] = jnp.zeros(carry_ref.shape, F32)
    acc_ref[...] = jnp.zeros(acc_ref.shape, F32)

    def block(jb, on_diagonal):
        k0 = pl.multiple_of(jb * NK, NK)
        if on_diagonal:
            row = lax.broadcasted_iota(jnp.int32, (2 * QB, NK), 0)
            col = lax.broadcasted_iota(jnp.int32, (2 * QB, NK), 1)
            causal = (k0 + col) < (i * QB + row % QB)
        for p in range(N_PAIR):
            kp = k_ref[pl.ds(k0, NK), p * 128:(p + 1) * 128]
            z_ref[p] = lax.dot_general(qm_ref[p], kp, _NT, preferred_element_type=F32)
        for p in range(N_PAIR):
            z = z_ref[p]
            neg_abs = lax.bitcast_convert_type(
                lax.bitcast_convert_type(z, jnp.int32) | INT_MIN, F32)
            sp = jnp.maximum(z, 0.0) + jnp.log(1.0 + jnp.exp2(neg_abs)) * LOG2E
            if on_diagonal:
                sp = jnp.where(causal, sp, 0.0)
            sp_ref[p] = sp
            hi = sp.astype(BF16)
            lo = (sp - hi.astype(F32)).astype(BF16)
            later_ref[p] = jnp.dot(jnp.concatenate([hi, lo], axis=1), uu_ref[...],
                                   preferred_element_type=F32)
        for p in range(N_PAIR):
            vp = v_ref[pl.ds(k0, NK), p * 128:(p + 1) * 128]
            carry = carry_ref[p]
            sp = sp_ref[p]
            a = jnp.exp2(z_ref[p] - sp - later_ref[p] - carry)
            if on_diagonal:
                a = jnp.where(causal, a, 0.0)
            acc_ref[p] += jnp.dot(a.astype(BF16), vp, preferred_element_type=F32)
            carry_ref[p] = carry + jnp.sum(sp, axis=1, keepdims=True)

    block(diag, True)

    def alive():
        return (jnp.min(carry_ref[...]) <= SB_DEAD_MASS).astype(jnp.int32)

    def cond(state):
        jb, go = state
        return jnp.logical_and(jb >= 0, go > 0)

    def body(state):
        jb, _ = state
        block(jb, False)
        return jb - 1, alive()

    lax.while_loop(cond, body, (diag - 1, alive()))

    for p in range(N_PAIR):
        o_ref[:, p * 128:(p + 1) * 128] = _merge_pair(
            acc_ref[p, :QB, :], acc_ref[p, QB:, :]).astype(BF16)


def _stick_breaking(proj3):
    bsz, seq, _ = proj3.shape
    resident = dict(pipeline_mode=pl.Buffered(1))
    return pl.pallas_call(
        _sb_kernel,
        name="stick_breaking",
        grid=(bsz, seq // QB),
        in_specs=[
            pl.BlockSpec((None, QB, 512), lambda b, i: (b, i, COL_BQ)),
            pl.BlockSpec((None, seq, 512), lambda b, i: (b, 0, COL_BK), **resident),
            pl.BlockSpec((None, seq, 512), lambda b, i: (b, 0, COL_BV), **resident),
        ],
        out_specs=pl.BlockSpec((None, QB, 512), lambda b, i: (b, i, 0)),
        out_shape=jax.ShapeDtypeStruct((bsz, seq, 512), BF16),
        scratch_shapes=[
            pltpu.VMEM((N_PAIR, 2 * QB, 128), BF16),
            pltpu.VMEM((2 * NK, NK), BF16),
            pltpu.VMEM((N_PAIR, 2 * QB, 1), F32),
            pltpu.VMEM((N_PAIR, 2 * QB, 128), F32),
            pltpu.VMEM((N_PAIR, 2 * QB, NK), F32),
            pltpu.VMEM((N_PAIR, 2 * QB, NK), F32),
            pltpu.VMEM((N_PAIR, 2 * QB, NK), F32),
        ],
        compiler_params=pltpu.CompilerParams(
            dimension_semantics=("arbitrary", "arbitrary"), vmem_limit_bytes=VMEM_LIMIT),
    )(proj3, proj3, proj3)


def _merge_kernel(x_ref, ya_ref, yb_ref, cq_ref, g0_ref, g1_ref, g2_ref, mk_ref, mv_ref,
                  wa_ref, wb_ref, wc_ref, wo_ref, gp_ref, o_ref):
    cq = cq_ref[...]
    heads = []
    for h in range(C_HEADS):
        sl = slice(h * C_HEAD_DIM, (h + 1) * C_HEAD_DIM)
        s = lax.dot_general(cq[:, sl], mk_ref[:, sl], _NT,
                            preferred_element_type=F32) * (C_HEAD_DIM ** -0.5)
        e = jnp.exp(s - jnp.max(s, axis=1, keepdims=True))
        p = e / jnp.sum(e, axis=1, keepdims=True)
        heads.append(jnp.dot(p.astype(BF16), mv_ref[:, sl], preferred_element_type=F32))
    yc_pre = jnp.concatenate(heads, axis=1).astype(BF16)
    ya = jnp.dot(ya_ref[...], wa_ref[...], preferred_element_type=F32)
    yb = jnp.dot(yb_ref[...], wb_ref[...], preferred_element_type=F32)
    yc = jnp.dot(yc_pre, wc_ref[...], preferred_element_type=F32)
    merged = (g0_ref[...].astype(F32) * ya + g1_ref[...].astype(F32) * yb
              + g2_ref[...].astype(F32) * yc)
    o = jnp.dot(merged.astype(BF16), wo_ref[...], preferred_element_type=F32)
    o_ref[...] = x_ref[...] + _rms(o, gp_ref[...])


def _merge(x2, ya2, yb2, proj2, mkv3, wa, wb, wc, wo, g_post, seq):
    n = x2.shape[0]
    tm = min(512, seq)
    per_batch = seq // tm
    n_mem = mkv3.shape[1]
    c_dim = C_HEADS * C_HEAD_DIM
    const = lambda t: (0, 0)
    return pl.pallas_call(
        _merge_kernel,
        name="merge",
        grid=(n // tm,),
        in_specs=[
            pl.BlockSpec((tm, D_MODEL), lambda t: (t, 0)),
            pl.BlockSpec((tm, 512), lambda t: (t, 0)),
            pl.BlockSpec((tm, 512), lambda t: (t, 0)),
            pl.BlockSpec((tm, 512), lambda t: (t, COL_CQ)),
            pl.BlockSpec((tm, D_MODEL), lambda t: (t, 0)),
            pl.BlockSpec((tm, D_MODEL), lambda t: (t, 1)),
            pl.BlockSpec((tm, D_MODEL), lambda t: (t, 2)),
            pl.BlockSpec((None, n_mem, c_dim), lambda t: (t // per_batch, 0, 0)),
            pl.BlockSpec((None, n_mem, c_dim), lambda t: (t // per_batch, 0, 1)),
            pl.BlockSpec(wa.shape, const),
            pl.BlockSpec(wb.shape, const),
            pl.BlockSpec(wc.shape, const),
            pl.BlockSpec(wo.shape, const),
            pl.BlockSpec((1, D_MODEL), const),
        ],
        out_specs=pl.BlockSpec((tm, D_MODEL), lambda t: (t, 0)),
        out_shape=jax.ShapeDtypeStruct((n, D_MODEL), F32),
        compiler_params=pltpu.CompilerParams(
            dimension_semantics=("arbitrary",), vmem_limit_bytes=VMEM_LIMIT),
    )(x2, ya2, yb2, proj2, proj2, proj2, proj2, mkv3, mkv3, wa, wb, wc, wo, g_post)


def _ffn_kernel(x_ref, gpre_ref, wg_ref, wu_ref, wo_ref, gpost_ref, o_ref, h_ref, acc_ref):
    k = pl.program_id(1)

    @pl.when(k == 0)
    def _():
        h_ref[...] = _rms(x_ref[...], gpre_ref[...]).astype(BF16)
        acc_ref[...] = jnp.zeros(acc_ref.shape, F32)

    h = h_ref[...]
    g = jnp.dot(h, wg_ref[...], preferred_element_type=F32)
    u = jnp.dot(h, wu_ref[...], preferred_element_type=F32)
    act = (g * jax.nn.sigmoid(g) * u).astype(BF16)
    acc_ref[...] += jnp.dot(act, wo_ref[...], preferred_element_type=F32)

    @pl.when(k == pl.num_programs(1) - 1)
    def _():
        o_ref[...] = x_ref[...] + _rms(acc_ref[...], gpost_ref[...])


def _ffn(x2, g_pre, wg, wu, wo, g_post):
    n = x2.shape[0]
    d_ff = wg.shape[1]
    tm = min(512, n)
    tf = d_ff // 2
    return pl.pallas_call(
        _ffn_kernel,
        name="ffn",
        grid=(n // tm, d_ff // tf),
        in_specs=[
            pl.BlockSpec((tm, D_MODEL), lambda t, k: (t, 0)),
            pl.BlockSpec((1, D_MODEL), lambda t, k: (0, 0)),
            pl.BlockSpec((D_MODEL, tf), lambda t, k: (0, k)),
            pl.BlockSpec((D_MODEL, tf), lambda t, k: (0, k)),
            pl.BlockSpec((tf, D_MODEL), lambda t, k: (k, 0)),
            pl.BlockSpec((1, D_MODEL), lambda t, k: (0, 0)),
        ],
        out_specs=pl.BlockSpec((tm, D_MODEL), lambda t, k: (t, 0)),
        out_shape=jax.ShapeDtypeStruct((n, D_MODEL), F32),
        scratch_shapes=[pltpu.VMEM((tm, D_MODEL), BF16), pltpu.VMEM((tm, D_MODEL), F32)],
        compiler_params=pltpu.CompilerParams(
            dimension_semantics=("arbitrary", "arbitrary"), vmem_limit_bytes=VMEM_LIMIT),
    )(x2, g_pre, wg, wu, wo, g_post)


def _pack_w_in(w, b_gate):
    sizes = (512, 512, 512, IDX_HEADS * 64, 64, IDX_HEADS, 512, 512, 512, 512,
             N_BRANCH * D_MODEL)
    aq, ak, av, iq, ik, iw, bq, bk, bv, cq, gates = jnp.split(w, np.cumsum(sizes)[:-1], axis=1)
    scale = HEAD_DIM ** -0.5
    scale2 = scale * LOG2E
    w_main = jnp.concatenate(
        [gates, aq * scale2, ak, iq * scale, bq * scale2, bk, bv, cq], axis=1).astype(BF16)
    z64 = jnp.zeros((D_MODEL, 64), F32)
    w_ik = jnp.concatenate([ik, z64, z64, ik], axis=1).astype(BF16)
    w_trans = jnp.concatenate(
        [av, iw, jnp.zeros((D_MODEL, N_TRANS - 512 - IDX_HEADS), F32)], axis=1).T.astype(BF16)
    b_main = jnp.concatenate([b_gate, jnp.zeros((N_MAIN - N_GATE,), F32)])[None, :]
    return w_main, w_ik, w_trans, b_main


def kernel(x, mem, rel_bias, g_mix_pre, w_in, b_gate, g_mem, w_mem_kv, w_up_a, w_up_b, w_up_c,
           w_out, g_mix_post, g_ffn_pre, w_ffn_in, w_ffn_out, g_ffn_post):
    bsz, seq, _ = x.shape
    n_mem = mem.shape[1]
    k_sel = min(TOPK_MAX, seq // 4)
    bias_tiles = _bias_tiles(rel_bias)
    x2 = x.reshape(bsz * seq, D_MODEL)
    for l in range(w_in.shape[0]):
        w_main, w_ik, w_trans, b_main = _pack_w_in(w_in[l], b_gate[l])
        proj2, ik2, avt, iwt = _project(x2, g_mix_pre[l][None, :], w_main, b_main, w_ik, w_trans)
        proj3 = proj2.reshape(bsz, seq, N_MAIN)
        mkv = _memkv(mem.reshape(bsz * n_mem, D_MODEL), g_mem[l][None, :],
                     w_mem_kv[l].astype(BF16))
        ya = _dsa(proj3, ik2.reshape(bsz, seq, N_IK), avt.reshape(bsz, seq // NK, 512, NK),
                  iwt, bias_tiles, k_sel)
        yb = _stick_breaking(proj3)
        x2 = _merge(x2, ya.reshape(bsz * seq, 512), yb.reshape(bsz * seq, 512), proj2,
                    mkv.reshape(bsz, n_mem, 2 * C_HEADS * C_HEAD_DIM),
                    w_up_a[l].astype(BF16), w_up_b[l].astype(BF16), w_up_c[l].astype(BF16),
                    w_out[l].astype(BF16), g_mix_post[l][None, :], seq)
        d_ff = w_ffn_out.shape[1]
        w_ffn = w_ffn_in[l].astype(BF16)
        x2 = _ffn(x2, g_ffn_pre[l][None, :], w_ffn[:, :d_ff], w_ffn[:, d_ff:],
                  w_ffn_out[l].astype(BF16), g_ffn_post[l][None, :])
    return x2.reshape(bsz, seq, D_MODEL)
```

```python
import functools

import numpy as np
import jax
import jax.numpy as jnp
from jax import lax
from jax.experimental import pallas as pl
from jax.experimental.pallas import tpu as pltpu

D_MODEL = 1024
CHUNK = 64
HEAD_DIM = 64
N_HEADS = 8
IDX_HEADS = 8
TOPK_MAX = 256
C_HEADS = 4
C_HEAD_DIM = 128
N_BRANCH = 3
REL_BUCKETS = 32
EPS = 1e-6

F32 = jnp.float32
BF16 = jnp.bfloat16
INT_MIN = -2 ** 31
NEG_BIG = -1e30
LOG2E = 1.4426950408889634

QB = 256
DQ = 256
NK = 256
N_PAIR = N_HEADS // 2
ONES_ROWS = 16

N_GATE = N_BRANCH * D_MODEL
COL_AQ, COL_AK, COL_IQ, COL_BQ, COL_BK, COL_BV, COL_CQ = range(N_GATE // 512, N_GATE // 512 + 7)
N_MAIN = N_GATE + 7 * 512
N_IK = 256
N_TRANS = 512 + 16

VMEM_LIMIT = 56 * 1024 * 1024

_NT = (((1,), (1,)), ((), ()))


def _rms(x, g):
    return x * lax.rsqrt(jnp.mean(x * x, axis=-1, keepdims=True) + EPS) * g


PROJ_ROWS = 1024


def _proj_kernel(x_ref, g_ref, w_ref, b_ref, wik_ref, wt_ref, o_ref, ik_ref, avt_ref, iwt_ref,
                 h_ref, *, n_gate_tiles):
    j = pl.program_id(1)

    @pl.when(j == 0)
    def _():
        for r0 in range(0, h_ref.shape[0], PROJ_ROWS):
            rows = slice(r0, min(r0 + PROJ_ROWS, h_ref.shape[0]))
            hb = _rms(x_ref[rows, :], g_ref[...]).astype(BF16)
            h_ref[rows, :] = hb
            ik_ref[rows, :] = jnp.dot(hb, wik_ref[...], preferred_element_type=F32).astype(BF16)
            tr = lax.dot_general(wt_ref[...], hb, _NT, preferred_element_type=F32)
            for c in range(hb.shape[0] // NK):
                avt_ref[r0 // NK + c] = tr[:512, c * NK:(c + 1) * NK].astype(BF16)
            iwt_ref[:, rows] = tr[512:512 + IDX_HEADS, :]

    def tile(finish):
        for r0 in range(0, h_ref.shape[0], PROJ_ROWS):
            rows = slice(r0, min(r0 + PROJ_ROWS, h_ref.shape[0]))
            acc = jnp.dot(h_ref[rows, :], w_ref[...], preferred_element_type=F32)
            o_ref[rows, :] = finish(acc).astype(BF16)

    @pl.when(j >= n_gate_tiles)
    def _():
        tile(lambda acc: acc)

    @pl.when(j < n_gate_tiles)
    def _():
        tile(lambda acc: 0.5 + 0.5 * jnp.tanh(0.5 * (acc + b_ref[...])))


def _project(x2, g, w_main, b_main, w_ik, w_trans):
    n = x2.shape[0]
    tm = min(2048, n)
    tn = 512
    grid = (n // tm, N_MAIN // tn)
    return pl.pallas_call(
        functools.partial(_proj_kernel, n_gate_tiles=N_GATE // tn),
        name="in_proj",
        grid=grid,
        in_specs=[
            pl.BlockSpec((tm, D_MODEL), lambda i, j: (i, 0)),
            pl.BlockSpec((1, D_MODEL), lambda i, j: (0, 0)),
            pl.BlockSpec((D_MODEL, tn), lambda i, j: (0, j)),
            pl.BlockSpec((1, tn), lambda i, j: (0, j)),
            pl.BlockSpec((D_MODEL, N_IK), lambda i, j: (0, 0)),
            pl.BlockSpec((N_TRANS, D_MODEL), lambda i, j: (0, 0)),
        ],
        out_specs=[
            pl.BlockSpec((tm, tn), lambda i, j: (i, j)),
            pl.BlockSpec((tm, N_IK), lambda i, j: (i, 0)),
            pl.BlockSpec((tm // NK, 512, NK), lambda i, j: (i, 0, 0)),
            pl.BlockSpec((IDX_HEADS, tm), lambda i, j: (0, i)),
        ],
        out_shape=[
            jax.ShapeDtypeStruct((n, N_MAIN), BF16),
            jax.ShapeDtypeStruct((n, N_IK), BF16),
            jax.ShapeDtypeStruct((n // NK, 512, NK), BF16),
            jax.ShapeDtypeStruct((IDX_HEADS, n), F32),
        ],
        scratch_shapes=[pltpu.VMEM((tm, D_MODEL), BF16)],
        compiler_params=pltpu.CompilerParams(
            dimension_semantics=("arbitrary", "arbitrary"), vmem_limit_bytes=VMEM_LIMIT),
    )(x2, g, w_main, b_main, w_ik, w_trans)


def _memkv_kernel(x_ref, g_ref, w_ref, o_ref):
    hb = _rms(x_ref[...], g_ref[...]).astype(BF16)
    o_ref[...] = jnp.dot(hb, w_ref[...], preferred_element_type=F32).astype(BF16)


def _memkv(mem2, g, w):
    n = mem2.shape[0]
    tm = min(512, n)
    return pl.pallas_call(
        _memkv_kernel,
        name="mem_kv",
        grid=(n // tm,),
        in_specs=[
            pl.BlockSpec((tm, D_MODEL), lambda i: (i, 0)),
            pl.BlockSpec((1, D_MODEL), lambda i: (0, 0)),
            pl.BlockSpec((D_MODEL, w.shape[1]), lambda i: (0, 0)),
        ],
        out_specs=pl.BlockSpec((tm, w.shape[1]), lambda i: (i, 0)),
        out_shape=jax.ShapeDtypeStruct((n, w.shape[1]), BF16),
        compiler_params=pltpu.CompilerParams(
            dimension_semantics=("arbitrary",), vmem_limit_bytes=VMEM_LIMIT),
    )(mem2, g, w)


BIAS_OFFSETS = (0, -NK)
_LOG_BUCKET_STARTS = (12, 16, 23, 32, 46, 64, 91)
FAR_BUCKET = 15
assert DQ == NK


def _bias_kernel(rb_ref, o_ref):
    key = lax.broadcasted_iota(jnp.int32, (NK, DQ), 0)
    qry = lax.broadcasted_iota(jnp.int32, (NK, DQ), 1)
    for c, off in enumerate(BIAS_OFFSETS):
        rel = key - qry + off
        n = jnp.abs(rel)
        large = jnp.full((NK, DQ), 8, jnp.int32)
        for start in _LOG_BUCKET_STARTS:
            large = large + jnp.where(n >= start, 1, 0)
        bucket = jnp.where(rel > 0, REL_BUCKETS // 2, 0) + jnp.where(n < 8, n, large)
        for h in range(N_HEADS):
            val = jnp.full((NK, DQ), rb_ref[0, h], F32)
            for b in range(1, REL_BUCKETS):
                val = jnp.where(bucket == b, rb_ref[b, h], val)
            o_ref[c, h] = (val - rb_ref[FAR_BUCKET, h]) * LOG2E


def _bias_tiles(rel_bias):
    return pl.pallas_call(
        _bias_kernel,
        name="rel_bias_tiles",
        in_specs=[pl.BlockSpec(memory_space=pltpu.SMEM)],
        out_specs=pl.BlockSpec(memory_space=pltpu.VMEM),
        out_shape=jax.ShapeDtypeStruct((len(BIAS_OFFSETS), N_HEADS, NK, DQ), F32),
    )(rel_bias)


def _split_heads_into(qm_ref, q):
    lane = lax.broadcasted_iota(jnp.int32, (QB, 128), 1)
    for p in range(N_PAIR):
        qp = q[:, p * 128:(p + 1) * 128].astype(F32)
        qm_ref[p, :QB, :] = jnp.where(lane < HEAD_DIM, qp, 0.0).astype(BF16)
        qm_ref[p, QB:, :] = jnp.where(lane >= HEAD_DIM, qp, 0.0).astype(BF16)


def _two_stage_pipeline(n, first, second):
    first(0, 0)

    def two(u, carry):
        j = 2 * u
        first(j + 1, 1)
        second(j, 0, False)
        first(j + 2, 0)
        second(j + 1, 1, False)
        return carry

    lax.fori_loop(0, (n - 1) // 2, two, 0)

    @pl.when(n % 2 == 0)
    def _():
        first(n - 1, 1)
        second(n - 2, 0, False)
        second(n - 1, 1, True)

    @pl.when(n % 2 == 1)
    def _():
        second(n - 1, 0, True)


def _merge_pair(o_even, o_odd):
    lane = lax.broadcasted_iota(jnp.int32, (QB, 128), 1)
    return jnp.where(lane < HEAD_DIM, o_even, o_odd)


CNT_BLOCKS = 4
KEY_ROWS = 32
assert NK == 8 * 32


def _bit_transpose32(load_row, tmp_ref, store_row):
    def swap(a, b, j, m):
        t = (a ^ lax.shift_right_logical(b, jnp.int32(j))) & m
        return a ^ t, b ^ (t << j)

    lower = []
    for k in range(16):
        a, b = swap(load_row(k), load_row(k + 16), 16, 0x0000FFFF)
        lower.append(a)
        tmp_ref[k] = b
    for base in (0, 16):
        x = lower if base == 0 else [tmp_ref[k] for k in range(16)]
        j, m = 8, 0x00FF00FF
        while j:
            k = 0
            while k < 16:
                x[k], x[k + j] = swap(x[k], x[k + j], j, m)
                k = (k + j + 1) & ~j
            j >>= 1
            m ^= m << j
        for i in range(16):
            store_row(base + i, x[i])


def _demote_extra_ties(key_ref, alive_ref, kept_ref, thr, need, nkb, n_groups):
    n_blocks = key_ref.shape[0]
    sub =lax.broadcasted_iota(jnp.int32, (8, DQ), 0)
    zero_masks = []
    for c in reversed(range(max(n_blocks - 1, 1).bit_length())):
        zero_masks.append(
            lambda jb, c=c: jnp.where(((jb >> c) & 1) == 0, jnp.int32(-1), jnp.int32(0)))
    for word in (0xFFFF0000, 0xFF00FF00, 0xF0F0F0F0, 0xCCCCCCCC, 0xAAAAAAAA):
        zero_masks.append(lambda jb, word=word: jnp.int32(word - (1 << 32)))
    for c in (2, 1, 0):
        zero_masks.append(lambda jb, c=c: jnp.where(((sub >> c) & 1) == 0, -1, 0))

    remaining = need
    took_zero = None
    for p, zero_mask in enumerate(zero_masks):
        prev_mask = zero_masks[p - 1] if p else None

        def body(g, cnts, zero_mask=zero_mask, prev_mask=prev_mask, took_zero=took_zero):
            cnts = list(cnts)
            for u in range(CNT_BLOCKS):
                jb = g * CNT_BLOCKS + u
                alive = alive_ref[jb]
                if prev_mask is None:
                    kept_ref[jb] = jnp.zeros((8, DQ), jnp.int32)
                else:
                    lows = alive & prev_mask(jb)
                    kept_ref[jb] = kept_ref[jb] | jnp.where(took_zero, 0, lows)
                    alive = jnp.where(took_zero, lows, alive ^ lows)
                    alive_ref[jb] = alive
                cnts[u] = cnts[u] + lax.population_count(alive & zero_mask(jb))
            return tuple(cnts)

        zeros = jnp.zeros((8, DQ), jnp.int32)
        cnts = lax.fori_loop(0, n_groups, body, (zeros,) * CNT_BLOCKS)
        n_zero = jnp.sum(sum(cnts[1:], cnts[0]), axis=0, keepdims=True)
        took_zero = n_zero >= remaining
        remaining = jnp.where(took_zero, remaining, remaining - n_zero)

    last_mask = zero_masks[-1]

    def demote(jb, carry):
        alive = alive_ref[jb]
        kept = kept_ref[jb] | jnp.where(took_zero, alive & last_mask(jb), alive)
        for r in range(32):
            rows = slice(8 * r, 8 * r + 8)
            key = key_ref[jb, rows, :]
            dropped = (lax.shift_right_logical(kept, jnp.int32(31 - r)) & 1) == 0
            key_ref[jb, rows, :] = jnp.where(jnp.logical_and(key == thr, dropped), key - 1, key)
        return carry

    lax.fori_loop(0, nkb, demote, 0)


def _dsa_kernel(aq_ref, iq_ref, iwt_ref, ak_ref, avt_ref, ik_ref, bias_ref, o_ref,
                key_ref, plane_ref, alive_ref, kept_ref, qm_ref, m_ref, acc_ref,
                s0_ref, cm0_ref, s1_ref, cm1_ref, raw0_ref, raw1_ref, tmp_ref, *,
                k_sel):
    diag = pl.program_id(1)
    nkb = diag + 1
    krow = lax.broadcasted_iota(jnp.int32, (KEY_ROWS, DQ), 0)
    qcol = lax.broadcasted_iota(jnp.int32, (KEY_ROWS, DQ), 1)
    qchunk = (diag * DQ + qcol) // CHUNK

    iwt = iwt_ref[...] * (IDX_HEADS ** -0.5)

    def dots_stage(jb, slot):
        raw_ref = (raw0_ref, raw1_ref)[slot]
        k0 = pl.multiple_of(jb * NK, NK)
        for h in range(IDX_HEADS):
            ikh = ik_ref[pl.ds(k0, NK), (h % 2) * 128:(h % 2 + 1) * 128]
            iqp = iq_ref[:, (h // 2) * 128:(h // 2 + 1) * 128]
            raw_ref[h] = lax.dot_general(ikh, iqp, _NT, preferred_element_type=F32)

    def keys_stage(jb, slot, last):
        raw_ref = (raw0_ref, raw1_ref)[slot]
        k0 = pl.multiple_of(jb * NK, NK)
        for c in range(NK // KEY_ROWS):
            rows = slice(c * KEY_ROWS, (c + 1) * KEY_ROWS)
            acc = jnp.zeros((KEY_ROWS, DQ), F32)
            for h in range(IDX_HEADS):
                acc = acc + iwt[h:h + 1, :] * jnp.maximum(raw_ref[h, rows, :], 0.0)
            bits = lax.bitcast_convert_type(acc, jnp.int32)
            key = bits ^ ((bits >> 31) & 0x7FFFFFFF)
            key = jnp.where(key == -1, 0, key)
            if last:
                admissible = ((k0 + c * KEY_ROWS + krow) // CHUNK) <= qchunk
                key = jnp.where(admissible, key, INT_MIN)
            key_ref[jb, rows, :] = key
        for half in range(DQ // 128):
            lanes = slice(half * 128, (half + 1) * 128)

            def load_row(r):
                return key_ref[jb, 8 * r:8 * r + 8, lanes] ^ INT_MIN

            def store_plane(i, v):
                plane_ref[31 - i, jb, :, lanes] = v

            _bit_transpose32(load_row, tmp_ref, store_plane)
        plane_ref[32, jb] = jnp.full((8, DQ), -1, jnp.int32)
        alive_ref[jb] = jnp.full((8, DQ), -1, jnp.int32)

    _two_stage_pipeline(nkb, dots_stage, keys_stage)

    n_groups = (nkb + CNT_BLOCKS - 1) // CNT_BLOCKS

    def pad_block(jb, carry):
        for b in range(33):
            plane_ref[b, jb] = jnp.zeros((8, DQ), jnp.int32)
        alive_ref[jb] = jnp.zeros((8, DQ), jnp.int32)
        return carry

    lax.fori_loop(nkb, n_groups * CNT_BLOCKS, pad_block, 0)

    def select_pass(it, state):
        took_prev, n_above, thr_u = state
        b = 31 - it
        take_prev = took_prev != 0

        def body(g, cnts):
            cnts = list(cnts)
            for u in range(CNT_BLOCKS):
                jb = g * CNT_BLOCKS + u
                alive = alive_ref[jb]
                with_prev = alive & plane_ref[b + 1, jb]
                alive = jnp.where(take_prev, with_prev, alive ^ with_prev)
                alive_ref[jb] = alive
                cnts[u] = cnts[u] + lax.population_count(alive & plane_ref[b, jb])
            return tuple(cnts)

        zeros = jnp.zeros((8, DQ), jnp.int32)
        cnts = lax.fori_loop(0, n_groups, body, (zeros,) * CNT_BLOCKS)
        n_one = jnp.sum(sum(cnts[1:], cnts[0]), axis=0, keepdims=True)
        take = (n_above + n_one) >= k_sel
        n_above = jnp.where(take, n_above, n_above + n_one)
        thr_u = jnp.where(take, thr_u | jnp.left_shift(jnp.int32(1), b), thr_u)
        return take.astype(jnp.int32), n_above, thr_u

    row0 = jnp.zeros((1, DQ), jnp.int32)
    took_last, n_above, thr_u = lax.fori_loop(0, 32, select_pass, (row0 + 1, row0, row0))
    thr = thr_u ^ INT_MIN
    thr = jnp.maximum(thr, INT_MIN + 1)

    def settle(g, cnts):
        cnts = list(cnts)
        for u in range(CNT_BLOCKS):
            jb = g * CNT_BLOCKS + u
            alive = alive_ref[jb]
            with_last = alive & plane_ref[0, jb]
            alive = jnp.where(took_last != 0, with_last, alive ^ with_last)
            alive_ref[jb] = alive
            cnts[u] = cnts[u] + lax.population_count(alive)
        return tuple(cnts)

    zeros = jnp.zeros((8, DQ), jnp.int32)
    cnts = lax.fori_loop(0, n_groups, settle, (zeros,) * CNT_BLOCKS)
    n_tied = jnp.sum(sum(cnts[1:], cnts[0]), axis=0, keepdims=True)
    need = k_sel - n_above
    extra = jnp.logical_and(n_tied > need, thr_u != 0)

    @pl.when(jnp.max(extra.astype(jnp.int32)) > 0)
    def _():
        _demote_extra_ties(key_ref, alive_ref, kept_ref, thr, need, nkb, n_groups)

    lane = lax.broadcasted_iota(jnp.int32, (DQ, 128), 1)
    for p in range(N_PAIR):
        qp = aq_ref[:, p * 128:(p + 1) * 128].astype(F32)
        qm_ref[2 * p] = jnp.where(lane < HEAD_DIM, qp, 0.0).astype(BF16)
        qm_ref[2 * p + 1] = jnp.where(lane >= HEAD_DIM, qp, 0.0).astype(BF16)
    m_ref[...] = jnp.full(m_ref.shape, NEG_BIG, F32)
    acc_ref[...] = jnp.zeros(acc_ref.shape, F32)
    ones = jnp.ones((ONES_ROWS, NK), BF16)

    slots = ((s0_ref, cm0_ref), (s1_ref, cm1_ref))

    def logits_stage(jb, slot, bias_idx):
        s_ref, cm_ref = slots[slot]
        k0 = pl.multiple_of(jb * NK, NK)
        mask = jnp.where(key_ref[jb] >= thr, 0.0, NEG_BIG).astype(BF16)
        for h in range(N_HEADS):
            kp = ak_ref[pl.ds(k0, NK), (h // 2) * 128:(h // 2 + 1) * 128]
            s = lax.dot_general(kp, qm_ref[h], _NT, preferred_element_type=F32)
            if bias_idx is not None:
                s = s + bias_ref[bias_idx, h]
            sb = s.astype(BF16) + mask
            s_ref[h] = sb
            cm_ref[h] = jnp.max(sb, axis=0, keepdims=True).astype(F32)

    def softmax_stage(jb, slot):
        s_ref, cm_ref = slots[slot]
        for h in range(N_HEADS):
            m_prev = m_ref[h]
            m_new = jnp.maximum(m_prev, cm_ref[h])
            alpha = jnp.exp2(m_prev - m_new)
            pe = jnp.exp2(s_ref[h] - m_new.astype(BF16))
            vt = jnp.concatenate([avt_ref[jb, h * HEAD_DIM:(h + 1) * HEAD_DIM, :], ones], axis=0)
            acc_ref[h] = alpha * acc_ref[h] + jnp.dot(vt, pe, preferred_element_type=F32)
            m_ref[h] = m_new

    odd_diag = diag % 2
    logits_stage(diag, 0, 0)
    logits_stage(jnp.maximum(diag - 1, 0), 1, 1)
    softmax_stage(diag, 0)
    n_steps = diag - 1

    def two_steps(u, carry):
        b = diag - 1 - 2 * u
        logits_stage(b - 1, 0, None)
        softmax_stage(b, 1)
        logits_stage(b - 2, 1, None)
        softmax_stage(b - 1, 0)
        return carry

    lax.fori_loop(0, jnp.maximum(n_steps, 0) // 2, two_steps, 0)

    @pl.when(jnp.logical_and(n_steps >= 1, n_steps % 2 == 1))
    def _():
        logits_stage(0, 0, None)
        softmax_stage(1, 1)

    @pl.when(jnp.logical_and(diag >= 1, odd_diag == 1))
    def _():
        softmax_stage(0, 1)

    @pl.when(jnp.logical_and(diag >= 1, odd_diag == 0))
    def _():
        softmax_stage(0, 0)

    for p in range(N_PAIR):
        halves = []
        for h in (2 * p, 2 * p + 1):
            a = acc_ref[h]
            halves.append(a[:HEAD_DIM, :] / a[HEAD_DIM:HEAD_DIM + 1, :])
        o_ref[:, p * 128:(p + 1) * 128] = jnp.concatenate(halves, axis=0).T.astype(BF16)


def _dsa(proj3, ik3, avt4, iwt, bias_tiles, k_sel):
    bsz, seq, _ = proj3.shape
    nq = seq // DQ
    assert seq % (NK * CNT_BLOCKS) == 0
    resident = dict(pipeline_mode=pl.Buffered(1))
    return pl.pallas_call(
        functools.partial(_dsa_kernel, k_sel=k_sel),
        name="dsa",
        grid=(bsz, nq),
        in_specs=[
            pl.BlockSpec((None, DQ, 512), lambda b, i: (b, i, COL_AQ)),
            pl.BlockSpec((None, DQ, 512), lambda b, i: (b, i, COL_IQ)),
            pl.BlockSpec((IDX_HEADS, DQ), lambda b, i: (0, b * nq + i)),
            pl.BlockSpec((None, seq, 512), lambda b, i: (b, 0, COL_AK), **resident),
            pl.BlockSpec((None, seq // NK, 512, NK), lambda b, i: (b, 0, 0, 0), **resident),
            pl.BlockSpec((None, seq, N_IK), lambda b, i: (b, 0, 0), **resident),
            pl.BlockSpec(bias_tiles.shape, lambda b, i: (0, 0, 0, 0), **resident),
        ],
        out_specs=pl.BlockSpec((None, DQ, 512), lambda b, i: (b, i, 0)),
        out_shape=jax.ShapeDtypeStruct((bsz, seq, 512), BF16),
        scratch_shapes=[
            pltpu.VMEM((seq // NK, NK, DQ), jnp.int32),
            pltpu.VMEM((33, seq // NK, 8, DQ), jnp.int32),
            pltpu.VMEM((seq // NK, 8, DQ), jnp.int32),
            pltpu.VMEM((seq // NK, 8, DQ), jnp.int32),
            pltpu.VMEM((N_HEADS, DQ, 128), BF16),
            pltpu.VMEM((N_HEADS, 1, DQ), F32),
            pltpu.VMEM((N_HEADS, HEAD_DIM + ONES_ROWS, DQ), F32),
            pltpu.VMEM((N_HEADS, NK, DQ), BF16),
            pltpu.VMEM((N_HEADS, 1, DQ), F32),
            pltpu.VMEM((N_HEADS, NK, DQ), BF16),
            pltpu.VMEM((N_HEADS, 1, DQ), F32),
            pltpu.VMEM((IDX_HEADS, NK, DQ), F32),
            pltpu.VMEM((IDX_HEADS, NK, DQ), F32),
            pltpu.VMEM((16, 8, 128), jnp.int32),
        ],
        compiler_params=pltpu.CompilerParams(
            dimension_semantics=("arbitrary", "arbitrary"), vmem_limit_bytes=VMEM_LIMIT),
    )(proj3, proj3, iwt, proj3, avt4, ik3, bias_tiles)


SB_DEAD_MASS = 104.0 * LOG2E


def _sb_kernel(q_ref, k_ref, v_ref, o_ref, qm_ref, uu_ref, carry_ref, acc_ref, z_ref, sp_ref,
               later_ref):
    i = pl.program_id(1)
    diag = (i * QB + QB - 1) // NK

    @pl.when(jnp.logical_and(pl.program_id(0) == 0, i == 0))
    def _():
        kr = lax.broadcasted_iota(jnp.int32, (2 * NK, NK), 0) % NK
        kc = lax.broadcasted_iota(jnp.int32, (2 * NK, NK), 1)
        uu_ref[...] = jnp.where(kr > kc, 1.0, 0.0).astype(BF16)

    _split_heads_into(qm_ref, q_ref[...])
    carry_ref[...] = jnp.zeros(carry_ref.shape, F32)
    acc_ref[...] = jnp.zeros(acc_ref.shape, F32)

    def block(jb, on_diagonal):
        k0 = pl.multiple_of(jb * NK, NK)
        if on_diagonal:
            row = lax.broadcasted_iota(jnp.int32, (2 * QB, NK), 0)
            col = lax.broadcasted_iota(jnp.int32, (2 * QB, NK), 1)
            causal = (k0 + col) < (i * QB + row % QB)
        for p in range(N_PAIR):
            kp = k_ref[pl.ds(k0, NK), p * 128:(p + 1) * 128]
            z_ref[p] = lax.dot_general(qm_ref[p], kp, _NT, preferred_element_type=F32)
        for p in range(N_PAIR):
            z = z_ref[p]
            neg_abs = lax.bitcast_convert_type(
                lax.bitcast_convert_type(z, jnp.int32) | INT_MIN, F32)
            sp = jnp.maximum(z, 0.0) + jnp.log(1.0 + jnp.exp2(neg_abs)) * LOG2E
            if on_diagonal:
                sp = jnp.where(causal, sp, 0.0)
            sp_ref[p] = sp
            hi = sp.astype(BF16)
            lo = (sp - hi.astype(F32)).astype(BF16)
            later_ref[p] = jnp.dot(jnp.concatenate([hi, lo], axis=1), uu_ref[...],
                                   preferred_element_type=F32)
        for p in range(N_PAIR):
            vp = v_ref[pl.ds(k0, NK), p * 128:(p + 1) * 128]
            carry = carry_ref[p]
            sp = sp_ref[p]
            a = jnp.exp2(z_ref[p] - sp - later_ref[p] - carry)
            if on_diagonal:
                a = jnp.where(causal, a, 0.0)
            acc_ref[p] += jnp.dot(a.astype(BF16), vp, preferred_element_type=F32)
            carry_ref[p] = carry + jnp.sum(sp, axis=1, keepdims=True)

    block(diag, True)

    def alive():
        return (jnp.min(carry_ref[...]) <= SB_DEAD_MASS).astype(jnp.int32)

    def cond(state):
        jb, go = state
        return jnp.logical_and(jb >= 0, go > 0)

    def body(state):
        jb, _ = state
        block(jb, False)
        return jb - 1, alive()

    lax.while_loop(cond, body, (diag - 1, alive()))

    for p in range(N_PAIR):
        o_ref[:, p * 128:(p + 1) * 128] = _merge_pair(
            acc_ref[p, :QB, :], acc_ref[p, QB:, :]).astype(BF16)


def _stick_breaking(proj3):
    bsz, seq, _ = proj3.shape
    resident = dict(pipeline_mode=pl.Buffered(1))
    return pl.pallas_call(
        _sb_kernel,
        name="stick_breaking",
        grid=(bsz, seq // QB),
        in_specs=[
            pl.BlockSpec((None, QB, 512), lambda b, i: (b, i, COL_BQ)),
            pl.BlockSpec((None, seq, 512), lambda b, i: (b, 0, COL_BK), **resident),
            pl.BlockSpec((None, seq, 512), lambda b, i: (b, 0, COL_BV), **resident),
        ],
        out_specs=pl.BlockSpec((None, QB, 512), lambda b, i: (b, i, 0)),
        out_shape=jax.ShapeDtypeStruct((bsz, seq, 512), BF16),
        scratch_shapes=[
            pltpu.VMEM((N_PAIR, 2 * QB, 128), BF16),
            pltpu.VMEM((2 * NK, NK), BF16),
            pltpu.VMEM((N_PAIR, 2 * QB, 1), F32),
            pltpu.VMEM((N_PAIR, 2 * QB, 128), F32),
            pltpu.VMEM((N_PAIR, 2 * QB, NK), F32),
            pltpu.VMEM((N_PAIR, 2 * QB, NK), F32),
            pltpu.VMEM((N_PAIR, 2 * QB, NK), F32),
        ],
        compiler_params=pltpu.CompilerParams(
            dimension_semantics=("arbitrary", "arbitrary"), vmem_limit_bytes=VMEM_LIMIT),
    )(proj3, proj3, proj3)


MERGE_ROWS = 512


def _merge_kernel(x_ref, ya_ref, yb_ref, cq_ref, g0_ref, g1_ref, g2_ref, mk_ref, mv_ref,
                  wa_ref, wb_ref, wc_ref, wo_ref, gp_ref, o_ref):
    n_rows = x_ref.shape[0]
    for r0 in range(0, n_rows, MERGE_ROWS):
        rows = slice(r0, r0 + MERGE_ROWS)
        heads = []
        for h in range(C_HEADS):
            sl = slice(h * C_HEAD_DIM, (h + 1) * C_HEAD_DIM)
            s = lax.dot_general(cq_ref[rows, sl], mk_ref[:, sl], _NT,
                                preferred_element_type=F32) * (C_HEAD_DIM ** -0.5)
            e = jnp.exp(s - jnp.max(s, axis=1, keepdims=True))
            p = e / jnp.sum(e, axis=1, keepdims=True)
            heads.append(jnp.dot(p.astype(BF16), mv_ref[:, sl], preferred_element_type=F32))
        yc_pre = jnp.concatenate(heads, axis=1).astype(BF16)
        ya = jnp.dot(ya_ref[rows, :], wa_ref[...], preferred_element_type=F32)
        yb = jnp.dot(yb_ref[rows, :], wb_ref[...], preferred_element_type=F32)
        yc = jnp.dot(yc_pre, wc_ref[...], preferred_element_type=F32)
        merged = (g0_ref[rows, :].astype(F32) * ya + g1_ref[rows, :].astype(F32) * yb
                  + g2_ref[rows, :].astype(F32) * yc)
        o = jnp.dot(merged.astype(BF16), wo_ref[...], preferred_element_type=F32)
        o_ref[rows, :] = x_ref[rows, :] + _rms(o, gp_ref[...])


def _merge(x2, ya2, yb2, proj2, mkv3, wa, wb, wc, wo, g_post, seq):
    n = x2.shape[0]
    tm = min(2 * MERGE_ROWS, seq)
    per_batch = seq // tm
    n_mem = mkv3.shape[1]
    c_dim = C_HEADS * C_HEAD_DIM
    const = lambda t: (0, 0)
    return pl.pallas_call(
        _merge_kernel,
        name="merge",
        grid=(n // tm,),
        in_specs=[
            pl.BlockSpec((tm, D_MODEL), lambda t: (t, 0)),
            pl.BlockSpec((tm, 512), lambda t: (t, 0)),
            pl.BlockSpec((tm, 512), lambda t: (t, 0)),
            pl.BlockSpec((tm, 512), lambda t: (t, COL_CQ)),
            pl.BlockSpec((tm, D_MODEL), lambda t: (t, 0)),
            pl.BlockSpec((tm, D_MODEL), lambda t: (t, 1)),
            pl.BlockSpec((tm, D_MODEL), lambda t: (t, 2)),
            pl.BlockSpec((None, n_mem, c_dim), lambda t: (t // per_batch, 0, 0)),
            pl.BlockSpec((None, n_mem, c_dim), lambda t: (t // per_batch, 0, 1)),
            pl.BlockSpec(wa.shape, const),
            pl.BlockSpec(wb.shape, const),
            pl.BlockSpec(wc.shape, const),
            pl.BlockSpec(wo.shape, const),
            pl.BlockSpec((1, D_MODEL), const),
        ],
        out_specs=pl.BlockSpec((tm, D_MODEL), lambda t: (t, 0)),
        out_shape=jax.ShapeDtypeStruct((n, D_MODEL), F32),
        compiler_params=pltpu.CompilerParams(
            dimension_semantics=("arbitrary",), vmem_limit_bytes=VMEM_LIMIT),
    )(x2, ya2, yb2, proj2, proj2, proj2, proj2, mkv3, mkv3, wa, wb, wc, wo, g_post)


def _ffn_kernel(x_ref, gpre_ref, wg_ref, wu_ref, wo_ref, gpost_ref, o_ref, h_ref, acc_ref):
    k = pl.program_id(1)

    @pl.when(k == 0)
    def _():
        h_ref[...] = _rms(x_ref[...], gpre_ref[...]).astype(BF16)
        acc_ref[...] = jnp.zeros(acc_ref.shape, F32)

    h = h_ref[...]
    g = jnp.dot(h, wg_ref[...], preferred_element_type=F32)
    u = jnp.dot(h, wu_ref[...], preferred_element_type=F32)
    act = (g * jax.nn.sigmoid(g) * u).astype(BF16)
    acc_ref[...] += jnp.dot(act, wo_ref[...], preferred_element_type=F32)

    @pl.when(k == pl.num_programs(1) - 1)
    def _():
        o_ref[...] = x_ref[...] + _rms(acc_ref[...], gpost_ref[...])


def _ffn(x2, g_pre, wg, wu, wo, g_post):
    n = x2.shape[0]
    d_ff = wg.shape[1]
    tm = min(512, n)
    tf = d_ff // 2
    return pl.pallas_call(
        _ffn_kernel,
        name="ffn",
        grid=(n // tm, d_ff // tf),
        in_specs=[
            pl.BlockSpec((tm, D_MODEL), lambda t, k: (t, 0)),
            pl.BlockSpec((1, D_MODEL), lambda t, k: (0, 0)),
            pl.BlockSpec((D_MODEL, tf), lambda t, k: (0, k)),
            pl.BlockSpec((D_MODEL, tf), lambda t, k: (0, k)),
            pl.BlockSpec((tf, D_MODEL), lambda t, k: (k, 0)),
            pl.BlockSpec((1, D_MODEL), lambda t, k: (0, 0)),
        ],
        out_specs=pl.BlockSpec((tm, D_MODEL), lambda t, k: (t, 0)),
        out_shape=jax.ShapeDtypeStruct((n, D_MODEL), F32),
        scratch_shapes=[pltpu.VMEM((tm, D_MODEL), BF16), pltpu.VMEM((tm, D_MODEL), F32)],
        compiler_params=pltpu.CompilerParams(
            dimension_semantics=("arbitrary", "arbitrary"), vmem_limit_bytes=VMEM_LIMIT),
    )(x2, g_pre, wg, wu, wo, g_post)


def _pack_w_in(w, b_gate):
    sizes = (512, 512, 512, IDX_HEADS * 64, 64, IDX_HEADS, 512, 512, 512, 512,
             N_BRANCH * D_MODEL)
    aq, ak, av, iq, ik, iw, bq, bk, bv, cq, gates = jnp.split(w, np.cumsum(sizes)[:-1], axis=1)
    scale = HEAD_DIM ** -0.5
    scale2 = scale * LOG2E
    w_main = jnp.concatenate(
        [gates, aq * scale2, ak, iq * scale, bq * scale2, bk, bv, cq], axis=1).astype(BF16)
    z64 = jnp.zeros((D_MODEL, 64), F32)
    w_ik = jnp.concatenate([ik, z64, z64, ik], axis=1).astype(BF16)
    w_trans = jnp.concatenate(
        [av, iw, jnp.zeros((D_MODEL, N_TRANS - 512 - IDX_HEADS), F32)], axis=1).T.astype(BF16)
    b_main = jnp.concatenate([b_gate, jnp.zeros((N_MAIN - N_GATE,), F32)])[None, :]
    return w_main, w_ik, w_trans, b_main


def kernel(x, mem, rel_bias, g_mix_pre, w_in, b_gate, g_mem, w_mem_kv, w_up_a, w_up_b, w_up_c,
           w_out, g_mix_post, g_ffn_pre, w_ffn_in, w_ffn_out, g_ffn_post):
    bsz, seq, _ = x.shape
    n_mem = mem.shape[1]
    k_sel = min(TOPK_MAX, seq // 4)
    bias_tiles = _bias_tiles(rel_bias)
    x2 = x.reshape(bsz * seq, D_MODEL)
    for l in range(w_in.shape[0]):
        w_main, w_ik, w_trans, b_main = _pack_w_in(w_in[l], b_gate[l])
        proj2, ik2, avt, iwt = _project(x2, g_mix_pre[l][None, :], w_main, b_main, w_ik, w_trans)
        proj3 = proj2.reshape(bsz, seq, N_MAIN)
        mkv = _memkv(mem.reshape(bsz * n_mem, D_MODEL), g_mem[l][None, :],
                     w_mem_kv[l].astype(BF16))
        ya = _dsa(proj3, ik2.reshape(bsz, seq, N_IK), avt.reshape(bsz, seq // NK, 512, NK),
                  iwt, bias_tiles, k_sel)
        yb = _stick_breaking(proj3)
        x2 = _merge(x2, ya.reshape(bsz * seq, 512), yb.reshape(bsz * seq, 512), proj2,
                    mkv.reshape(bsz, n_mem, 2 * C_HEADS * C_HEAD_DIM),
                    w_up_a[l].astype(BF16), w_up_b[l].astype(BF16), w_up_c[l].astype(BF16),
                    w_out[l].astype(BF16), g_mix_post[l][None, :], seq)
        d_ff = w_ffn_out.shape[1]
        w_ffn = w_ffn_in[l].astype(BF16)
        x2 = _ffn(x2, g_ffn_pre[l][None, :], w_ffn[:, :d_ff], w_ffn[:, d_ff:],
                  w_ffn_out[l].astype(BF16), g_ffn_post[l][None, :])
    return x2.reshape(bsz, seq, D_MODEL)
```

```python
import functools

import numpy as np
import jax
import jax.numpy as jnp
from jax import lax
from jax.experimental import pallas as pl
from jax.experimental.pallas import tpu as pltpu

D_MODEL = 1024
CHUNK = 64
HEAD_DIM = 64
N_HEADS = 8
IDX_HEADS = 8
TOPK_MAX = 256
C_HEADS = 4
C_HEAD_DIM = 128
N_BRANCH = 3
REL_BUCKETS = 32
EPS = 1e-6

F32 = jnp.float32
BF16 = jnp.bfloat16
INT_MIN = -2 ** 31
NEG_BIG = -1e30
LOG2E = 1.4426950408889634

QB = 256
DQ = 256
NK = 256
N_PAIR = N_HEADS // 2
ONES_ROWS = 16

N_GATE = N_BRANCH * D_MODEL
COL_AQ, COL_AK, COL_IQ, COL_BQ, COL_BK, COL_BV, COL_CQ = range(N_GATE // 512, N_GATE // 512 + 7)
N_MAIN = N_GATE + 7 * 512
N_IK = 256
N_TRANS = 512 + 16

VMEM_LIMIT = 56 * 1024 * 1024

_NT = (((1,), (1,)), ((), ()))


def _rms(x, g):
    return x * lax.rsqrt(jnp.mean(x * x, axis=-1, keepdims=True) + EPS) * g


PROJ_ROWS = 1024


def _proj_kernel(x_ref, g_ref, w_ref, b_ref, wik_ref, wt_ref, o_ref, ik_ref, avt_ref, iwt_ref,
                 h_ref, *, n_gate_tiles):
    j = pl.program_id(1)

    @pl.when(j == 0)
    def _():
        for r0 in range(0, h_ref.shape[0], PROJ_ROWS):
            rows = slice(r0, min(r0 + PROJ_ROWS, h_ref.shape[0]))
            hb = _rms(x_ref[rows, :], g_ref[...]).astype(BF16)
            h_ref[rows, :] = hb
            ik_ref[rows, :] = jnp.dot(hb, wik_ref[...], preferred_element_type=F32).astype(BF16)
            tr = lax.dot_general(wt_ref[...], hb, _NT, preferred_element_type=F32)
            for c in range(hb.shape[0] // NK):
                avt_ref[r0 // NK + c] = tr[:512, c * NK:(c + 1) * NK].astype(BF16)
            iwt_ref[:, rows] = tr[512:512 + IDX_HEADS, :]

    def tile(finish):
        for r0 in range(0, h_ref.shape[0], PROJ_ROWS):
            rows = slice(r0, min(r0 + PROJ_ROWS, h_ref.shape[0]))
            acc = jnp.dot(h_ref[rows, :], w_ref[...], preferred_element_type=F32)
            o_ref[rows, :] = finish(acc).astype(BF16)

    @pl.when(j >= n_gate_tiles)
    def _():
        tile(lambda acc: acc)

    @pl.when(j < n_gate_tiles)
    def _():
        tile(lambda acc: 0.5 + 0.5 * jnp.tanh(0.5 * (acc + b_ref[...])))


def _project(x2, g, w_main, b_main, w_ik, w_trans):
    n = x2.shape[0]
    tm = min(2048, n)
    tn = 512
    grid = (n // tm, N_MAIN // tn)
    return pl.pallas_call(
        functools.partial(_proj_kernel, n_gate_tiles=N_GATE // tn),
        name="in_proj",
        grid=grid,
        in_specs=[
            pl.BlockSpec((tm, D_MODEL), lambda i, j: (i, 0)),
            pl.BlockSpec((1, D_MODEL), lambda i, j: (0, 0)),
            pl.BlockSpec((D_MODEL, tn), lambda i, j: (0, j)),
            pl.BlockSpec((1, tn), lambda i, j: (0, j)),
            pl.BlockSpec((D_MODEL, N_IK), lambda i, j: (0, 0)),
            pl.BlockSpec((N_TRANS, D_MODEL), lambda i, j: (0, 0)),
        ],
        out_specs=[
            pl.BlockSpec((tm, tn), lambda i, j: (i, j)),
            pl.BlockSpec((tm, N_IK), lambda i, j: (i, 0)),
            pl.BlockSpec((tm // NK, 512, NK), lambda i, j: (i, 0, 0)),
            pl.BlockSpec((IDX_HEADS, tm), lambda i, j: (0, i)),
        ],
        out_shape=[
            jax.ShapeDtypeStruct((n, N_MAIN), BF16),
            jax.ShapeDtypeStruct((n, N_IK), BF16),
            jax.ShapeDtypeStruct((n // NK, 512, NK), BF16),
            jax.ShapeDtypeStruct((IDX_HEADS, n), F32),
        ],
        scratch_shapes=[pltpu.VMEM((tm, D_MODEL), BF16)],
        compiler_params=pltpu.CompilerParams(
            dimension_semantics=("arbitrary", "arbitrary"), vmem_limit_bytes=VMEM_LIMIT),
    )(x2, g, w_main, b_main, w_ik, w_trans)


def _memkv_kernel(x_ref, g_ref, w_ref, o_ref):
    hb = _rms(x_ref[...], g_ref[...]).astype(BF16)
    o_ref[...] = jnp.dot(hb, w_ref[...], preferred_element_type=F32).astype(BF16)


def _memkv(mem2, g, w):
    n = mem2.shape[0]
    tm = min(512, n)
    return pl.pallas_call(
        _memkv_kernel,
        name="mem_kv",
        grid=(n // tm,),
        in_specs=[
            pl.BlockSpec((tm, D_MODEL), lambda i: (i, 0)),
            pl.BlockSpec((1, D_MODEL), lambda i: (0, 0)),
            pl.BlockSpec((D_MODEL, w.shape[1]), lambda i: (0, 0)),
        ],
        out_specs=pl.BlockSpec((tm, w.shape[1]), lambda i: (i, 0)),
        out_shape=jax.ShapeDtypeStruct((n, w.shape[1]), BF16),
        compiler_params=pltpu.CompilerParams(
            dimension_semantics=("arbitrary",), vmem_limit_bytes=VMEM_LIMIT),
    )(mem2, g, w)


BIAS_OFFSETS = (0, -NK)
_LOG_BUCKET_STARTS = (12, 16, 23, 32, 46, 64, 91)
FAR_BUCKET = 15
assert DQ == NK


def _bias_kernel(rb_ref, o_ref):
    key = lax.broadcasted_iota(jnp.int32, (NK, DQ), 0)
    qry = lax.broadcasted_iota(jnp.int32, (NK, DQ), 1)
    for c, off in enumerate(BIAS_OFFSETS):
        rel = key - qry + off
        n = jnp.abs(rel)
        large = jnp.full((NK, DQ), 8, jnp.int32)
        for start in _LOG_BUCKET_STARTS:
            large = large + jnp.where(n >= start, 1, 0)
        bucket = jnp.where(rel > 0, REL_BUCKETS // 2, 0) + jnp.where(n < 8, n, large)
        for h in range(N_HEADS):
            val = jnp.full((NK, DQ), rb_ref[0, h], F32)
            for b in range(1, REL_BUCKETS):
                val = jnp.where(bucket == b, rb_ref[b, h], val)
            o_ref[c, h] = (val - rb_ref[FAR_BUCKET, h]) * LOG2E


def _bias_tiles(rel_bias):
    return pl.pallas_call(
        _bias_kernel,
        name="rel_bias_tiles",
        in_specs=[pl.BlockSpec(memory_space=pltpu.SMEM)],
        out_specs=pl.BlockSpec(memory_space=pltpu.VMEM),
        out_shape=jax.ShapeDtypeStruct((len(BIAS_OFFSETS), N_HEADS, NK, DQ), F32),
    )(rel_bias)


def _split_heads_into(qm_ref, q):
    lane = lax.broadcasted_iota(jnp.int32, (QB, 128), 1)
    for p in range(N_PAIR):
        qp = q[:, p * 128:(p + 1) * 128].astype(F32)
        qm_ref[p, :QB, :] = jnp.where(lane < HEAD_DIM, qp, 0.0).astype(BF16)
        qm_ref[p, QB:, :] = jnp.where(lane >= HEAD_DIM, qp, 0.0).astype(BF16)


def _two_stage_pipeline(n, first, second):
    first(0, 0)

    def two(u, carry):
        j = 2 * u
        first(j + 1, 1)
        second(j, 0, False)
        first(j + 2, 0)
        second(j + 1, 1, False)
        return carry

    lax.fori_loop(0, (n - 1) // 2, two, 0)

    @pl.when(n % 2 == 0)
    def _():
        first(n - 1, 1)
        second(n - 2, 0, False)
        second(n - 1, 1, True)

    @pl.when(n % 2 == 1)
    def _():
        second(n - 1, 0, True)


def _merge_pair(o_even, o_odd):
    lane = lax.broadcasted_iota(jnp.int32, (QB, 128), 1)
    return jnp.where(lane < HEAD_DIM, o_even, o_odd)


CNT_BLOCKS = 4
KEY_ROWS = 32
assert NK == 8 * 32


def _bit_transpose32(load_row, tmp_ref, store_row):
    def swap(a, b, j, m):
        t = (a ^ lax.shift_right_logical(b, jnp.int32(j))) & m
        return a ^ t, b ^ (t << j)

    lower = []
    for k in range(16):
        a, b = swap(load_row(k), load_row(k + 16), 16, 0x0000FFFF)
        lower.append(a)
        tmp_ref[k] = b
    for base in (0, 16):
        x = lower if base == 0 else [tmp_ref[k] for k in range(16)]
        j, m = 8, 0x00FF00FF
        while j:
            k = 0
            while k < 16:
                x[k], x[k + j] = swap(x[k], x[k + j], j, m)
                k = (k + j + 1) & ~j
            j >>= 1
            m ^= m << j
        for i in range(16):
            store_row(base + i, x[i])


def _keep_lowest_ties(alive_ref, kept_ref, need, n_groups):
    n_blocks = alive_ref.shape[0]
    sub = lax.broadcasted_iota(jnp.int32, (8, DQ), 0)
    zero_masks = []
    for c in reversed(range(max(n_blocks - 1, 1).bit_length())):
        zero_masks.append(
            lambda jb, c=c: jnp.where(((jb >> c) & 1) == 0, jnp.int32(-1), jnp.int32(0)))
    for word in (0xFFFF0000, 0xFF00FF00, 0xF0F0F0F0, 0xCCCCCCCC, 0xAAAAAAAA):
        zero_masks.append(lambda jb, word=word: jnp.int32(word - (1 << 32)))
    for c in (2, 1, 0):
        zero_masks.append(lambda jb, c=c: jnp.where(((sub >> c) & 1) == 0, -1, 0))

    remaining = need
    took_zero = None
    for p, zero_mask in enumerate(zero_masks):
        prev_mask = zero_masks[p - 1] if p else None

        def body(g, cnts, zero_mask=zero_mask, prev_mask=prev_mask, took_zero=took_zero):
            cnts = list(cnts)
            for u in range(CNT_BLOCKS):
                jb = g * CNT_BLOCKS + u
                alive = alive_ref[jb]
                if prev_mask is None:
                    kept_ref[jb] = jnp.zeros((8, DQ), jnp.int32)
                else:
                    lows = alive & prev_mask(jb)
                    kept_ref[jb] = kept_ref[jb] | jnp.where(took_zero, 0, lows)
                    alive = jnp.where(took_zero, lows, alive ^ lows)
                    alive_ref[jb] = alive
                cnts[u] = cnts[u] + lax.population_count(alive & zero_mask(jb))
            return tuple(cnts)

        zeros = jnp.zeros((8, DQ), jnp.int32)
        cnts = lax.fori_loop(0, n_groups, body, (zeros,) * CNT_BLOCKS)
        n_zero = jnp.sum(sum(cnts[1:], cnts[0]), axis=0, keepdims=True)
        took_zero = n_zero >= remaining
        remaining = jnp.where(took_zero, remaining, remaining - n_zero)

    last_mask = zero_masks[-1]

    def finish(jb, carry):
        alive = alive_ref[jb]
        alive_ref[jb] = kept_ref[jb] | jnp.where(took_zero, alive & last_mask(jb), alive)
        return carry

    lax.fori_loop(0, n_groups * CNT_BLOCKS, finish, 0)


def _dsa_kernel(aq_ref, iq_ref, iwt_ref, ak_ref, avt_ref, ik_ref, bias_ref, o_ref,
                keybuf_ref, plane_ref, alive_ref, above_ref, kept_ref, qm_ref, m_ref, acc_ref,
                s0_ref, cm0_ref, s1_ref, cm1_ref, raw0_ref, raw1_ref, tmp_ref, *,
                k_sel):
    diag = pl.program_id(1)
    nkb = diag + 1
    krow = lax.broadcasted_iota(jnp.int32, (KEY_ROWS, DQ), 0)
    qcol = lax.broadcasted_iota(jnp.int32, (KEY_ROWS, DQ), 1)
    qchunk = (diag * DQ + qcol) // CHUNK

    iwt = iwt_ref[...] * (IDX_HEADS ** -0.5)

    def dots_stage(jb, slot):
        raw_ref = (raw0_ref, raw1_ref)[slot]
        k0 = pl.multiple_of(jb * NK, NK)
        for h in range(IDX_HEADS):
            ikh = ik_ref[pl.ds(k0, NK), (h % 2) * 128:(h % 2 + 1) * 128]
            iqp = iq_ref[:, (h // 2) * 128:(h // 2 + 1) * 128]
            raw_ref[h] = lax.dot_general(ikh, iqp, _NT, preferred_element_type=F32)

    def keys_stage(jb, slot, last):
        raw_ref = (raw0_ref, raw1_ref)[slot]
        k0 = pl.multiple_of(jb * NK, NK)
        for c in range(NK // KEY_ROWS):
            rows = slice(c * KEY_ROWS, (c + 1) * KEY_ROWS)
            acc = jnp.zeros((KEY_ROWS, DQ), F32)
            for h in range(IDX_HEADS):
                acc = acc + iwt[h:h + 1, :] * jnp.maximum(raw_ref[h, rows, :], 0.0)
            bits = lax.bitcast_convert_type(acc, jnp.int32)
            key = bits ^ ((bits >> 31) & 0x7FFFFFFF)
            key = jnp.where(key == -1, 0, key)
            ukey = key ^ INT_MIN
            if last:
                admissible = ((k0 + c * KEY_ROWS + krow) // CHUNK) <= qchunk
                ukey = jnp.where(admissible, ukey, 0)
            keybuf_ref[slot, rows, :] = ukey
        for half in range(DQ // 128):
            lanes = slice(half * 128, (half + 1) * 128)

            def load_row(r):
                return keybuf_ref[slot, 8 * r:8 * r + 8, lanes]

            def store_plane(i, v):
                plane_ref[31 - i, jb, :, lanes] = v

            _bit_transpose32(load_row, tmp_ref, store_plane)
        plane_ref[32, jb] = jnp.full((8, DQ), -1, jnp.int32)
        alive_ref[jb] = jnp.full((8, DQ), -1, jnp.int32)
        above_ref[jb] = jnp.zeros((8, DQ), jnp.int32)

    _two_stage_pipeline(nkb, dots_stage, keys_stage)

    n_groups = (nkb + CNT_BLOCKS - 1) // CNT_BLOCKS

    def pad_block(jb, carry):
        for b in range(33):
            plane_ref[b, jb] = jnp.zeros((8, DQ), jnp.int32)
        alive_ref[jb] = jnp.zeros((8, DQ), jnp.int32)
        above_ref[jb] = jnp.zeros((8, DQ), jnp.int32)
        return carry

    lax.fori_loop(nkb, n_groups * CNT_BLOCKS, pad_block, 0)

    def select_pass(it, state):
        took_prev, n_above, thr_u = state
        b = 31 - it
        take_prev = took_prev != 0

        def body(g, cnts):
            cnts = list(cnts)
            for u in range(CNT_BLOCKS):
                jb = g * CNT_BLOCKS + u
                alive = alive_ref[jb]
                with_prev = alive & plane_ref[b + 1, jb]
                above_ref[jb] = above_ref[jb] | jnp.where(take_prev, 0, with_prev)
                alive = jnp.where(take_prev, with_prev, alive ^ with_prev)
                alive_ref[jb] = alive
                cnts[u] = cnts[u] + lax.population_count(alive & plane_ref[b, jb])
            return tuple(cnts)

        zeros = jnp.zeros((8, DQ), jnp.int32)
        cnts = lax.fori_loop(0, n_groups, body, (zeros,) * CNT_BLOCKS)
        n_one = jnp.sum(sum(cnts[1:], cnts[0]), axis=0, keepdims=True)
        take = (n_above + n_one) >= k_sel
        n_above = jnp.where(take, n_above, n_above + n_one)
        thr_u = jnp.where(take, thr_u | jnp.left_shift(jnp.int32(1), b), thr_u)
        return take.astype(jnp.int32), n_above, thr_u

    row0 = jnp.zeros((1, DQ), jnp.int32)
    took_last, n_above, thr_u = lax.fori_loop(0, 32, select_pass, (row0 + 1, row0, row0))

    def settle(g, cnts):
        cnts = list(cnts)
        for u in range(CNT_BLOCKS):
            jb = g * CNT_BLOCKS + u
            alive = alive_ref[jb]
            with_last = alive & plane_ref[0, jb]
            above_ref[jb] = above_ref[jb] | jnp.where(took_last != 0, 0, with_last)
            alive = jnp.where(took_last != 0, with_last, alive ^ with_last)
            alive_ref[jb] = alive
            cnts[u] = cnts[u] + lax.population_count(alive)
        return tuple(cnts)

    zeros = jnp.zeros((8, DQ), jnp.int32)
    cnts = lax.fori_loop(0, n_groups, settle, (zeros,) * CNT_BLOCKS)
    n_tied = jnp.sum(sum(cnts[1:], cnts[0]), axis=0, keepdims=True)
    need = k_sel - n_above
    real = thr_u != 0
    extra = jnp.logical_and(n_tied > need, real)

    @pl.when(jnp.max(extra.astype(jnp.int32)) > 0)
    def _():
        _keep_lowest_ties(alive_ref, kept_ref, need, n_groups)

    def finalize(jb, carry):
        above_ref[jb] = above_ref[jb] | jnp.where(real, alive_ref[jb], 0)
        return carry

    lax.fori_loop(0, nkb, finalize, 0)

    lane = lax.broadcasted_iota(jnp.int32, (DQ, 128), 1)
    for p in range(N_PAIR):
        qp = aq_ref[:, p * 128:(p + 1) * 128].astype(F32)
        qm_ref[2 * p] = jnp.where(lane < HEAD_DIM, qp, 0.0).astype(BF16)
        qm_ref[2 * p + 1] = jnp.where(lane >= HEAD_DIM, qp, 0.0).astype(BF16)
    m_ref[...] = jnp.full(m_ref.shape, NEG_BIG, F32)
    acc_ref[...] = jnp.zeros(acc_ref.shape, F32)
    ones = jnp.ones((ONES_ROWS, NK), BF16)

    slots = ((s0_ref, cm0_ref), (s1_ref, cm1_ref))

    def logits_stage(jb, slot, bias_idx):
        s_ref, cm_ref = slots[slot]
        k0 = pl.multiple_of(jb * NK, NK)
        sel = above_ref[jb]
        mask = jnp.concatenate(
            [jnp.where((lax.shift_right_logical(sel, jnp.int32(31 - r)) & 1) != 0, 0.0, NEG_BIG)
             for r in range(32)], axis=0).astype(BF16)
        for h in range(N_HEADS):
            kp = ak_ref[pl.ds(k0, NK), (h // 2) * 128:(h // 2 + 1) * 128]
            s = lax.dot_general(kp, qm_ref[h], _NT, preferred_element_type=F32)
            if bias_idx is not None:
                s = s + bias_ref[bias_idx, h]
            sb = s.astype(BF16) + mask
            s_ref[h] = sb
            cm_ref[h] = jnp.max(sb, axis=0, keepdims=True).astype(F32)

    def softmax_stage(jb, slot):
        s_ref, cm_ref = slots[slot]
        for h in range(N_HEADS):
            m_prev = m_ref[h]
            m_new = jnp.maximum(m_prev, cm_ref[h])
            alpha = jnp.exp2(m_prev - m_new)
            pe = jnp.exp2(s_ref[h] - m_new.astype(BF16))
            vt = jnp.concatenate([avt_ref[jb, h * HEAD_DIM:(h + 1) * HEAD_DIM, :], ones], axis=0)
            acc_ref[h] = alpha * acc_ref[h] + jnp.dot(vt, pe, preferred_element_type=F32)
            m_ref[h] = m_new

    odd_diag = diag % 2
    logits_stage(diag, 0, 0)
    logits_stage(jnp.maximum(diag - 1, 0), 1, 1)
    softmax_stage(diag, 0)
    n_steps = diag - 1

    def two_steps(u, carry):
        b = diag - 1 - 2 * u
        logits_stage(b - 1, 0, None)
        softmax_stage(b, 1)
        logits_stage(b - 2, 1, None)
        softmax_stage(b - 1, 0)
        return carry

    lax.fori_loop(0, jnp.maximum(n_steps, 0) // 2, two_steps, 0)

    @pl.when(jnp.logical_and(n_steps >= 1, n_steps % 2 == 1))
    def _():
        logits_stage(0, 0, None)
        softmax_stage(1, 1)

    @pl.when(jnp.logical_and(diag >= 1, odd_diag == 1))
    def _():
        softmax_stage(0, 1)

    @pl.when(jnp.logical_and(diag >= 1, odd_diag == 0))
    def _():
        softmax_stage(0, 0)

    for p in range(N_PAIR):
        halves = []
        for h in (2 * p, 2 * p + 1):
            a = acc_ref[h]
            halves.append(a[:HEAD_DIM, :] / a[HEAD_DIM:HEAD_DIM + 1, :])
        o_ref[:, p * 128:(p + 1) * 128] = jnp.concatenate(halves, axis=0).T.astype(BF16)


def _dsa(proj3, ik3, avt4, iwt, bias_tiles, k_sel):
    bsz, seq, _ = proj3.shape
    nq = seq // DQ
    assert seq % (NK * CNT_BLOCKS) == 0
    resident = dict(pipeline_mode=pl.Buffered(1))
    return pl.pallas_call(
        functools.partial(_dsa_kernel, k_sel=k_sel),
        name="dsa",
        grid=(bsz, nq),
        in_specs=[
            pl.BlockSpec((None, DQ, 512), lambda b, i: (b, i, COL_AQ)),
            pl.BlockSpec((None, DQ, 512), lambda b, i: (b, i, COL_IQ)),
            pl.BlockSpec((IDX_HEADS, DQ), lambda b, i: (0, b * nq + i)),
            pl.BlockSpec((None, seq, 512), lambda b, i: (b, 0, COL_AK), **resident),
            pl.BlockSpec((None, seq // NK, 512, NK), lambda b, i: (b, 0, 0, 0), **resident),
            pl.BlockSpec((None, seq, N_IK), lambda b, i: (b, 0, 0), **resident),
            pl.BlockSpec(bias_tiles.shape, lambda b, i: (0, 0, 0, 0), **resident),
        ],
        out_specs=pl.BlockSpec((None, DQ, 512), lambda b, i: (b, i, 0)),
        out_shape=jax.ShapeDtypeStruct((bsz, seq, 512), BF16),
        scratch_shapes=[
            pltpu.VMEM((2, NK, DQ), jnp.int32),
            pltpu.VMEM((33, seq // NK, 8, DQ), jnp.int32),
            pltpu.VMEM((seq // NK, 8, DQ), jnp.int32),
            pltpu.VMEM((seq // NK, 8, DQ), jnp.int32),
            pltpu.VMEM((seq // NK, 8, DQ), jnp.int32),
            pltpu.VMEM((N_HEADS, DQ, 128), BF16),
            pltpu.VMEM((N_HEADS, 1, DQ), F32),
            pltpu.VMEM((N_HEADS, HEAD_DIM + ONES_ROWS, DQ), F32),
            pltpu.VMEM((N_HEADS, NK, DQ), BF16),
            pltpu.VMEM((N_HEADS, 1, DQ), F32),
            pltpu.VMEM((N_HEADS, NK, DQ), BF16),
            pltpu.VMEM((N_HEADS, 1, DQ), F32),
            pltpu.VMEM((IDX_HEADS, NK, DQ), F32),
            pltpu.VMEM((IDX_HEADS, NK, DQ), F32),
            pltpu.VMEM((16, 8, 128), jnp.int32),
        ],
        compiler_params=pltpu.CompilerParams(
            dimension_semantics=("arbitrary", "arbitrary"), vmem_limit_bytes=VMEM_LIMIT),
    )(proj3, proj3, iwt, proj3, avt4, ik3, bias_tiles)


SB_DEAD_MASS = 104.0 * LOG2E


def _sb_kernel(q_ref, k_ref, v_ref, o_ref, qm_ref, uu_ref, carry_ref, acc_ref, z_ref, sp_ref,
               later_ref):
    i = pl.program_id(1)
    diag = (i * QB + QB - 1) // NK

    @pl.when(jnp.logical_and(pl.program_id(0) == 0, i == 0))
    def _():
        kr = lax.broadcasted_iota(jnp.int32, (2 * NK, NK), 0) % NK
        kc = lax.broadcasted_iota(jnp.int32, (2 * NK, NK), 1)
        uu_ref[...] = jnp.where(kr > kc, 1.0, 0.0).astype(BF16)

    _split_heads_into(qm_ref, q_ref[...])
    carry_ref[...] = jnp.zeros(carry_ref.shape, F32)
    acc_ref[...] = jnp.zeros(acc_ref.shape, F32)

    def block(jb, on_diagonal):
        k0 = pl.multiple_of(jb * NK, NK)
        if on_diagonal:
            row = lax.broadcasted_iota(jnp.int32, (2 * QB, NK), 0)
            col = lax.broadcasted_iota(jnp.int32, (2 * QB, NK), 1)
            causal = (k0 + col) < (i * QB + row % QB)
        for p in range(N_PAIR):
            kp = k_ref[pl.ds(k0, NK), p * 128:(p + 1) * 128]
            z_ref[p] = lax.dot_general(qm_ref[p], kp, _NT, preferred_element_type=F32)
        for p in range(N_PAIR):
            z = z_ref[p]
            neg_abs = lax.bitcast_convert_type(
                lax.bitcast_convert_type(z, jnp.int32) | INT_MIN, F32)
            sp = jnp.maximum(z, 0.0) + jnp.log(1.0 + jnp.exp2(neg_abs)) * LOG2E
            if on_diagonal:
                sp = jnp.where(causal, sp, 0.0)
            sp_ref[p] = sp
            hi = sp.astype(BF16)
            lo = (sp - hi.astype(F32)).astype(BF16)
            later_ref[p] = jnp.dot(jnp.concatenate([hi, lo], axis=1), uu_ref[...],
                                   preferred_element_type=F32)
        for p in range(N_PAIR):
            vp = v_ref[pl.ds(k0, NK), p * 128:(p + 1) * 128]
            carry = carry_ref[p]
            sp = sp_ref[p]
            a = jnp.exp2(z_ref[p] - sp - later_ref[p] - carry)
            if on_diagonal:
                a = jnp.where(causal, a, 0.0)
            acc_ref[p] += jnp.dot(a.astype(BF16), vp, preferred_element_type=F32)
            carry_ref[p] = carry + jnp.sum(sp, axis=1, keepdims=True)

    block(diag, True)

    def alive():
        return (jnp.min(carry_ref[...]) <= SB_DEAD_MASS).astype(jnp.int32)

    def cond(state):
        jb, go = state
        return jnp.logical_and(jb >= 0, go > 0)

    def body(state):
        jb, _ = state
        block(jb, False)
        return jb - 1, alive()

    lax.while_loop(cond, body, (diag - 1, alive()))

    for p in range(N_PAIR):
        o_ref[:, p * 128:(p + 1) * 128] = _merge_pair(
            acc_ref[p, :QB, :], acc_ref[p, QB:, :]).astype(BF16)


def _stick_breaking(proj3):
    bsz, seq, _ = proj3.shape
    resident = dict(pipeline_mode=pl.Buffered(1))
    return pl.pallas_call(
        _sb_kernel,
        name="stick_breaking",
        grid=(bsz, seq // QB),
        in_specs=[
            pl.BlockSpec((None, QB, 512), lambda b, i: (b, i, COL_BQ)),
            pl.BlockSpec((None, seq, 512), lambda b, i: (b, 0, COL_BK), **resident),
            pl.BlockSpec((None, seq, 512), lambda b, i: (b, 0, COL_BV), **resident),
        ],
        out_specs=pl.BlockSpec((None, QB, 512), lambda b, i: (b, i, 0)),
        out_shape=jax.ShapeDtypeStruct((bsz, seq, 512), BF16),
        scratch_shapes=[
            pltpu.VMEM((N_PAIR, 2 * QB, 128), BF16),
            pltpu.VMEM((2 * NK, NK), BF16),
            pltpu.VMEM((N_PAIR, 2 * QB, 1), F32),
            pltpu.VMEM((N_PAIR, 2 * QB, 128), F32),
            pltpu.VMEM((N_PAIR, 2 * QB, NK), F32),
            pltpu.VMEM((N_PAIR, 2 * QB, NK), F32),
            pltpu.VMEM((N_PAIR, 2 * QB, NK), F32),
        ],
        compiler_params=pltpu.CompilerParams(
            dimension_semantics=("arbitrary", "arbitrary"), vmem_limit_bytes=VMEM_LIMIT),
    )(proj3, proj3, proj3)


MERGE_ROWS = 512


def _merge_kernel(x_ref, ya_ref, yb_ref, cq_ref, g0_ref, g1_ref, g2_ref, mk_ref, mv_ref,
                  wa_ref, wb_ref, wc_ref, wo_ref, gp_ref, o_ref):
    n_rows = x_ref.shape[0]
    for r0 in range(0, n_rows, MERGE_ROWS):
        rows = slice(r0, r0 + MERGE_ROWS)
        heads = []
        for h in range(C_HEADS):
            sl = slice(h * C_HEAD_DIM, (h + 1) * C_HEAD_DIM)
            s = lax.dot_general(cq_ref[rows, sl], mk_ref[:, sl], _NT,
                                preferred_element_type=F32) * (C_HEAD_DIM ** -0.5)
            e = jnp.exp(s - jnp.max(s, axis=1, keepdims=True))
            p = e / jnp.sum(e, axis=1, keepdims=True)
            heads.append(jnp.dot(p.astype(BF16), mv_ref[:, sl], preferred_element_type=F32))
        yc_pre = jnp.concatenate(heads, axis=1).astype(BF16)
        ya = jnp.dot(ya_ref[rows, :], wa_ref[...], preferred_element_type=F32)
        yb = jnp.dot(yb_ref[rows, :], wb_ref[...], preferred_element_type=F32)
        yc = jnp.dot(yc_pre, wc_ref[...], preferred_element_type=F32)
        merged = (g0_ref[rows, :].astype(F32) * ya + g1_ref[rows, :].astype(F32) * yb
                  + g2_ref[rows, :].astype(F32) * yc)
        o = jnp.dot(merged.astype(BF16), wo_ref[...], preferred_element_type=F32)
        o_ref[rows, :] = x_ref[rows, :] + _rms(o, gp_ref[...])


def _merge(x2, ya2, yb2, proj2, mkv3, wa, wb, wc, wo, g_post, seq):
    n = x2.shape[0]
    tm = min(2 * MERGE_ROWS, seq)
    per_batch = seq // tm
    n_mem = mkv3.shape[1]
    c_dim = C_HEADS * C_HEAD_DIM
    const = lambda t: (0, 0)
    return pl.pallas_call(
        _merge_kernel,
        name="merge",
        grid=(n // tm,),
        in_specs=[
            pl.BlockSpec((tm, D_MODEL), lambda t: (t, 0)),
            pl.BlockSpec((tm, 512), lambda t: (t, 0)),
            pl.BlockSpec((tm, 512), lambda t: (t, 0)),
            pl.BlockSpec((tm, 512), lambda t: (t, COL_CQ)),
            pl.BlockSpec((tm, D_MODEL), lambda t: (t, 0)),
            pl.BlockSpec((tm, D_MODEL), lambda t: (t, 1)),
            pl.BlockSpec((tm, D_MODEL), lambda t: (t, 2)),
            pl.BlockSpec((None, n_mem, c_dim), lambda t: (t // per_batch, 0, 0)),
            pl.BlockSpec((None, n_mem, c_dim), lambda t: (t // per_batch, 0, 1)),
            pl.BlockSpec(wa.shape, const),
            pl.BlockSpec(wb.shape, const),
            pl.BlockSpec(wc.shape, const),
            pl.BlockSpec(wo.shape, const),
            pl.BlockSpec((1, D_MODEL), const),
        ],
        out_specs=pl.BlockSpec((tm, D_MODEL), lambda t: (t, 0)),
        out_shape=jax.ShapeDtypeStruct((n, D_MODEL), F32),
        compiler_params=pltpu.CompilerParams(
            dimension_semantics=("arbitrary",), vmem_limit_bytes=VMEM_LIMIT),
    )(x2, ya2, yb2, proj2, proj2, proj2, proj2, mkv3, mkv3, wa, wb, wc, wo, g_post)


def _ffn_kernel(x_ref, gpre_ref, wg_ref, wu_ref, wo_ref, gpost_ref, o_ref, h_ref, acc_ref):
    k = pl.program_id(1)

    @pl.when(k == 0)
    def _():
        h_ref[...] = _rms(x_ref[...], gpre_ref[...]).astype(BF16)
        acc_ref[...] = jnp.zeros(acc_ref.shape, F32)

    h = h_ref[...]
    g = jnp.dot(h, wg_ref[...], preferred_element_type=F32)
    u = jnp.dot(h, wu_ref[...], preferred_element_type=F32)
    act = (g * jax.nn.sigmoid(g) * u).astype(BF16)
    acc_ref[...] += jnp.dot(act, wo_ref[...], preferred_element_type=F32)

    @pl.when(k == pl.num_programs(1) - 1)
    def _():
        o_ref[...] = x_ref[...] + _rms(acc_ref[...], gpost_ref[...])


def _ffn(x2, g_pre, wg, wu, wo, g_post):
    n = x2.shape[0]
    d_ff = wg.shape[1]
    tm = min(512, n)
    tf = d_ff // 2
    return pl.pallas_call(
        _ffn_kernel,
        name="ffn",
        grid=(n // tm, d_ff // tf),
        in_specs=[
            pl.BlockSpec((tm, D_MODEL), lambda t, k: (t, 0)),
            pl.BlockSpec((1, D_MODEL), lambda t, k: (0, 0)),
            pl.BlockSpec((D_MODEL, tf), lambda t, k: (0, k)),
            pl.BlockSpec((D_MODEL, tf), lambda t, k: (0, k)),
            pl.BlockSpec((tf, D_MODEL), lambda t, k: (k, 0)),
            pl.BlockSpec((1, D_MODEL), lambda t, k: (0, 0)),
        ],
        out_specs=pl.BlockSpec((tm, D_MODEL), lambda t, k: (t, 0)),
        out_shape=jax.ShapeDtypeStruct((n, D_MODEL), F32),
        scratch_shapes=[pltpu.VMEM((tm, D_MODEL), BF16), pltpu.VMEM((tm, D_MODEL), F32)],
        compiler_params=pltpu.CompilerParams(
            dimension_semantics=("arbitrary", "arbitrary"), vmem_limit_bytes=VMEM_LIMIT),
    )(x2, g_pre, wg, wu, wo, g_post)


def _pack_w_in(w, b_gate):
    sizes = (512, 512, 512, IDX_HEADS * 64, 64, IDX_HEADS, 512, 512, 512, 512,
             N_BRANCH * D_MODEL)
    aq, ak, av, iq, ik, iw, bq, bk, bv, cq, gates = jnp.split(w, np.cumsum(sizes)[:-1], axis=1)
    scale = HEAD_DIM ** -0.5
    scale2 = scale * LOG2E
    w_main = jnp.concatenate(
        [gates, aq * scale2, ak, iq * scale, bq * scale2, bk, bv, cq], axis=1).astype(BF16)
    z64 = jnp.zeros((D_MODEL, 64), F32)
    w_ik = jnp.concatenate([ik, z64, z64, ik], axis=1).astype(BF16)
    w_trans = jnp.concatenate(
        [av, iw, jnp.zeros((D_MODEL, N_TRANS - 512 - IDX_HEADS), F32)], axis=1).T.astype(BF16)
    b_main = jnp.concatenate([b_gate, jnp.zeros((N_MAIN - N_GATE,), F32)])[None, :]
    return w_main, w_ik, w_trans, b_main


def kernel(x, mem, rel_bias, g_mix_pre, w_in, b_gate, g_mem, w_mem_kv, w_up_a, w_up_b, w_up_c,
           w_out, g_mix_post, g_ffn_pre, w_ffn_in, w_ffn_out, g_ffn_post):
    bsz, seq, _ = x.shape
    n_mem = mem.shape[1]
    k_sel = min(TOPK_MAX, seq // 4)
    bias_tiles = _bias_tiles(rel_bias)
    x2 = x.reshape(bsz * seq, D_MODEL)
    for l in range(w_in.shape[0]):
        w_main, w_ik, w_trans, b_main = _pack_w_in(w_in[l], b_gate[l])
        proj2, ik2, avt, iwt = _project(x2, g_mix_pre[l][None, :], w_main, b_main, w_ik, w_trans)
        proj3 = proj2.reshape(bsz, seq, N_MAIN)
        mkv = _memkv(mem.reshape(bsz * n_mem, D_MODEL), g_mem[l][None, :],
                     w_mem_kv[l].astype(BF16))
        ya = _dsa(proj3, ik2.reshape(bsz, seq, N_IK), avt.reshape(bsz, seq // NK, 512, NK),
                  iwt, bias_tiles, k_sel)
        yb = _stick_breaking(proj3)
        x2 = _merge(x2, ya.reshape(bsz * seq, 512), yb.reshape(bsz * seq, 512), proj2,
                    mkv.reshape(bsz, n_mem, 2 * C_HEADS * C_HEAD_DIM),
                    w_up_a[l].astype(BF16), w_up_b[l].astype(BF16), w_up_c[l].astype(BF16),
                    w_out[l].astype(BF16), g_mix_post[l][None, :], seq)
        d_ff = w_ffn_out.shape[1]
        w_ffn = w_ffn_in[l].astype(BF16)
        x2 = _ffn(x2, g_ffn_pre[l][None, :], w_ffn[:, :d_ff], w_ffn[:, d_ff:],
                  w_ffn_out[l].astype(BF16), g_ffn_post[l][None, :])
    return x2.reshape(bsz, seq, D_MODEL)
```

```python
import functools

import numpy as np
import jax
import jax.numpy as jnp
from jax import lax
from jax.experimental import pallas as pl
from jax.experimental.pallas import tpu as pltpu

D_MODEL = 1024
CHUNK = 64
HEAD_DIM = 64
N_HEADS = 8
IDX_HEADS = 8
TOPK_MAX = 256
C_HEADS = 4
C_HEAD_DIM = 128
N_BRANCH = 3
REL_BUCKETS = 32
EPS = 1e-6

F32 = jnp.float32
BF16 = jnp.bfloat16
INT_MIN = -2 ** 31
NEG_BIG = -1e30
LOG2E = 1.4426950408889634

QB = 256
DQ = 256
NK = 256
N_PAIR = N_HEADS // 2
ONES_ROWS = 16

N_GATE = N_BRANCH * D_MODEL
COL_AQ, COL_AK, COL_IQ, COL_BQ, COL_BK, COL_BV, COL_CQ = range(N_GATE // 512, N_GATE // 512 + 7)
N_MAIN = N_GATE + 7 * 512
N_IK = 256
N_TRANS = 512 + 16

VMEM_LIMIT = 56 * 1024 * 1024

_NT = (((1,), (1,)), ((), ()))


def _rms(x, g):
    return x * lax.rsqrt(jnp.mean(x * x, axis=-1, keepdims=True) + EPS) * g


PROJ_ROWS = 1024


def _proj_kernel(x_ref, g_ref, w_ref, b_ref, wik_ref, wt_ref, o_ref, ik_ref, avt_ref, iwt_ref,
                 h_ref, *, n_gate_tiles):
    j = pl.program_id(1)

    @pl.when(j == 0)
    def _():
        for r0 in range(0, h_ref.shape[0], PROJ_ROWS):
            rows = slice(r0, min(r0 + PROJ_ROWS, h_ref.shape[0]))
            hb = _rms(x_ref[rows, :], g_ref[...]).astype(BF16)
            h_ref[rows, :] = hb
            ik_ref[rows, :] = jnp.dot(hb, wik_ref[...], preferred_element_type=F32).astype(BF16)
            tr = lax.dot_general(wt_ref[...], hb, _NT, preferred_element_type=F32)
            for c in range(hb.shape[0] // NK):
                avt_ref[r0 // NK + c] = tr[:512, c * NK:(c + 1) * NK].astype(BF16)
            iwt_ref[:, rows] = tr[512:512 + IDX_HEADS, :]

    def tile(finish):
        for r0 in range(0, h_ref.shape[0], PROJ_ROWS):
            rows = slice(r0, min(r0 + PROJ_ROWS, h_ref.shape[0]))
            acc = jnp.dot(h_ref[rows, :], w_ref[...], preferred_element_type=F32)
            o_ref[rows, :] = finish(acc).astype(BF16)

    @pl.when(j >= n_gate_tiles)
    def _():
        tile(lambda acc: acc)

    @pl.when(j < n_gate_tiles)
    def _():
        tile(lambda acc: 0.5 + 0.5 * jnp.tanh(0.5 * (acc + b_ref[...])))


def _project(x2, g, w_main, b_main, w_ik, w_trans):
    n = x2.shape[0]
    tm = min(2048, n)
    tn = 512
    grid = (n // tm, N_MAIN // tn)
    return pl.pallas_call(
        functools.partial(_proj_kernel, n_gate_tiles=N_GATE // tn),
        name="in_proj",
        grid=grid,
        in_specs=[
            pl.BlockSpec((tm, D_MODEL), lambda i, j: (i, 0)),
            pl.BlockSpec((1, D_MODEL), lambda i, j: (0, 0)),
            pl.BlockSpec((D_MODEL, tn), lambda i, j: (0, j)),
            pl.BlockSpec((1, tn), lambda i, j: (0, j)),
            pl.BlockSpec((D_MODEL, N_IK), lambda i, j: (0, 0)),
            pl.BlockSpec((N_TRANS, D_MODEL), lambda i, j: (0, 0)),
        ],
        out_specs=[
            pl.BlockSpec((tm, tn), lambda i, j: (i, j)),
            pl.BlockSpec((tm, N_IK), lambda i, j: (i, 0)),
            pl.BlockSpec((tm // NK, 512, NK), lambda i, j: (i, 0, 0)),
            pl.BlockSpec((IDX_HEADS, tm), lambda i, j: (0, i)),
        ],
        out_shape=[
            jax.ShapeDtypeStruct((n, N_MAIN), BF16),
            jax.ShapeDtypeStruct((n, N_IK), BF16),
            jax.ShapeDtypeStruct((n // NK, 512, NK), BF16),
            jax.ShapeDtypeStruct((IDX_HEADS, n), F32),
        ],
        scratch_shapes=[pltpu.VMEM((tm, D_MODEL), BF16)],
        compiler_params=pltpu.CompilerParams(
            dimension_semantics=("arbitrary", "arbitrary"), vmem_limit_bytes=VMEM_LIMIT),
    )(x2, g, w_main, b_main, w_ik, w_trans)


def _memkv_kernel(x_ref, g_ref, w_ref, o_ref):
    hb = _rms(x_ref[...], g_ref[...]).astype(BF16)
    o_ref[...] = jnp.dot(hb, w_ref[...], preferred_element_type=F32).astype(BF16)


def _memkv(mem2, g, w):
    n = mem2.shape[0]
    tm = min(512, n)
    return pl.pallas_call(
        _memkv_kernel,
        name="mem_kv",
        grid=(n // tm,),
        in_specs=[
            pl.BlockSpec((tm, D_MODEL), lambda i: (i, 0)),
            pl.BlockSpec((1, D_MODEL), lambda i: (0, 0)),
            pl.BlockSpec((D_MODEL, w.shape[1]), lambda i: (0, 0)),
        ],
        out_specs=pl.BlockSpec((tm, w.shape[1]), lambda i: (i, 0)),
        out_shape=jax.ShapeDtypeStruct((n, w.shape[1]), BF16),
        compiler_params=pltpu.CompilerParams(
            dimension_semantics=("arbitrary",), vmem_limit_bytes=VMEM_LIMIT),
    )(mem2, g, w)


BIAS_OFFSETS = (0, -NK)
_LOG_BUCKET_STARTS = (12, 16, 23, 32, 46, 64, 91)
FAR_BUCKET = 15
assert DQ == NK


def _bias_kernel(rb_ref, o_ref):
    key = lax.broadcasted_iota(jnp.int32, (NK, DQ), 0)
    qry = lax.broadcasted_iota(jnp.int32, (NK, DQ), 1)
    for c, off in enumerate(BIAS_OFFSETS):
        rel = key - qry + off
        n = jnp.abs(rel)
        large = jnp.full((NK, DQ), 8, jnp.int32)
        for start in _LOG_BUCKET_STARTS:
            large = large + jnp.where(n >= start, 1, 0)
        bucket = jnp.where(rel > 0, REL_BUCKETS // 2, 0) + jnp.where(n < 8, n, large)
        for h in range(N_HEADS):
            val = jnp.full((NK, DQ), rb_ref[0, h], F32)
            for b in range(1, REL_BUCKETS):
                val = jnp.where(bucket == b, rb_ref[b, h], val)
            o_ref[c, h] = (val - rb_ref[FAR_BUCKET, h]) * LOG2E


def _bias_tiles(rel_bias):
    return pl.pallas_call(
        _bias_kernel,
        name="rel_bias_tiles",
        in_specs=[pl.BlockSpec(memory_space=pltpu.SMEM)],
        out_specs=pl.BlockSpec(memory_space=pltpu.VMEM),
        out_shape=jax.ShapeDtypeStruct((len(BIAS_OFFSETS), N_HEADS, NK, DQ), F32),
    )(rel_bias)


def _split_heads_into(qm_ref, q):
    lane = lax.broadcasted_iota(jnp.int32, (QB, 128), 1)
    for p in range(N_PAIR):
        qp = q[:, p * 128:(p + 1) * 128].astype(F32)
        qm_ref[p, :QB, :] = jnp.where(lane < HEAD_DIM, qp, 0.0).astype(BF16)
        qm_ref[p, QB:, :] = jnp.where(lane >= HEAD_DIM, qp, 0.0).astype(BF16)


def _two_stage_pipeline(n, first, second):
    first(0, 0)

    def two(u, carry):
        j = 2 * u
        first(j + 1, 1)
        second(j, 0, False)
        first(j + 2, 0)
        second(j + 1, 1, False)
        return carry

    lax.fori_loop(0, (n - 1) // 2, two, 0)

    @pl.when(n % 2 == 0)
    def _():
        first(n - 1, 1)
        second(n - 2, 0, False)
        second(n - 1, 1, True)

    @pl.when(n % 2 == 1)
    def _():
        second(n - 1, 0, True)


def _merge_pair(o_even, o_odd):
    lane = lax.broadcasted_iota(jnp.int32, (QB, 128), 1)
    return jnp.where(lane < HEAD_DIM, o_even, o_odd)


CNT_BLOCKS = 4
KEY_ROWS = 32
assert NK == 8 * 32


def _bit_transpose32(load_row, tmp_ref, store_row):
    def swap(a, b, j, m):
        t = (a ^ lax.shift_right_logical(b, jnp.int32(j))) & m
        return a ^ t, b ^ (t << j)

    lower = []
    for k in range(16):
        a, b = swap(load_row(k), load_row(k + 16), 16, 0x0000FFFF)
        lower.append(a)
        tmp_ref[k] = b
    for base in (0, 16):
        x = lower if base == 0 else [tmp_ref[k] for k in range(16)]
        j, m = 8, 0x00FF00FF
        while j:
            k = 0
            while k < 16:
                x[k], x[k + j] = swap(x[k], x[k + j], j, m)
                k = (k + j + 1) & ~j
            j >>= 1
            m ^= m << j
        for i in range(16):
            store_row(base + i, x[i])


def _keep_lowest_ties(alive_ref, kept_ref, need, n_groups):
    n_blocks = alive_ref.shape[0]
    sub = lax.broadcasted_iota(jnp.int32, (8, DQ), 0)
    zero_masks = []
    for c in reversed(range(max(n_blocks - 1, 1).bit_length())):
        zero_masks.append(
            lambda jb, c=c: jnp.where(((jb >> c) & 1) == 0, jnp.int32(-1), jnp.int32(0)))
    for word in (0xFFFF0000, 0xFF00FF00, 0xF0F0F0F0, 0xCCCCCCCC, 0xAAAAAAAA):
        zero_masks.append(lambda jb, word=word: jnp.int32(word - (1 << 32)))
    for c in (2, 1, 0):
        zero_masks.append(lambda jb, c=c: jnp.where(((sub >> c) & 1) == 0, -1, 0))

    remaining = need
    took_zero = None
    for p, zero_mask in enumerate(zero_masks):
        prev_mask = zero_masks[p - 1] if p else None

        def body(g, cnts, zero_mask=zero_mask, prev_mask=prev_mask, took_zero=took_zero):
            cnts = list(cnts)
            for u in range(CNT_BLOCKS):
                jb = g * CNT_BLOCKS + u
                alive = alive_ref[jb]
                if prev_mask is None:
                    kept_ref[jb] = jnp.zeros((8, DQ), jnp.int32)
                else:
                    lows = alive & prev_mask(jb)
                    kept_ref[jb] = kept_ref[jb] | jnp.where(took_zero, 0, lows)
                    alive = jnp.where(took_zero, lows, alive ^ lows)
                    alive_ref[jb] = alive
                cnts[u] = cnts[u] + lax.population_count(alive & zero_mask(jb))
            return tuple(cnts)

        zeros = jnp.zeros((8, DQ), jnp.int32)
        cnts = lax.fori_loop(0, n_groups, body, (zeros,) * CNT_BLOCKS)
        n_zero = jnp.sum(sum(cnts[1:], cnts[0]), axis=0, keepdims=True)
        took_zero = n_zero >= remaining
        remaining = jnp.where(took_zero, remaining, remaining - n_zero)

    last_mask = zero_masks[-1]

    def finish(jb, carry):
        alive = alive_ref[jb]
        alive_ref[jb] = kept_ref[jb] | jnp.where(took_zero, alive & last_mask(jb), alive)
        return carry

    lax.fori_loop(0, n_groups * CNT_BLOCKS, finish, 0)


def _dsa_kernel(aq_ref, iq_ref, iwt_ref, ak_ref, avt_ref, ik_ref, bias_ref, o_ref,
                keybuf_ref, plane_ref, alive_ref, above_ref, kept_ref, sel_ref, qm_ref, m_ref,
                acc_ref, s0_ref, cm0_ref, s1_ref, cm1_ref, raw0_ref, raw1_ref, tmp_ref, *,
                k_sel):
    step = pl.program_id(1)
    n_q = pl.num_programs(1) - 1
    att = step - 1
    krow = lax.broadcasted_iota(jnp.int32, (KEY_ROWS, DQ), 0)
    qcol = lax.broadcasted_iota(jnp.int32, (KEY_ROWS, DQ), 1)
    qchunk = (step * DQ + qcol) // CHUNK

    iwt = iwt_ref[...] * (IDX_HEADS ** -0.5)

    def dots_stage(jb, slot):
        raw_ref = (raw0_ref, raw1_ref)[slot]
        k0 = pl.multiple_of(jb * NK, NK)
        for h in range(IDX_HEADS):
            ikh = ik_ref[pl.ds(k0, NK), (h % 2) * 128:(h % 2 + 1) * 128]
            iqp = iq_ref[:, (h // 2) * 128:(h // 2 + 1) * 128]
            raw_ref[h] = lax.dot_general(ikh, iqp, _NT, preferred_element_type=F32)

    def keys_stage(jb, slot, last):
        raw_ref = (raw0_ref, raw1_ref)[slot]
        k0 = pl.multiple_of(jb * NK, NK)
        for c in range(NK // KEY_ROWS):
            rows = slice(c * KEY_ROWS, (c + 1) * KEY_ROWS)
            acc = jnp.zeros((KEY_ROWS, DQ), F32)
            for h in range(IDX_HEADS):
                acc = acc + iwt[h:h + 1, :] * jnp.maximum(raw_ref[h, rows, :], 0.0)
            bits = lax.bitcast_convert_type(acc, jnp.int32)
            key = bits ^ ((bits >> 31) & 0x7FFFFFFF)
            key = jnp.where(key == -1, 0, key)
            ukey = key ^ INT_MIN
            if last:
                admissible = ((k0 + c * KEY_ROWS + krow) // CHUNK) <= qchunk
                ukey = jnp.where(admissible, ukey, 0)
            keybuf_ref[slot, rows, :] = ukey
        for half in range(DQ // 128):
            lanes = slice(half * 128, (half + 1) * 128)

            def load_row(r):
                return keybuf_ref[slot, 8 * r:8 * r + 8, lanes]

            def store_plane(i, v):
                plane_ref[31 - i, jb, :, lanes] = v

            _bit_transpose32(load_row, tmp_ref, store_plane)
        plane_ref[32, jb] = jnp.full((8, DQ), -1, jnp.int32)
        alive_ref[jb] = jnp.full((8, DQ), -1, jnp.int32)
        above_ref[jb] = jnp.zeros((8, DQ), jnp.int32)

    ones = jnp.ones((ONES_ROWS, NK), BF16)
    slots = ((s0_ref, cm0_ref), (s1_ref, cm1_ref))

    def logits_stage(jb, slot, bias_idx):
        s_ref, cm_ref = slots[slot]
        k0 = pl.multiple_of(jb * NK, NK)
        sel = sel_ref[jb]
        mask = jnp.concatenate(
            [jnp.where((lax.shift_right_logical(sel, jnp.int32(31 - r)) & 1) != 0, 0.0, NEG_BIG)
             for r in range(32)], axis=0).astype(BF16)
        for h in range(N_HEADS):
            kp = ak_ref[pl.ds(k0, NK), (h // 2) * 128:(h // 2 + 1) * 128]
            s = lax.dot_general(kp, qm_ref[h], _NT, preferred_element_type=F32)
            if bias_idx is not None:
                s = s + bias_ref[bias_idx, h]
            sb = s.astype(BF16) + mask
            s_ref[h] = sb
            cm_ref[h] = jnp.max(sb, axis=0, keepdims=True).astype(F32)

    def softmax_stage(jb, slot):
        s_ref, cm_ref = slots[slot]
        for h in range(N_HEADS):
            m_prev = m_ref[h]
            m_new = jnp.maximum(m_prev, cm_ref[h])
            alpha = jnp.exp2(m_prev - m_new)
            pe = jnp.exp2(s_ref[h] - m_new.astype(BF16))
            vt = jnp.concatenate([avt_ref[jb, h * HEAD_DIM:(h + 1) * HEAD_DIM, :], ones], axis=0)
            acc_ref[h] = alpha * acc_ref[h] + jnp.dot(vt, pe, preferred_element_type=F32)
            m_ref[h] = m_new

    def start_attention():
        lane = lax.broadcasted_iota(jnp.int32, (DQ, 128), 1)
        for p in range(N_PAIR):
            qp = aq_ref[:, p * 128:(p + 1) * 128].astype(F32)
            qm_ref[2 * p] = jnp.where(lane < HEAD_DIM, qp, 0.0).astype(BF16)
            qm_ref[2 * p + 1] = jnp.where(lane >= HEAD_DIM, qp, 0.0).astype(BF16)
        m_ref[...] = jnp.full(m_ref.shape, NEG_BIG, F32)
        acc_ref[...] = jnp.zeros(acc_ref.shape, F32)
        logits_stage(att, 0, 0)
        logits_stage(jnp.maximum(att - 1, 0), 1, 1)
        softmax_stage(att, 0)

    @pl.when(step == 0)
    def _():
        dots_stage(0, 0)
        keys_stage(0, 0, True)

    @pl.when(step == n_q)
    def _():
        start_attention()
        n_steps = att - 1

        def two_steps(u, carry):
            b = att - 1 - 2 * u
            logits_stage(b - 1, 0, None)
            softmax_stage(b, 1)
            logits_stage(b - 2, 1, None)
            softmax_stage(b - 1, 0)
            return carry

        lax.fori_loop(0, jnp.maximum(n_steps, 0) // 2, two_steps, 0)

        @pl.when(jnp.logical_and(n_steps >= 1, n_steps % 2 == 1))
        def _():
            logits_stage(0, 0, None)
            softmax_stage(1, 1)

        @pl.when(jnp.logical_and(att >= 1, att % 2 == 1))
        def _():
            softmax_stage(0, 1)

        @pl.when(jnp.logical_and(att >= 1, att % 2 == 0))
        def _():
            softmax_stage(0, 0)

    @pl.when(jnp.logical_and(step >= 1, step < n_q))
    def _():
        start_attention()
        dots_stage(0, 0)
        n_fused = jnp.maximum(att - 1, 0) // 2

        def fused_trip(u, carry):
            b = att - 1 - 2 * u
            j = 2 * u
            logits_stage(b - 1, 0, None)
            dots_stage(j + 1, 1)
            softmax_stage(b, 1)
            keys_stage(j, 0, False)
            logits_stage(b - 2, 1, None)
            dots_stage(j + 2, 0)
            softmax_stage(b - 1, 0)
            keys_stage(j + 1, 1, False)
            return carry

        lax.fori_loop(0, n_fused, fused_trip, 0)
        j0 = 2 * n_fused

        @pl.when(att == 0)
        def _():
            dots_stage(1, 1)
            keys_stage(0, 0, False)
            keys_stage(1, 1, True)

        @pl.when(att % 2 == 1)
        def _():
            dots_stage(j0 + 1, 1)
            softmax_stage(0, 1)
            keys_stage(j0, 0, False)
            dots_stage(j0 + 2, 0)
            keys_stage(j0 + 1, 1, False)
            keys_stage(j0 + 2, 0, True)

        @pl.when(jnp.logical_and(att >= 2, att % 2 == 0))
        def _():
            logits_stage(0, 0, None)
            dots_stage(j0 + 1, 1)
            softmax_stage(1, 1)
            keys_stage(j0, 0, False)
            dots_stage(j0 + 2, 0)
            softmax_stage(0, 0)
            keys_stage(j0 + 1, 1, False)
            dots_stage(j0 + 3, 1)
            keys_stage(j0 + 2, 0, False)
            keys_stage(j0 + 3, 1, True)

    @pl.when(step >= 1)
    def _():
        for p in range(N_PAIR):
            halves = []
            for h in (2 * p, 2 * p + 1):
                a = acc_ref[h]
                halves.append(a[:HEAD_DIM, :] / a[HEAD_DIM:HEAD_DIM + 1, :])
            o_ref[:, p * 128:(p + 1) * 128] = jnp.concatenate(halves, axis=0).T.astype(BF16)

    nkb = jnp.where(step < n_q, step + 1, 0)
    n_groups = (nkb + CNT_BLOCKS - 1) // CNT_BLOCKS

    def pad_block(jb, carry):
        for b in range(33):
            plane_ref[b, jb] = jnp.zeros((8, DQ), jnp.int32)
        alive_ref[jb] = jnp.zeros((8, DQ), jnp.int32)
        above_ref[jb] = jnp.zeros((8, DQ), jnp.int32)
        return carry

    lax.fori_loop(nkb, n_groups * CNT_BLOCKS, pad_block, 0)

    def select_pass(it, state):
        took_prev, n_above, thr_u = state
        b = 31 - it
        take_prev = took_prev != 0

        def body(g, cnts):
            cnts = list(cnts)
            for u in range(CNT_BLOCKS):
                jb = g * CNT_BLOCKS + u
                alive = alive_ref[jb]
                with_prev = alive & plane_ref[b + 1, jb]
                above_ref[jb] = above_ref[jb] | jnp.where(take_prev, 0, with_prev)
                alive = jnp.where(take_prev, with_prev, alive ^ with_prev)
                alive_ref[jb] = alive
                cnts[u] = cnts[u] + lax.population_count(alive & plane_ref[b, jb])
            return tuple(cnts)

        zeros = jnp.zeros((8, DQ), jnp.int32)
        cnts = lax.fori_loop(0, n_groups, body, (zeros,) * CNT_BLOCKS)
        n_one = jnp.sum(sum(cnts[1:], cnts[0]), axis=0, keepdims=True)
        take = (n_above + n_one) >= k_sel
        n_above = jnp.where(take, n_above, n_above + n_one)
        thr_u = jnp.where(take, thr_u | jnp.left_shift(jnp.int32(1), b), thr_u)
        return take.astype(jnp.int32), n_above, thr_u

    row0 = jnp.zeros((1, DQ), jnp.int32)
    took_last, n_above, thr_u = lax.fori_loop(0, 32, select_pass, (row0 + 1, row0, row0))

    def settle(g, cnts):
        cnts = list(cnts)
        for u in range(CNT_BLOCKS):
            jb = g * CNT_BLOCKS + u
            alive = alive_ref[jb]
            with_last = alive & plane_ref[0, jb]
            above_ref[jb] = above_ref[jb] | jnp.where(took_last != 0, 0, with_last)
            alive = jnp.where(took_last != 0, with_last, alive ^ with_last)
            alive_ref[jb] = alive
            cnts[u] = cnts[u] + lax.population_count(alive)
        return tuple(cnts)

    zeros = jnp.zeros((8, DQ), jnp.int32)
    cnts = lax.fori_loop(0, n_groups, settle, (zeros,) * CNT_BLOCKS)
    n_tied = jnp.sum(sum(cnts[1:], cnts[0]), axis=0, keepdims=True)
    need = k_sel - n_above
    real = thr_u != 0
    extra = jnp.logical_and(n_tied > need, real)

    @pl.when(jnp.max(extra.astype(jnp.int32)) > 0)
    def _():
        _keep_lowest_ties(alive_ref, kept_ref, need, n_groups)

    def finalize(jb, carry):
        sel_ref[jb] = above_ref[jb] | jnp.where(real, alive_ref[jb], 0)
        return carry

    lax.fori_loop(0, nkb, finalize, 0)


def _dsa(proj3, ik3, avt4, iwt, bias_tiles, k_sel):
    bsz, seq, _ = proj3.shape
    nq = seq // DQ
    assert seq % (NK * CNT_BLOCKS) == 0
    resident = dict(pipeline_mode=pl.Buffered(1))
    return pl.pallas_call(
        functools.partial(_dsa_kernel, k_sel=k_sel),
        name="dsa",
        grid=(bsz, nq + 1),
        in_specs=[
            pl.BlockSpec((None, DQ, 512), lambda b, s: (b, jnp.maximum(s - 1, 0), COL_AQ)),
            pl.BlockSpec((None, DQ, 512), lambda b, s: (b, jnp.minimum(s, nq - 1), COL_IQ)),
            pl.BlockSpec((IDX_HEADS, DQ), lambda b, s: (0, b * nq + jnp.minimum(s, nq - 1))),
            pl.BlockSpec((None, seq, 512), lambda b, i: (b, 0, COL_AK), **resident),
            pl.BlockSpec((None, seq // NK, 512, NK), lambda b, i: (b, 0, 0, 0), **resident),
            pl.BlockSpec((None, seq, N_IK), lambda b, i: (b, 0, 0), **resident),
            pl.BlockSpec(bias_tiles.shape, lambda b, i: (0, 0, 0, 0), **resident),
        ],
        out_specs=pl.BlockSpec((None, DQ, 512), lambda b, s: (b, jnp.maximum(s - 1, 0), 0)),
        out_shape=jax.ShapeDtypeStruct((bsz, seq, 512), BF16),
        scratch_shapes=[
            pltpu.VMEM((2, NK, DQ), jnp.int32),
            pltpu.VMEM((33, seq // NK, 8, DQ), jnp.int32),
            pltpu.VMEM((seq // NK, 8, DQ), jnp.int32),
            pltpu.VMEM((seq // NK, 8, DQ), jnp.int32),
            pltpu.VMEM((seq // NK, 8, DQ), jnp.int32),
            pltpu.VMEM((seq // NK, 8, DQ), jnp.int32),
            pltpu.VMEM((N_HEADS, DQ, 128), BF16),
            pltpu.VMEM((N_HEADS, 1, DQ), F32),
            pltpu.VMEM((N_HEADS, HEAD_DIM + ONES_ROWS, DQ), F32),
            pltpu.VMEM((N_HEADS, NK, DQ), BF16),
            pltpu.VMEM((N_HEADS, 1, DQ), F32),
            pltpu.VMEM((N_HEADS, NK, DQ), BF16),
            pltpu.VMEM((N_HEADS, 1, DQ), F32),
            pltpu.VMEM((IDX_HEADS, NK, DQ), F32),
            pltpu.VMEM((IDX_HEADS, NK, DQ), F32),
            pltpu.VMEM((16, 8, 128), jnp.int32),
        ],
        compiler_params=pltpu.CompilerParams(
            dimension_semantics=("arbitrary", "arbitrary"), vmem_limit_bytes=VMEM_LIMIT),
    )(proj3, proj3, iwt, proj3, avt4, ik3, bias_tiles)


SB_DEAD_MASS = 104.0 * LOG2E


def _sb_kernel(q_ref, k_ref, v_ref, o_ref, qm_ref, uu_ref, carry_ref, acc_ref, z_ref, sp_ref,
               later_ref):
    i = pl.program_id(1)
    diag = (i * QB + QB - 1) // NK

    @pl.when(jnp.logical_and(pl.program_id(0) == 0, i == 0))
    def _():
        kr = lax.broadcasted_iota(jnp.int32, (2 * NK, NK), 0) % NK
        kc = lax.broadcasted_iota(jnp.int32, (2 * NK, NK), 1)
        uu_ref[...] = jnp.where(kr > kc, 1.0, 0.0).astype(BF16)

    _split_heads_into(qm_ref, q_ref[...])
    carry_ref[...] = jnp.zeros(carry_ref.shape, F32)
    acc_ref[...] = jnp.zeros(acc_ref.shape, F32)

    def block(jb, on_diagonal):
        k0 = pl.multiple_of(jb * NK, NK)
        if on_diagonal:
            row = lax.broadcasted_iota(jnp.int32, (2 * QB, NK), 0)
            col = lax.broadcasted_iota(jnp.int32, (2 * QB, NK), 1)
            causal = (k0 + col) < (i * QB + row % QB)
        for p in range(N_PAIR):
            kp = k_ref[pl.ds(k0, NK), p * 128:(p + 1) * 128]
            z_ref[p] = lax.dot_general(qm_ref[p], kp, _NT, preferred_element_type=F32)
        for p in range(N_PAIR):
            z = z_ref[p]
            neg_abs = lax.bitcast_convert_type(
                lax.bitcast_convert_type(z, jnp.int32) | INT_MIN, F32)
            sp = jnp.maximum(z, 0.0) + jnp.log(1.0 + jnp.exp2(neg_abs)) * LOG2E
            if on_diagonal:
                sp = jnp.where(causal, sp, 0.0)
            sp_ref[p] = sp
            hi = sp.astype(BF16)
            lo = (sp - hi.astype(F32)).astype(BF16)
            later_ref[p] = jnp.dot(jnp.concatenate([hi, lo], axis=1), uu_ref[...],
                                   preferred_element_type=F32)
        for p in range(N_PAIR):
            vp = v_ref[pl.ds(k0, NK), p * 128:(p + 1) * 128]
            carry = carry_ref[p]
            sp = sp_ref[p]
            a = jnp.exp2(z_ref[p] - sp - later_ref[p] - carry)
            if on_diagonal:
                a = jnp.where(causal, a, 0.0)
            acc_ref[p] += jnp.dot(a.astype(BF16), vp, preferred_element_type=F32)
            carry_ref[p] = carry + jnp.sum(sp, axis=1, keepdims=True)

    block(diag, True)

    def alive():
        return (jnp.min(carry_ref[...]) <= SB_DEAD_MASS).astype(jnp.int32)

    def cond(state):
        jb, go = state
        return jnp.logical_and(jb >= 0, go > 0)

    def body(state):
        jb, _ = state
        block(jb, False)
        return jb - 1, alive()

    lax.while_loop(cond, body, (diag - 1, alive()))

    for p in range(N_PAIR):
        o_ref[:, p * 128:(p + 1) * 128] = _merge_pair(
            acc_ref[p, :QB, :], acc_ref[p, QB:, :]).astype(BF16)


def _stick_breaking(proj3):
    bsz, seq, _ = proj3.shape
    resident = dict(pipeline_mode=pl.Buffered(1))
    return pl.pallas_call(
        _sb_kernel,
        name="stick_breaking",
        grid=(bsz, seq // QB),
        in_specs=[
            pl.BlockSpec((None, QB, 512), lambda b, i: (b, i, COL_BQ)),
            pl.BlockSpec((None, seq, 512), lambda b, i: (b, 0, COL_BK), **resident),
            pl.BlockSpec((None, seq, 512), lambda b, i: (b, 0, COL_BV), **resident),
        ],
        out_specs=pl.BlockSpec((None, QB, 512), lambda b, i: (b, i, 0)),
        out_shape=jax.ShapeDtypeStruct((bsz, seq, 512), BF16),
        scratch_shapes=[
            pltpu.VMEM((N_PAIR, 2 * QB, 128), BF16),
            pltpu.VMEM((2 * NK, NK), BF16),
            pltpu.VMEM((N_PAIR, 2 * QB, 1), F32),
            pltpu.VMEM((N_PAIR, 2 * QB, 128), F32),
            pltpu.VMEM((N_PAIR, 2 * QB, NK), F32),
            pltpu.VMEM((N_PAIR, 2 * QB, NK), F32),
            pltpu.VMEM((N_PAIR, 2 * QB, NK), F32),
        ],
        compiler_params=pltpu.CompilerParams(
            dimension_semantics=("arbitrary", "arbitrary"), vmem_limit_bytes=VMEM_LIMIT),
    )(proj3, proj3, proj3)


MERGE_ROWS = 512


def _merge_kernel(x_ref, ya_ref, yb_ref, cq_ref, g0_ref, g1_ref, g2_ref, mk_ref, mv_ref,
                  wa_ref, wb_ref, wc_ref, wo_ref, gp_ref, o_ref):
    n_rows = x_ref.shape[0]
    for r0 in range(0, n_rows, MERGE_ROWS):
        rows = slice(r0, r0 + MERGE_ROWS)
        heads = []
        for h in range(C_HEADS):
            sl = slice(h * C_HEAD_DIM, (h + 1) * C_HEAD_DIM)
            s = lax.dot_general(cq_ref[rows, sl], mk_ref[:, sl], _NT,
                                preferred_element_type=F32) * (C_HEAD_DIM ** -0.5)
            e = jnp.exp(s - jnp.max(s, axis=1, keepdims=True))
            p = e / jnp.sum(e, axis=1, keepdims=True)
            heads.append(jnp.dot(p.astype(BF16), mv_ref[:, sl], preferred_element_type=F32))
        yc_pre = jnp.concatenate(heads, axis=1).astype(BF16)
        ya = jnp.dot(ya_ref[rows, :], wa_ref[...], preferred_element_type=F32)
        yb = jnp.dot(yb_ref[rows, :], wb_ref[...], preferred_element_type=F32)
        yc = jnp.dot(yc_pre, wc_ref[...], preferred_element_type=F32)
        merged = (g0_ref[rows, :].astype(F32) * ya + g1_ref[rows, :].astype(F32) * yb
                  + g2_ref[rows, :].astype(F32) * yc)
        o = jnp.dot(merged.astype(BF16), wo_ref[...], preferred_element_type=F32)
        o_ref[rows, :] = x_ref[rows, :] + _rms(o, gp_ref[...])


def _merge(x2, ya2, yb2, proj2, mkv3, wa, wb, wc, wo, g_post, seq):
    n = x2.shape[0]
    tm = min(2 * MERGE_ROWS, seq)
    per_batch = seq // tm
    n_mem = mkv3.shape[1]
    c_dim = C_HEADS * C_HEAD_DIM
    const = lambda t: (0, 0)
    return pl.pallas_call(
        _merge_kernel,
        name="merge",
        grid=(n // tm,),
        in_specs=[
            pl.BlockSpec((tm, D_MODEL), lambda t: (t, 0)),
            pl.BlockSpec((tm, 512), lambda t: (t, 0)),
            pl.BlockSpec((tm, 512), lambda t: (t, 0)),
            pl.BlockSpec((tm, 512), lambda t: (t, COL_CQ)),
            pl.BlockSpec((tm, D_MODEL), lambda t: (t, 0)),
            pl.BlockSpec((tm, D_MODEL), lambda t: (t, 1)),
            pl.BlockSpec((tm, D_MODEL), lambda t: (t, 2)),
            pl.BlockSpec((None, n_mem, c_dim), lambda t: (t // per_batch, 0, 0)),
            pl.BlockSpec((None, n_mem, c_dim), lambda t: (t // per_batch, 0, 1)),
            pl.BlockSpec(wa.shape, const),
            pl.BlockSpec(wb.shape, const),
            pl.BlockSpec(wc.shape, const),
            pl.BlockSpec(wo.shape, const),
            pl.BlockSpec((1, D_MODEL), const),
        ],
        out_specs=pl.BlockSpec((tm, D_MODEL), lambda t: (t, 0)),
        out_shape=jax.ShapeDtypeStruct((n, D_MODEL), F32),
        compiler_params=pltpu.CompilerParams(
            dimension_semantics=("arbitrary",), vmem_limit_bytes=VMEM_LIMIT),
    )(x2, ya2, yb2, proj2, proj2, proj2, proj2, mkv3, mkv3, wa, wb, wc, wo, g_post)


def _ffn_kernel(x_ref, gpre_ref, wg_ref, wu_ref, wo_ref, gpost_ref, o_ref, h_ref, acc_ref):
    k = pl.program_id(1)

    @pl.when(k == 0)
    def _():
        h_ref[...] = _rms(x_ref[...], gpre_ref[...]).astype(BF16)
        acc_ref[...] = jnp.zeros(acc_ref.shape, F32)

    h = h_ref[...]
    g = jnp.dot(h, wg_ref[...], preferred_element_type=F32)
    u = jnp.dot(h, wu_ref[...], preferred_element_type=F32)
    act = (g * jax.nn.sigmoid(g) * u).astype(BF16)
    acc_ref[...] += jnp.dot(act, wo_ref[...], preferred_element_type=F32)

    @pl.when(k == pl.num_programs(1) - 1)
    def _():
        o_ref[...] = x_ref[...] + _rms(acc_ref[...], gpost_ref[...])


def _ffn(x2, g_pre, wg, wu, wo, g_post):
    n = x2.shape[0]
    d_ff = wg.shape[1]
    tm = min(512, n)
    tf = d_ff // 2
    return pl.pallas_call(
        _ffn_kernel,
        name="ffn",
        grid=(n // tm, d_ff // tf),
        in_specs=[
            pl.BlockSpec((tm, D_MODEL), lambda t, k: (t, 0)),
            pl.BlockSpec((1, D_MODEL), lambda t, k: (0, 0)),
            pl.BlockSpec((D_MODEL, tf), lambda t, k: (0, k)),
            pl.BlockSpec((D_MODEL, tf), lambda t, k: (0, k)),
            pl.BlockSpec((tf, D_MODEL), lambda t, k: (k, 0)),
            pl.BlockSpec((1, D_MODEL), lambda t, k: (0, 0)),
        ],
        out_specs=pl.BlockSpec((tm, D_MODEL), lambda t, k: (t, 0)),
        out_shape=jax.ShapeDtypeStruct((n, D_MODEL), F32),
        scratch_shapes=[pltpu.VMEM((tm, D_MODEL), BF16), pltpu.VMEM((tm, D_MODEL), F32)],
        compiler_params=pltpu.CompilerParams(
            dimension_semantics=("arbitrary", "arbitrary"), vmem_limit_bytes=VMEM_LIMIT),
    )(x2, g_pre, wg, wu, wo, g_post)


def _pack_w_in(w, b_gate):
    sizes = (512, 512, 512, IDX_HEADS * 64, 64, IDX_HEADS, 512, 512, 512, 512,
             N_BRANCH * D_MODEL)
    aq, ak, av, iq, ik, iw, bq, bk, bv, cq, gates = jnp.split(w, np.cumsum(sizes)[:-1], axis=1)
    scale = HEAD_DIM ** -0.5
    scale2 = scale * LOG2E
    w_main = jnp.concatenate(
        [gates, aq * scale2, ak, iq * scale, bq * scale2, bk, bv, cq], axis=1).astype(BF16)
    z64 = jnp.zeros((D_MODEL, 64), F32)
    w_ik = jnp.concatenate([ik, z64, z64, ik], axis=1).astype(BF16)
    w_trans = jnp.concatenate(
        [av, iw, jnp.zeros((D_MODEL, N_TRANS - 512 - IDX_HEADS), F32)], axis=1).T.astype(BF16)
    b_main = jnp.concatenate([b_gate, jnp.zeros((N_MAIN - N_GATE,), F32)])[None, :]
    return w_main, w_ik, w_trans, b_main


def kernel(x, mem, rel_bias, g_mix_pre, w_in, b_gate, g_mem, w_mem_kv, w_up_a, w_up_b, w_up_c,
           w_out, g_mix_post, g_ffn_pre, w_ffn_in, w_ffn_out, g_ffn_post):
    bsz, seq, _ = x.shape
    n_mem = mem.shape[1]
    k_sel = min(TOPK_MAX, seq // 4)
    bias_tiles = _bias_tiles(rel_bias)
    x2 = x.reshape(bsz * seq, D_MODEL)
    for l in range(w_in.shape[0]):
        w_main, w_ik, w_trans, b_main = _pack_w_in(w_in[l], b_gate[l])
        proj2, ik2, avt, iwt = _project(x2, g_mix_pre[l][None, :], w_main, b_main, w_ik, w_trans)
        proj3 = proj2.reshape(bsz, seq, N_MAIN)
        mkv = _memkv(mem.reshape(bsz * n_mem, D_MODEL), g_mem[l][None, :],
                     w_mem_kv[l].astype(BF16))
        ya = _dsa(proj3, ik2.reshape(bsz, seq, N_IK), avt.reshape(bsz, seq // NK, 512, NK),
                  iwt, bias_tiles, k_sel)
        yb = _stick_breaking(proj3)
        x2 = _merge(x2, ya.reshape(bsz * seq, 512), yb.reshape(bsz * seq, 512), proj2,
                    mkv.reshape(bsz, n_mem, 2 * C_HEADS * C_HEAD_DIM),
                    w_up_a[l].astype(BF16), w_up_b[l].astype(BF16), w_up_c[l].astype(BF16),
                    w_out[l].astype(BF16), g_mix_post[l][None, :], seq)
        d_ff = w_ffn_out.shape[1]
        w_ffn = w_ffn_in[l].astype(BF16)
        x2 = _ffn(x2, g_ffn_pre[l][None, :], w_ffn[:, :d_ff], w_ffn[:, d_ff:],
                  w_ffn_out[l].astype(BF16), g_ffn_post[l][None, :])
    return x2.reshape(bsz, seq, D_MODEL)
```

```python
import functools

import numpy as np
import jax
import jax.numpy as jnp
from jax import lax
from jax.experimental import pallas as pl
from jax.experimental.pallas import tpu as pltpu

D_MODEL = 1024
CHUNK = 64
HEAD_DIM = 64
N_HEADS = 8
IDX_HEADS = 8
TOPK_MAX = 256
C_HEADS = 4
C_HEAD_DIM = 128
N_BRANCH = 3
REL_BUCKETS = 32
EPS = 1e-6

F32 = jnp.float32
BF16 = jnp.bfloat16
INT_MIN = -2 ** 31
NEG_BIG = -1e30
LOG2E = 1.4426950408889634

QB = 256
DQ = 256
NK = 256
N_PAIR = N_HEADS // 2
ONES_ROWS = 16

N_GATE = N_BRANCH * D_MODEL
COL_AQ, COL_AK, COL_IQ, COL_BQ, COL_BK, COL_BV, COL_CQ = range(N_GATE // 512, N_GATE // 512 + 7)
N_MAIN = N_GATE + 7 * 512
N_IK = 256
N_TRANS = 512 + 16

VMEM_LIMIT = 56 * 1024 * 1024

_NT = (((1,), (1,)), ((), ()))


def _rms(x, g):
    return x * lax.rsqrt(jnp.mean(x * x, axis=-1, keepdims=True) + EPS) * g


PROJ_ROWS = 1024


def _proj_kernel(x_ref, g_ref, w_ref, b_ref, wik_ref, wt_ref, o_ref, ik_ref, avt_ref, iwt_ref,
                 h_ref, *, n_gate_tiles):
    j = pl.program_id(1)

    @pl.when(j == 0)
    def _():
        for r0 in range(0, h_ref.shape[0], PROJ_ROWS):
            rows = slice(r0, min(r0 + PROJ_ROWS, h_ref.shape[0]))
            hb = _rms(x_ref[rows, :], g_ref[...]).astype(BF16)
            h_ref[rows, :] = hb
            ik_ref[rows, :] = jnp.dot(hb, wik_ref[...], preferred_element_type=F32).astype(BF16)
            tr = lax.dot_general(wt_ref[...], hb, _NT, preferred_element_type=F32)
            for c in range(hb.shape[0] // NK):
                avt_ref[r0 // NK + c] = tr[:512, c * NK:(c + 1) * NK].astype(BF16)
            iwt_ref[:, rows] = tr[512:512 + IDX_HEADS, :]

    def tile(finish):
        for r0 in range(0, h_ref.shape[0], PROJ_ROWS):
            rows = slice(r0, min(r0 + PROJ_ROWS, h_ref.shape[0]))
            acc = jnp.dot(h_ref[rows, :], w_ref[...], preferred_element_type=F32)
            o_ref[rows, :] = finish(acc).astype(BF16)

    @pl.when(j >= n_gate_tiles)
    def _():
        tile(lambda acc: acc)

    @pl.when(j < n_gate_tiles)
    def _():
        tile(lambda acc: 0.5 + 0.5 * jnp.tanh(0.5 * (acc + b_ref[...])))


def _project(x2, g, w_main, b_main, w_ik, w_trans):
    n = x2.shape[0]
    tm = min(2048, n)
    tn = 512
    grid = (n // tm, N_MAIN // tn)
    return pl.pallas_call(
        functools.partial(_proj_kernel, n_gate_tiles=N_GATE // tn),
        name="in_proj",
        grid=grid,
        in_specs=[
            pl.BlockSpec((tm, D_MODEL), lambda i, j: (i, 0)),
            pl.BlockSpec((1, D_MODEL), lambda i, j: (0, 0)),
            pl.BlockSpec((D_MODEL, tn), lambda i, j: (0, j)),
            pl.BlockSpec((1, tn), lambda i, j: (0, j)),
            pl.BlockSpec((D_MODEL, N_IK), lambda i, j: (0, 0)),
            pl.BlockSpec((N_TRANS, D_MODEL), lambda i, j: (0, 0)),
        ],
        out_specs=[
            pl.BlockSpec((tm, tn), lambda i, j: (i, j)),
            pl.BlockSpec((tm, N_IK), lambda i, j: (i, 0)),
            pl.BlockSpec((tm // NK, 512, NK), lambda i, j: (i, 0, 0)),
            pl.BlockSpec((IDX_HEADS, tm), lambda i, j: (0, i)),
        ],
        out_shape=[
            jax.ShapeDtypeStruct((n, N_MAIN), BF16),
            jax.ShapeDtypeStruct((n, N_IK), BF16),
            jax.ShapeDtypeStruct((n // NK, 512, NK), BF16),
            jax.ShapeDtypeStruct((IDX_HEADS, n), F32),
        ],
        scratch_shapes=[pltpu.VMEM((tm, D_MODEL), BF16)],
        compiler_params=pltpu.CompilerParams(
            dimension_semantics=("arbitrary", "arbitrary"), vmem_limit_bytes=VMEM_LIMIT),
    )(x2, g, w_main, b_main, w_ik, w_trans)


def _memkv_kernel(x_ref, g_ref, w_ref, o_ref):
    hb = _rms(x_ref[...], g_ref[...]).astype(BF16)
    o_ref[...] = jnp.dot(hb, w_ref[...], preferred_element_type=F32).astype(BF16)


def _memkv(mem2, g, w):
    n = mem2.shape[0]
    tm = min(512, n)
    return pl.pallas_call(
        _memkv_kernel,
        name="mem_kv",
        grid=(n // tm,),
        in_specs=[
            pl.BlockSpec((tm, D_MODEL), lambda i: (i, 0)),
            pl.BlockSpec((1, D_MODEL), lambda i: (0, 0)),
            pl.BlockSpec((D_MODEL, w.shape[1]), lambda i: (0, 0)),
        ],
        out_specs=pl.BlockSpec((tm, w.shape[1]), lambda i: (i, 0)),
        out_shape=jax.ShapeDtypeStruct((n, w.shape[1]), BF16),
        compiler_params=pltpu.CompilerParams(
            dimension_semantics=("arbitrary",), vmem_limit_bytes=VMEM_LIMIT),
    )(mem2, g, w)


BIAS_OFFSETS = (0, -NK)
_LOG_BUCKET_STARTS = (12, 16, 23, 32, 46, 64, 91)
FAR_BUCKET = 15
assert DQ == NK


def _bias_kernel(rb_ref, o_ref):
    key = lax.broadcasted_iota(jnp.int32, (NK, DQ), 0)
    qry = lax.broadcasted_iota(jnp.int32, (NK, DQ), 1)
    for c, off in enumerate(BIAS_OFFSETS):
        rel = key - qry + off
        n = jnp.abs(rel)
        large = jnp.full((NK, DQ), 8, jnp.int32)
        for start in _LOG_BUCKET_STARTS:
            large = large + jnp.where(n >= start, 1, 0)
        bucket = jnp.where(rel > 0, REL_BUCKETS // 2, 0) + jnp.where(n < 8, n, large)
        for h in range(N_HEADS):
            val = jnp.full((NK, DQ), rb_ref[0, h], F32)
            for b in range(1, REL_BUCKETS):
                val = jnp.where(bucket == b, rb_ref[b, h], val)
            o_ref[c, h] = (val - rb_ref[FAR_BUCKET, h]) * LOG2E


def _bias_tiles(rel_bias):
    return pl.pallas_call(
        _bias_kernel,
        name="rel_bias_tiles",
        in_specs=[pl.BlockSpec(memory_space=pltpu.SMEM)],
        out_specs=pl.BlockSpec(memory_space=pltpu.VMEM),
        out_shape=jax.ShapeDtypeStruct((len(BIAS_OFFSETS), N_HEADS, NK, DQ), F32),
    )(rel_bias)


def _split_heads_into(qm_ref, q):
    lane = lax.broadcasted_iota(jnp.int32, (QB, 128), 1)
    for p in range(N_PAIR):
        qp = q[:, p * 128:(p + 1) * 128].astype(F32)
        qm_ref[p, :QB, :] = jnp.where(lane < HEAD_DIM, qp, 0.0).astype(BF16)
        qm_ref[p, QB:, :] = jnp.where(lane >= HEAD_DIM, qp, 0.0).astype(BF16)


def _two_stage_pipeline(n, first, second):
    first(0, 0)

    def two(u, carry):
        j = 2 * u
        first(j + 1, 1)
        second(j, 0, False)
        first(j + 2, 0)
        second(j + 1, 1, False)
        return carry

    lax.fori_loop(0, (n - 1) // 2, two, 0)

    @pl.when(n % 2 == 0)
    def _():
        first(n - 1, 1)
        second(n - 2, 0, False)
        second(n - 1, 1, True)

    @pl.when(n % 2 == 1)
    def _():
        second(n - 1, 0, True)


def _merge_pair(o_even, o_odd):
    lane = lax.broadcasted_iota(jnp.int32, (QB, 128), 1)
    return jnp.where(lane < HEAD_DIM, o_even, o_odd)


CNT_BLOCKS = 4
KEY_ROWS = 32
assert NK == 8 * 32


def _bit_transpose32(load_row, tmp_ref, store_row):
    def swap(a, b, j, m):
        t = (a ^ lax.shift_right_logical(b, jnp.int32(j))) & m
        return a ^ t, b ^ (t << j)

    lower = []
    for k in range(16):
        a, b = swap(load_row(k), load_row(k + 16), 16, 0x0000FFFF)
        lower.append(a)
        tmp_ref[k] = b
    for base in (0, 16):
        x = lower if base == 0 else [tmp_ref[k] for k in range(16)]
        j, m = 8, 0x00FF00FF
        while j:
            k = 0
            while k < 16:
                x[k], x[k + j] = swap(x[k], x[k + j], j, m)
                k = (k + j + 1) & ~j
            j >>= 1
            m ^= m << j
        for i in range(16):
            store_row(base + i, x[i])


def _keep_lowest_ties(alive_ref, kept_ref, need, n_groups):
    n_blocks = alive_ref.shape[0]
    sub = lax.broadcasted_iota(jnp.int32, (8, DQ), 0)
    zero_masks = []
    for c in reversed(range(max(n_blocks - 1, 1).bit_length())):
        zero_masks.append(
            lambda jb, c=c: jnp.where(((jb >> c) & 1) == 0, jnp.int32(-1), jnp.int32(0)))
    for word in (0xFFFF0000, 0xFF00FF00, 0xF0F0F0F0, 0xCCCCCCCC, 0xAAAAAAAA):
        zero_masks.append(lambda jb, word=word: jnp.int32(word - (1 << 32)))
    for c in (2, 1, 0):
        zero_masks.append(lambda jb, c=c: jnp.where(((sub >> c) & 1) == 0, -1, 0))

    remaining = need
    took_zero = None
    for p, zero_mask in enumerate(zero_masks):
        prev_mask = zero_masks[p - 1] if p else None

        def body(g, cnts, zero_mask=zero_mask, prev_mask=prev_mask, took_zero=took_zero):
            cnts = list(cnts)
            for u in range(CNT_BLOCKS):
                jb = g * CNT_BLOCKS + u
                alive = alive_ref[jb]
                if prev_mask is None:
                    kept_ref[jb] = jnp.zeros((8, DQ), jnp.int32)
                else:
                    lows = alive & prev_mask(jb)
                    kept_ref[jb] = kept_ref[jb] | jnp.where(took_zero, 0, lows)
                    alive = jnp.where(took_zero, lows, alive ^ lows)
                    alive_ref[jb] = alive
                cnts[u] = cnts[u] + lax.population_count(alive & zero_mask(jb))
            return tuple(cnts)

        zeros = jnp.zeros((8, DQ), jnp.int32)
        cnts = lax.fori_loop(0, n_groups, body, (zeros,) * CNT_BLOCKS)
        n_zero = jnp.sum(sum(cnts[1:], cnts[0]), axis=0, keepdims=True)
        took_zero = n_zero >= remaining
        remaining = jnp.where(took_zero, remaining, remaining - n_zero)

    last_mask = zero_masks[-1]

    def finish(jb, carry):
        alive = alive_ref[jb]
        alive_ref[jb] = kept_ref[jb] | jnp.where(took_zero, alive & last_mask(jb), alive)
        return carry

    lax.fori_loop(0, n_groups * CNT_BLOCKS, finish, 0)


def _dsa_kernel(aq_ref, iq_ref, iwt_ref, ak_ref, avt_ref, ik_ref, bias_ref, o_ref,
                keybuf_ref, plane_ref, alive_ref, above_ref, kept_ref, sel_ref, qm_ref, m_ref,
                acc_ref, s0_ref, cm0_ref, s1_ref, cm1_ref, raw0_ref, raw1_ref, tmp_ref, *,
                k_sel):
    step = pl.program_id(1)
    n_q = pl.num_programs(1) - 1
    att = step - 1
    krow = lax.broadcasted_iota(jnp.int32, (KEY_ROWS, DQ), 0)
    qcol = lax.broadcasted_iota(jnp.int32, (KEY_ROWS, DQ), 1)
    qchunk = (step * DQ + qcol) // CHUNK

    iwt = iwt_ref[...] * (IDX_HEADS ** -0.5)

    def dots_stage(jb, slot):
        raw_ref = (raw0_ref, raw1_ref)[slot]
        k0 = pl.multiple_of(jb * NK, NK)
        for h in range(IDX_HEADS):
            ikh = ik_ref[pl.ds(k0, NK), (h % 2) * 128:(h % 2 + 1) * 128]
            iqp = iq_ref[:, (h // 2) * 128:(h // 2 + 1) * 128]
            raw_ref[h] = lax.dot_general(ikh, iqp, _NT, preferred_element_type=F32)

    def keys_stage(jb, slot, last):
        raw_ref = (raw0_ref, raw1_ref)[slot]
        k0 = pl.multiple_of(jb * NK, NK)
        for c in range(NK // KEY_ROWS):
            rows = slice(c * KEY_ROWS, (c + 1) * KEY_ROWS)
            acc = jnp.zeros((KEY_ROWS, DQ), F32)
            for h in range(IDX_HEADS):
                acc = acc + iwt[h:h + 1, :] * jnp.maximum(raw_ref[h, rows, :], 0.0)
            bits = lax.bitcast_convert_type(acc, jnp.int32)
            key = bits ^ ((bits >> 31) & 0x7FFFFFFF)
            key = jnp.where(key == -1, 0, key)
            ukey = key ^ INT_MIN
            if last:
                admissible = ((k0 + c * KEY_ROWS + krow) // CHUNK) <= qchunk
                ukey = jnp.where(admissible, ukey, 0)
            keybuf_ref[slot, rows, :] = ukey
        for half in range(DQ // 128):
            lanes = slice(half * 128, (half + 1) * 128)

            def load_row(r):
                return keybuf_ref[slot, 8 * r:8 * r + 8, lanes]

            def store_plane(i, v):
                plane_ref[31 - i, jb, :, lanes] = v

            _bit_transpose32(load_row, tmp_ref, store_plane)
        plane_ref[32, jb] = jnp.full((8, DQ), -1, jnp.int32)
        alive_ref[jb] = jnp.full((8, DQ), -1, jnp.int32)
        above_ref[jb] = jnp.zeros((8, DQ), jnp.int32)

    ones = jnp.ones((ONES_ROWS, NK), BF16)
    slots = ((s0_ref, cm0_ref), (s1_ref, cm1_ref))

    def logits_stage(jb, slot, bias_idx):
        s_ref, cm_ref = slots[slot]
        k0 = pl.multiple_of(jb * NK, NK)
        sel = sel_ref[jb]
        mask = jnp.concatenate(
            [jnp.where((sel << r) < 0, 0.0, NEG_BIG)
             for r in range(32)], axis=0).astype(BF16)
        for h in range(N_HEADS):
            kp = ak_ref[pl.ds(k0, NK), (h // 2) * 128:(h // 2 + 1) * 128]
            s = lax.dot_general(kp, qm_ref[h], _NT, preferred_element_type=F32)
            if bias_idx is not None:
                s = s + bias_ref[bias_idx, h]
            sb = s.astype(BF16) + mask
            s_ref[h] = sb
            cm_ref[h] = jnp.max(sb, axis=0, keepdims=True).astype(F32)

    def softmax_stage(jb, slot):
        s_ref, cm_ref = slots[slot]
        for h in range(N_HEADS):
            m_prev = m_ref[h]
            m_new = jnp.maximum(m_prev, cm_ref[h])
            alpha = jnp.exp2(m_prev - m_new)
            pe = jnp.exp2(s_ref[h] - m_new.astype(BF16))
            vt = jnp.concatenate([avt_ref[jb, h * HEAD_DIM:(h + 1) * HEAD_DIM, :], ones], axis=0)
            acc_ref[h] = alpha * acc_ref[h] + jnp.dot(vt, pe, preferred_element_type=F32)
            m_ref[h] = m_new

    def start_attention():
        lane = lax.broadcasted_iota(jnp.int32, (DQ, 128), 1)
        for p in range(N_PAIR):
            qp = aq_ref[:, p * 128:(p + 1) * 128].astype(F32)
            qm_ref[2 * p] = jnp.where(lane < HEAD_DIM, qp, 0.0).astype(BF16)
            qm_ref[2 * p + 1] = jnp.where(lane >= HEAD_DIM, qp, 0.0).astype(BF16)
        m_ref[...] = jnp.full(m_ref.shape, NEG_BIG, F32)
        acc_ref[...] = jnp.zeros(acc_ref.shape, F32)
        logits_stage(att, 0, 0)
        logits_stage(jnp.maximum(att - 1, 0), 1, 1)
        softmax_stage(att, 0)

    @pl.when(step == 0)
    def _():
        dots_stage(0, 0)
        keys_stage(0, 0, True)

    @pl.when(step == n_q)
    def _():
        start_attention()
        n_steps = att - 1

        def two_steps(u, carry):
            b = att - 1 - 2 * u
            logits_stage(b - 1, 0, None)
            softmax_stage(b, 1)
            logits_stage(b - 2, 1, None)
            softmax_stage(b - 1, 0)
            return carry

        lax.fori_loop(0, jnp.maximum(n_steps, 0) // 2, two_steps, 0)

        @pl.when(jnp.logical_and(n_steps >= 1, n_steps % 2 == 1))
        def _():
            logits_stage(0, 0, None)
            softmax_stage(1, 1)

        @pl.when(jnp.logical_and(att >= 1, att % 2 == 1))
        def _():
            softmax_stage(0, 1)

        @pl.when(jnp.logical_and(att >= 1, att % 2 == 0))
        def _():
            softmax_stage(0, 0)

    @pl.when(jnp.logical_and(step >= 1, step < n_q))
    def _():
        start_attention()
        dots_stage(0, 0)
        n_fused = jnp.maximum(att - 1, 0) // 2

        def fused_trip(u, carry):
            b = att - 1 - 2 * u
            j = 2 * u
            logits_stage(b - 1, 0, None)
            dots_stage(j + 1, 1)
            softmax_stage(b, 1)
            keys_stage(j, 0, False)
            logits_stage(b - 2, 1, None)
            dots_stage(j + 2, 0)
            softmax_stage(b - 1, 0)
            keys_stage(j + 1, 1, False)
            return carry

        lax.fori_loop(0, n_fused, fused_trip, 0)
        j0 = 2 * n_fused

        @pl.when(att == 0)
        def _():
            dots_stage(1, 1)
            keys_stage(0, 0, False)
            keys_stage(1, 1, True)

        @pl.when(att % 2 == 1)
        def _():
            dots_stage(j0 + 1, 1)
            softmax_stage(0, 1)
            keys_stage(j0, 0, False)
            dots_stage(j0 + 2, 0)
            keys_stage(j0 + 1, 1, False)
            keys_stage(j0 + 2, 0, True)

        @pl.when(jnp.logical_and(att >= 2, att % 2 == 0))
        def _():
            logits_stage(0, 0, None)
            dots_stage(j0 + 1, 1)
            softmax_stage(1, 1)
            keys_stage(j0, 0, False)
            dots_stage(j0 + 2, 0)
            softmax_stage(0, 0)
            keys_stage(j0 + 1, 1, False)
            dots_stage(j0 + 3, 1)
            keys_stage(j0 + 2, 0, False)
            keys_stage(j0 + 3, 1, True)

    @pl.when(step >= 1)
    def _():
        for p in range(N_PAIR):
            halves = []
            for h in (2 * p, 2 * p + 1):
                a = acc_ref[h]
                halves.append(a[:HEAD_DIM, :] / a[HEAD_DIM:HEAD_DIM + 1, :])
            o_ref[:, p * 128:(p + 1) * 128] = jnp.concatenate(halves, axis=0).T.astype(BF16)

    nkb = jnp.where(step < n_q, step + 1, 0)
    n_groups = (nkb + CNT_BLOCKS - 1) // CNT_BLOCKS

    def pad_block(jb, carry):
        for b in range(33):
            plane_ref[b, jb] = jnp.zeros((8, DQ), jnp.int32)
        alive_ref[jb] = jnp.zeros((8, DQ), jnp.int32)
        above_ref[jb] = jnp.zeros((8, DQ), jnp.int32)
        return carry

    lax.fori_loop(nkb, n_groups * CNT_BLOCKS, pad_block, 0)

    def select_pass(it, state):
        took_prev, n_above, thr_u = state
        b = 31 - it
        take_prev = took_prev != 0

        def body(g, cnts):
            cnts = list(cnts)
            for u in range(CNT_BLOCKS):
                jb = g * CNT_BLOCKS + u
                alive = alive_ref[jb]
                with_prev = alive & plane_ref[b + 1, jb]
                above_ref[jb] = above_ref[jb] | jnp.where(take_prev, 0, with_prev)
                alive = jnp.where(take_prev, with_prev, alive ^ with_prev)
                alive_ref[jb] = alive
                cnts[u] = cnts[u] + lax.population_count(alive & plane_ref[b, jb])
            return tuple(cnts)

        zeros = jnp.zeros((8, DQ), jnp.int32)
        cnts = lax.fori_loop(0, n_groups, body, (zeros,) * CNT_BLOCKS)
        n_one = jnp.sum(sum(cnts[1:], cnts[0]), axis=0, keepdims=True)
        take = (n_above + n_one) >= k_sel
        n_above = jnp.where(take, n_above, n_above + n_one)
        thr_u = jnp.where(take, thr_u | jnp.left_shift(jnp.int32(1), b), thr_u)
        return take.astype(jnp.int32), n_above, thr_u

    row0 = jnp.zeros((1, DQ), jnp.int32)
    took_last, n_above, thr_u = lax.fori_loop(0, 32, select_pass, (row0 + 1, row0, row0))

    def settle(g, cnts):
        cnts = list(cnts)
        for u in range(CNT_BLOCKS):
            jb = g * CNT_BLOCKS + u
            alive = alive_ref[jb]
            with_last = alive & plane_ref[0, jb]
            above_ref[jb] = above_ref[jb] | jnp.where(took_last != 0, 0, with_last)
            alive = jnp.where(took_last != 0, with_last, alive ^ with_last)
            alive_ref[jb] = alive
            cnts[u] = cnts[u] + lax.population_count(alive)
        return tuple(cnts)

    zeros = jnp.zeros((8, DQ), jnp.int32)
    cnts = lax.fori_loop(0, n_groups, settle, (zeros,) * CNT_BLOCKS)
    n_tied = jnp.sum(sum(cnts[1:], cnts[0]), axis=0, keepdims=True)
    need = k_sel - n_above
    real = thr_u != 0
    extra = jnp.logical_and(n_tied > need, real)

    @pl.when(jnp.max(extra.astype(jnp.int32)) > 0)
    def _():
        _keep_lowest_ties(alive_ref, kept_ref, need, n_groups)

    def finalize(jb, carry):
        sel_ref[jb] = above_ref[jb] | jnp.where(real, alive_ref[jb], 0)
        return carry

    lax.fori_loop(0, nkb, finalize, 0)


def _dsa(proj3, ik3, avt4, iwt, bias_tiles, k_sel):
    bsz, seq, _ = proj3.shape
    nq = seq // DQ
    assert seq % (NK * CNT_BLOCKS) == 0
    resident = dict(pipeline_mode=pl.Buffered(1))
    return pl.pallas_call(
        functools.partial(_dsa_kernel, k_sel=k_sel),
        name="dsa",
        grid=(bsz, nq + 1),
        in_specs=[
            pl.BlockSpec((None, DQ, 512), lambda b, s: (b, jnp.maximum(s - 1, 0), COL_AQ)),
            pl.BlockSpec((None, DQ, 512), lambda b, s: (b, jnp.minimum(s, nq - 1), COL_IQ)),
            pl.BlockSpec((IDX_HEADS, DQ), lambda b, s: (0, b * nq + jnp.minimum(s, nq - 1))),
            pl.BlockSpec((None, seq, 512), lambda b, i: (b, 0, COL_AK), **resident),
            pl.BlockSpec((None, seq // NK, 512, NK), lambda b, i: (b, 0, 0, 0), **resident),
            pl.BlockSpec((None, seq, N_IK), lambda b, i: (b, 0, 0), **resident),
            pl.BlockSpec(bias_tiles.shape, lambda b, i: (0, 0, 0, 0), **resident),
        ],
        out_specs=pl.BlockSpec((None, DQ, 512), lambda b, s: (b, jnp.maximum(s - 1, 0), 0)),
        out_shape=jax.ShapeDtypeStruct((bsz, seq, 512), BF16),
        scratch_shapes=[
            pltpu.VMEM((2, NK, DQ), jnp.int32),
            pltpu.VMEM((33, seq // NK, 8, DQ), jnp.int32),
            pltpu.VMEM((seq // NK, 8, DQ), jnp.int32),
            pltpu.VMEM((seq // NK, 8, DQ), jnp.int32),
            pltpu.VMEM((seq // NK, 8, DQ), jnp.int32),
            pltpu.VMEM((seq // NK, 8, DQ), jnp.int32),
            pltpu.VMEM((N_HEADS, DQ, 128), BF16),
            pltpu.VMEM((N_HEADS, 1, DQ), F32),
            pltpu.VMEM((N_HEADS, HEAD_DIM + ONES_ROWS, DQ), F32),
            pltpu.VMEM((N_HEADS, NK, DQ), BF16),
            pltpu.VMEM((N_HEADS, 1, DQ), F32),
            pltpu.VMEM((N_HEADS, NK, DQ), BF16),
            pltpu.VMEM((N_HEADS, 1, DQ), F32),
            pltpu.VMEM((IDX_HEADS, NK, DQ), F32),
            pltpu.VMEM((IDX_HEADS, NK, DQ), F32),
            pltpu.VMEM((16, 8, 128), jnp.int32),
        ],
        compiler_params=pltpu.CompilerParams(
            dimension_semantics=("arbitrary", "arbitrary"), vmem_limit_bytes=VMEM_LIMIT),
    )(proj3, proj3, iwt, proj3, avt4, ik3, bias_tiles)


SB_DEAD_MASS = 104.0 * LOG2E


def _sb_kernel(q_ref, k_ref, v_ref, o_ref, qm_ref, uu_ref, carry_ref, acc_ref, z_ref, sp_ref,
               later_ref):
    i = pl.program_id(1)
    diag = (i * QB + QB - 1) // NK

    @pl.when(jnp.logical_and(pl.program_id(0) == 0, i == 0))
    def _():
        kr = lax.broadcasted_iota(jnp.int32, (2 * NK, NK), 0) % NK
        kc = lax.broadcasted_iota(jnp.int32, (2 * NK, NK), 1)
        uu_ref[...] = jnp.where(kr > kc, 1.0, 0.0).astype(BF16)

    _split_heads_into(qm_ref, q_ref[...])
    carry_ref[...] = jnp.zeros(carry_ref.shape, F32)
    acc_ref[...] = jnp.zeros(acc_ref.shape, F32)

    def block(jb, on_diagonal):
        k0 = pl.multiple_of(jb * NK, NK)
        if on_diagonal:
            row = lax.broadcasted_iota(jnp.int32, (2 * QB, NK), 0)
            col = lax.broadcasted_iota(jnp.int32, (2 * QB, NK), 1)
            causal = (k0 + col) < (i * QB + row % QB)
        for p in range(N_PAIR):
            kp = k_ref[pl.ds(k0, NK), p * 128:(p + 1) * 128]
            z_ref[p] = lax.dot_general(qm_ref[p], kp, _NT, preferred_element_type=F32)
        for p in range(N_PAIR):
            z = z_ref[p]
            neg_abs = lax.bitcast_convert_type(
                lax.bitcast_convert_type(z, jnp.int32) | INT_MIN, F32)
            sp = jnp.maximum(z, 0.0) + jnp.log(1.0 + jnp.exp2(neg_abs)) * LOG2E
            if on_diagonal:
                sp = jnp.where(causal, sp, 0.0)
            sp_ref[p] = sp
            hi = sp.astype(BF16)
            lo = (sp - hi.astype(F32)).astype(BF16)
            later_ref[p] = jnp.dot(jnp.concatenate([hi, lo], axis=1), uu_ref[...],
                                   preferred_element_type=F32)
        for p in range(N_PAIR):
            vp = v_ref[pl.ds(k0, NK), p * 128:(p + 1) * 128]
            carry = carry_ref[p]
            sp = sp_ref[p]
            a = jnp.exp2(z_ref[p] - sp - later_ref[p] - carry)
            if on_diagonal:
                a = jnp.where(causal, a, 0.0)
            acc_ref[p] += jnp.dot(a.astype(BF16), vp, preferred_element_type=F32)
            carry_ref[p] = carry + jnp.sum(sp, axis=1, keepdims=True)

    block(diag, True)

    def alive():
        return (jnp.min(carry_ref[...]) <= SB_DEAD_MASS).astype(jnp.int32)

    def cond(state):
        jb, go = state
        return jnp.logical_and(jb >= 0, go > 0)

    def body(state):
        jb, _ = state
        block(jb, False)
        return jb - 1, alive()

    lax.while_loop(cond, body, (diag - 1, alive()))

    for p in range(N_PAIR):
        o_ref[:, p * 128:(p + 1) * 128] = _merge_pair(
            acc_ref[p, :QB, :], acc_ref[p, QB:, :]).astype(BF16)


def _stick_breaking(proj3):
    bsz, seq, _ = proj3.shape
    resident = dict(pipeline_mode=pl.Buffered(1))
    return pl.pallas_call(
        _sb_kernel,
        name="stick_breaking",
        grid=(bsz, seq // QB),
        in_specs=[
            pl.BlockSpec((None, QB, 512), lambda b, i: (b, i, COL_BQ)),
            pl.BlockSpec((None, seq, 512), lambda b, i: (b, 0, COL_BK), **resident),
            pl.BlockSpec((None, seq, 512), lambda b, i: (b, 0, COL_BV), **resident),
        ],
        out_specs=pl.BlockSpec((None, QB, 512), lambda b, i: (b, i, 0)),
        out_shape=jax.ShapeDtypeStruct((bsz, seq, 512), BF16),
        scratch_shapes=[
            pltpu.VMEM((N_PAIR, 2 * QB, 128), BF16),
            pltpu.VMEM((2 * NK, NK), BF16),
            pltpu.VMEM((N_PAIR, 2 * QB, 1), F32),
            pltpu.VMEM((N_PAIR, 2 * QB, 128), F32),
            pltpu.VMEM((N_PAIR, 2 * QB, NK), F32),
            pltpu.VMEM((N_PAIR, 2 * QB, NK), F32),
            pltpu.VMEM((N_PAIR, 2 * QB, NK), F32),
        ],
        compiler_params=pltpu.CompilerParams(
            dimension_semantics=("arbitrary", "arbitrary"), vmem_limit_bytes=VMEM_LIMIT),
    )(proj3, proj3, proj3)


MERGE_ROWS = 512


def _merge_kernel(x_ref, ya_ref, yb_ref, cq_ref, g0_ref, g1_ref, g2_ref, mk_ref, mv_ref,
                  wa_ref, wb_ref, wc_ref, wo_ref, gp_ref, o_ref):
    n_rows = x_ref.shape[0]
    for r0 in range(0, n_rows, MERGE_ROWS):
        rows = slice(r0, r0 + MERGE_ROWS)
        heads = []
        for h in range(C_HEADS):
            sl = slice(h * C_HEAD_DIM, (h + 1) * C_HEAD_DIM)
            s = lax.dot_general(cq_ref[rows, sl], mk_ref[:, sl], _NT,
                                preferred_element_type=F32) * (C_HEAD_DIM ** -0.5)
            e = jnp.exp(s - jnp.max(s, axis=1, keepdims=True))
            p = e / jnp.sum(e, axis=1, keepdims=True)
            heads.append(jnp.dot(p.astype(BF16), mv_ref[:, sl], preferred_element_type=F32))
        yc_pre = jnp.concatenate(heads, axis=1).astype(BF16)
        ya = jnp.dot(ya_ref[rows, :], wa_ref[...], preferred_element_type=F32)
        yb = jnp.dot(yb_ref[rows, :], wb_ref[...], preferred_element_type=F32)
        yc = jnp.dot(yc_pre, wc_ref[...], preferred_element_type=F32)
        merged = (g0_ref[rows, :].astype(F32) * ya + g1_ref[rows, :].astype(F32) * yb
                  + g2_ref[rows, :].astype(F32) * yc)
        o = jnp.dot(merged.astype(BF16), wo_ref[...], preferred_element_type=F32)
        o_ref[rows, :] = x_ref[rows, :] + _rms(o, gp_ref[...])


def _merge(x2, ya2, yb2, proj2, mkv3, wa, wb, wc, wo, g_post, seq):
    n = x2.shape[0]
    tm = min(2 * MERGE_ROWS, seq)
    per_batch = seq // tm
    n_mem = mkv3.shape[1]
    c_dim = C_HEADS * C_HEAD_DIM
    const = lambda t: (0, 0)
    return pl.pallas_call(
        _merge_kernel,
        name="merge",
        grid=(n // tm,),
        in_specs=[
            pl.BlockSpec((tm, D_MODEL), lambda t: (t, 0)),
            pl.BlockSpec((tm, 512), lambda t: (t, 0)),
            pl.BlockSpec((tm, 512), lambda t: (t, 0)),
            pl.BlockSpec((tm, 512), lambda t: (t, COL_CQ)),
            pl.BlockSpec((tm, D_MODEL), lambda t: (t, 0)),
            pl.BlockSpec((tm, D_MODEL), lambda t: (t, 1)),
            pl.BlockSpec((tm, D_MODEL), lambda t: (t, 2)),
            pl.BlockSpec((None, n_mem, c_dim), lambda t: (t // per_batch, 0, 0)),
            pl.BlockSpec((None, n_mem, c_dim), lambda t: (t // per_batch, 0, 1)),
            pl.BlockSpec(wa.shape, const),
            pl.BlockSpec(wb.shape, const),
            pl.BlockSpec(wc.shape, const),
            pl.BlockSpec(wo.shape, const),
            pl.BlockSpec((1, D_MODEL), const),
        ],
        out_specs=pl.BlockSpec((tm, D_MODEL), lambda t: (t, 0)),
        out_shape=jax.ShapeDtypeStruct((n, D_MODEL), F32),
        compiler_params=pltpu.CompilerParams(
            dimension_semantics=("arbitrary",), vmem_limit_bytes=VMEM_LIMIT),
    )(x2, ya2, yb2, proj2, proj2, proj2, proj2, mkv3, mkv3, wa, wb, wc, wo, g_post)


def _ffn_kernel(x_ref, gpre_ref, wg_ref, wu_ref, wo_ref, gpost_ref, o_ref, h_ref, acc_ref):
    k = pl.program_id(1)

    @pl.when(k == 0)
    def _():
        h_ref[...] = _rms(x_ref[...], gpre_ref[...]).astype(BF16)
        acc_ref[...] = jnp.zeros(acc_ref.shape, F32)

    h = h_ref[...]
    g = jnp.dot(h, wg_ref[...], preferred_element_type=F32)
    u = jnp.dot(h, wu_ref[...], preferred_element_type=F32)
    act = (g * jax.nn.sigmoid(g) * u).astype(BF16)
    acc_ref[...] += jnp.dot(act, wo_ref[...], preferred_element_type=F32)

    @pl.when(k == pl.num_programs(1) - 1)
    def _():
        o_ref[...] = x_ref[...] + _rms(acc_ref[...], gpost_ref[...])


def _ffn(x2, g_pre, wg, wu, wo, g_post):
    n = x2.shape[0]
    d_ff = wg.shape[1]
    tm = min(512, n)
    tf = d_ff // 2
    return pl.pallas_call(
        _ffn_kernel,
        name="ffn",
        grid=(n // tm, d_ff // tf),
        in_specs=[
            pl.BlockSpec((tm, D_MODEL), lambda t, k: (t, 0)),
            pl.BlockSpec((1, D_MODEL), lambda t, k: (0, 0)),
            pl.BlockSpec((D_MODEL, tf), lambda t, k: (0, k)),
            pl.BlockSpec((D_MODEL, tf), lambda t, k: (0, k)),
            pl.BlockSpec((tf, D_MODEL), lambda t, k: (k, 0)),
            pl.BlockSpec((1, D_MODEL), lambda t, k: (0, 0)),
        ],
        out_specs=pl.BlockSpec((tm, D_MODEL), lambda t, k: (t, 0)),
        out_shape=jax.ShapeDtypeStruct((n, D_MODEL), F32),
        scratch_shapes=[pltpu.VMEM((tm, D_MODEL), BF16), pltpu.VMEM((tm, D_MODEL), F32)],
        compiler_params=pltpu.CompilerParams(
            dimension_semantics=("arbitrary", "arbitrary"), vmem_limit_bytes=VMEM_LIMIT),
    )(x2, g_pre, wg, wu, wo, g_post)


def _pack_w_in(w, b_gate):
    sizes = (512, 512, 512, IDX_HEADS * 64, 64, IDX_HEADS, 512, 512, 512, 512,
             N_BRANCH * D_MODEL)
    aq, ak, av, iq, ik, iw, bq, bk, bv, cq, gates = jnp.split(w, np.cumsum(sizes)[:-1], axis=1)
    scale = HEAD_DIM ** -0.5
    scale2 = scale * LOG2E
    w_main = jnp.concatenate(
        [gates, aq * scale2, ak, iq * scale, bq * scale2, bk, bv, cq], axis=1).astype(BF16)
    z64 = jnp.zeros((D_MODEL, 64), F32)
    w_ik = jnp.concatenate([ik, z64, z64, ik], axis=1).astype(BF16)
    w_trans = jnp.concatenate(
        [av, iw, jnp.zeros((D_MODEL, N_TRANS - 512 - IDX_HEADS), F32)], axis=1).T.astype(BF16)
    b_main = jnp.concatenate([b_gate, jnp.zeros((N_MAIN - N_GATE,), F32)])[None, :]
    return w_main, w_ik, w_trans, b_main


def kernel(x, mem, rel_bias, g_mix_pre, w_in, b_gate, g_mem, w_mem_kv, w_up_a, w_up_b, w_up_c,
           w_out, g_mix_post, g_ffn_pre, w_ffn_in, w_ffn_out, g_ffn_post):
    bsz, seq, _ = x.shape
    n_mem = mem.shape[1]
    k_sel = min(TOPK_MAX, seq // 4)
    bias_tiles = _bias_tiles(rel_bias)
    x2 = x.reshape(bsz * seq, D_MODEL)
    for l in range(w_in.shape[0]):
        w_main, w_ik, w_trans, b_main = _pack_w_in(w_in[l], b_gate[l])
        proj2, ik2, avt, iwt = _project(x2, g_mix_pre[l][None, :], w_main, b_main, w_ik, w_trans)
        proj3 = proj2.reshape(bsz, seq, N_MAIN)
        mkv = _memkv(mem.reshape(bsz * n_mem, D_MODEL), g_mem[l][None, :],
                     w_mem_kv[l].astype(BF16))
        ya = _dsa(proj3, ik2.reshape(bsz, seq, N_IK), avt.reshape(bsz, seq // NK, 512, NK),
                  iwt, bias_tiles, k_sel)
        yb = _stick_breaking(proj3)
        x2 = _merge(x2, ya.reshape(bsz * seq, 512), yb.reshape(bsz * seq, 512), proj2,
                    mkv.reshape(bsz, n_mem, 2 * C_HEADS * C_HEAD_DIM),
                    w_up_a[l].astype(BF16), w_up_b[l].astype(BF16), w_up_c[l].astype(BF16),
                    w_out[l].astype(BF16), g_mix_post[l][None, :], seq)
        d_ff = w_ffn_out.shape[1]
        w_ffn = w_ffn_in[l].astype(BF16)
        x2 = _ffn(x2, g_ffn_pre[l][None, :], w_ffn[:, :d_ff], w_ffn[:, d_ff:],
                  w_ffn_out[l].astype(BF16), g_ffn_post[l][None, :])
    return x2.reshape(bsz, seq, D_MODEL)
```

```python
import functools

import numpy as np
import jax
import jax.numpy as jnp
from jax import lax
from jax.experimental import pallas as pl
from jax.experimental.pallas import tpu as pltpu

D_MODEL = 1024
CHUNK = 64
HEAD_DIM = 64
N_HEADS = 8
IDX_HEADS = 8
TOPK_MAX = 256
C_HEADS = 4
C_HEAD_DIM = 128
N_BRANCH = 3
REL_BUCKETS = 32
EPS = 1e-6

F32 = jnp.float32
BF16 = jnp.bfloat16
INT_MIN = -2 ** 31
NEG_BIG = -1e30
LOG2E = 1.4426950408889634

QB = 256
DQ = 256
NK = 256
N_PAIR = N_HEADS // 2
ONES_ROWS = 16

N_GATE = N_BRANCH * D_MODEL
COL_AQ, COL_AK, COL_IQ, COL_BQ, COL_BK, COL_BV, COL_CQ = range(N_GATE // 512, N_GATE // 512 + 7)
N_MAIN = N_GATE + 7 * 512
N_IK = 256
N_TRANS = 512 + 16

VMEM_LIMIT = 56 * 1024 * 1024

_NT = (((1,), (1,)), ((), ()))


def _rms(x, g):
    return x * lax.rsqrt(jnp.mean(x * x, axis=-1, keepdims=True) + EPS) * g


PROJ_ROWS = 1024


def _proj_kernel(x_ref, g_ref, w_ref, b_ref, wik_ref, wt_ref, o_ref, ik_ref, avt_ref, iwt_ref,
                 h_ref, *, n_gate_tiles):
    j = pl.program_id(1)

    @pl.when(j == 0)
    def _():
        for r0 in range(0, h_ref.shape[0], PROJ_ROWS):
            rows = slice(r0, min(r0 + PROJ_ROWS, h_ref.shape[0]))
            hb = _rms(x_ref[rows, :], g_ref[...]).astype(BF16)
            h_ref[rows, :] = hb
            ik_ref[rows, :] = jnp.dot(hb, wik_ref[...], preferred_element_type=F32).astype(BF16)
            tr = lax.dot_general(wt_ref[...], hb, _NT, preferred_element_type=F32)
            for c in range(hb.shape[0] // NK):
                avt_ref[r0 // NK + c] = tr[:512, c * NK:(c + 1) * NK].astype(BF16)
            iwt_ref[:, rows] = tr[512:512 + IDX_HEADS, :]

    def tile(finish):
        for r0 in range(0, h_ref.shape[0], PROJ_ROWS):
            rows = slice(r0, min(r0 + PROJ_ROWS, h_ref.shape[0]))
            acc = jnp.dot(h_ref[rows, :], w_ref[...], preferred_element_type=F32)
            o_ref[rows, :] = finish(acc).astype(BF16)

    @pl.when(j >= n_gate_tiles)
    def _():
        tile(lambda acc: acc)

    @pl.when(j < n_gate_tiles)
    def _():
        tile(lambda acc: 0.5 + 0.5 * jnp.tanh(0.5 * (acc + b_ref[...])))


def _project(x2, g, w_main, b_main, w_ik, w_trans):
    n = x2.shape[0]
    tm = min(2048, n)
    tn = 512
    grid = (n // tm, N_MAIN // tn)
    return pl.pallas_call(
        functools.partial(_proj_kernel, n_gate_tiles=N_GATE // tn),
        name="in_proj",
        grid=grid,
        in_specs=[
            pl.BlockSpec((tm, D_MODEL), lambda i, j: (i, 0)),
            pl.BlockSpec((1, D_MODEL), lambda i, j: (0, 0)),
            pl.BlockSpec((D_MODEL, tn), lambda i, j: (0, j)),
            pl.BlockSpec((1, tn), lambda i, j: (0, j)),
            pl.BlockSpec((D_MODEL, N_IK), lambda i, j: (0, 0)),
            pl.BlockSpec((N_TRANS, D_MODEL), lambda i, j: (0, 0)),
        ],
        out_specs=[
            pl.BlockSpec((tm, tn), lambda i, j: (i, j)),
            pl.BlockSpec((tm, N_IK), lambda i, j: (i, 0)),
            pl.BlockSpec((tm // NK, 512, NK), lambda i, j: (i, 0, 0)),
            pl.BlockSpec((IDX_HEADS, tm), lambda i, j: (0, i)),
        ],
        out_shape=[
            jax.ShapeDtypeStruct((n, N_MAIN), BF16),
            jax.ShapeDtypeStruct((n, N_IK), BF16),
            jax.ShapeDtypeStruct((n // NK, 512, NK), BF16),
            jax.ShapeDtypeStruct((IDX_HEADS, n), F32),
        ],
        scratch_shapes=[pltpu.VMEM((tm, D_MODEL), BF16)],
        compiler_params=pltpu.CompilerParams(
            dimension_semantics=("arbitrary", "arbitrary"), vmem_limit_bytes=VMEM_LIMIT),
    )(x2, g, w_main, b_main, w_ik, w_trans)


def _memkv_kernel(x_ref, g_ref, w_ref, o_ref):
    hb = _rms(x_ref[...], g_ref[...]).astype(BF16)
    o_ref[...] = jnp.dot(hb, w_ref[...], preferred_element_type=F32).astype(BF16)


def _memkv(mem2, g, w):
    n = mem2.shape[0]
    tm = min(512, n)
    return pl.pallas_call(
        _memkv_kernel,
        name="mem_kv",
        grid=(n // tm,),
        in_specs=[
            pl.BlockSpec((tm, D_MODEL), lambda i: (i, 0)),
            pl.BlockSpec((1, D_MODEL), lambda i: (0, 0)),
            pl.BlockSpec((D_MODEL, w.shape[1]), lambda i: (0, 0)),
        ],
        out_specs=pl.BlockSpec((tm, w.shape[1]), lambda i: (i, 0)),
        out_shape=jax.ShapeDtypeStruct((n, w.shape[1]), BF16),
        compiler_params=pltpu.CompilerParams(
            dimension_semantics=("arbitrary",), vmem_limit_bytes=VMEM_LIMIT),
    )(mem2, g, w)


BIAS_OFFSETS = (0, -NK)
_LOG_BUCKET_STARTS = (12, 16, 23, 32, 46, 64, 91)
FAR_BUCKET = 15
assert DQ == NK


def _bias_kernel(rb_ref, o_ref):
    key = lax.broadcasted_iota(jnp.int32, (NK, DQ), 0)
    qry = lax.broadcasted_iota(jnp.int32, (NK, DQ), 1)
    for c, off in enumerate(BIAS_OFFSETS):
        rel = key - qry + off
        n = jnp.abs(rel)
        large = jnp.full((NK, DQ), 8, jnp.int32)
        for start in _LOG_BUCKET_STARTS:
            large = large + jnp.where(n >= start, 1, 0)
        bucket = jnp.where(rel > 0, REL_BUCKETS // 2, 0) + jnp.where(n < 8, n, large)
        for h in range(N_HEADS):
            val = jnp.full((NK, DQ), rb_ref[0, h], F32)
            for b in range(1, REL_BUCKETS):
                val = jnp.where(bucket == b, rb_ref[b, h], val)
            o_ref[c, h] = (val - rb_ref[FAR_BUCKET, h]) * LOG2E


def _bias_tiles(rel_bias):
    return pl.pallas_call(
        _bias_kernel,
        name="rel_bias_tiles",
        in_specs=[pl.BlockSpec(memory_space=pltpu.SMEM)],
        out_specs=pl.BlockSpec(memory_space=pltpu.VMEM),
        out_shape=jax.ShapeDtypeStruct((len(BIAS_OFFSETS), N_HEADS, NK, DQ), F32),
    )(rel_bias)


def _split_heads_into(qm_ref, q):
    lane = lax.broadcasted_iota(jnp.int32, (QB, 128), 1)
    for p in range(N_PAIR):
        qp = q[:, p * 128:(p + 1) * 128].astype(F32)
        qm_ref[p, :QB, :] = jnp.where(lane < HEAD_DIM, qp, 0.0).astype(BF16)
        qm_ref[p, QB:, :] = jnp.where(lane >= HEAD_DIM, qp, 0.0).astype(BF16)


def _two_stage_pipeline(n, first, second):
    first(0, 0)

    def two(u, carry):
        j = 2 * u
        first(j + 1, 1)
        second(j, 0, False)
        first(j + 2, 0)
        second(j + 1, 1, False)
        return carry

    lax.fori_loop(0, (n - 1) // 2, two, 0)

    @pl.when(n % 2 == 0)
    def _():
        first(n - 1, 1)
        second(n - 2, 0, False)
        second(n - 1, 1, True)

    @pl.when(n % 2 == 1)
    def _():
        second(n - 1, 0, True)


def _merge_pair(o_even, o_odd):
    lane = lax.broadcasted_iota(jnp.int32, (QB, 128), 1)
    return jnp.where(lane < HEAD_DIM, o_even, o_odd)


CNT_BLOCKS = 4
KEY_ROWS = 32
assert NK == 8 * 32


def _bit_transpose32(load_row, tmp_ref, store_row):
    def swap(a, b, j, m):
        t = (a ^ lax.shift_right_logical(b, jnp.int32(j))) & m
        return a ^ t, b ^ (t << j)

    lower = []
    for k in range(16):
        a, b = swap(load_row(k), load_row(k + 16), 16, 0x0000FFFF)
        lower.append(a)
        tmp_ref[k] = b
    for base in (0, 16):
        x = lower if base == 0 else [tmp_ref[k] for k in range(16)]
        j, m = 8, 0x00FF00FF
        while j:
            k = 0
            while k < 16:
                x[k], x[k + j] = swap(x[k], x[k + j], j, m)
                k = (k + j + 1) & ~j
            j >>= 1
            m ^= m << j
        for i in range(16):
            store_row(base + i, x[i])


def _keep_lowest_ties(alive_ref, kept_ref, need, n_groups):
    n_blocks = alive_ref.shape[0]
    sub = lax.broadcasted_iota(jnp.int32, (8, DQ), 0)
    zero_masks = []
    for c in reversed(range(max(n_blocks - 1, 1).bit_length())):
        zero_masks.append(
            lambda jb, c=c: jnp.where(((jb >> c) & 1) == 0, jnp.int32(-1), jnp.int32(0)))
    for word in (0xFFFF0000, 0xFF00FF00, 0xF0F0F0F0, 0xCCCCCCCC, 0xAAAAAAAA):
        zero_masks.append(lambda jb, word=word: jnp.int32(word - (1 << 32)))
    for c in (2, 1, 0):
        zero_masks.append(lambda jb, c=c: jnp.where(((sub >> c) & 1) == 0, -1, 0))

    remaining = need
    took_zero = None
    for p, zero_mask in enumerate(zero_masks):
        prev_mask = zero_masks[p - 1] if p else None

        def body(g, cnts, zero_mask=zero_mask, prev_mask=prev_mask, took_zero=took_zero):
            cnts = list(cnts)
            for u in range(CNT_BLOCKS):
                jb = g * CNT_BLOCKS + u
                alive = alive_ref[jb]
                if prev_mask is None:
                    kept_ref[jb] = jnp.zeros((8, DQ), jnp.int32)
                else:
                    lows = alive & prev_mask(jb)
                    kept_ref[jb] = kept_ref[jb] | jnp.where(took_zero, 0, lows)
                    alive = jnp.where(took_zero, lows, alive ^ lows)
                    alive_ref[jb] = alive
                cnts[u] = cnts[u] + lax.population_count(alive & zero_mask(jb))
            return tuple(cnts)

        zeros = jnp.zeros((8, DQ), jnp.int32)
        cnts = lax.fori_loop(0, n_groups, body, (zeros,) * CNT_BLOCKS)
        n_zero = jnp.sum(sum(cnts[1:], cnts[0]), axis=0, keepdims=True)
        took_zero = n_zero >= remaining
        remaining = jnp.where(took_zero, remaining, remaining - n_zero)

    last_mask = zero_masks[-1]

    def finish(jb, carry):
        alive = alive_ref[jb]
        alive_ref[jb] = kept_ref[jb] | jnp.where(took_zero, alive & last_mask(jb), alive)
        return carry

    lax.fori_loop(0, n_groups * CNT_BLOCKS, finish, 0)


def _dsa_kernel(aq_ref, iq_ref, iwt_ref, ak_ref, avt_ref, ik_ref, bias_ref, o_ref,
                keybuf_ref, plane_ref, alive_ref, above_ref, kept_ref, sel_ref, qm_ref, m_ref,
                acc_ref, s0_ref, cm0_ref, s1_ref, cm1_ref, raw0_ref, raw1_ref, tmp_ref, *,
                k_sel):
    step = pl.program_id(1)
    n_q = pl.num_programs(1) - 1
    att = step - 1
    krow = lax.broadcasted_iota(jnp.int32, (KEY_ROWS, DQ), 0)
    qcol = lax.broadcasted_iota(jnp.int32, (KEY_ROWS, DQ), 1)
    qchunk = (step * DQ + qcol) // CHUNK

    iwt = iwt_ref[...] * (IDX_HEADS ** -0.5)

    def dots_stage(jb, slot):
        raw_ref = (raw0_ref, raw1_ref)[slot]
        k0 = pl.multiple_of(jb * NK, NK)
        for h in range(IDX_HEADS):
            ikh = ik_ref[pl.ds(k0, NK), (h % 2) * 128:(h % 2 + 1) * 128]
            iqp = iq_ref[:, (h // 2) * 128:(h // 2 + 1) * 128]
            raw_ref[h] = lax.dot_general(ikh, iqp, _NT, preferred_element_type=F32)

    def keys_stage(jb, slot, last):
        raw_ref = (raw0_ref, raw1_ref)[slot]
        k0 = pl.multiple_of(jb * NK, NK)
        for c in range(NK // KEY_ROWS):
            rows = slice(c * KEY_ROWS, (c + 1) * KEY_ROWS)
            acc = jnp.zeros((KEY_ROWS, DQ), F32)
            for h in range(IDX_HEADS):
                acc = acc + iwt[h:h + 1, :] * jnp.maximum(raw_ref[h, rows, :], 0.0)
            bits = lax.bitcast_convert_type(acc, jnp.int32)
            key = bits ^ ((bits >> 31) & 0x7FFFFFFF)
            key = jnp.where(key == -1, 0, key)
            ukey = key ^ INT_MIN
            if last:
                admissible = ((k0 + c * KEY_ROWS + krow) // CHUNK) <= qchunk
                ukey = jnp.where(admissible, ukey, 0)
            keybuf_ref[slot, rows, :] = ukey
        for half in range(DQ // 128):
            lanes = slice(half * 128, (half + 1) * 128)

            def load_row(r):
                return keybuf_ref[slot, 8 * r:8 * r + 8, lanes]

            def store_plane(i, v):
                plane_ref[31 - i, jb, :, lanes] = v

            _bit_transpose32(load_row, tmp_ref, store_plane)
        plane_ref[32, jb] = jnp.full((8, DQ), -1, jnp.int32)
        alive_ref[jb] = jnp.full((8, DQ), -1, jnp.int32)
        above_ref[jb] = jnp.zeros((8, DQ), jnp.int32)

    ones = jnp.ones((ONES_ROWS, NK), BF16)
    slots = ((s0_ref, cm0_ref), (s1_ref, cm1_ref))

    def logits_stage(jb, slot, bias_idx):
        s_ref, cm_ref = slots[slot]
        k0 = pl.multiple_of(jb * NK, NK)
        sel = sel_ref[jb]
        mask = jnp.concatenate(
            [jnp.where((sel << r) < 0, 0.0, NEG_BIG)
             for r in range(32)], axis=0).astype(BF16)
        for h in range(N_HEADS):
            kp = ak_ref[pl.ds(k0, NK), (h // 2) * 128:(h // 2 + 1) * 128]
            s = lax.dot_general(kp, qm_ref[h], _NT, preferred_element_type=F32)
            if bias_idx is not None:
                s = s + bias_ref[bias_idx, h]
            sb = s.astype(BF16) + mask
            s_ref[h] = sb
            cm_ref[h] = jnp.max(sb, axis=0, keepdims=True).astype(F32)

    def softmax_stage(jb, slot):
        s_ref, cm_ref = slots[slot]
        for h in range(N_HEADS):
            m_prev = m_ref[h]
            m_new = jnp.maximum(m_prev, cm_ref[h])
            alpha = jnp.exp2(m_prev - m_new)
            pe = jnp.exp2(s_ref[h] - m_new.astype(BF16))
            vt = jnp.concatenate([avt_ref[jb, h * HEAD_DIM:(h + 1) * HEAD_DIM, :], ones], axis=0)
            acc_ref[h] = alpha * acc_ref[h] + jnp.dot(vt, pe, preferred_element_type=F32)
            m_ref[h] = m_new

    def start_attention():
        lane = lax.broadcasted_iota(jnp.int32, (DQ, 128), 1)
        for p in range(N_PAIR):
            qp = aq_ref[:, p * 128:(p + 1) * 128].astype(F32)
            qm_ref[2 * p] = jnp.where(lane < HEAD_DIM, qp, 0.0).astype(BF16)
            qm_ref[2 * p + 1] = jnp.where(lane >= HEAD_DIM, qp, 0.0).astype(BF16)
        m_ref[...] = jnp.full(m_ref.shape, NEG_BIG, F32)
        acc_ref[...] = jnp.zeros(acc_ref.shape, F32)
        logits_stage(att, 0, 0)
        logits_stage(jnp.maximum(att - 1, 0), 1, 1)
        softmax_stage(att, 0)

    @pl.when(step == 0)
    def _():
        dots_stage(0, 0)
        keys_stage(0, 0, True)

    @pl.when(step == n_q)
    def _():
        start_attention()
        n_steps = att - 1

        def two_steps(u, carry):
            b = att - 1 - 2 * u
            logits_stage(b - 1, 0, None)
            softmax_stage(b, 1)
            logits_stage(b - 2, 1, None)
            softmax_stage(b - 1, 0)
            return carry

        lax.fori_loop(0, jnp.maximum(n_steps, 0) // 2, two_steps, 0)

        @pl.when(jnp.logical_and(n_steps >= 1, n_steps % 2 == 1))
        def _():
            logits_stage(0, 0, None)
            softmax_stage(1, 1)

        @pl.when(jnp.logical_and(att >= 1, att % 2 == 1))
        def _():
            softmax_stage(0, 1)

        @pl.when(jnp.logical_and(att >= 1, att % 2 == 0))
        def _():
            softmax_stage(0, 0)

    @pl.when(jnp.logical_and(step >= 1, step < n_q))
    def _():
        start_attention()
        dots_stage(0, 0)
        n_fused = jnp.maximum(att - 1, 0) // 2

        def fused_trip(u, carry):
            b = att - 1 - 2 * u
            j = 2 * u
            logits_stage(b - 1, 0, None)
            dots_stage(j + 1, 1)
            softmax_stage(b, 1)
            keys_stage(j, 0, False)
            logits_stage(b - 2, 1, None)
            dots_stage(j + 2, 0)
            softmax_stage(b - 1, 0)
            keys_stage(j + 1, 1, False)
            return carry

        lax.fori_loop(0, n_fused, fused_trip, 0)
        j0 = 2 * n_fused

        @pl.when(att == 0)
        def _():
            dots_stage(1, 1)
            keys_stage(0, 0, False)
            keys_stage(1, 1, True)

        @pl.when(att % 2 == 1)
        def _():
            dots_stage(j0 + 1, 1)
            softmax_stage(0, 1)
            keys_stage(j0, 0, False)
            dots_stage(j0 + 2, 0)
            keys_stage(j0 + 1, 1, False)
            keys_stage(j0 + 2, 0, True)

        @pl.when(jnp.logical_and(att >= 2, att % 2 == 0))
        def _():
            logits_stage(0, 0, None)
            dots_stage(j0 + 1, 1)
            softmax_stage(1, 1)
            keys_stage(j0, 0, False)
            dots_stage(j0 + 2, 0)
            softmax_stage(0, 0)
            keys_stage(j0 + 1, 1, False)
            dots_stage(j0 + 3, 1)
            keys_stage(j0 + 2, 0, False)
            keys_stage(j0 + 3, 1, True)

    @pl.when(step >= 1)
    def _():
        for p in range(N_PAIR):
            halves = []
            for h in (2 * p, 2 * p + 1):
                a = acc_ref[h]
                halves.append(a[:HEAD_DIM, :] / a[HEAD_DIM:HEAD_DIM + 1, :])
            o_ref[:, p * 128:(p + 1) * 128] = jnp.concatenate(halves, axis=0).T.astype(BF16)

    nkb = jnp.where(step < n_q, step + 1, 0)
    n_groups = (nkb + CNT_BLOCKS - 1) // CNT_BLOCKS

    def pad_block(jb, carry):
        for b in range(33):
            plane_ref[b, jb] = jnp.zeros((8, DQ), jnp.int32)
        alive_ref[jb] = jnp.zeros((8, DQ), jnp.int32)
        above_ref[jb] = jnp.zeros((8, DQ), jnp.int32)
        return carry

    lax.fori_loop(nkb, n_groups * CNT_BLOCKS, pad_block, 0)

    def select_pass(it, state):
        took_prev, n_above, thr_u = state
        b = 31 - it
        take_prev = took_prev != 0

        def body(g, cnts):
            cnts = list(cnts)
            for u in range(CNT_BLOCKS):
                jb = g * CNT_BLOCKS + u
                alive = alive_ref[jb]
                with_prev = alive & plane_ref[b + 1, jb]
                above_ref[jb] = above_ref[jb] | jnp.where(take_prev, 0, with_prev)
                alive = jnp.where(take_prev, with_prev, alive ^ with_prev)
                alive_ref[jb] = alive
                cnts[u] = cnts[u] + lax.population_count(alive & plane_ref[b, jb])
            return tuple(cnts)

        zeros = jnp.zeros((8, DQ), jnp.int32)
        cnts = lax.fori_loop(0, n_groups, body, (zeros,) * CNT_BLOCKS)
        n_one = jnp.sum(sum(cnts[1:], cnts[0]), axis=0, keepdims=True)
        take = (n_above + n_one) >= k_sel
        n_above = jnp.where(take, n_above, n_above + n_one)
        thr_u = jnp.where(take, thr_u | jnp.left_shift(jnp.int32(1), b), thr_u)
        return take.astype(jnp.int32), n_above, thr_u

    row0 = jnp.zeros((1, DQ), jnp.int32)
    took_last, n_above, thr_u = lax.fori_loop(0, 32, select_pass, (row0 + 1, row0, row0))

    def settle(g, cnts):
        cnts = list(cnts)
        for u in range(CNT_BLOCKS):
            jb = g * CNT_BLOCKS + u
            alive = alive_ref[jb]
            with_last = alive & plane_ref[0, jb]
            above_ref[jb] = above_ref[jb] | jnp.where(took_last != 0, 0, with_last)
            alive = jnp.where(took_last != 0, with_last, alive ^ with_last)
            alive_ref[jb] = alive
            cnts[u] = cnts[u] + lax.population_count(alive)
        return tuple(cnts)

    zeros = jnp.zeros((8, DQ), jnp.int32)
    cnts = lax.fori_loop(0, n_groups, settle, (zeros,) * CNT_BLOCKS)
    n_tied = jnp.sum(sum(cnts[1:], cnts[0]), axis=0, keepdims=True)
    need = k_sel - n_above
    real = thr_u != 0
    extra = jnp.logical_and(n_tied > need, real)

    @pl.when(jnp.max(extra.astype(jnp.int32)) > 0)
    def _():
        _keep_lowest_ties(alive_ref, kept_ref, need, n_groups)

    def finalize(jb, carry):
        sel_ref[jb] = above_ref[jb] | jnp.where(real, alive_ref[jb], 0)
        return carry

    lax.fori_loop(0, nkb, finalize, 0)


def _dsa(proj3, ik3, avt4, iwt, bias_tiles, k_sel):
    bsz, seq, _ = proj3.shape
    nq = seq // DQ
    assert seq % (NK * CNT_BLOCKS) == 0
    resident = dict(pipeline_mode=pl.Buffered(1))
    return pl.pallas_call(
        functools.partial(_dsa_kernel, k_sel=k_sel),
        name="dsa",
        grid=(bsz, nq + 1),
        in_specs=[
            pl.BlockSpec((None, DQ, 512), lambda b, s: (b, jnp.maximum(s - 1, 0), COL_AQ)),
            pl.BlockSpec((None, DQ, 512), lambda b, s: (b, jnp.minimum(s, nq - 1), COL_IQ)),
            pl.BlockSpec((IDX_HEADS, DQ), lambda b, s: (0, b * nq + jnp.minimum(s, nq - 1))),
            pl.BlockSpec((None, seq, 512), lambda b, i: (b, 0, COL_AK), **resident),
            pl.BlockSpec((None, seq // NK, 512, NK), lambda b, i: (b, 0, 0, 0), **resident),
            pl.BlockSpec((None, seq, N_IK), lambda b, i: (b, 0, 0), **resident),
            pl.BlockSpec(bias_tiles.shape, lambda b, i: (0, 0, 0, 0), **resident),
        ],
        out_specs=pl.BlockSpec((None, DQ, 512), lambda b, s: (b, jnp.maximum(s - 1, 0), 0)),
        out_shape=jax.ShapeDtypeStruct((bsz, seq, 512), BF16),
        scratch_shapes=[
            pltpu.VMEM((2, NK, DQ), jnp.int32),
            pltpu.VMEM((33, seq // NK, 8, DQ), jnp.int32),
            pltpu.VMEM((seq // NK, 8, DQ), jnp.int32),
            pltpu.VMEM((seq // NK, 8, DQ), jnp.int32),
            pltpu.VMEM((seq // NK, 8, DQ), jnp.int32),
            pltpu.VMEM((seq // NK, 8, DQ), jnp.int32),
            pltpu.VMEM((N_HEADS, DQ, 128), BF16),
            pltpu.VMEM((N_HEADS, 1, DQ), F32),
            pltpu.VMEM((N_HEADS, HEAD_DIM + ONES_ROWS, DQ), F32),
            pltpu.VMEM((N_HEADS, NK, DQ), BF16),
            pltpu.VMEM((N_HEADS, 1, DQ), F32),
            pltpu.VMEM((N_HEADS, NK, DQ), BF16),
            pltpu.VMEM((N_HEADS, 1, DQ), F32),
            pltpu.VMEM((IDX_HEADS, NK, DQ), F32),
            pltpu.VMEM((IDX_HEADS, NK, DQ), F32),
            pltpu.VMEM((16, 8, 128), jnp.int32),
        ],
        compiler_params=pltpu.CompilerParams(
            dimension_semantics=("arbitrary", "arbitrary"), vmem_limit_bytes=VMEM_LIMIT),
    )(proj3, proj3, iwt, proj3, avt4, ik3, bias_tiles)


SB_DEAD_MASS = 104.0 * LOG2E


def _sb_kernel(q_ref, k_ref, v_ref, o_ref, qm_ref, uu_ref, carry_ref, acc_ref, z_ref, sp_ref,
               later_ref):
    i = pl.program_id(1)
    diag = (i * QB + QB - 1) // NK

    @pl.when(jnp.logical_and(pl.program_id(0) == 0, i == 0))
    def _():
        kr = lax.broadcasted_iota(jnp.int32, (2 * NK, NK), 0) % NK
        kc = lax.broadcasted_iota(jnp.int32, (2 * NK, NK), 1)
        uu_ref[...] = jnp.where(kr > kc, 1.0, 0.0).astype(BF16)

    _split_heads_into(qm_ref, q_ref[...])
    carry_ref[...] = jnp.zeros(carry_ref.shape, F32)
    acc_ref[...] = jnp.zeros(acc_ref.shape, F32)

    def block(jb, on_diagonal):
        k0 = pl.multiple_of(jb * NK, NK)
        if on_diagonal:
            row = lax.broadcasted_iota(jnp.int32, (2 * QB, NK), 0)
            col = lax.broadcasted_iota(jnp.int32, (2 * QB, NK), 1)
            causal = (k0 + col) < (i * QB + row % QB)
        for p in range(N_PAIR):
            kp = k_ref[pl.ds(k0, NK), p * 128:(p + 1) * 128]
            z_ref[p] = lax.dot_general(qm_ref[p], kp, _NT, preferred_element_type=F32)
        for p in range(N_PAIR):
            z = z_ref[p]
            neg_abs = lax.bitcast_convert_type(
                lax.bitcast_convert_type(z, jnp.int32) | INT_MIN, F32)
            sp = jnp.maximum(z, 0.0) + jnp.log(1.0 + jnp.exp2(neg_abs)) * LOG2E
            if on_diagonal:
                sp = jnp.where(causal, sp, 0.0)
            sp_ref[p] = sp
            later_ref[p] = jnp.dot(sp.astype(BF16), uu_ref[:NK, :], preferred_element_type=F32)
        for p in range(N_PAIR):
            vp = v_ref[pl.ds(k0, NK), p * 128:(p + 1) * 128]
            carry = carry_ref[p]
            sp = sp_ref[p]
            a = jnp.exp2(z_ref[p] - sp - later_ref[p] - carry)
            if on_diagonal:
                a = jnp.where(causal, a, 0.0)
            acc_ref[p] += jnp.dot(a.astype(BF16), vp, preferred_element_type=F32)
            carry_ref[p] = carry + jnp.sum(sp, axis=1, keepdims=True)

    block(diag, True)

    def alive():
        return (jnp.min(carry_ref[...]) <= SB_DEAD_MASS).astype(jnp.int32)

    def cond(state):
        jb, go = state
        return jnp.logical_and(jb >= 0, go > 0)

    def body(state):
        jb, _ = state
        block(jb, False)
        return jb - 1, alive()

    lax.while_loop(cond, body, (diag - 1, alive()))

    for p in range(N_PAIR):
        o_ref[:, p * 128:(p + 1) * 128] = _merge_pair(
            acc_ref[p, :QB, :], acc_ref[p, QB:, :]).astype(BF16)


def _stick_breaking(proj3):
    bsz, seq, _ = proj3.shape
    resident = dict(pipeline_mode=pl.Buffered(1))
    return pl.pallas_call(
        _sb_kernel,
        name="stick_breaking",
        grid=(bsz, seq // QB),
        in_specs=[
            pl.BlockSpec((None, QB, 512), lambda b, i: (b, i, COL_BQ)),
            pl.BlockSpec((None, seq, 512), lambda b, i: (b, 0, COL_BK), **resident),
            pl.BlockSpec((None, seq, 512), lambda b, i: (b, 0, COL_BV), **resident),
        ],
        out_specs=pl.BlockSpec((None, QB, 512), lambda b, i: (b, i, 0)),
        out_shape=jax.ShapeDtypeStruct((bsz, seq, 512), BF16),
        scratch_shapes=[
            pltpu.VMEM((N_PAIR, 2 * QB, 128), BF16),
            pltpu.VMEM((2 * NK, NK), BF16),
            pltpu.VMEM((N_PAIR, 2 * QB, 1), F32),
            pltpu.VMEM((N_PAIR, 2 * QB, 128), F32),
            pltpu.VMEM((N_PAIR, 2 * QB, NK), F32),
            pltpu.VMEM((N_PAIR, 2 * QB, NK), F32),
            pltpu.VMEM((N_PAIR, 2 * QB, NK), F32),
        ],
        compiler_params=pltpu.CompilerParams(
            dimension_semantics=("arbitrary", "arbitrary"), vmem_limit_bytes=VMEM_LIMIT),
    )(proj3, proj3, proj3)


MERGE_ROWS = 512


def _merge_kernel(x_ref, ya_ref, yb_ref, cq_ref, g0_ref, g1_ref, g2_ref, mk_ref, mv_ref,
                  wa_ref, wb_ref, wc_ref, wo_ref, gp_ref, o_ref):
    n_rows = x_ref.shape[0]
    for r0 in range(0, n_rows, MERGE_ROWS):
        rows = slice(r0, r0 + MERGE_ROWS)
        heads = []
        for h in range(C_HEADS):
            sl = slice(h * C_HEAD_DIM, (h + 1) * C_HEAD_DIM)
            s = lax.dot_general(cq_ref[rows, sl], mk_ref[:, sl], _NT,
                                preferred_element_type=F32) * (C_HEAD_DIM ** -0.5)
            e = jnp.exp(s - jnp.max(s, axis=1, keepdims=True))
            p = e / jnp.sum(e, axis=1, keepdims=True)
            heads.append(jnp.dot(p.astype(BF16), mv_ref[:, sl], preferred_element_type=F32))
        yc_pre = jnp.concatenate(heads, axis=1).astype(BF16)
        ya = jnp.dot(ya_ref[rows, :], wa_ref[...], preferred_element_type=F32)
        yb = jnp.dot(yb_ref[rows, :], wb_ref[...], preferred_element_type=F32)
        yc = jnp.dot(yc_pre, wc_ref[...], preferred_element_type=F32)
        merged = (g0_ref[rows, :].astype(F32) * ya + g1_ref[rows, :].astype(F32) * yb
                  + g2_ref[rows, :].astype(F32) * yc)
        o = jnp.dot(merged.astype(BF16), wo_ref[...], preferred_element_type=F32)
        o_ref[rows, :] = x_ref[rows, :] + _rms(o, gp_ref[...])


def _merge(x2, ya2, yb2, proj2, mkv3, wa, wb, wc, wo, g_post, seq):
    n = x2.shape[0]
    tm = min(2 * MERGE_ROWS, seq)
    per_batch = seq // tm
    n_mem = mkv3.shape[1]
    c_dim = C_HEADS * C_HEAD_DIM
    const = lambda t: (0, 0)
    return pl.pallas_call(
        _merge_kernel,
        name="merge",
        grid=(n // tm,),
        in_specs=[
            pl.BlockSpec((tm, D_MODEL), lambda t: (t, 0)),
            pl.BlockSpec((tm, 512), lambda t: (t, 0)),
            pl.BlockSpec((tm, 512), lambda t: (t, 0)),
            pl.BlockSpec((tm, 512), lambda t: (t, COL_CQ)),
            pl.BlockSpec((tm, D_MODEL), lambda t: (t, 0)),
            pl.BlockSpec((tm, D_MODEL), lambda t: (t, 1)),
            pl.BlockSpec((tm, D_MODEL), lambda t: (t, 2)),
            pl.BlockSpec((None, n_mem, c_dim), lambda t: (t // per_batch, 0, 0)),
            pl.BlockSpec((None, n_mem, c_dim), lambda t: (t // per_batch, 0, 1)),
            pl.BlockSpec(wa.shape, const),
            pl.BlockSpec(wb.shape, const),
            pl.BlockSpec(wc.shape, const),
            pl.BlockSpec(wo.shape, const),
            pl.BlockSpec((1, D_MODEL), const),
        ],
        out_specs=pl.BlockSpec((tm, D_MODEL), lambda t: (t, 0)),
        out_shape=jax.ShapeDtypeStruct((n, D_MODEL), F32),
        compiler_params=pltpu.CompilerParams(
            dimension_semantics=("arbitrary",), vmem_limit_bytes=VMEM_LIMIT),
    )(x2, ya2, yb2, proj2, proj2, proj2, proj2, mkv3, mkv3, wa, wb, wc, wo, g_post)


def _ffn_kernel(x_ref, gpre_ref, wg_ref, wu_ref, wo_ref, gpost_ref, o_ref, h_ref, acc_ref):
    k = pl.program_id(1)

    @pl.when(k == 0)
    def _():
        h_ref[...] = _rms(x_ref[...], gpre_ref[...]).astype(BF16)
        acc_ref[...] = jnp.zeros(acc_ref.shape, F32)

    h = h_ref[...]
    g = jnp.dot(h, wg_ref[...], preferred_element_type=F32)
    u = jnp.dot(h, wu_ref[...], preferred_element_type=F32)
    act = (g * jax.nn.sigmoid(g) * u).astype(BF16)
    acc_ref[...] += jnp.dot(act, wo_ref[...], preferred_element_type=F32)

    @pl.when(k == pl.num_programs(1) - 1)
    def _():
        o_ref[...] = x_ref[...] + _rms(acc_ref[...], gpost_ref[...])


def _ffn(x2, g_pre, wg, wu, wo, g_post):
    n = x2.shape[0]
    d_ff = wg.shape[1]
    tm = min(512, n)
    tf = d_ff // 2
    return pl.pallas_call(
        _ffn_kernel,
        name="ffn",
        grid=(n // tm, d_ff // tf),
        in_specs=[
            pl.BlockSpec((tm, D_MODEL), lambda t, k: (t, 0)),
            pl.BlockSpec((1, D_MODEL), lambda t, k: (0, 0)),
            pl.BlockSpec((D_MODEL, tf), lambda t, k: (0, k)),
            pl.BlockSpec((D_MODEL, tf), lambda t, k: (0, k)),
            pl.BlockSpec((tf, D_MODEL), lambda t, k: (k, 0)),
            pl.BlockSpec((1, D_MODEL), lambda t, k: (0, 0)),
        ],
        out_specs=pl.BlockSpec((tm, D_MODEL), lambda t, k: (t, 0)),
        out_shape=jax.ShapeDtypeStruct((n, D_MODEL), F32),
        scratch_shapes=[pltpu.VMEM((tm, D_MODEL), BF16), pltpu.VMEM((tm, D_MODEL), F32)],
        compiler_params=pltpu.CompilerParams(
            dimension_semantics=("arbitrary", "arbitrary"), vmem_limit_bytes=VMEM_LIMIT),
    )(x2, g_pre, wg, wu, wo, g_post)


def _pack_w_in(w, b_gate):
    sizes = (512, 512, 512, IDX_HEADS * 64, 64, IDX_HEADS, 512, 512, 512, 512,
             N_BRANCH * D_MODEL)
    aq, ak, av, iq, ik, iw, bq, bk, bv, cq, gates = jnp.split(w, np.cumsum(sizes)[:-1], axis=1)
    scale = HEAD_DIM ** -0.5
    scale2 = scale * LOG2E
    w_main = jnp.concatenate(
        [gates, aq * scale2, ak, iq * scale, bq * scale2, bk, bv, cq], axis=1).astype(BF16)
    z64 = jnp.zeros((D_MODEL, 64), F32)
    w_ik = jnp.concatenate([ik, z64, z64, ik], axis=1).astype(BF16)
    w_trans = jnp.concatenate(
        [av, iw, jnp.zeros((D_MODEL, N_TRANS - 512 - IDX_HEADS), F32)], axis=1).T.astype(BF16)
    b_main = jnp.concatenate([b_gate, jnp.zeros((N_MAIN - N_GATE,), F32)])[None, :]
    return w_main, w_ik, w_trans, b_main


def kernel(x, mem, rel_bias, g_mix_pre, w_in, b_gate, g_mem, w_mem_kv, w_up_a, w_up_b, w_up_c,
           w_out, g_mix_post, g_ffn_pre, w_ffn_in, w_ffn_out, g_ffn_post):
    bsz, seq, _ = x.shape
    n_mem = mem.shape[1]
    k_sel = min(TOPK_MAX, seq // 4)
    bias_tiles = _bias_tiles(rel_bias)
    x2 = x.reshape(bsz * seq, D_MODEL)
    for l in range(w_in.shape[0]):
        w_main, w_ik, w_trans, b_main = _pack_w_in(w_in[l], b_gate[l])
        proj2, ik2, avt, iwt = _project(x2, g_mix_pre[l][None, :], w_main, b_main, w_ik, w_trans)
        proj3 = proj2.reshape(bsz, seq, N_MAIN)
        mkv = _memkv(mem.reshape(bsz * n_mem, D_MODEL), g_mem[l][None, :],
                     w_mem_kv[l].astype(BF16))
        ya = _dsa(proj3, ik2.reshape(bsz, seq, N_IK), avt.reshape(bsz, seq // NK, 512, NK),
                  iwt, bias_tiles, k_sel)
        yb = _stick_breaking(proj3)
        x2 = _merge(x2, ya.reshape(bsz * seq, 512), yb.reshape(bsz * seq, 512), proj2,
                    mkv.reshape(bsz, n_mem, 2 * C_HEADS * C_HEAD_DIM),
                    w_up_a[l].astype(BF16), w_up_b[l].astype(BF16), w_up_c[l].astype(BF16),
                    w_out[l].astype(BF16), g_mix_post[l][None, :], seq)
        d_ff = w_ffn_out.shape[1]
        w_ffn = w_ffn_in[l].astype(BF16)
        x2 = _ffn(x2, g_ffn_pre[l][None, :], w_ffn[:, :d_ff], w_ffn[:, d_ff:],
                  w_ffn_out[l].astype(BF16), g_ffn_post[l][None, :])
    return x2.reshape(bsz, seq, D_MODEL)
```

```python
import functools

import numpy as np
import jax
import jax.numpy as jnp
from jax import lax
from jax.experimental import pallas as pl
from jax.experimental.pallas import tpu as pltpu

D_MODEL = 1024
CHUNK = 64
HEAD_DIM = 64
N_HEADS = 8
IDX_HEADS = 8
TOPK_MAX = 256
C_HEADS = 4
C_HEAD_DIM = 128
N_BRANCH = 3
REL_BUCKETS = 32
EPS = 1e-6

F32 = jnp.float32
BF16 = jnp.bfloat16
INT_MIN = -2 ** 31
NEG_BIG = -1e30
LOG2E = 1.4426950408889634

QB = 256
DQ = 256
NK = 256
N_PAIR = N_HEADS // 2
ONES_ROWS = 16

N_GATE = N_BRANCH * D_MODEL
COL_AQ, COL_AK, COL_IQ, COL_BQ, COL_BK, COL_BV, COL_CQ = range(N_GATE // 512, N_GATE // 512 + 7)
N_MAIN = N_GATE + 7 * 512
N_IK = 256
N_TRANS = 512 + 16

VMEM_LIMIT = 56 * 1024 * 1024

_NT = (((1,), (1,)), ((), ()))


def _rms(x, g):
    return x * lax.rsqrt(jnp.mean(x * x, axis=-1, keepdims=True) + EPS) * g


PROJ_ROWS = 1024


def _proj_kernel(x_ref, g_ref, w_ref, b_ref, wik_ref, wt_ref, o_ref, ik_ref, avt_ref, iwt_ref,
                 h_ref, *, n_gate_tiles):
    j = pl.program_id(1)

    @pl.when(j == 0)
    def _():
        for r0 in range(0, h_ref.shape[0], PROJ_ROWS):
            rows = slice(r0, min(r0 + PROJ_ROWS, h_ref.shape[0]))
            hb = _rms(x_ref[rows, :], g_ref[...]).astype(BF16)
            h_ref[rows, :] = hb
            ik_ref[rows, :] = jnp.dot(hb, wik_ref[...], preferred_element_type=F32).astype(BF16)
            tr = lax.dot_general(wt_ref[...], hb, _NT, preferred_element_type=F32)
            for c in range(hb.shape[0] // NK):
                avt_ref[r0 // NK + c] = tr[:512, c * NK:(c + 1) * NK].astype(BF16)
            iwt_ref[:, rows] = tr[512:512 + IDX_HEADS, :]

    def tile(finish):
        for r0 in range(0, h_ref.shape[0], PROJ_ROWS):
            rows = slice(r0, min(r0 + PROJ_ROWS, h_ref.shape[0]))
            acc = jnp.dot(h_ref[rows, :], w_ref[...], preferred_element_type=F32)
            o_ref[rows, :] = finish(acc).astype(BF16)

    @pl.when(j >= n_gate_tiles)
    def _():
        tile(lambda acc: acc)

    @pl.when(j < n_gate_tiles)
    def _():
        tile(lambda acc: 0.5 + 0.5 * jnp.tanh(0.5 * (acc + b_ref[...])))


def _project(x2, g, w_main, b_main, w_ik, w_trans):
    n = x2.shape[0]
    tm = min(2048, n)
    tn = 512
    grid = (n // tm, N_MAIN // tn)
    return pl.pallas_call(
        functools.partial(_proj_kernel, n_gate_tiles=N_GATE // tn),
        name="in_proj",
        grid=grid,
        in_specs=[
            pl.BlockSpec((tm, D_MODEL), lambda i, j: (i, 0)),
            pl.BlockSpec((1, D_MODEL), lambda i, j: (0, 0)),
            pl.BlockSpec((D_MODEL, tn), lambda i, j: (0, j)),
            pl.BlockSpec((1, tn), lambda i, j: (0, j)),
            pl.BlockSpec((D_MODEL, N_IK), lambda i, j: (0, 0)),
            pl.BlockSpec((N_TRANS, D_MODEL), lambda i, j: (0, 0)),
        ],
        out_specs=[
            pl.BlockSpec((tm, tn), lambda i, j: (i, j)),
            pl.BlockSpec((tm, N_IK), lambda i, j: (i, 0)),
            pl.BlockSpec((tm // NK, 512, NK), lambda i, j: (i, 0, 0)),
            pl.BlockSpec((IDX_HEADS, tm), lambda i, j: (0, i)),
        ],
        out_shape=[
            jax.ShapeDtypeStruct((n, N_MAIN), BF16),
            jax.ShapeDtypeStruct((n, N_IK), BF16),
            jax.ShapeDtypeStruct((n // NK, 512, NK), BF16),
            jax.ShapeDtypeStruct((IDX_HEADS, n), F32),
        ],
        scratch_shapes=[pltpu.VMEM((tm, D_MODEL), BF16)],
        compiler_params=pltpu.CompilerParams(
            dimension_semantics=("arbitrary", "arbitrary"), vmem_limit_bytes=VMEM_LIMIT),
    )(x2, g, w_main, b_main, w_ik, w_trans)


def _memkv_kernel(x_ref, g_ref, w_ref, o_ref):
    hb = _rms(x_ref[...], g_ref[...]).astype(BF16)
    o_ref[...] = jnp.dot(hb, w_ref[...], preferred_element_type=F32).astype(BF16)


def _memkv(mem2, g, w):
    n = mem2.shape[0]
    tm = min(512, n)
    return pl.pallas_call(
        _memkv_kernel,
        name="mem_kv",
        grid=(n // tm,),
        in_specs=[
            pl.BlockSpec((tm, D_MODEL), lambda i: (i, 0)),
            pl.BlockSpec((1, D_MODEL), lambda i: (0, 0)),
            pl.BlockSpec((D_MODEL, w.shape[1]), lambda i: (0, 0)),
        ],
        out_specs=pl.BlockSpec((tm, w.shape[1]), lambda i: (i, 0)),
        out_shape=jax.ShapeDtypeStruct((n, w.shape[1]), BF16),
        compiler_params=pltpu.CompilerParams(
            dimension_semantics=("arbitrary",), vmem_limit_bytes=VMEM_LIMIT),
    )(mem2, g, w)


BIAS_OFFSETS = (0, -NK)
_LOG_BUCKET_STARTS = (12, 16, 23, 32, 46, 64, 91)
FAR_BUCKET = 15
assert DQ == NK


def _bias_kernel(rb_ref, o_ref):
    key = lax.broadcasted_iota(jnp.int32, (NK, DQ), 0)
    qry = lax.broadcasted_iota(jnp.int32, (NK, DQ), 1)
    for c, off in enumerate(BIAS_OFFSETS):
        rel = key - qry + off
        n = jnp.abs(rel)
        large = jnp.full((NK, DQ), 8, jnp.int32)
        for start in _LOG_BUCKET_STARTS:
            large = large + jnp.where(n >= start, 1, 0)
        bucket = jnp.where(rel > 0, REL_BUCKETS // 2, 0) + jnp.where(n < 8, n, large)
        for h in range(N_HEADS):
            val = jnp.full((NK, DQ), rb_ref[0, h], F32)
            for b in range(1, REL_BUCKETS):
                val = jnp.where(bucket == b, rb_ref[b, h], val)
            o_ref[c, h] = (val - rb_ref[FAR_BUCKET, h]) * LOG2E


def _bias_tiles(rel_bias):
    return pl.pallas_call(
        _bias_kernel,
        name="rel_bias_tiles",
        in_specs=[pl.BlockSpec(memory_space=pltpu.SMEM)],
        out_specs=pl.BlockSpec(memory_space=pltpu.VMEM),
        out_shape=jax.ShapeDtypeStruct((len(BIAS_OFFSETS), N_HEADS, NK, DQ), F32),
    )(rel_bias)


def _split_heads_into(qm_ref, q):
    lane = lax.broadcasted_iota(jnp.int32, (QB, 128), 1)
    for p in range(N_PAIR):
        qp = q[:, p * 128:(p + 1) * 128].astype(F32)
        qm_ref[p, :QB, :] = jnp.where(lane < HEAD_DIM, qp, 0.0).astype(BF16)
        qm_ref[p, QB:, :] = jnp.where(lane >= HEAD_DIM, qp, 0.0).astype(BF16)


def _merge_pair(o_even, o_odd):
    lane = lax.broadcasted_iota(jnp.int32, (QB, 128), 1)
    return jnp.where(lane < HEAD_DIM, o_even, o_odd)


CNT_BLOCKS = 4
KEY_ROWS = 32
assert NK == 8 * 32


def _bit_transpose32(load_row, tmp_ref, store_row):
    def swap(a, b, j, m):
        t = (a ^ lax.shift_right_logical(b, jnp.int32(j))) & m
        return a ^ t, b ^ (t << j)

    lower = []
    for k in range(16):
        a, b = swap(load_row(k), load_row(k + 16), 16, 0x0000FFFF)
        lower.append(a)
        tmp_ref[k] = b
    for base in (0, 16):
        x = lower if base == 0 else [tmp_ref[k] for k in range(16)]
        j, m = 8, 0x00FF00FF
        while j:
            k = 0
            while k < 16:
                x[k], x[k + j] = swap(x[k], x[k + j], j, m)
                k = (k + j + 1) & ~j
            j >>= 1
            m ^= m << j
        for i in range(16):
            store_row(base + i, x[i])


def _keep_lowest_ties(alive_ref, kept_ref, need, n_groups):
    n_blocks = alive_ref.shape[0]
    sub = lax.broadcasted_iota(jnp.int32, (8, DQ), 0)
    zero_masks = []
    for c in reversed(range(max(n_blocks - 1, 1).bit_length())):
        zero_masks.append(
            lambda jb, c=c: jnp.where(((jb >> c) & 1) == 0, jnp.int32(-1), jnp.int32(0)))
    for word in (0xFFFF0000, 0xFF00FF00, 0xF0F0F0F0, 0xCCCCCCCC, 0xAAAAAAAA):
        zero_masks.append(lambda jb, word=word: jnp.int32(word - (1 << 32)))
    for c in (2, 1, 0):
        zero_masks.append(lambda jb, c=c: jnp.where(((sub >> c) & 1) == 0, -1, 0))

    remaining = need
    took_zero = None
    for p, zero_mask in enumerate(zero_masks):
        prev_mask = zero_masks[p - 1] if p else None

        def body(g, cnts, zero_mask=zero_mask, prev_mask=prev_mask, took_zero=took_zero):
            cnts = list(cnts)
            for u in range(CNT_BLOCKS):
                jb = g * CNT_BLOCKS + u
                alive = alive_ref[jb]
                if prev_mask is None:
                    kept_ref[jb] = jnp.zeros((8, DQ), jnp.int32)
                else:
                    lows = alive & prev_mask(jb)
                    kept_ref[jb] = kept_ref[jb] | jnp.where(took_zero, 0, lows)
                    alive = jnp.where(took_zero, lows, alive ^ lows)
                    alive_ref[jb] = alive
                cnts[u] = cnts[u] + lax.population_count(alive & zero_mask(jb))
            return tuple(cnts)

        zeros = jnp.zeros((8, DQ), jnp.int32)
        cnts = lax.fori_loop(0, n_groups, body, (zeros,) * CNT_BLOCKS)
        n_zero = jnp.sum(sum(cnts[1:], cnts[0]), axis=0, keepdims=True)
        took_zero = n_zero >= remaining
        remaining = jnp.where(took_zero, remaining, remaining - n_zero)

    last_mask = zero_masks[-1]

    def finish(jb, carry):
        alive = alive_ref[jb]
        alive_ref[jb] = kept_ref[jb] | jnp.where(took_zero, alive & last_mask(jb), alive)
        return carry

    lax.fori_loop(0, n_groups * CNT_BLOCKS, finish, 0)


def _dsa_kernel(aq_ref, iq_ref, iwt_ref, ak_ref, avt_ref, ik_ref, bias_ref, o_ref,
                keybuf_ref, plane_ref, alive_ref, above_ref, kept_ref, sel_ref, qm_ref, m_ref,
                acc_ref, s0_ref, cm0_ref, s1_ref, cm1_ref, raw0_ref, raw1_ref, tmp_ref, *,
                k_sel):
    step = pl.program_id(1)
    n_q = pl.num_programs(1) - 1
    att = step - 1
    krow = lax.broadcasted_iota(jnp.int32, (KEY_ROWS, DQ), 0)
    qcol = lax.broadcasted_iota(jnp.int32, (KEY_ROWS, DQ), 1)
    qchunk = (step * DQ + qcol) // CHUNK

    iwt = iwt_ref[...] * (IDX_HEADS ** -0.5)

    def dots_stage(jb, slot):
        raw_ref = (raw0_ref, raw1_ref)[slot]
        k0 = pl.multiple_of(jb * NK, NK)
        for h in range(IDX_HEADS):
            ikh = ik_ref[pl.ds(k0, NK), (h % 2) * 128:(h % 2 + 1) * 128]
            iqp = iq_ref[:, (h // 2) * 128:(h // 2 + 1) * 128]
            raw_ref[h] = lax.dot_general(ikh, iqp, _NT, preferred_element_type=F32)

    def keys_stage(jb, slot, last):
        raw_ref = (raw0_ref, raw1_ref)[slot]
        k0 = pl.multiple_of(jb * NK, NK)
        for c in range(NK // KEY_ROWS):
            rows = slice(c * KEY_ROWS, (c + 1) * KEY_ROWS)
            acc = jnp.zeros((KEY_ROWS, DQ), F32)
            for h in range(IDX_HEADS):
                acc = acc + iwt[h:h + 1, :] * jnp.maximum(raw_ref[h, rows, :], 0.0)
            bits = lax.bitcast_convert_type(acc, jnp.int32)
            key = bits ^ ((bits >> 31) & 0x7FFFFFFF)
            key = jnp.where(key == -1, 0, key)
            ukey = key ^ INT_MIN
            if last:
                admissible = ((k0 + c * KEY_ROWS + krow) // CHUNK) <= qchunk
                ukey = jnp.where(admissible, ukey, 0)
            keybuf_ref[slot, rows, :] = ukey
        for half in range(DQ // 128):
            lanes = slice(half * 128, (half + 1) * 128)

            def load_row(r):
                return keybuf_ref[slot, 8 * r:8 * r + 8, lanes]

            def store_plane(i, v):
                plane_ref[31 - i, jb, :, lanes] = v

            _bit_transpose32(load_row, tmp_ref, store_plane)
        plane_ref[32, jb] = jnp.full((8, DQ), -1, jnp.int32)
        alive_ref[jb] = jnp.full((8, DQ), -1, jnp.int32)
        above_ref[jb] = jnp.zeros((8, DQ), jnp.int32)

    ones = jnp.ones((ONES_ROWS, NK), BF16)
    slots = ((s0_ref, cm0_ref), (s1_ref, cm1_ref))

    def logits_stage(jb, slot, bias_idx):
        s_ref, cm_ref = slots[slot]
        k0 = pl.multiple_of(jb * NK, NK)
        sel = sel_ref[jb]
        mask = jnp.concatenate(
            [jnp.where((sel << r) < 0, 0.0, NEG_BIG)
             for r in range(32)], axis=0).astype(BF16)
        for h in range(N_HEADS):
            kp = ak_ref[pl.ds(k0, NK), (h // 2) * 128:(h // 2 + 1) * 128]
            s = lax.dot_general(kp, qm_ref[h], _NT, preferred_element_type=F32)
            if bias_idx is not None:
                s = s + bias_ref[bias_idx, h]
            sb = s.astype(BF16) + mask
            s_ref[h] = sb
            cm_ref[h] = jnp.max(sb, axis=0, keepdims=True).astype(F32)

    def softmax_stage(jb, slot):
        s_ref, cm_ref = slots[slot]
        for h in range(N_HEADS):
            m_prev = m_ref[h]
            m_new = jnp.maximum(m_prev, cm_ref[h])
            alpha = jnp.exp2(m_prev - m_new)
            pe = jnp.exp2(s_ref[h] - m_new.astype(BF16))
            vt = jnp.concatenate([avt_ref[jb, h * HEAD_DIM:(h + 1) * HEAD_DIM, :], ones], axis=0)
            acc_ref[h] = alpha * acc_ref[h] + jnp.dot(vt, pe, preferred_element_type=F32)
            m_ref[h] = m_new

    def start_attention():
        lane = lax.broadcasted_iota(jnp.int32, (DQ, 128), 1)
        for p in range(N_PAIR):
            qp = aq_ref[:, p * 128:(p + 1) * 128].astype(F32)
            qm_ref[2 * p] = jnp.where(lane < HEAD_DIM, qp, 0.0).astype(BF16)
            qm_ref[2 * p + 1] = jnp.where(lane >= HEAD_DIM, qp, 0.0).astype(BF16)
        m_ref[...] = jnp.full(m_ref.shape, NEG_BIG, F32)
        acc_ref[...] = jnp.zeros(acc_ref.shape, F32)
        logits_stage(att, 0, 0)
        logits_stage(jnp.maximum(att - 1, 0), 1, 1)
        softmax_stage(att, 0)

    @pl.when(step == 0)
    def _():
        dots_stage(0, 0)
        keys_stage(0, 0, True)

    @pl.when(step == n_q)
    def _():
        start_attention()
        n_steps = att - 1

        def two_steps(u, carry):
            b = att - 1 - 2 * u
            logits_stage(b - 1, 0, None)
            softmax_stage(b, 1)
            logits_stage(b - 2, 1, None)
            softmax_stage(b - 1, 0)
            return carry

        lax.fori_loop(0, jnp.maximum(n_steps, 0) // 2, two_steps, 0)

        @pl.when(jnp.logical_and(n_steps >= 1, n_steps % 2 == 1))
        def _():
            logits_stage(0, 0, None)
            softmax_stage(1, 1)

        @pl.when(jnp.logical_and(att >= 1, att % 2 == 1))
        def _():
            softmax_stage(0, 1)

        @pl.when(jnp.logical_and(att >= 1, att % 2 == 0))
        def _():
            softmax_stage(0, 0)

    @pl.when(jnp.logical_and(step >= 1, step < n_q))
    def _():
        start_attention()
        dots_stage(0, 0)
        n_fused = jnp.maximum(att - 1, 0) // 2

        def fused_trip(u, carry):
            b = att - 1 - 2 * u
            j = 2 * u
            logits_stage(b - 1, 0, None)
            dots_stage(j + 1, 1)
            softmax_stage(b, 1)
            keys_stage(j, 0, False)
            logits_stage(b - 2, 1, None)
            dots_stage(j + 2, 0)
            softmax_stage(b - 1, 0)
            keys_stage(j + 1, 1, False)
            return carry

        lax.fori_loop(0, n_fused, fused_trip, 0)
        j0 = 2 * n_fused

        @pl.when(att == 0)
        def _():
            dots_stage(1, 1)
            keys_stage(0, 0, False)
            keys_stage(1, 1, True)

        @pl.when(att % 2 == 1)
        def _():
            dots_stage(j0 + 1, 1)
            softmax_stage(0, 1)
            keys_stage(j0, 0, False)
            dots_stage(j0 + 2, 0)
            keys_stage(j0 + 1, 1, False)
            keys_stage(j0 + 2, 0, True)

        @pl.when(jnp.logical_and(att >= 2, att % 2 == 0))
        def _():
            logits_stage(0, 0, None)
            dots_stage(j0 + 1, 1)
            softmax_stage(1, 1)
            keys_stage(j0, 0, False)
            dots_stage(j0 + 2, 0)
            softmax_stage(0, 0)
            keys_stage(j0 + 1, 1, False)
            dots_stage(j0 + 3, 1)
            keys_stage(j0 + 2, 0, False)
            keys_stage(j0 + 3, 1, True)

    @pl.when(step >= 1)
    def _():
        for p in range(N_PAIR):
            halves = []
            for h in (2 * p, 2 * p + 1):
                a = acc_ref[h]
                halves.append(a[:HEAD_DIM, :] / a[HEAD_DIM:HEAD_DIM + 1, :])
            o_ref[:, p * 128:(p + 1) * 128] = jnp.concatenate(halves, axis=0).T.astype(BF16)

    nkb = jnp.where(step < n_q, step + 1, 0)
    n_groups = (nkb + CNT_BLOCKS - 1) // CNT_BLOCKS

    def pad_block(jb, carry):
        for b in range(33):
            plane_ref[b, jb] = jnp.zeros((8, DQ), jnp.int32)
        alive_ref[jb] = jnp.zeros((8, DQ), jnp.int32)
        above_ref[jb] = jnp.zeros((8, DQ), jnp.int32)
        return carry

    lax.fori_loop(nkb, n_groups * CNT_BLOCKS, pad_block, 0)

    def select_pass(it, state):
        took_prev, n_above, thr_u = state
        b = 31 - it
        take_prev = took_prev != 0

        def body(g, cnts):
            cnts = list(cnts)
            for u in range(CNT_BLOCKS):
                jb = g * CNT_BLOCKS + u
                alive = alive_ref[jb]
                with_prev = alive & plane_ref[b + 1, jb]
                above_ref[jb] = above_ref[jb] | jnp.where(take_prev, 0, with_prev)
                alive = jnp.where(take_prev, with_prev, alive ^ with_prev)
                alive_ref[jb] = alive
                cnts[u] = cnts[u] + lax.population_count(alive & plane_ref[b, jb])
            return tuple(cnts)

        zeros = jnp.zeros((8, DQ), jnp.int32)
        cnts = lax.fori_loop(0, n_groups, body, (zeros,) * CNT_BLOCKS)
        n_one = jnp.sum(sum(cnts[1:], cnts[0]), axis=0, keepdims=True)
        take = (n_above + n_one) >= k_sel
        n_above = jnp.where(take, n_above, n_above + n_one)
        thr_u = jnp.where(take, thr_u | jnp.left_shift(jnp.int32(1), b), thr_u)
        return take.astype(jnp.int32), n_above, thr_u

    row0 = jnp.zeros((1, DQ), jnp.int32)
    took_last, n_above, thr_u = lax.fori_loop(0, 32, select_pass, (row0 + 1, row0, row0))

    def settle(g, cnts):
        cnts = list(cnts)
        for u in range(CNT_BLOCKS):
            jb = g * CNT_BLOCKS + u
            alive = alive_ref[jb]
            with_last = alive & plane_ref[0, jb]
            above_ref[jb] = above_ref[jb] | jnp.where(took_last != 0, 0, with_last)
            alive = jnp.where(took_last != 0, with_last, alive ^ with_last)
            alive_ref[jb] = alive
            cnts[u] = cnts[u] + lax.population_count(alive)
        return tuple(cnts)

    zeros = jnp.zeros((8, DQ), jnp.int32)
    cnts = lax.fori_loop(0, n_groups, settle, (zeros,) * CNT_BLOCKS)
    n_tied = jnp.sum(sum(cnts[1:], cnts[0]), axis=0, keepdims=True)
    need = k_sel - n_above
    real = thr_u != 0
    extra = jnp.logical_and(n_tied > need, real)

    @pl.when(jnp.max(extra.astype(jnp.int32)) > 0)
    def _():
        _keep_lowest_ties(alive_ref, kept_ref, need, n_groups)

    def finalize(jb, carry):
        sel_ref[jb] = above_ref[jb] | jnp.where(real, alive_ref[jb], 0)
        return carry

    lax.fori_loop(0, nkb, finalize, 0)


def _dsa(proj3, ik3, avt4, iwt, bias_tiles, k_sel):
    bsz, seq, _ = proj3.shape
    nq = seq // DQ
    assert seq % (NK * CNT_BLOCKS) == 0
    resident = dict(pipeline_mode=pl.Buffered(1))
    return pl.pallas_call(
        functools.partial(_dsa_kernel, k_sel=k_sel),
        name="dsa",
        grid=(bsz, nq + 1),
        in_specs=[
            pl.BlockSpec((None, DQ, 512), lambda b, s: (b, jnp.maximum(s - 1, 0), COL_AQ)),
            pl.BlockSpec((None, DQ, 512), lambda b, s: (b, jnp.minimum(s, nq - 1), COL_IQ)),
            pl.BlockSpec((IDX_HEADS, DQ), lambda b, s: (0, b * nq + jnp.minimum(s, nq - 1))),
            pl.BlockSpec((None, seq, 512), lambda b, i: (b, 0, COL_AK), **resident),
            pl.BlockSpec((None, seq // NK, 512, NK), lambda b, i: (b, 0, 0, 0), **resident),
            pl.BlockSpec((None, seq, N_IK), lambda b, i: (b, 0, 0), **resident),
            pl.BlockSpec(bias_tiles.shape, lambda b, i: (0, 0, 0, 0), **resident),
        ],
        out_specs=pl.BlockSpec((None, DQ, 512), lambda b, s: (b, jnp.maximum(s - 1, 0), 0)),
        out_shape=jax.ShapeDtypeStruct((bsz, seq, 512), BF16),
        scratch_shapes=[
            pltpu.VMEM((2, NK, DQ), jnp.int32),
            pltpu.VMEM((33, seq // NK, 8, DQ), jnp.int32),
            pltpu.VMEM((seq // NK, 8, DQ), jnp.int32),
            pltpu.VMEM((seq // NK, 8, DQ), jnp.int32),
            pltpu.VMEM((seq // NK, 8, DQ), jnp.int32),
            pltpu.VMEM((seq // NK, 8, DQ), jnp.int32),
            pltpu.VMEM((N_HEADS, DQ, 128), BF16),
            pltpu.VMEM((N_HEADS, 1, DQ), F32),
            pltpu.VMEM((N_HEADS, HEAD_DIM + ONES_ROWS, DQ), F32),
            pltpu.VMEM((N_HEADS, NK, DQ), BF16),
            pltpu.VMEM((N_HEADS, 1, DQ), F32),
            pltpu.VMEM((N_HEADS, NK, DQ), BF16),
            pltpu.VMEM((N_HEADS, 1, DQ), F32),
            pltpu.VMEM((IDX_HEADS, NK, DQ), F32),
            pltpu.VMEM((IDX_HEADS, NK, DQ), F32),
            pltpu.VMEM((16, 8, 128), jnp.int32),
        ],
        compiler_params=pltpu.CompilerParams(
            dimension_semantics=("arbitrary", "arbitrary"), vmem_limit_bytes=VMEM_LIMIT),
    )(proj3, proj3, iwt, proj3, avt4, ik3, bias_tiles)


SB_DEAD_MASS = 104.0 * LOG2E


def _sb_kernel(q_ref, k_ref, v_ref, o_ref, qm_ref, uu_ref, carry_ref, acc_ref, z_ref, sp_ref,
               later_ref):
    i = pl.program_id(1)
    diag = (i * QB + QB - 1) // NK

    @pl.when(jnp.logical_and(pl.program_id(0) == 0, i == 0))
    def _():
        kr = lax.broadcasted_iota(jnp.int32, (NK, NK), 0)
        kc = lax.broadcasted_iota(jnp.int32, (NK, NK), 1)
        uu_ref[...] = jnp.where(kr > kc, 1.0, 0.0).astype(BF16)

    _split_heads_into(qm_ref, q_ref[...])
    carry_ref[...] = jnp.zeros(carry_ref.shape, F32)
    acc_ref[...] = jnp.zeros(acc_ref.shape, F32)

    def block(jb, on_diagonal):
        k0 = pl.multiple_of(jb * NK, NK)
        if on_diagonal:
            row = lax.broadcasted_iota(jnp.int32, (2 * QB, NK), 0)
            col = lax.broadcasted_iota(jnp.int32, (2 * QB, NK), 1)
            causal = (k0 + col) < (i * QB + row % QB)
        for p in range(N_PAIR):
            kp = k_ref[pl.ds(k0, NK), p * 128:(p + 1) * 128]
            z_ref[p] = lax.dot_general(qm_ref[p], kp, _NT, preferred_element_type=F32)
        for p in range(N_PAIR):
            z = z_ref[p]
            neg_abs = lax.bitcast_convert_type(
                lax.bitcast_convert_type(z, jnp.int32) | INT_MIN, F32)
            sp = jnp.maximum(z, 0.0) + jnp.log(1.0 + jnp.exp2(neg_abs)) * LOG2E
            if on_diagonal:
                sp = jnp.where(causal, sp, 0.0)
            sp_ref[p] = sp
            later_ref[p] = jnp.dot(sp.astype(BF16), uu_ref[...], preferred_element_type=F32)
        for p in range(N_PAIR):
            vp = v_ref[pl.ds(k0, NK), p * 128:(p + 1) * 128]
            carry = carry_ref[p]
            sp = sp_ref[p]
            a = jnp.exp2(z_ref[p] - sp - later_ref[p] - carry)
            if on_diagonal:
                a = jnp.where(causal, a, 0.0)
            acc_ref[p] += jnp.dot(a.astype(BF16), vp, preferred_element_type=F32)
            carry_ref[p] = carry + jnp.sum(sp, axis=1, keepdims=True)

    block(diag, True)

    def alive():
        return (jnp.min(carry_ref[...]) <= SB_DEAD_MASS).astype(jnp.int32)

    def cond(state):
        jb, go = state
        return jnp.logical_and(jb >= 0, go > 0)

    def body(state):
        jb, _ = state
        block(jb, False)
        return jb - 1, alive()

    lax.while_loop(cond, body, (diag - 1, alive()))

    for p in range(N_PAIR):
        o_ref[:, p * 128:(p + 1) * 128] = _merge_pair(
            acc_ref[p, :QB, :], acc_ref[p, QB:, :]).astype(BF16)


def _stick_breaking(proj3):
    bsz, seq, _ = proj3.shape
    resident = dict(pipeline_mode=pl.Buffered(1))
    return pl.pallas_call(
        _sb_kernel,
        name="stick_breaking",
        grid=(bsz, seq // QB),
        in_specs=[
            pl.BlockSpec((None, QB, 512), lambda b, i: (b, i, COL_BQ)),
            pl.BlockSpec((None, seq, 512), lambda b, i: (b, 0, COL_BK), **resident),
            pl.BlockSpec((None, seq, 512), lambda b, i: (b, 0, COL_BV), **resident),
        ],
        out_specs=pl.BlockSpec((None, QB, 512), lambda b, i: (b, i, 0)),
        out_shape=jax.ShapeDtypeStruct((bsz, seq, 512), BF16),
        scratch_shapes=[
            pltpu.VMEM((N_PAIR, 2 * QB, 128), BF16),
            pltpu.VMEM((NK, NK), BF16),
            pltpu.VMEM((N_PAIR, 2 * QB, 1), F32),
            pltpu.VMEM((N_PAIR, 2 * QB, 128), F32),
            pltpu.VMEM((N_PAIR, 2 * QB, NK), F32),
            pltpu.VMEM((N_PAIR, 2 * QB, NK), F32),
            pltpu.VMEM((N_PAIR, 2 * QB, NK), F32),
        ],
        compiler_params=pltpu.CompilerParams(
            dimension_semantics=("arbitrary", "arbitrary"), vmem_limit_bytes=VMEM_LIMIT),
    )(proj3, proj3, proj3)


MERGE_ROWS = 512


def _merge_kernel(x_ref, ya_ref, yb_ref, cq_ref, g0_ref, g1_ref, g2_ref, mk_ref, mv_ref,
                  wa_ref, wb_ref, wc_ref, wo_ref, gp_ref, o_ref):
    n_rows = x_ref.shape[0]
    for r0 in range(0, n_rows, MERGE_ROWS):
        rows = slice(r0, r0 + MERGE_ROWS)
        heads = []
        for h in range(C_HEADS):
            sl = slice(h * C_HEAD_DIM, (h + 1) * C_HEAD_DIM)
            s = lax.dot_general(cq_ref[rows, sl], mk_ref[:, sl], _NT,
                                preferred_element_type=F32) * (C_HEAD_DIM ** -0.5)
            e = jnp.exp(s - jnp.max(s, axis=1, keepdims=True))
            p = e / jnp.sum(e, axis=1, keepdims=True)
            heads.append(jnp.dot(p.astype(BF16), mv_ref[:, sl], preferred_element_type=F32))
        yc_pre = jnp.concatenate(heads, axis=1).astype(BF16)
        ya = jnp.dot(ya_ref[rows, :], wa_ref[...], preferred_element_type=F32)
        yb = jnp.dot(yb_ref[rows, :], wb_ref[...], preferred_element_type=F32)
        yc = jnp.dot(yc_pre, wc_ref[...], preferred_element_type=F32)
        merged = (g0_ref[rows, :].astype(F32) * ya + g1_ref[rows, :].astype(F32) * yb
                  + g2_ref[rows, :].astype(F32) * yc)
        o = jnp.dot(merged.astype(BF16), wo_ref[...], preferred_element_type=F32)
        o_ref[rows, :] = x_ref[rows, :] + _rms(o, gp_ref[...])


def _merge(x2, ya2, yb2, proj2, mkv3, wa, wb, wc, wo, g_post, seq):
    n = x2.shape[0]
    tm = min(2 * MERGE_ROWS, seq)
    per_batch = seq // tm
    n_mem = mkv3.shape[1]
    c_dim = C_HEADS * C_HEAD_DIM
    const = lambda t: (0, 0)
    return pl.pallas_call(
        _merge_kernel,
        name="merge",
        grid=(n // tm,),
        in_specs=[
            pl.BlockSpec((tm, D_MODEL), lambda t: (t, 0)),
            pl.BlockSpec((tm, 512), lambda t: (t, 0)),
            pl.BlockSpec((tm, 512), lambda t: (t, 0)),
            pl.BlockSpec((tm, 512), lambda t: (t, COL_CQ)),
            pl.BlockSpec((tm, D_MODEL), lambda t: (t, 0)),
            pl.BlockSpec((tm, D_MODEL), lambda t: (t, 1)),
            pl.BlockSpec((tm, D_MODEL), lambda t: (t, 2)),
            pl.BlockSpec((None, n_mem, c_dim), lambda t: (t // per_batch, 0, 0)),
            pl.BlockSpec((None, n_mem, c_dim), lambda t: (t // per_batch, 0, 1)),
            pl.BlockSpec(wa.shape, const),
            pl.BlockSpec(wb.shape, const),
            pl.BlockSpec(wc.shape, const),
            pl.BlockSpec(wo.shape, const),
            pl.BlockSpec((1, D_MODEL), const),
        ],
        out_specs=pl.BlockSpec((tm, D_MODEL), lambda t: (t, 0)),
        out_shape=jax.ShapeDtypeStruct((n, D_MODEL), F32),
        compiler_params=pltpu.CompilerParams(
            dimension_semantics=("arbitrary",), vmem_limit_bytes=VMEM_LIMIT),
    )(x2, ya2, yb2, proj2, proj2, proj2, proj2, mkv3, mkv3, wa, wb, wc, wo, g_post)


FFN_ROWS = 512


def _ffn_kernel(x_ref, gpre_ref, wg_ref, wu_ref, wo_ref, gpost_ref, o_ref, h_ref, acc_ref):
    k = pl.program_id(1)

    @pl.when(k == 0)
    def _():
        h_ref[...] = _rms(x_ref[...], gpre_ref[...]).astype(BF16)
        acc_ref[...] = jnp.zeros(acc_ref.shape, F32)

    for r0 in range(0, h_ref.shape[0], FFN_ROWS):
        rows = slice(r0, min(r0 + FFN_ROWS, h_ref.shape[0]))
        h = h_ref[rows, :]
        g = jnp.dot(h, wg_ref[...], preferred_element_type=F32)
        u = jnp.dot(h, wu_ref[...], preferred_element_type=F32)
        act = (g * jax.nn.sigmoid(g) * u).astype(BF16)
        acc_ref[rows, :] += jnp.dot(act, wo_ref[...], preferred_element_type=F32)

    @pl.when(k == pl.num_programs(1) - 1)
    def _():
        o_ref[...] = x_ref[...] + _rms(acc_ref[...], gpost_ref[...])


def _ffn(x2, g_pre, wg, wu, wo, g_post):
    n = x2.shape[0]
    d_ff = wg.shape[1]
    tm = min(2 * FFN_ROWS, n)
    tf = d_ff // 2
    return pl.pallas_call(
        _ffn_kernel,
        name="ffn",
        grid=(n // tm, d_ff // tf),
        in_specs=[
            pl.BlockSpec((tm, D_MODEL), lambda t, k: (t, 0)),
            pl.BlockSpec((1, D_MODEL), lambda t, k: (0, 0)),
            pl.BlockSpec((D_MODEL, tf), lambda t, k: (0, k)),
            pl.BlockSpec((D_MODEL, tf), lambda t, k: (0, k)),
            pl.BlockSpec((tf, D_MODEL), lambda t, k: (k, 0)),
            pl.BlockSpec((1, D_MODEL), lambda t, k: (0, 0)),
        ],
        out_specs=pl.BlockSpec((tm, D_MODEL), lambda t, k: (t, 0)),
        out_shape=jax.ShapeDtypeStruct((n, D_MODEL), F32),
        scratch_shapes=[pltpu.VMEM((tm, D_MODEL), BF16), pltpu.VMEM((tm, D_MODEL), F32)],
        compiler_params=pltpu.CompilerParams(
            dimension_semantics=("arbitrary", "arbitrary"), vmem_limit_bytes=VMEM_LIMIT),
    )(x2, g_pre, wg, wu, wo, g_post)


def _pack_w_in(w, b_gate):
    sizes = (512, 512, 512, IDX_HEADS * 64, 64, IDX_HEADS, 512, 512, 512, 512,
             N_BRANCH * D_MODEL)
    aq, ak, av, iq, ik, iw, bq, bk, bv, cq, gates = jnp.split(w, np.cumsum(sizes)[:-1], axis=1)
    scale = HEAD_DIM ** -0.5
    scale2 = scale * LOG2E
    w_main = jnp.concatenate(
        [gates, aq * scale2, ak, iq * scale, bq * scale2, bk, bv, cq], axis=1).astype(BF16)
    z64 = jnp.zeros((D_MODEL, 64), F32)
    w_ik = jnp.concatenate([ik, z64, z64, ik], axis=1).astype(BF16)
    w_trans = jnp.concatenate(
        [av, iw, jnp.zeros((D_MODEL, N_TRANS - 512 - IDX_HEADS), F32)], axis=1).T.astype(BF16)
    b_main = jnp.concatenate([b_gate, jnp.zeros((N_MAIN - N_GATE,), F32)])[None, :]
    return w_main, w_ik, w_trans, b_main


def kernel(x, mem, rel_bias, g_mix_pre, w_in, b_gate, g_mem, w_mem_kv, w_up_a, w_up_b, w_up_c,
           w_out, g_mix_post, g_ffn_pre, w_ffn_in, w_ffn_out, g_ffn_post):
    bsz, seq, _ = x.shape
    n_mem = mem.shape[1]
    k_sel = min(TOPK_MAX, seq // 4)
    bias_tiles = _bias_tiles(rel_bias)
    x2 = x.reshape(bsz * seq, D_MODEL)
    for l in range(w_in.shape[0]):
        w_main, w_ik, w_trans, b_main = _pack_w_in(w_in[l], b_gate[l])
        proj2, ik2, avt, iwt = _project(x2, g_mix_pre[l][None, :], w_main, b_main, w_ik, w_trans)
        proj3 = proj2.reshape(bsz, seq, N_MAIN)
        mkv = _memkv(mem.reshape(bsz * n_mem, D_MODEL), g_mem[l][None, :],
                     w_mem_kv[l].astype(BF16))
        ya = _dsa(proj3, ik2.reshape(bsz, seq, N_IK), avt.reshape(bsz, seq // NK, 512, NK),
                  iwt, bias_tiles, k_sel)
        yb = _stick_breaking(proj3)
        x2 = _merge(x2, ya.reshape(bsz * seq, 512), yb.reshape(bsz * seq, 512), proj2,
                    mkv.reshape(bsz, n_mem, 2 * C_HEADS * C_HEAD_DIM),
                    w_up_a[l].astype(BF16), w_up_b[l].astype(BF16), w_up_c[l].astype(BF16),
                    w_out[l].astype(BF16), g_mix_post[l][None, :], seq)
        d_ff = w_ffn_out.shape[1]
        w_ffn = w_ffn_in[l].astype(BF16)
        x2 = _ffn(x2, g_ffn_pre[l][None, :], w_ffn[:, :d_ff], w_ffn[:, d_ff:],
                  w_ffn_out[l].astype(BF16), g_ffn_post[l][None, :])
    return x2.reshape(bsz, seq, D_MODEL)
```

```python
import functools

import numpy as np
import jax
import jax.numpy as jnp
from jax import lax
from jax.experimental import pallas as pl
from jax.experimental.pallas import tpu as pltpu

D_MODEL = 1024
CHUNK = 64
HEAD_DIM = 64
N_HEADS = 8
IDX_HEADS = 8
TOPK_MAX = 256
C_HEADS = 4
C_HEAD_DIM = 128
N_BRANCH = 3
REL_BUCKETS = 32
EPS = 1e-6

F32 = jnp.float32
BF16 = jnp.bfloat16
INT_MIN = -2 ** 31
NEG_BIG = -1e30
LOG2E = 1.4426950408889634

QB = 256
DQ = 256
NK = 256
N_PAIR = N_HEADS // 2
ONES_ROWS = 16

N_GATE = N_BRANCH * D_MODEL
COL_AQ, COL_AK, COL_IQ, COL_BQ, COL_BK, COL_BV, COL_CQ = range(N_GATE // 512, N_GATE // 512 + 7)
N_MAIN = N_GATE + 7 * 512
N_IK = 256
N_TRANS = 512 + 16

VMEM_LIMIT = 56 * 1024 * 1024

_NT = (((1,), (1,)), ((), ()))


def _rms(x, g):
    return x * lax.rsqrt(jnp.mean(x * x, axis=-1, keepdims=True) + EPS) * g


PROJ_ROWS = 1024


def _proj_kernel(x_ref, g_ref, w_ref, b_ref, wik_ref, wt_ref, o_ref, ik_ref, avt_ref, iwt_ref,
                 h_ref, *, n_gate_tiles):
    j = pl.program_id(1)

    @pl.when(j == 0)
    def _():
        for r0 in range(0, h_ref.shape[0], PROJ_ROWS):
            rows = slice(r0, min(r0 + PROJ_ROWS, h_ref.shape[0]))
            hb = _rms(x_ref[rows, :], g_ref[...]).astype(BF16)
            h_ref[rows, :] = hb
            ik_ref[rows, :] = jnp.dot(hb, wik_ref[...], preferred_element_type=F32).astype(BF16)
            tr = lax.dot_general(wt_ref[...], hb, _NT, preferred_element_type=F32)
            for c in range(hb.shape[0] // NK):
                avt_ref[r0 // NK + c] = tr[:512, c * NK:(c + 1) * NK].astype(BF16)
            iwt_ref[:, rows] = tr[512:512 + IDX_HEADS, :]

    def tile(finish):
        for r0 in range(0, h_ref.shape[0], PROJ_ROWS):
            rows = slice(r0, min(r0 + PROJ_ROWS, h_ref.shape[0]))
            acc = jnp.dot(h_ref[rows, :], w_ref[...], preferred_element_type=F32)
            o_ref[rows, :] = finish(acc).astype(BF16)

    @pl.when(j >= n_gate_tiles)
    def _():
        tile(lambda acc: acc)

    @pl.when(j < n_gate_tiles)
    def _():
        tile(lambda acc: 0.5 + 0.5 * jnp.tanh(0.5 * (acc + b_ref[...])))


def _project(x2, g, w_main, b_main, w_ik, w_trans):
    n = x2.shape[0]
    tm = min(2048, n)
    tn = 512
    grid = (n // tm, N_MAIN // tn)
    return pl.pallas_call(
        functools.partial(_proj_kernel, n_gate_tiles=N_GATE // tn),
        name="in_proj",
        grid=grid,
        in_specs=[
            pl.BlockSpec((tm, D_MODEL), lambda i, j: (i, 0)),
            pl.BlockSpec((1, D_MODEL), lambda i, j: (0, 0)),
            pl.BlockSpec((D_MODEL, tn), lambda i, j: (0, j)),
            pl.BlockSpec((1, tn), lambda i, j: (0, j)),
            pl.BlockSpec((D_MODEL, N_IK), lambda i, j: (0, 0)),
            pl.BlockSpec((N_TRANS, D_MODEL), lambda i, j: (0, 0)),
        ],
        out_specs=[
            pl.BlockSpec((tm, tn), lambda i, j: (i, j)),
            pl.BlockSpec((tm, N_IK), lambda i, j: (i, 0)),
            pl.BlockSpec((tm // NK, 512, NK), lambda i, j: (i, 0, 0)),
            pl.BlockSpec((IDX_HEADS, tm), lambda i, j: (0, i)),
        ],
        out_shape=[
            jax.ShapeDtypeStruct((n, N_MAIN), BF16),
            jax.ShapeDtypeStruct((n, N_IK), BF16),
            jax.ShapeDtypeStruct((n // NK, 512, NK), BF16),
            jax.ShapeDtypeStruct((IDX_HEADS, n), F32),
        ],
        scratch_shapes=[pltpu.VMEM((tm, D_MODEL), BF16)],
        compiler_params=pltpu.CompilerParams(
            dimension_semantics=("arbitrary", "arbitrary"), vmem_limit_bytes=VMEM_LIMIT),
    )(x2, g, w_main, b_main, w_ik, w_trans)


def _memkv_kernel(x_ref, g_ref, w_ref, o_ref):
    hb = _rms(x_ref[...], g_ref[...]).astype(BF16)
    o_ref[...] = jnp.dot(hb, w_ref[...], preferred_element_type=F32).astype(BF16)


def _memkv(mem2, g, w):
    n = mem2.shape[0]
    tm = min(512, n)
    return pl.pallas_call(
        _memkv_kernel,
        name="mem_kv",
        grid=(n // tm,),
        in_specs=[
            pl.BlockSpec((tm, D_MODEL), lambda i: (i, 0)),
            pl.BlockSpec((1, D_MODEL), lambda i: (0, 0)),
            pl.BlockSpec((D_MODEL, w.shape[1]), lambda i: (0, 0)),
        ],
        out_specs=pl.BlockSpec((tm, w.shape[1]), lambda i: (i, 0)),
        out_shape=jax.ShapeDtypeStruct((n, w.shape[1]), BF16),
        compiler_params=pltpu.CompilerParams(
            dimension_semantics=("arbitrary",), vmem_limit_bytes=VMEM_LIMIT),
    )(mem2, g, w)


BIAS_OFFSETS = (0, -NK)
_LOG_BUCKET_STARTS = (12, 16, 23, 32, 46, 64, 91)
FAR_BUCKET = 15
assert DQ == NK


def _bias_kernel(rb_ref, o_ref):
    key = lax.broadcasted_iota(jnp.int32, (NK, DQ), 0)
    qry = lax.broadcasted_iota(jnp.int32, (NK, DQ), 1)
    for c, off in enumerate(BIAS_OFFSETS):
        rel = key - qry + off
        n = jnp.abs(rel)
        large = jnp.full((NK, DQ), 8, jnp.int32)
        for start in _LOG_BUCKET_STARTS:
            large = large + jnp.where(n >= start, 1, 0)
        bucket = jnp.where(rel > 0, REL_BUCKETS // 2, 0) + jnp.where(n < 8, n, large)
        for h in range(N_HEADS):
            val = jnp.full((NK, DQ), rb_ref[0, h], F32)
            for b in range(1, REL_BUCKETS):
                val = jnp.where(bucket == b, rb_ref[b, h], val)
            o_ref[c, h] = (val - rb_ref[FAR_BUCKET, h]) * LOG2E


def _bias_tiles(rel_bias):
    return pl.pallas_call(
        _bias_kernel,
        name="rel_bias_tiles",
        in_specs=[pl.BlockSpec(memory_space=pltpu.SMEM)],
        out_specs=pl.BlockSpec(memory_space=pltpu.VMEM),
        out_shape=jax.ShapeDtypeStruct((len(BIAS_OFFSETS), N_HEADS, NK, DQ), F32),
    )(rel_bias)


def _split_heads_into(qm_ref, q):
    lane = lax.broadcasted_iota(jnp.int32, (QB, 128), 1)
    for p in range(N_PAIR):
        qp = q[:, p * 128:(p + 1) * 128].astype(F32)
        qm_ref[p, :QB, :] = jnp.where(lane < HEAD_DIM, qp, 0.0).astype(BF16)
        qm_ref[p, QB:, :] = jnp.where(lane >= HEAD_DIM, qp, 0.0).astype(BF16)


def _merge_pair(o_even, o_odd):
    lane = lax.broadcasted_iota(jnp.int32, (QB, 128), 1)
    return jnp.where(lane < HEAD_DIM, o_even, o_odd)


CNT_BLOCKS = 4
KEY_ROWS = 32
assert NK == 8 * 32


def _bit_transpose32(load_row, tmp_ref, store_row):
    def swap(a, b, j, m):
        t = (a ^ lax.shift_right_logical(b, jnp.int32(j))) & m
        return a ^ t, b ^ (t << j)

    lower = []
    for k in range(16):
        a, b = swap(load_row(k), load_row(k + 16), 16, 0x0000FFFF)
        lower.append(a)
        tmp_ref[k] = b
    for base in (0, 16):
        x = lower if base == 0 else [tmp_ref[k] for k in range(16)]
        j, m = 8, 0x00FF00FF
        while j:
            k = 0
            while k < 16:
                x[k], x[k + j] = swap(x[k], x[k + j], j, m)
                k = (k + j + 1) & ~j
            j >>= 1
            m ^= m << j
        for i in range(16):
            store_row(base + i, x[i])


def _keep_lowest_ties(alive_ref, kept_ref, need, n_groups):
    n_blocks = alive_ref.shape[0]
    sub = lax.broadcasted_iota(jnp.int32, (8, DQ), 0)
    zero_masks = []
    for c in reversed(range(max(n_blocks - 1, 1).bit_length())):
        zero_masks.append(
            lambda jb, c=c: jnp.where(((jb >> c) & 1) == 0, jnp.int32(-1), jnp.int32(0)))
    for word in (0xFFFF0000, 0xFF00FF00, 0xF0F0F0F0, 0xCCCCCCCC, 0xAAAAAAAA):
        zero_masks.append(lambda jb, word=word: jnp.int32(word - (1 << 32)))
    for c in (2, 1, 0):
        zero_masks.append(lambda jb, c=c: jnp.where(((sub >> c) & 1) == 0, -1, 0))

    remaining = need
    took_zero = None
    for p, zero_mask in enumerate(zero_masks):
        prev_mask = zero_masks[p - 1] if p else None

        def body(g, cnts, zero_mask=zero_mask, prev_mask=prev_mask, took_zero=took_zero):
            cnts = list(cnts)
            for u in range(CNT_BLOCKS):
                jb = g * CNT_BLOCKS + u
                alive = alive_ref[jb]
                if prev_mask is None:
                    kept_ref[jb] = jnp.zeros((8, DQ), jnp.int32)
                else:
                    lows = alive & prev_mask(jb)
                    kept_ref[jb] = kept_ref[jb] | jnp.where(took_zero, 0, lows)
                    alive = jnp.where(took_zero, lows, alive ^ lows)
                    alive_ref[jb] = alive
                cnts[u] = cnts[u] + lax.population_count(alive & zero_mask(jb))
            return tuple(cnts)

        zeros = jnp.zeros((8, DQ), jnp.int32)
        cnts = lax.fori_loop(0, n_groups, body, (zeros,) * CNT_BLOCKS)
        n_zero = jnp.sum(sum(cnts[1:], cnts[0]), axis=0, keepdims=True)
        took_zero = n_zero >= remaining
        remaining = jnp.where(took_zero, remaining, remaining - n_zero)

    last_mask = zero_masks[-1]

    def finish(jb, carry):
        alive = alive_ref[jb]
        alive_ref[jb] = kept_ref[jb] | jnp.where(took_zero, alive & last_mask(jb), alive)
        return carry

    lax.fori_loop(0, n_groups * CNT_BLOCKS, finish, 0)


def _dsa_kernel(aq_ref, iq_ref, iwt_ref, ak_ref, avt_ref, ik_ref, bias_ref, o_ref,
                keybuf_ref, plane_ref, alive_ref, above_ref, kept_ref, sel_ref, qm_ref, m_ref,
                acc_ref, s0_ref, cm0_ref, s1_ref, cm1_ref, raw0_ref, raw1_ref, tmp_ref, *,
                k_sel):
    step = pl.program_id(1)
    n_q = pl.num_programs(1) - 1
    att = step - 1
    krow = lax.broadcasted_iota(jnp.int32, (KEY_ROWS, DQ), 0)
    qcol = lax.broadcasted_iota(jnp.int32, (KEY_ROWS, DQ), 1)
    qchunk = (step * DQ + qcol) // CHUNK

    iwt = iwt_ref[...] * (IDX_HEADS ** -0.5)

    def dots_stage(jb, slot):
        raw_ref = (raw0_ref, raw1_ref)[slot]
        k0 = pl.multiple_of(jb * NK, NK)
        for h in range(IDX_HEADS):
            ikh = ik_ref[pl.ds(k0, NK), (h % 2) * 128:(h % 2 + 1) * 128]
            iqp = iq_ref[:, (h // 2) * 128:(h // 2 + 1) * 128]
            raw_ref[h] = lax.dot_general(ikh, iqp, _NT, preferred_element_type=F32)

    def keys_stage(jb, slot, last):
        raw_ref = (raw0_ref, raw1_ref)[slot]
        k0 = pl.multiple_of(jb * NK, NK)
        for c in range(NK // KEY_ROWS):
            rows = slice(c * KEY_ROWS, (c + 1) * KEY_ROWS)
            acc = jnp.zeros((KEY_ROWS, DQ), F32)
            for h in range(IDX_HEADS):
                acc = acc + iwt[h:h + 1, :] * jnp.maximum(raw_ref[h, rows, :], 0.0)
            bits = lax.bitcast_convert_type(acc, jnp.int32)
            key = bits ^ ((bits >> 31) & 0x7FFFFFFF)
            key = jnp.where(key == -1, 0, key)
            ukey = key ^ INT_MIN
            if last:
                admissible = ((k0 + c * KEY_ROWS + krow) // CHUNK) <= qchunk
                ukey = jnp.where(admissible, ukey, 0)
            keybuf_ref[slot, rows, :] = ukey
        for half in range(DQ // 128):
            lanes = slice(half * 128, (half + 1) * 128)

            def load_row(r):
                return keybuf_ref[slot, 8 * r:8 * r + 8, lanes]

            def store_plane(i, v):
                plane_ref[31 - i, jb, :, lanes] = v

            _bit_transpose32(load_row, tmp_ref, store_plane)
        plane_ref[32, jb] = jnp.full((8, DQ), -1, jnp.int32)
        alive_ref[jb] = jnp.full((8, DQ), -1, jnp.int32)
        above_ref[jb] = jnp.zeros((8, DQ), jnp.int32)

    ones = jnp.ones((ONES_ROWS, NK), BF16)
    slots = ((s0_ref, cm0_ref), (s1_ref, cm1_ref))

    def logits_stage(jb, slot, bias_idx):
        s_ref, cm_ref = slots[slot]
        k0 = pl.multiple_of(jb * NK, NK)
        sel = sel_ref[jb]
        mask = jnp.concatenate(
            [jnp.where((sel << r) < 0, 0.0, NEG_BIG)
             for r in range(32)], axis=0).astype(BF16)
        for h in range(N_HEADS):
            kp = ak_ref[pl.ds(k0, NK), (h // 2) * 128:(h // 2 + 1) * 128]
            s = lax.dot_general(kp, qm_ref[h], _NT, preferred_element_type=F32)
            if bias_idx is not None:
                s = s + bias_ref[bias_idx, h]
            sb = s.astype(BF16) + mask
            s_ref[h] = sb
            cm_ref[h] = jnp.max(sb, axis=0, keepdims=True).astype(F32)

    def softmax_stage(jb, slot):
        s_ref, cm_ref = slots[slot]
        for h in range(N_HEADS):
            m_prev = m_ref[h]
            m_new = jnp.maximum(m_prev, cm_ref[h])
            alpha = jnp.exp2(m_prev - m_new)
            pe = jnp.exp2(s_ref[h] - m_new.astype(BF16))
            vt = jnp.concatenate([avt_ref[jb, h * HEAD_DIM:(h + 1) * HEAD_DIM, :], ones], axis=0)
            acc_ref[h] = alpha * acc_ref[h] + jnp.dot(vt, pe, preferred_element_type=F32)
            m_ref[h] = m_new

    def start_attention():
        lane = lax.broadcasted_iota(jnp.int32, (DQ, 128), 1)
        for p in range(N_PAIR):
            qp = aq_ref[:, p * 128:(p + 1) * 128].astype(F32)
            qm_ref[2 * p] = jnp.where(lane < HEAD_DIM, qp, 0.0).astype(BF16)
            qm_ref[2 * p + 1] = jnp.where(lane >= HEAD_DIM, qp, 0.0).astype(BF16)
        m_ref[...] = jnp.full(m_ref.shape, NEG_BIG, F32)
        acc_ref[...] = jnp.zeros(acc_ref.shape, F32)
        logits_stage(att, 0, 0)
        logits_stage(jnp.maximum(att - 1, 0), 1, 1)
        softmax_stage(att, 0)

    @pl.when(step == 0)
    def _():
        dots_stage(0, 0)
        keys_stage(0, 0, True)

    @pl.when(step == n_q)
    def _():
        start_attention()
        n_steps = att - 1

        def two_steps(u, carry):
            b = att - 1 - 2 * u
            logits_stage(b - 1, 0, None)
            softmax_stage(b, 1)
            logits_stage(b - 2, 1, None)
            softmax_stage(b - 1, 0)
            return carry

        lax.fori_loop(0, jnp.maximum(n_steps, 0) // 2, two_steps, 0)

        @pl.when(jnp.logical_and(n_steps >= 1, n_steps % 2 == 1))
        def _():
            logits_stage(0, 0, None)
            softmax_stage(1, 1)

        @pl.when(jnp.logical_and(att >= 1, att % 2 == 1))
        def _():
            softmax_stage(0, 1)

        @pl.when(jnp.logical_and(att >= 1, att % 2 == 0))
        def _():
            softmax_stage(0, 0)

    @pl.when(jnp.logical_and(step >= 1, step < n_q))
    def _():
        start_attention()
        dots_stage(0, 0)
        dots_stage(1, 1)
        keys_stage(0, 0, False)
        n_fused = jnp.maximum(att - 1, 0) // 2

        def fused_trip(u, carry):
            b = att - 1 - 2 * u
            j = 2 * u + 1
            logits_stage(b - 1, 0, None)
            dots_stage(j + 1, 0)
            softmax_stage(b, 1)
            keys_stage(j, 1, False)
            logits_stage(b - 2, 1, None)
            dots_stage(j + 2, 1)
            softmax_stage(b - 1, 0)
            keys_stage(j + 1, 0, False)
            return carry

        lax.fori_loop(0, n_fused, fused_trip, 0)

        @pl.when(att == 0)
        def _():
            keys_stage(1, 1, True)

        @pl.when(att % 2 == 1)
        def _():
            dots_stage(att + 1, 0)
            softmax_stage(0, 1)
            keys_stage(att, 1, False)
            keys_stage(att + 1, 0, True)

        @pl.when(jnp.logical_and(att >= 2, att % 2 == 0))
        def _():
            logits_stage(0, 0, None)
            dots_stage(att, 0)
            softmax_stage(1, 1)
            keys_stage(att - 1, 1, False)
            dots_stage(att + 1, 1)
            softmax_stage(0, 0)
            keys_stage(att, 0, False)
            keys_stage(att + 1, 1, True)

    @pl.when(step >= 1)
    def _():
        for p in range(N_PAIR):
            halves = []
            for h in (2 * p, 2 * p + 1):
                a = acc_ref[h]
                halves.append(a[:HEAD_DIM, :] / a[HEAD_DIM:HEAD_DIM + 1, :])
            o_ref[:, p * 128:(p + 1) * 128] = jnp.concatenate(halves, axis=0).T.astype(BF16)

    nkb = jnp.where(step < n_q, step + 1, 0)
    n_groups = (nkb + CNT_BLOCKS - 1) // CNT_BLOCKS

    def pad_block(jb, carry):
        for b in range(33):
            plane_ref[b, jb] = jnp.zeros((8, DQ), jnp.int32)
        alive_ref[jb] = jnp.zeros((8, DQ), jnp.int32)
        above_ref[jb] = jnp.zeros((8, DQ), jnp.int32)
        return carry

    lax.fori_loop(nkb, n_groups * CNT_BLOCKS, pad_block, 0)

    def select_pass(it, state):
        took_prev, n_above, thr_u = state
        b = 31 - it
        take_prev = took_prev != 0

        def body(g, cnts):
            cnts = list(cnts)
            for u in range(CNT_BLOCKS):
                jb = g * CNT_BLOCKS + u
                alive = alive_ref[jb]
                with_prev = alive & plane_ref[b + 1, jb]
                above_ref[jb] = above_ref[jb] | jnp.where(take_prev, 0, with_prev)
                alive = jnp.where(take_prev, with_prev, alive ^ with_prev)
                alive_ref[jb] = alive
                cnts[u] = cnts[u] + lax.population_count(alive & plane_ref[b, jb])
            return tuple(cnts)

        zeros = jnp.zeros((8, DQ), jnp.int32)
        cnts = lax.fori_loop(0, n_groups, body, (zeros,) * CNT_BLOCKS)
        n_one = jnp.sum(sum(cnts[1:], cnts[0]), axis=0, keepdims=True)
        take = (n_above + n_one) >= k_sel
        n_above = jnp.where(take, n_above, n_above + n_one)
        thr_u = jnp.where(take, thr_u | jnp.left_shift(jnp.int32(1), b), thr_u)
        return take.astype(jnp.int32), n_above, thr_u

    row0 = jnp.zeros((1, DQ), jnp.int32)
    took_last, n_above, thr_u = lax.fori_loop(0, 32, select_pass, (row0 + 1, row0, row0))

    def settle(g, cnts):
        cnts = list(cnts)
        for u in range(CNT_BLOCKS):
            jb = g * CNT_BLOCKS + u
            alive = alive_ref[jb]
            with_last = alive & plane_ref[0, jb]
            above_ref[jb] = above_ref[jb] | jnp.where(took_last != 0, 0, with_last)
            alive = jnp.where(took_last != 0, with_last, alive ^ with_last)
            alive_ref[jb] = alive
            cnts[u] = cnts[u] + lax.population_count(alive)
        return tuple(cnts)

    zeros = jnp.zeros((8, DQ), jnp.int32)
    cnts = lax.fori_loop(0, n_groups, settle, (zeros,) * CNT_BLOCKS)
    n_tied = jnp.sum(sum(cnts[1:], cnts[0]), axis=0, keepdims=True)
    need = k_sel - n_above
    real = thr_u != 0
    extra = jnp.logical_and(n_tied > need, real)

    @pl.when(jnp.max(extra.astype(jnp.int32)) > 0)
    def _():
        _keep_lowest_ties(alive_ref, kept_ref, need, n_groups)

    def finalize(jb, carry):
        sel_ref[jb] = above_ref[jb] | jnp.where(real, alive_ref[jb], 0)
        return carry

    lax.fori_loop(0, nkb, finalize, 0)


def _dsa(proj3, ik3, avt4, iwt, bias_tiles, k_sel):
    bsz, seq, _ = proj3.shape
    nq = seq // DQ
    assert seq % (NK * CNT_BLOCKS) == 0
    resident = dict(pipeline_mode=pl.Buffered(1))
    return pl.pallas_call(
        functools.partial(_dsa_kernel, k_sel=k_sel),
        name="dsa",
        grid=(bsz, nq + 1),
        in_specs=[
            pl.BlockSpec((None, DQ, 512), lambda b, s: (b, jnp.maximum(s - 1, 0), COL_AQ)),
            pl.BlockSpec((None, DQ, 512), lambda b, s: (b, jnp.minimum(s, nq - 1), COL_IQ)),
            pl.BlockSpec((IDX_HEADS, DQ), lambda b, s: (0, b * nq + jnp.minimum(s, nq - 1))),
            pl.BlockSpec((None, seq, 512), lambda b, i: (b, 0, COL_AK), **resident),
            pl.BlockSpec((None, seq // NK, 512, NK), lambda b, i: (b, 0, 0, 0), **resident),
            pl.BlockSpec((None, seq, N_IK), lambda b, i: (b, 0, 0), **resident),
            pl.BlockSpec(bias_tiles.shape, lambda b, i: (0, 0, 0, 0), **resident),
        ],
        out_specs=pl.BlockSpec((None, DQ, 512), lambda b, s: (b, jnp.maximum(s - 1, 0), 0)),
        out_shape=jax.ShapeDtypeStruct((bsz, seq, 512), BF16),
        scratch_shapes=[
            pltpu.VMEM((2, NK, DQ), jnp.int32),
            pltpu.VMEM((33, seq // NK, 8, DQ), jnp.int32),
            pltpu.VMEM((seq // NK, 8, DQ), jnp.int32),
            pltpu.VMEM((seq // NK, 8, DQ), jnp.int32),
            pltpu.VMEM((seq // NK, 8, DQ), jnp.int32),
            pltpu.VMEM((seq // NK, 8, DQ), jnp.int32),
            pltpu.VMEM((N_HEADS, DQ, 128), BF16),
            pltpu.VMEM((N_HEADS, 1, DQ), F32),
            pltpu.VMEM((N_HEADS, HEAD_DIM + ONES_ROWS, DQ), F32),
            pltpu.VMEM((N_HEADS, NK, DQ), BF16),
            pltpu.VMEM((N_HEADS, 1, DQ), F32),
            pltpu.VMEM((N_HEADS, NK, DQ), BF16),
            pltpu.VMEM((N_HEADS, 1, DQ), F32),
            pltpu.VMEM((IDX_HEADS, NK, DQ), F32),
            pltpu.VMEM((IDX_HEADS, NK, DQ), F32),
            pltpu.VMEM((16, 8, 128), jnp.int32),
        ],
        compiler_params=pltpu.CompilerParams(
            dimension_semantics=("arbitrary", "arbitrary"), vmem_limit_bytes=VMEM_LIMIT),
    )(proj3, proj3, iwt, proj3, avt4, ik3, bias_tiles)


SB_DEAD_MASS = 104.0 * LOG2E


def _sb_kernel(q_ref, k_ref, v_ref, o_ref, qm_ref, uu_ref, carry_ref, acc_ref, z_ref, sp_ref,
               later_ref):
    i = pl.program_id(1)
    diag = (i * QB + QB - 1) // NK

    @pl.when(jnp.logical_and(pl.program_id(0) == 0, i == 0))
    def _():
        kr = lax.broadcasted_iota(jnp.int32, (NK, NK), 0)
        kc = lax.broadcasted_iota(jnp.int32, (NK, NK), 1)
        uu_ref[...] = jnp.where(kr > kc, 1.0, 0.0).astype(BF16)

    _split_heads_into(qm_ref, q_ref[...])
    carry_ref[...] = jnp.zeros(carry_ref.shape, F32)
    acc_ref[...] = jnp.zeros(acc_ref.shape, F32)

    def block(jb, on_diagonal):
        k0 = pl.multiple_of(jb * NK, NK)
        if on_diagonal:
            row = lax.broadcasted_iota(jnp.int32, (2 * QB, NK), 0)
            col = lax.broadcasted_iota(jnp.int32, (2 * QB, NK), 1)
            causal = (k0 + col) < (i * QB + row % QB)
        for p in range(N_PAIR):
            kp = k_ref[pl.ds(k0, NK), p * 128:(p + 1) * 128]
            z_ref[p] = lax.dot_general(qm_ref[p], kp, _NT, preferred_element_type=F32)
        for p in range(N_PAIR):
            z = z_ref[p]
            neg_abs = lax.bitcast_convert_type(
                lax.bitcast_convert_type(z, jnp.int32) | INT_MIN, F32)
            sp = jnp.maximum(z, 0.0) + jnp.log(1.0 + jnp.exp2(neg_abs)) * LOG2E
            if on_diagonal:
                sp = jnp.where(causal, sp, 0.0)
            sp_ref[p] = sp
            later_ref[p] = jnp.dot(sp.astype(BF16), uu_ref[...], preferred_element_type=F32)
        for p in range(N_PAIR):
            vp = v_ref[pl.ds(k0, NK), p * 128:(p + 1) * 128]
            carry = carry_ref[p]
            sp = sp_ref[p]
            a = jnp.exp2(z_ref[p] - sp - later_ref[p] - carry)
            if on_diagonal:
                a = jnp.where(causal, a, 0.0)
            acc_ref[p] += jnp.dot(a.astype(BF16), vp, preferred_element_type=F32)
            carry_ref[p] = carry + jnp.sum(sp, axis=1, keepdims=True)

    block(diag, True)

    def alive():
        return (jnp.min(carry_ref[...]) <= SB_DEAD_MASS).astype(jnp.int32)

    def cond(state):
        jb, go = state
        return jnp.logical_and(jb >= 0, go > 0)

    def body(state):
        jb, _ = state
        block(jb, False)
        return jb - 1, alive()

    lax.while_loop(cond, body, (diag - 1, alive()))

    for p in range(N_PAIR):
        o_ref[:, p * 128:(p + 1) * 128] = _merge_pair(
            acc_ref[p, :QB, :], acc_ref[p, QB:, :]).astype(BF16)


def _stick_breaking(proj3):
    bsz, seq, _ = proj3.shape
    resident = dict(pipeline_mode=pl.Buffered(1))
    return pl.pallas_call(
        _sb_kernel,
        name="stick_breaking",
        grid=(bsz, seq // QB),
        in_specs=[
            pl.BlockSpec((None, QB, 512), lambda b, i: (b, i, COL_BQ)),
            pl.BlockSpec((None, seq, 512), lambda b, i: (b, 0, COL_BK), **resident),
            pl.BlockSpec((None, seq, 512), lambda b, i: (b, 0, COL_BV), **resident),
        ],
        out_specs=pl.BlockSpec((None, QB, 512), lambda b, i: (b, i, 0)),
        out_shape=jax.ShapeDtypeStruct((bsz, seq, 512), BF16),
        scratch_shapes=[
            pltpu.VMEM((N_PAIR, 2 * QB, 128), BF16),
            pltpu.VMEM((NK, NK), BF16),
            pltpu.VMEM((N_PAIR, 2 * QB, 1), F32),
            pltpu.VMEM((N_PAIR, 2 * QB, 128), F32),
            pltpu.VMEM((N_PAIR, 2 * QB, NK), F32),
            pltpu.VMEM((N_PAIR, 2 * QB, NK), F32),
            pltpu.VMEM((N_PAIR, 2 * QB, NK), F32),
        ],
        compiler_params=pltpu.CompilerParams(
            dimension_semantics=("arbitrary", "arbitrary"), vmem_limit_bytes=VMEM_LIMIT),
    )(proj3, proj3, proj3)


MERGE_ROWS = 512


def _merge_kernel(x_ref, ya_ref, yb_ref, cq_ref, g0_ref, g1_ref, g2_ref, mk_ref, mv_ref,
                  wa_ref, wb_ref, wc_ref, wo_ref, gp_ref, o_ref):
    n_rows = x_ref.shape[0]
    for r0 in range(0, n_rows, MERGE_ROWS):
        rows = slice(r0, r0 + MERGE_ROWS)
        heads = []
        for h in range(C_HEADS):
            sl = slice(h * C_HEAD_DIM, (h + 1) * C_HEAD_DIM)
            s = lax.dot_general(cq_ref[rows, sl], mk_ref[:, sl], _NT,
                                preferred_element_type=F32) * (C_HEAD_DIM ** -0.5)
            e = jnp.exp(s - jnp.max(s, axis=1, keepdims=True))
            p = e / jnp.sum(e, axis=1, keepdims=True)
            heads.append(jnp.dot(p.astype(BF16), mv_ref[:, sl], preferred_element_type=F32))
        yc_pre = jnp.concatenate(heads, axis=1).astype(BF16)
        ya = jnp.dot(ya_ref[rows, :], wa_ref[...], preferred_element_type=F32)
        yb = jnp.dot(yb_ref[rows, :], wb_ref[...], preferred_element_type=F32)
        yc = jnp.dot(yc_pre, wc_ref[...], preferred_element_type=F32)
        merged = (g0_ref[rows, :].astype(F32) * ya + g1_ref[rows, :].astype(F32) * yb
                  + g2_ref[rows, :].astype(F32) * yc)
        o = jnp.dot(merged.astype(BF16), wo_ref[...], preferred_element_type=F32)
        o_ref[rows, :] = x_ref[rows, :] + _rms(o, gp_ref[...])


def _merge(x2, ya2, yb2, proj2, mkv3, wa, wb, wc, wo, g_post, seq):
    n = x2.shape[0]
    tm = min(2 * MERGE_ROWS, seq)
    per_batch = seq // tm
    n_mem = mkv3.shape[1]
    c_dim = C_HEADS * C_HEAD_DIM
    const = lambda t: (0, 0)
    return pl.pallas_call(
        _merge_kernel,
        name="merge",
        grid=(n // tm,),
        in_specs=[
            pl.BlockSpec((tm, D_MODEL), lambda t: (t, 0)),
            pl.BlockSpec((tm, 512), lambda t: (t, 0)),
            pl.BlockSpec((tm, 512), lambda t: (t, 0)),
            pl.BlockSpec((tm, 512), lambda t: (t, COL_CQ)),
            pl.BlockSpec((tm, D_MODEL), lambda t: (t, 0)),
            pl.BlockSpec((tm, D_MODEL), lambda t: (t, 1)),
            pl.BlockSpec((tm, D_MODEL), lambda t: (t, 2)),
            pl.BlockSpec((None, n_mem, c_dim), lambda t: (t // per_batch, 0, 0)),
            pl.BlockSpec((None, n_mem, c_dim), lambda t: (t // per_batch, 0, 1)),
            pl.BlockSpec(wa.shape, const),
            pl.BlockSpec(wb.shape, const),
            pl.BlockSpec(wc.shape, const),
            pl.BlockSpec(wo.shape, const),
            pl.BlockSpec((1, D_MODEL), const),
        ],
        out_specs=pl.BlockSpec((tm, D_MODEL), lambda t: (t, 0)),
        out_shape=jax.ShapeDtypeStruct((n, D_MODEL), F32),
        compiler_params=pltpu.CompilerParams(
            dimension_semantics=("arbitrary",), vmem_limit_bytes=VMEM_LIMIT),
    )(x2, ya2, yb2, proj2, proj2, proj2, proj2, mkv3, mkv3, wa, wb, wc, wo, g_post)


FFN_ROWS = 512


def _ffn_kernel(x_ref, gpre_ref, wg_ref, wu_ref, wo_ref, gpost_ref, o_ref, h_ref, acc_ref):
    k = pl.program_id(1)

    @pl.when(k == 0)
    def _():
        h_ref[...] = _rms(x_ref[...], gpre_ref[...]).astype(BF16)
        acc_ref[...] = jnp.zeros(acc_ref.shape, F32)

    for r0 in range(0, h_ref.shape[0], FFN_ROWS):
        rows = slice(r0, min(r0 + FFN_ROWS, h_ref.shape[0]))
        h = h_ref[rows, :]
        g = jnp.dot(h, wg_ref[...], preferred_element_type=F32)
        u = jnp.dot(h, wu_ref[...], preferred_element_type=F32)
        act = (g * jax.nn.sigmoid(g) * u).astype(BF16)
        acc_ref[rows, :] += jnp.dot(act, wo_ref[...], preferred_element_type=F32)

    @pl.when(k == pl.num_programs(1) - 1)
    def _():
        o_ref[...] = x_ref[...] + _rms(acc_ref[...], gpost_ref[...])


def _ffn(x2, g_pre, wg, wu, wo, g_post):
    n = x2.shape[0]
    d_ff = wg.shape[1]
    tm = min(2 * FFN_ROWS, n)
    tf = d_ff // 2
    return pl.pallas_call(
        _ffn_kernel,
        name="ffn",
        grid=(n // tm, d_ff // tf),
        in_specs=[
            pl.BlockSpec((tm, D_MODEL), lambda t, k: (t, 0)),
            pl.BlockSpec((1, D_MODEL), lambda t, k: (0, 0)),
            pl.BlockSpec((D_MODEL, tf), lambda t, k: (0, k)),
            pl.BlockSpec((D_MODEL, tf), lambda t, k: (0, k)),
            pl.BlockSpec((tf, D_MODEL), lambda t, k: (k, 0)),
            pl.BlockSpec((1, D_MODEL), lambda t, k: (0, 0)),
        ],
        out_specs=pl.BlockSpec((tm, D_MODEL), lambda t, k: (t, 0)),
        out_shape=jax.ShapeDtypeStruct((n, D_MODEL), F32),
        scratch_shapes=[pltpu.VMEM((tm, D_MODEL), BF16), pltpu.VMEM((tm, D_MODEL), F32)],
        compiler_params=pltpu.CompilerParams(
            dimension_semantics=("arbitrary", "arbitrary"), vmem_limit_bytes=VMEM_LIMIT),
    )(x2, g_pre, wg, wu, wo, g_post)


def _pack_w_in(w, b_gate):
    sizes = (512, 512, 512, IDX_HEADS * 64, 64, IDX_HEADS, 512, 512, 512, 512,
             N_BRANCH * D_MODEL)
    aq, ak, av, iq, ik, iw, bq, bk, bv, cq, gates = jnp.split(w, np.cumsum(sizes)[:-1], axis=1)
    scale = HEAD_DIM ** -0.5
    scale2 = scale * LOG2E
    w_main = jnp.concatenate(
        [gates, aq * scale2, ak, iq * scale, bq * scale2, bk, bv, cq], axis=1).astype(BF16)
    z64 = jnp.zeros((D_MODEL, 64), F32)
    w_ik = jnp.concatenate([ik, z64, z64, ik], axis=1).astype(BF16)
    w_trans = jnp.concatenate(
        [av, iw, jnp.zeros((D_MODEL, N_TRANS - 512 - IDX_HEADS), F32)], axis=1).T.astype(BF16)
    b_main = jnp.concatenate([b_gate, jnp.zeros((N_MAIN - N_GATE,), F32)])[None, :]
    return w_main, w_ik, w_trans, b_main


def kernel(x, mem, rel_bias, g_mix_pre, w_in, b_gate, g_mem, w_mem_kv, w_up_a, w_up_b, w_up_c,
           w_out, g_mix_post, g_ffn_pre, w_ffn_in, w_ffn_out, g_ffn_post):
    bsz, seq, _ = x.shape
    n_mem = mem.shape[1]
    k_sel = min(TOPK_MAX, seq // 4)
    bias_tiles = _bias_tiles(rel_bias)
    x2 = x.reshape(bsz * seq, D_MODEL)
    for l in range(w_in.shape[0]):
        w_main, w_ik, w_trans, b_main = _pack_w_in(w_in[l], b_gate[l])
        proj2, ik2, avt, iwt = _project(x2, g_mix_pre[l][None, :], w_main, b_main, w_ik, w_trans)
        proj3 = proj2.reshape(bsz, seq, N_MAIN)
        mkv = _memkv(mem.reshape(bsz * n_mem, D_MODEL), g_mem[l][None, :],
                     w_mem_kv[l].astype(BF16))
        ya = _dsa(proj3, ik2.reshape(bsz, seq, N_IK), avt.reshape(bsz, seq // NK, 512, NK),
                  iwt, bias_tiles, k_sel)
        yb = _stick_breaking(proj3)
        x2 = _merge(x2, ya.reshape(bsz * seq, 512), yb.reshape(bsz * seq, 512), proj2,
                    mkv.reshape(bsz, n_mem, 2 * C_HEADS * C_HEAD_DIM),
                    w_up_a[l].astype(BF16), w_up_b[l].astype(BF16), w_up_c[l].astype(BF16),
                    w_out[l].astype(BF16), g_mix_post[l][None, :], seq)
        d_ff = w_ffn_out.shape[1]
        w_ffn = w_ffn_in[l].astype(BF16)
        x2 = _ffn(x2, g_ffn_pre[l][None, :], w_ffn[:, :d_ff], w_ffn[:, d_ff:],
                  w_ffn_out[l].astype(BF16), g_ffn_post[l][None, :])
    return x2.reshape(bsz, seq, D_MODEL)
```

```python
import functools

import numpy as np
import jax
import jax.numpy as jnp
from jax import lax
from jax.experimental import pallas as pl
from jax.experimental.pallas import tpu as pltpu

D_MODEL = 1024
CHUNK = 64
HEAD_DIM = 64
N_HEADS = 8
IDX_HEADS = 8
TOPK_MAX = 256
C_HEADS = 4
C_HEAD_DIM = 128
N_BRANCH = 3
REL_BUCKETS = 32
EPS = 1e-6

F32 = jnp.float32
BF16 = jnp.bfloat16
INT_MIN = -2 ** 31
NEG_BIG = -1e30
LOG2E = 1.4426950408889634

QB = 256
DQ = 256
NK = 256
N_PAIR = N_HEADS // 2
ONES_ROWS = 16

N_GATE = N_BRANCH * D_MODEL
COL_AQ, COL_AK, COL_IQ, COL_BQ, COL_BK, COL_BV, COL_CQ = range(N_GATE // 512, N_GATE // 512 + 7)
N_MAIN = N_GATE + 7 * 512
N_IK = 256
N_TRANS = 512 + 16

VMEM_LIMIT = 56 * 1024 * 1024

_NT = (((1,), (1,)), ((), ()))


def _rms(x, g):
    return x * lax.rsqrt(jnp.mean(x * x, axis=-1, keepdims=True) + EPS) * g


PROJ_ROWS = 1024


def _proj_kernel(x_ref, g_ref, w_ref, b_ref, wik_ref, wt_ref, o_ref, ik_ref, avt_ref, iwt_ref,
                 h_ref, *, n_gate_tiles):
    j = pl.program_id(1)

    @pl.when(j == 0)
    def _():
        for r0 in range(0, h_ref.shape[0], PROJ_ROWS):
            rows = slice(r0, min(r0 + PROJ_ROWS, h_ref.shape[0]))
            hb = _rms(x_ref[rows, :], g_ref[...]).astype(BF16)
            h_ref[rows, :] = hb
            ik_ref[rows, :] = jnp.dot(hb, wik_ref[...], preferred_element_type=F32).astype(BF16)
            tr = lax.dot_general(wt_ref[...], hb, _NT, preferred_element_type=F32)
            for c in range(hb.shape[0] // NK):
                avt_ref[r0 // NK + c] = tr[:512, c * NK:(c + 1) * NK].astype(BF16)
            iwt_ref[:, rows] = tr[512:512 + IDX_HEADS, :]

    def tile(finish):
        for r0 in range(0, h_ref.shape[0], PROJ_ROWS):
            rows = slice(r0, min(r0 + PROJ_ROWS, h_ref.shape[0]))
            acc = jnp.dot(h_ref[rows, :], w_ref[...], preferred_element_type=F32)
            o_ref[rows, :] = finish(acc).astype(BF16)

    @pl.when(j >= n_gate_tiles)
    def _():
        tile(lambda acc: acc)

    @pl.when(j < n_gate_tiles)
    def _():
        tile(lambda acc: 0.5 + 0.5 * jnp.tanh(0.5 * (acc + b_ref[...])))


def _project(x2, g, w_main, b_main, w_ik, w_trans):
    n = x2.shape[0]
    tm = min(2048, n)
    tn = 512
    grid = (n // tm, N_MAIN // tn)
    return pl.pallas_call(
        functools.partial(_proj_kernel, n_gate_tiles=N_GATE // tn),
        name="in_proj",
        grid=grid,
        in_specs=[
            pl.BlockSpec((tm, D_MODEL), lambda i, j: (i, 0)),
            pl.BlockSpec((1, D_MODEL), lambda i, j: (0, 0)),
            pl.BlockSpec((D_MODEL, tn), lambda i, j: (0, j)),
            pl.BlockSpec((1, tn), lambda i, j: (0, j)),
            pl.BlockSpec((D_MODEL, N_IK), lambda i, j: (0, 0)),
            pl.BlockSpec((N_TRANS, D_MODEL), lambda i, j: (0, 0)),
        ],
        out_specs=[
            pl.BlockSpec((tm, tn), lambda i, j: (i, j)),
            pl.BlockSpec((tm, N_IK), lambda i, j: (i, 0)),
            pl.BlockSpec((tm // NK, 512, NK), lambda i, j: (i, 0, 0)),
            pl.BlockSpec((IDX_HEADS, tm), lambda i, j: (0, i)),
        ],
        out_shape=[
            jax.ShapeDtypeStruct((n, N_MAIN), BF16),
            jax.ShapeDtypeStruct((n, N_IK), BF16),
            jax.ShapeDtypeStruct((n // NK, 512, NK), BF16),
            jax.ShapeDtypeStruct((IDX_HEADS, n), F32),
        ],
        scratch_shapes=[pltpu.VMEM((tm, D_MODEL), BF16)],
        compiler_params=pltpu.CompilerParams(
            dimension_semantics=("arbitrary", "arbitrary"), vmem_limit_bytes=VMEM_LIMIT),
    )(x2, g, w_main, b_main, w_ik, w_trans)


def _memkv_kernel(x_ref, g_ref, w_ref, o_ref):
    hb = _rms(x_ref[...], g_ref[...]).astype(BF16)
    o_ref[...] = jnp.dot(hb, w_ref[...], preferred_element_type=F32).astype(BF16)


def _memkv(mem2, g, w):
    n = mem2.shape[0]
    tm = min(512, n)
    return pl.pallas_call(
        _memkv_kernel,
        name="mem_kv",
        grid=(n // tm,),
        in_specs=[
            pl.BlockSpec((tm, D_MODEL), lambda i: (i, 0)),
            pl.BlockSpec((1, D_MODEL), lambda i: (0, 0)),
            pl.BlockSpec((D_MODEL, w.shape[1]), lambda i: (0, 0)),
        ],
        out_specs=pl.BlockSpec((tm, w.shape[1]), lambda i: (i, 0)),
        out_shape=jax.ShapeDtypeStruct((n, w.shape[1]), BF16),
        compiler_params=pltpu.CompilerParams(
            dimension_semantics=("arbitrary",), vmem_limit_bytes=VMEM_LIMIT),
    )(mem2, g, w)


BIAS_OFFSETS = (0, -NK)
_LOG_BUCKET_STARTS = (12, 16, 23, 32, 46, 64, 91)
FAR_BUCKET = 15
assert DQ == NK


def _bias_kernel(rb_ref, o_ref):
    key = lax.broadcasted_iota(jnp.int32, (NK, DQ), 0)
    qry = lax.broadcasted_iota(jnp.int32, (NK, DQ), 1)
    for c, off in enumerate(BIAS_OFFSETS):
        rel = key - qry + off
        n = jnp.abs(rel)
        large = jnp.full((NK, DQ), 8, jnp.int32)
        for start in _LOG_BUCKET_STARTS:
            large = large + jnp.where(n >= start, 1, 0)
        bucket = jnp.where(rel > 0, REL_BUCKETS // 2, 0) + jnp.where(n < 8, n, large)
        for h in range(N_HEADS):
            val = jnp.full((NK, DQ), rb_ref[0, h], F32)
            for b in range(1, REL_BUCKETS):
                val = jnp.where(bucket == b, rb_ref[b, h], val)
            o_ref[c, h] = (val - rb_ref[FAR_BUCKET, h]) * LOG2E


def _bias_tiles(rel_bias):
    return pl.pallas_call(
        _bias_kernel,
        name="rel_bias_tiles",
        in_specs=[pl.BlockSpec(memory_space=pltpu.SMEM)],
        out_specs=pl.BlockSpec(memory_space=pltpu.VMEM),
        out_shape=jax.ShapeDtypeStruct((len(BIAS_OFFSETS), N_HEADS, NK, DQ), F32),
    )(rel_bias)


def _split_heads_into(qm_ref, q):
    lane = lax.broadcasted_iota(jnp.int32, (QB, 128), 1)
    for p in range(N_PAIR):
        qp = q[:, p * 128:(p + 1) * 128].astype(F32)
        qm_ref[p, :QB, :] = jnp.where(lane < HEAD_DIM, qp, 0.0).astype(BF16)
        qm_ref[p, QB:, :] = jnp.where(lane >= HEAD_DIM, qp, 0.0).astype(BF16)


def _merge_pair(o_even, o_odd):
    lane = lax.broadcasted_iota(jnp.int32, (QB, 128), 1)
    return jnp.where(lane < HEAD_DIM, o_even, o_odd)


CNT_BLOCKS = 4
KEY_ROWS = 32
assert NK == 8 * 32


def _bit_transpose32(load_row, tmp_ref, store_row):
    def swap(a, b, j, m):
        t = (a ^ lax.shift_right_logical(b, jnp.int32(j))) & m
        return a ^ t, b ^ (t << j)

    lower = []
    for k in range(16):
        a, b = swap(load_row(k), load_row(k + 16), 16, 0x0000FFFF)
        lower.append(a)
        tmp_ref[k] = b
    for base in (0, 16):
        x = lower if base == 0 else [tmp_ref[k] for k in range(16)]
        j, m = 8, 0x00FF00FF
        while j:
            k = 0
            while k < 16:
                x[k], x[k + j] = swap(x[k], x[k + j], j, m)
                k = (k + j + 1) & ~j
            j >>= 1
            m ^= m << j
        for i in range(16):
            store_row(base + i, x[i])


def _keep_lowest_ties(alive_ref, kept_ref, need, n_groups):
    n_blocks = alive_ref.shape[0]
    sub = lax.broadcasted_iota(jnp.int32, (8, DQ), 0)
    zero_masks = []
    for c in reversed(range(max(n_blocks - 1, 1).bit_length())):
        zero_masks.append(
            lambda jb, c=c: jnp.where(((jb >> c) & 1) == 0, jnp.int32(-1), jnp.int32(0)))
    for word in (0xFFFF0000, 0xFF00FF00, 0xF0F0F0F0, 0xCCCCCCCC, 0xAAAAAAAA):
        zero_masks.append(lambda jb, word=word: jnp.int32(word - (1 << 32)))
    for c in (2, 1, 0):
        zero_masks.append(lambda jb, c=c: jnp.where(((sub >> c) & 1) == 0, -1, 0))

    remaining = need
    took_zero = None
    for p, zero_mask in enumerate(zero_masks):
        prev_mask = zero_masks[p - 1] if p else None

        def body(g, cnts, zero_mask=zero_mask, prev_mask=prev_mask, took_zero=took_zero):
            cnts = list(cnts)
            for u in range(CNT_BLOCKS):
                jb = g * CNT_BLOCKS + u
                alive = alive_ref[jb]
                if prev_mask is None:
                    kept_ref[jb] = jnp.zeros((8, DQ), jnp.int32)
                else:
                    lows = alive & prev_mask(jb)
                    kept_ref[jb] = kept_ref[jb] | jnp.where(took_zero, 0, lows)
                    alive = jnp.where(took_zero, lows, alive ^ lows)
                    alive_ref[jb] = alive
                cnts[u] = cnts[u] + lax.population_count(alive & zero_mask(jb))
            return tuple(cnts)

        zeros = jnp.zeros((8, DQ), jnp.int32)
        cnts = lax.fori_loop(0, n_groups, body, (zeros,) * CNT_BLOCKS)
        n_zero = jnp.sum(sum(cnts[1:], cnts[0]), axis=0, keepdims=True)
        took_zero = n_zero >= remaining
        remaining = jnp.where(took_zero, remaining, remaining - n_zero)

    last_mask = zero_masks[-1]

    def finish(jb, carry):
        alive = alive_ref[jb]
        alive_ref[jb] = kept_ref[jb] | jnp.where(took_zero, alive & last_mask(jb), alive)
        return carry

    lax.fori_loop(0, n_groups * CNT_BLOCKS, finish, 0)


def _dsa_kernel(aq_ref, iq_ref, iwt_ref, ak_ref, avt_ref, ik_ref, bias_ref, o_ref,
                keybuf_ref, plane_ref, alive_ref, above_ref, kept_ref, sel_ref, qm_ref, m_ref,
                acc_ref, s0_ref, cm0_ref, s1_ref, cm1_ref, raw0_ref, raw1_ref, tmp_ref, *,
                k_sel):
    step = pl.program_id(1)
    n_q = pl.num_programs(1) - 1
    att = step - 1
    krow = lax.broadcasted_iota(jnp.int32, (KEY_ROWS, DQ), 0)
    qcol = lax.broadcasted_iota(jnp.int32, (KEY_ROWS, DQ), 1)
    qchunk = (step * DQ + qcol) // CHUNK

    iwt = iwt_ref[...] * (IDX_HEADS ** -0.5)

    def dots_stage(jb, slot):
        raw_ref = (raw0_ref, raw1_ref)[slot]
        k0 = pl.multiple_of(jb * NK, NK)
        for h in range(IDX_HEADS):
            ikh = ik_ref[pl.ds(k0, NK), (h % 2) * 128:(h % 2 + 1) * 128]
            iqp = iq_ref[:, (h // 2) * 128:(h // 2 + 1) * 128]
            raw_ref[h] = lax.dot_general(ikh, iqp, _NT, preferred_element_type=F32)

    def keys_stage(jb, slot, last):
        raw_ref = (raw0_ref, raw1_ref)[slot]
        k0 = pl.multiple_of(jb * NK, NK)
        for c in range(NK // KEY_ROWS):
            rows = slice(c * KEY_ROWS, (c + 1) * KEY_ROWS)
            acc = jnp.zeros((KEY_ROWS, DQ), F32)
            for h in range(IDX_HEADS):
                acc = acc + iwt[h:h + 1, :] * jnp.maximum(raw_ref[h, rows, :], 0.0)
            bits = lax.bitcast_convert_type(acc, jnp.int32)
            key = bits ^ ((bits >> 31) & 0x7FFFFFFF)
            key = jnp.where(key == -1, 0, key)
            ukey = key ^ INT_MIN
            if last:
                admissible = ((k0 + c * KEY_ROWS + krow) // CHUNK) <= qchunk
                ukey = jnp.where(admissible, ukey, 0)
            keybuf_ref[slot, rows, :] = ukey
        for half in range(DQ // 128):
            lanes = slice(half * 128, (half + 1) * 128)

            def load_row(r):
                return keybuf_ref[slot, 8 * r:8 * r + 8, lanes]

            def store_plane(i, v):
                plane_ref[31 - i, jb, :, lanes] = v

            _bit_transpose32(load_row, tmp_ref, store_plane)
        plane_ref[32, jb] = jnp.full((8, DQ), -1, jnp.int32)
        alive_ref[jb] = jnp.full((8, DQ), -1, jnp.int32)
        above_ref[jb] = jnp.zeros((8, DQ), jnp.int32)

    ones = jnp.ones((ONES_ROWS, NK), BF16)
    slots = ((s0_ref, cm0_ref), (s1_ref, cm1_ref))

    def logits_stage(jb, slot, bias_idx):
        s_ref, cm_ref = slots[slot]
        k0 = pl.multiple_of(jb * NK, NK)
        sel = sel_ref[jb]
        mask = jnp.concatenate(
            [jnp.where((sel << r) < 0, 0.0, NEG_BIG)
             for r in range(32)], axis=0).astype(BF16)
        for h in range(N_HEADS):
            kp = ak_ref[pl.ds(k0, NK), (h // 2) * 128:(h // 2 + 1) * 128]
            s = lax.dot_general(kp, qm_ref[h], _NT, preferred_element_type=F32)
            if bias_idx is not None:
                s = s + bias_ref[bias_idx, h]
            sb = s.astype(BF16) + mask
            s_ref[h] = sb
            cm_ref[h] = jnp.max(sb, axis=0, keepdims=True).astype(F32)

    def softmax_stage(jb, slot):
        s_ref, cm_ref = slots[slot]
        for h in range(N_HEADS):
            m_prev = m_ref[h]
            m_new = jnp.maximum(m_prev, cm_ref[h])
            alpha = jnp.exp2(m_prev - m_new)
            pe = jnp.exp2(s_ref[h] - m_new.astype(BF16))
            vt = jnp.concatenate([avt_ref[jb, h * HEAD_DIM:(h + 1) * HEAD_DIM, :], ones], axis=0)
            acc_ref[h] = alpha * acc_ref[h] + jnp.dot(vt, pe, preferred_element_type=F32)
            m_ref[h] = m_new

    def start_attention():
        lane = lax.broadcasted_iota(jnp.int32, (DQ, 128), 1)
        for p in range(N_PAIR):
            qp = aq_ref[:, p * 128:(p + 1) * 128].astype(F32)
            qm_ref[2 * p] = jnp.where(lane < HEAD_DIM, qp, 0.0).astype(BF16)
            qm_ref[2 * p + 1] = jnp.where(lane >= HEAD_DIM, qp, 0.0).astype(BF16)
        m_ref[...] = jnp.full(m_ref.shape, NEG_BIG, F32)
        acc_ref[...] = jnp.zeros(acc_ref.shape, F32)
        logits_stage(att, 0, 0)
        logits_stage(jnp.maximum(att - 1, 0), 1, 1)
        softmax_stage(att, 0)

    @pl.when(step == 0)
    def _():
        dots_stage(0, 0)
        keys_stage(0, 0, True)

    @pl.when(step == n_q)
    def _():
        start_attention()
        n_steps = att - 1

        def two_steps(u, carry):
            b = att - 1 - 2 * u
            logits_stage(b - 1, 0, None)
            softmax_stage(b, 1)
            logits_stage(b - 2, 1, None)
            softmax_stage(b - 1, 0)
            return carry

        lax.fori_loop(0, jnp.maximum(n_steps, 0) // 2, two_steps, 0)

        @pl.when(jnp.logical_and(n_steps >= 1, n_steps % 2 == 1))
        def _():
            logits_stage(0, 0, None)
            softmax_stage(1, 1)

        @pl.when(jnp.logical_and(att >= 1, att % 2 == 1))
        def _():
            softmax_stage(0, 1)

        @pl.when(jnp.logical_and(att >= 1, att % 2 == 0))
        def _():
            softmax_stage(0, 0)

    @pl.when(jnp.logical_and(step >= 1, step < n_q))
    def _():
        start_attention()
        dots_stage(0, 0)
        dots_stage(1, 1)
        keys_stage(0, 0, False)
        n_fused = jnp.maximum(att - 1, 0) // 2

        def fused_trip(u, carry):
            b = att - 1 - 2 * u
            j = 2 * u + 1
            logits_stage(b - 1, 0, None)
            dots_stage(j + 1, 0)
            softmax_stage(b, 1)
            keys_stage(j, 1, False)
            logits_stage(b - 2, 1, None)
            dots_stage(j + 2, 1)
            softmax_stage(b - 1, 0)
            keys_stage(j + 1, 0, False)
            return carry

        lax.fori_loop(0, n_fused, fused_trip, 0)

        @pl.when(att == 0)
        def _():
            keys_stage(1, 1, True)

        @pl.when(att % 2 == 1)
        def _():
            dots_stage(att + 1, 0)
            softmax_stage(0, 1)
            keys_stage(att, 1, False)
            keys_stage(att + 1, 0, True)

        @pl.when(jnp.logical_and(att >= 2, att % 2 == 0))
        def _():
            logits_stage(0, 0, None)
            dots_stage(att, 0)
            softmax_stage(1, 1)
            keys_stage(att - 1, 1, False)
            dots_stage(att + 1, 1)
            softmax_stage(0, 0)
            keys_stage(att, 0, False)
            keys_stage(att + 1, 1, True)

    @pl.when(step >= 1)
    def _():
        for p in range(N_PAIR):
            halves = []
            for h in (2 * p, 2 * p + 1):
                a = acc_ref[h]
                halves.append(a[:HEAD_DIM, :] / a[HEAD_DIM:HEAD_DIM + 1, :])
            o_ref[:, p * 128:(p + 1) * 128] = jnp.concatenate(halves, axis=0).T.astype(BF16)

    nkb = jnp.where(step < n_q, step + 1, 0)
    n_groups = (nkb + CNT_BLOCKS - 1) // CNT_BLOCKS

    def pad_block(jb, carry):
        for b in range(33):
            plane_ref[b, jb] = jnp.zeros((8, DQ), jnp.int32)
        alive_ref[jb] = jnp.zeros((8, DQ), jnp.int32)
        above_ref[jb] = jnp.zeros((8, DQ), jnp.int32)
        return carry

    lax.fori_loop(nkb, n_groups * CNT_BLOCKS, pad_block, 0)

    def select_pass(it, state):
        took_prev, n_above, thr_u = state
        b = 31 - it
        take_prev = took_prev != 0

        def body(g, cnts):
            cnts = list(cnts)
            for u in range(CNT_BLOCKS):
                jb = g * CNT_BLOCKS + u
                alive = alive_ref[jb]
                with_prev = alive & plane_ref[b + 1, jb]
                above_ref[jb] = above_ref[jb] | jnp.where(take_prev, 0, with_prev)
                alive = jnp.where(take_prev, with_prev, alive ^ with_prev)
                alive_ref[jb] = alive
                cnts[u] = cnts[u] + lax.population_count(alive & plane_ref[b, jb])
            return tuple(cnts)

        zeros = jnp.zeros((8, DQ), jnp.int32)
        cnts = lax.fori_loop(0, n_groups, body, (zeros,) * CNT_BLOCKS)
        n_one = jnp.sum(sum(cnts[1:], cnts[0]), axis=0, keepdims=True)
        take = (n_above + n_one) >= k_sel
        n_above = jnp.where(take, n_above, n_above + n_one)
        thr_u = jnp.where(take, thr_u | jnp.left_shift(jnp.int32(1), b), thr_u)
        return take.astype(jnp.int32), n_above, thr_u

    row0 = jnp.zeros((1, DQ), jnp.int32)
    took_last, n_above, thr_u = lax.fori_loop(0, 32, select_pass, (row0 + 1, row0, row0))

    def settle(g, cnts):
        cnts = list(cnts)
        for u in range(CNT_BLOCKS):
            jb = g * CNT_BLOCKS + u
            alive = alive_ref[jb]
            with_last = alive & plane_ref[0, jb]
            above_ref[jb] = above_ref[jb] | jnp.where(took_last != 0, 0, with_last)
            alive = jnp.where(took_last != 0, with_last, alive ^ with_last)
            alive_ref[jb] = alive
            cnts[u] = cnts[u] + lax.population_count(alive)
        return tuple(cnts)

    zeros = jnp.zeros((8, DQ), jnp.int32)
    cnts = lax.fori_loop(0, n_groups, settle, (zeros,) * CNT_BLOCKS)
    n_tied = jnp.sum(sum(cnts[1:], cnts[0]), axis=0, keepdims=True)
    need = k_sel - n_above
    real = thr_u != 0
    extra = jnp.logical_and(n_tied > need, real)

    @pl.when(jnp.max(extra.astype(jnp.int32)) > 0)
    def _():
        _keep_lowest_ties(alive_ref, kept_ref, need, n_groups)

    def finalize(jb, carry):
        sel_ref[jb] = above_ref[jb] | jnp.where(real, alive_ref[jb], 0)
        return carry

    lax.fori_loop(0, nkb, finalize, 0)


def _dsa(proj3, ik3, avt4, iwt, bias_tiles, k_sel):
    bsz, seq, _ = proj3.shape
    nq = seq // DQ
    assert seq % (NK * CNT_BLOCKS) == 0
    resident = dict(pipeline_mode=pl.Buffered(1))
    return pl.pallas_call(
        functools.partial(_dsa_kernel, k_sel=k_sel),
        name="dsa",
        grid=(bsz, nq + 1),
        in_specs=[
            pl.BlockSpec((None, DQ, 512), lambda b, s: (b, jnp.maximum(s - 1, 0), COL_AQ)),
            pl.BlockSpec((None, DQ, 512), lambda b, s: (b, jnp.minimum(s, nq - 1), COL_IQ)),
            pl.BlockSpec((IDX_HEADS, DQ), lambda b, s: (0, b * nq + jnp.minimum(s, nq - 1))),
            pl.BlockSpec((None, seq, 512), lambda b, i: (b, 0, COL_AK), **resident),
            pl.BlockSpec((None, seq // NK, 512, NK), lambda b, i: (b, 0, 0, 0), **resident),
            pl.BlockSpec((None, seq, N_IK), lambda b, i: (b, 0, 0), **resident),
            pl.BlockSpec(bias_tiles.shape, lambda b, i: (0, 0, 0, 0), **resident),
        ],
        out_specs=pl.BlockSpec((None, DQ, 512), lambda b, s: (b, jnp.maximum(s - 1, 0), 0)),
        out_shape=jax.ShapeDtypeStruct((bsz, seq, 512), BF16),
        scratch_shapes=[
            pltpu.VMEM((2, NK, DQ), jnp.int32),
            pltpu.VMEM((33, seq // NK, 8, DQ), jnp.int32),
            pltpu.VMEM((seq // NK, 8, DQ), jnp.int32),
            pltpu.VMEM((seq // NK, 8, DQ), jnp.int32),
            pltpu.VMEM((seq // NK, 8, DQ), jnp.int32),
            pltpu.VMEM((seq // NK, 8, DQ), jnp.int32),
            pltpu.VMEM((N_HEADS, DQ, 128), BF16),
            pltpu.VMEM((N_HEADS, 1, DQ), F32),
            pltpu.VMEM((N_HEADS, HEAD_DIM + ONES_ROWS, DQ), F32),
            pltpu.VMEM((N_HEADS, NK, DQ), BF16),
            pltpu.VMEM((N_HEADS, 1, DQ), F32),
            pltpu.VMEM((N_HEADS, NK, DQ), BF16),
            pltpu.VMEM((N_HEADS, 1, DQ), F32),
            pltpu.VMEM((IDX_HEADS, NK, DQ), F32),
            pltpu.VMEM((IDX_HEADS, NK, DQ), F32),
            pltpu.VMEM((16, 8, 128), jnp.int32),
        ],
        compiler_params=pltpu.CompilerParams(
            dimension_semantics=("arbitrary", "arbitrary"), vmem_limit_bytes=VMEM_LIMIT),
    )(proj3, proj3, iwt, proj3, avt4, ik3, bias_tiles)


SB_DEAD_MASS = 104.0 * LOG2E


def _sb_kernel(q_ref, k_ref, v_ref, o_ref, qm_ref, uu_ref, carry_ref, acc_ref, z_ref, sp_ref,
               later_ref):
    i = pl.program_id(1)
    diag = (i * QB + QB - 1) // NK

    @pl.when(jnp.logical_and(pl.program_id(0) == 0, i == 0))
    def _():
        kr = lax.broadcasted_iota(jnp.int32, (NK, NK), 0)
        kc = lax.broadcasted_iota(jnp.int32, (NK, NK), 1)
        uu_ref[...] = jnp.where(kr > kc, 1.0, 0.0).astype(BF16)

    _split_heads_into(qm_ref, q_ref[...])
    carry_ref[...] = jnp.zeros(carry_ref.shape, F32)
    acc_ref[...] = jnp.zeros(acc_ref.shape, F32)

    def block(jb, on_diagonal):
        k0 = pl.multiple_of(jb * NK, NK)
        if on_diagonal:
            row = lax.broadcasted_iota(jnp.int32, (2 * QB, NK), 0)
            col = lax.broadcasted_iota(jnp.int32, (2 * QB, NK), 1)
            causal = (k0 + col) < (i * QB + row % QB)
        for p in range(N_PAIR):
            kp = k_ref[pl.ds(k0, NK), p * 128:(p + 1) * 128]
            z_ref[p] = lax.dot_general(qm_ref[p], kp, _NT, preferred_element_type=F32)
        for p in range(N_PAIR):
            z = z_ref[p]
            neg_abs = lax.bitcast_convert_type(
                lax.bitcast_convert_type(z, jnp.int32) | INT_MIN, F32)
            sp = jnp.maximum(z, 0.0) + jnp.log(1.0 + jnp.exp2(neg_abs)) * LOG2E
            if on_diagonal:
                sp = jnp.where(causal, sp, 0.0)
            sp_ref[p] = sp
            later_ref[p] = jnp.dot(sp.astype(BF16), uu_ref[...], preferred_element_type=F32)
        for p in range(N_PAIR):
            vp = v_ref[pl.ds(k0, NK), p * 128:(p + 1) * 128]
            carry = carry_ref[p]
            sp = sp_ref[p]
            a = jnp.exp2(z_ref[p] - sp - later_ref[p] - carry)
            if on_diagonal:
                a = jnp.where(causal, a, 0.0)
            acc_ref[p] += jnp.dot(a.astype(BF16), vp, preferred_element_type=F32)
            carry_ref[p] = carry + jnp.sum(sp, axis=1, keepdims=True)

    block(diag, True)

    def alive():
        return (jnp.min(carry_ref[...]) <= SB_DEAD_MASS).astype(jnp.int32)

    def cond(state):
        jb, go = state
        return jnp.logical_and(jb >= 0, go > 0)

    def body(state):
        jb, _ = state
        block(jb, False)
        return jb - 1, alive()

    lax.while_loop(cond, body, (diag - 1, alive()))

    for p in range(N_PAIR):
        o_ref[:, p * 128:(p + 1) * 128] = _merge_pair(
            acc_ref[p, :QB, :], acc_ref[p, QB:, :]).astype(BF16)


def _stick_breaking(proj3):
    bsz, seq, _ = proj3.shape
    resident = dict(pipeline_mode=pl.Buffered(1))
    return pl.pallas_call(
        _sb_kernel,
        name="stick_breaking",
        grid=(bsz, seq // QB),
        in_specs=[
            pl.BlockSpec((None, QB, 512), lambda b, i: (b, i, COL_BQ)),
            pl.BlockSpec((None, seq, 512), lambda b, i: (b, 0, COL_BK), **resident),
            pl.BlockSpec((None, seq, 512), lambda b, i: (b, 0, COL_BV), **resident),
        ],
        out_specs=pl.BlockSpec((None, QB, 512), lambda b, i: (b, i, 0)),
        out_shape=jax.ShapeDtypeStruct((bsz, seq, 512), BF16),
        scratch_shapes=[
            pltpu.VMEM((N_PAIR, 2 * QB, 128), BF16),
            pltpu.VMEM((NK, NK), BF16),
            pltpu.VMEM((N_PAIR, 2 * QB, 1), F32),
            pltpu.VMEM((N_PAIR, 2 * QB, 128), F32),
            pltpu.VMEM((N_PAIR, 2 * QB, NK), F32),
            pltpu.VMEM((N_PAIR, 2 * QB, NK), F32),
            pltpu.VMEM((N_PAIR, 2 * QB, NK), F32),
        ],
        compiler_params=pltpu.CompilerParams(
            dimension_semantics=("arbitrary", "arbitrary"), vmem_limit_bytes=VMEM_LIMIT),
    )(proj3, proj3, proj3)


MERGE_ROWS = 512


def _merge_kernel(x_ref, ya_ref, yb_ref, cq_ref, g0_ref, g1_ref, g2_ref, mk_ref, mv_ref,
                  wa_ref, wb_ref, wc_ref, wo_ref, gp_ref, o_ref):
    n_rows = x_ref.shape[0]
    for r0 in range(0, n_rows, MERGE_ROWS):
        rows = slice(r0, r0 + MERGE_ROWS)
        heads = []
        for h in range(C_HEADS):
            sl = slice(h * C_HEAD_DIM, (h + 1) * C_HEAD_DIM)
            s = lax.dot_general(cq_ref[rows, sl], mk_ref[:, sl], _NT,
                                preferred_element_type=F32) * (C_HEAD_DIM ** -0.5)
            e = jnp.exp(s - jnp.max(s, axis=1, keepdims=True))
            p = e / jnp.sum(e, axis=1, keepdims=True)
            heads.append(jnp.dot(p.astype(BF16), mv_ref[:, sl], preferred_element_type=F32))
        yc_pre = jnp.concatenate(heads, axis=1).astype(BF16)
        ya = jnp.dot(ya_ref[rows, :], wa_ref[...], preferred_element_type=F32)
        yb = jnp.dot(yb_ref[rows, :], wb_ref[...], preferred_element_type=F32)
        yc = jnp.dot(yc_pre, wc_ref[...], preferred_element_type=F32)
        merged = (g0_ref[rows, :].astype(F32) * ya + g1_ref[rows, :].astype(F32) * yb
                  + g2_ref[rows, :].astype(F32) * yc)
        o = jnp.dot(merged.astype(BF16), wo_ref[...], preferred_element_type=F32)
        o_ref[rows, :] = x_ref[rows, :] + _rms(o, gp_ref[...])


def _merge(x2, ya2, yb2, proj2, mkv3, wa, wb, wc, wo, g_post, seq):
    n = x2.shape[0]
    tm = min(2 * MERGE_ROWS, seq)
    per_batch = seq // tm
    n_mem = mkv3.shape[1]
    c_dim = C_HEADS * C_HEAD_DIM
    const = lambda t: (0, 0)
    return pl.pallas_call(
        _merge_kernel,
        name="merge",
        grid=(n // tm,),
        in_specs=[
            pl.BlockSpec((tm, D_MODEL), lambda t: (t, 0)),
            pl.BlockSpec((tm, 512), lambda t: (t, 0)),
            pl.BlockSpec((tm, 512), lambda t: (t, 0)),
            pl.BlockSpec((tm, 512), lambda t: (t, COL_CQ)),
            pl.BlockSpec((tm, D_MODEL), lambda t: (t, 0)),
            pl.BlockSpec((tm, D_MODEL), lambda t: (t, 1)),
            pl.BlockSpec((tm, D_MODEL), lambda t: (t, 2)),
            pl.BlockSpec((None, n_mem, c_dim), lambda t: (t // per_batch, 0, 0)),
            pl.BlockSpec((None, n_mem, c_dim), lambda t: (t // per_batch, 0, 1)),
            pl.BlockSpec(wa.shape, const),
            pl.BlockSpec(wb.shape, const),
            pl.BlockSpec(wc.shape, const),
            pl.BlockSpec(wo.shape, const),
            pl.BlockSpec((1, D_MODEL), const),
        ],
        out_specs=pl.BlockSpec((tm, D_MODEL), lambda t: (t, 0)),
        out_shape=jax.ShapeDtypeStruct((n, D_MODEL), F32),
        compiler_params=pltpu.CompilerParams(
            dimension_semantics=("arbitrary",), vmem_limit_bytes=VMEM_LIMIT),
    )(x2, ya2, yb2, proj2, proj2, proj2, proj2, mkv3, mkv3, wa, wb, wc, wo, g_post)


FFN_ROWS = 512


def _ffn_kernel(x_ref, gpre_ref, wg_ref, wu_ref, wo_ref, gpost_ref, o_ref):
    for r0 in range(0, x_ref.shape[0], FFN_ROWS):
        rows = slice(r0, min(r0 + FFN_ROWS, x_ref.shape[0]))
        x = x_ref[rows, :]
        h = _rms(x, gpre_ref[...]).astype(BF16)
        g = jnp.dot(h, wg_ref[...], preferred_element_type=F32)
        u = jnp.dot(h, wu_ref[...], preferred_element_type=F32)
        act = (g * jax.nn.sigmoid(g) * u).astype(BF16)
        f = jnp.dot(act, wo_ref[...], preferred_element_type=F32)
        o_ref[rows, :] = x + _rms(f, gpost_ref[...])


def _ffn(x2, g_pre, wg, wu, wo, g_post):
    n = x2.shape[0]
    tm = min(2 * FFN_ROWS, n)
    const = lambda t: (0, 0)
    resident = dict(pipeline_mode=pl.Buffered(1))
    return pl.pallas_call(
        _ffn_kernel,
        name="ffn",
        grid=(n // tm,),
        in_specs=[
            pl.BlockSpec((tm, D_MODEL), lambda t: (t, 0)),
            pl.BlockSpec((1, D_MODEL), const),
            pl.BlockSpec(wg.shape, const, **resident),
            pl.BlockSpec(wu.shape, const, **resident),
            pl.BlockSpec(wo.shape, const, **resident),
            pl.BlockSpec((1, D_MODEL), const),
        ],
        out_specs=pl.BlockSpec((tm, D_MODEL), lambda t: (t, 0)),
        out_shape=jax.ShapeDtypeStruct((n, D_MODEL), F32),
        compiler_params=pltpu.CompilerParams(
            dimension_semantics=("arbitrary",), vmem_limit_bytes=VMEM_LIMIT),
    )(x2, g_pre, wg, wu, wo, g_post)


def _pack_w_in(w, b_gate):
    sizes = (512, 512, 512, IDX_HEADS * 64, 64, IDX_HEADS, 512, 512, 512, 512,
             N_BRANCH * D_MODEL)
    aq, ak, av, iq, ik, iw, bq, bk, bv, cq, gates = jnp.split(w, np.cumsum(sizes)[:-1], axis=1)
    scale = HEAD_DIM ** -0.5
    scale2 = scale * LOG2E
    w_main = jnp.concatenate(
        [gates, aq * scale2, ak, iq * scale, bq * scale2, bk, bv, cq], axis=1).astype(BF16)
    z64 = jnp.zeros((D_MODEL, 64), F32)
    w_ik = jnp.concatenate([ik, z64, z64, ik], axis=1).astype(BF16)
    w_trans = jnp.concatenate(
        [av, iw, jnp.zeros((D_MODEL, N_TRANS - 512 - IDX_HEADS), F32)], axis=1).T.astype(BF16)
    b_main = jnp.concatenate([b_gate, jnp.zeros((N_MAIN - N_GATE,), F32)])[None, :]
    return w_main, w_ik, w_trans, b_main


def kernel(x, mem, rel_bias, g_mix_pre, w_in, b_gate, g_mem, w_mem_kv, w_up_a, w_up_b, w_up_c,
           w_out, g_mix_post, g_ffn_pre, w_ffn_in, w_ffn_out, g_ffn_post):
    bsz, seq, _ = x.shape
    n_mem = mem.shape[1]
    k_sel = min(TOPK_MAX, seq // 4)
    bias_tiles = _bias_tiles(rel_bias)
    x2 = x.reshape(bsz * seq, D_MODEL)
    for l in range(w_in.shape[0]):
        w_main, w_ik, w_trans, b_main = _pack_w_in(w_in[l], b_gate[l])
        proj2, ik2, avt, iwt = _project(x2, g_mix_pre[l][None, :], w_main, b_main, w_ik, w_trans)
        proj3 = proj2.reshape(bsz, seq, N_MAIN)
        mkv = _memkv(mem.reshape(bsz * n_mem, D_MODEL), g_mem[l][None, :],
                     w_mem_kv[l].astype(BF16))
        ya = _dsa(proj3, ik2.reshape(bsz, seq, N_IK), avt.reshape(bsz, seq // NK, 512, NK),
                  iwt, bias_tiles, k_sel)
        yb = _stick_breaking(proj3)
        x2 = _merge(x2, ya.reshape(bsz * seq, 512), yb.reshape(bsz * seq, 512), proj2,
                    mkv.reshape(bsz, n_mem, 2 * C_HEADS * C_HEAD_DIM),
                    w_up_a[l].astype(BF16), w_up_b[l].astype(BF16), w_up_c[l].astype(BF16),
                    w_out[l].astype(BF16), g_mix_post[l][None, :], seq)
        d_ff = w_ffn_out.shape[1]
        w_ffn = w_ffn_in[l].astype(BF16)
        x2 = _ffn(x2, g_ffn_pre[l][None, :], w_ffn[:, :d_ff], w_ffn[:, d_ff:],
                  w_ffn_out[l].astype(BF16), g_ffn_post[l][None, :])
    return x2.reshape(bsz, seq, D_MODEL)
```

```python
import functools

import numpy as np
import jax
import jax.numpy as jnp
from jax import lax
from jax.experimental import pallas as pl
from jax.experimental.pallas import tpu as pltpu

D_MODEL = 1024
CHUNK = 64
HEAD_DIM = 64
N_HEADS = 8
IDX_HEADS = 8
TOPK_MAX = 256
C_HEADS = 4
C_HEAD_DIM = 128
N_BRANCH = 3
REL_BUCKETS = 32
EPS = 1e-6

F32 = jnp.float32
BF16 = jnp.bfloat16
INT_MIN = -2 ** 31
NEG_BIG = -1e30
LOG2E = 1.4426950408889634

QB = 256
DQ = 256
NK = 256
N_PAIR = N_HEADS // 2
ONES_ROWS = 16

N_GATE = N_BRANCH * D_MODEL
COL_AQ, COL_AK, COL_IQ, COL_BQ, COL_BK, COL_BV, COL_CQ = range(N_GATE // 512, N_GATE // 512 + 7)
N_MAIN = N_GATE + 7 * 512
N_IK = 256
N_TRANS = 512 + 16

VMEM_LIMIT = 56 * 1024 * 1024

_NT = (((1,), (1,)), ((), ()))


def _rms(x, g):
    return x * lax.rsqrt(jnp.mean(x * x, axis=-1, keepdims=True) + EPS) * g


PROJ_ROWS = 512
PROJ_COLS = 512


def _proj_kernel(x_ref, g_ref, w_ref, b_ref, wik_ref, wt_ref, o_ref, ik_ref, avt_ref, iwt_ref):
    hb = _rms(x_ref[...], g_ref[...]).astype(BF16)
    ik_ref[...] = jnp.dot(hb, wik_ref[...], preferred_element_type=F32).astype(BF16)
    tr = lax.dot_general(wt_ref[...], hb, _NT, preferred_element_type=F32)
    for c in range(avt_ref.shape[0]):
        avt_ref[c] = tr[:512, c * NK:(c + 1) * NK].astype(BF16)
    iwt_ref[...] = tr[512:512 + IDX_HEADS, :]
    for c0 in range(0, N_MAIN, PROJ_COLS):
        cols = slice(c0, c0 + PROJ_COLS)
        acc = jnp.dot(hb, w_ref[:, cols], preferred_element_type=F32)
        if c0 < N_GATE:
            acc = 0.5 + 0.5 * jnp.tanh(0.5 * (acc + b_ref[:, cols]))
        o_ref[:, cols] = acc.astype(BF16)


def _project(x2, g, w_main, b_gate, w_ik, w_trans):
    n = x2.shape[0]
    tm = min(PROJ_ROWS, n)
    const = lambda i: (0, 0)
    resident = dict(pipeline_mode=pl.Buffered(1))
    return pl.pallas_call(
        _proj_kernel,
        name="in_proj",
        grid=(n // tm,),
        in_specs=[
            pl.BlockSpec((tm, D_MODEL), lambda i: (i, 0)),
            pl.BlockSpec((1, D_MODEL), const),
            pl.BlockSpec(w_main.shape, const, **resident),
            pl.BlockSpec(b_gate.shape, const),
            pl.BlockSpec(w_ik.shape, const, **resident),
            pl.BlockSpec(w_trans.shape, const, **resident),
        ],
        out_specs=[
            pl.BlockSpec((tm, N_MAIN), lambda i: (i, 0)),
            pl.BlockSpec((tm, N_IK), lambda i: (i, 0)),
            pl.BlockSpec((tm // NK, 512, NK), lambda i: (i, 0, 0)),
            pl.BlockSpec((IDX_HEADS, tm), lambda i: (0, i)),
        ],
        out_shape=[
            jax.ShapeDtypeStruct((n, N_MAIN), BF16),
            jax.ShapeDtypeStruct((n, N_IK), BF16),
            jax.ShapeDtypeStruct((n // NK, 512, NK), BF16),
            jax.ShapeDtypeStruct((IDX_HEADS, n), F32),
        ],
        compiler_params=pltpu.CompilerParams(
            dimension_semantics=("arbitrary",), vmem_limit_bytes=VMEM_LIMIT),
    )(x2, g, w_main, b_gate, w_ik, w_trans)


def _memkv_kernel(x_ref, g_ref, w_ref, o_ref):
    hb = _rms(x_ref[...], g_ref[...]).astype(BF16)
    o_ref[...] = jnp.dot(hb, w_ref[...], preferred_element_type=F32).astype(BF16)


def _memkv(mem2, g, w):
    n = mem2.shape[0]
    tm = min(512, n)
    return pl.pallas_call(
        _memkv_kernel,
        name="mem_kv",
        grid=(n // tm,),
        in_specs=[
            pl.BlockSpec((tm, D_MODEL), lambda i: (i, 0)),
            pl.BlockSpec((1, D_MODEL), lambda i: (0, 0)),
            pl.BlockSpec((D_MODEL, w.shape[1]), lambda i: (0, 0)),
        ],
        out_specs=pl.BlockSpec((tm, w.shape[1]), lambda i: (i, 0)),
        out_shape=jax.ShapeDtypeStruct((n, w.shape[1]), BF16),
        compiler_params=pltpu.CompilerParams(
            dimension_semantics=("arbitrary",), vmem_limit_bytes=VMEM_LIMIT),
    )(mem2, g, w)


BIAS_OFFSETS = (0, -NK)
_LOG_BUCKET_STARTS = (12, 16, 23, 32, 46, 64, 91)
FAR_BUCKET = 15
assert DQ == NK


def _bias_kernel(rb_ref, o_ref):
    key = lax.broadcasted_iota(jnp.int32, (NK, DQ), 0)
    qry = lax.broadcasted_iota(jnp.int32, (NK, DQ), 1)
    for c, off in enumerate(BIAS_OFFSETS):
        rel = key - qry + off
        n = jnp.abs(rel)
        large = jnp.full((NK, DQ), 8, jnp.int32)
        for start in _LOG_BUCKET_STARTS:
            large = large + jnp.where(n >= start, 1, 0)
        bucket = jnp.where(rel > 0, REL_BUCKETS // 2, 0) + jnp.where(n < 8, n, large)
        for h in range(N_HEADS):
            val = jnp.full((NK, DQ), rb_ref[0, h], F32)
            for b in range(1, REL_BUCKETS):
                val = jnp.where(bucket == b, rb_ref[b, h], val)
            o_ref[c, h] = (val - rb_ref[FAR_BUCKET, h]) * LOG2E


def _bias_tiles(rel_bias):
    return pl.pallas_call(
        _bias_kernel,
        name="rel_bias_tiles",
        in_specs=[pl.BlockSpec(memory_space=pltpu.SMEM)],
        out_specs=pl.BlockSpec(memory_space=pltpu.VMEM),
        out_shape=jax.ShapeDtypeStruct((len(BIAS_OFFSETS), N_HEADS, NK, DQ), F32),
    )(rel_bias)


def _split_heads_into(qm_ref, q):
    lane = lax.broadcasted_iota(jnp.int32, (QB, 128), 1)
    for p in range(N_PAIR):
        qp = q[:, p * 128:(p + 1) * 128].astype(F32)
        qm_ref[p, :QB, :] = jnp.where(lane < HEAD_DIM, qp, 0.0).astype(BF16)
        qm_ref[p, QB:, :] = jnp.where(lane >= HEAD_DIM, qp, 0.0).astype(BF16)


def _merge_pair(o_even, o_odd):
    lane = lax.broadcasted_iota(jnp.int32, (QB, 128), 1)
    return jnp.where(lane < HEAD_DIM, o_even, o_odd)


CNT_BLOCKS = 4
KEY_ROWS = 32
assert NK == 8 * 32


def _bit_transpose32(load_row, tmp_ref, store_row):
    def swap(a, b, j, m):
        t = (a ^ lax.shift_right_logical(b, jnp.int32(j))) & m
        return a ^ t, b ^ (t << j)

    lower = []
    for k in range(16):
        a, b = swap(load_row(k), load_row(k + 16), 16, 0x0000FFFF)
        lower.append(a)
        tmp_ref[k] = b
    for base in (0, 16):
        x = lower if base == 0 else [tmp_ref[k] for k in range(16)]
        j, m = 8, 0x00FF00FF
        while j:
            k = 0
            while k < 16:
                x[k], x[k + j] = swap(x[k], x[k + j], j, m)
                k = (k + j + 1) & ~j
            j >>= 1
            m ^= m << j
        for i in range(16):
            store_row(base + i, x[i])


def _keep_lowest_ties(alive_ref, kept_ref, need, n_groups):
    n_blocks = alive_ref.shape[0]
    sub = lax.broadcasted_iota(jnp.int32, (8, DQ), 0)
    zero_masks = []
    for c in reversed(range(max(n_blocks - 1, 1).bit_length())):
        zero_masks.append(
            lambda jb, c=c: jnp.where(((jb >> c) & 1) == 0, jnp.int32(-1), jnp.int32(0)))
    for word in (0xFFFF0000, 0xFF00FF00, 0xF0F0F0F0, 0xCCCCCCCC, 0xAAAAAAAA):
        zero_masks.append(lambda jb, word=word: jnp.int32(word - (1 << 32)))
    for c in (2, 1, 0):
        zero_masks.append(lambda jb, c=c: jnp.where(((sub >> c) & 1) == 0, -1, 0))

    remaining = need
    took_zero = None
    for p, zero_mask in enumerate(zero_masks):
        prev_mask = zero_masks[p - 1] if p else None

        def body(g, cnts, zero_mask=zero_mask, prev_mask=prev_mask, took_zero=took_zero):
            cnts = list(cnts)
            for u in range(CNT_BLOCKS):
                jb = g * CNT_BLOCKS + u
                alive = alive_ref[jb]
                if prev_mask is None:
                    kept_ref[jb] = jnp.zeros((8, DQ), jnp.int32)
                else:
                    lows = alive & prev_mask(jb)
                    kept_ref[jb] = kept_ref[jb] | jnp.where(took_zero, 0, lows)
                    alive = jnp.where(took_zero, lows, alive ^ lows)
                    alive_ref[jb] = alive
                cnts[u] = cnts[u] + lax.population_count(alive & zero_mask(jb))
            return tuple(cnts)

        zeros = jnp.zeros((8, DQ), jnp.int32)
        cnts = lax.fori_loop(0, n_groups, body, (zeros,) * CNT_BLOCKS)
        n_zero = jnp.sum(sum(cnts[1:], cnts[0]), axis=0, keepdims=True)
        took_zero = n_zero >= remaining
        remaining = jnp.where(took_zero, remaining, remaining - n_zero)

    last_mask = zero_masks[-1]

    def finish(jb, carry):
        alive = alive_ref[jb]
        alive_ref[jb] = kept_ref[jb] | jnp.where(took_zero, alive & last_mask(jb), alive)
        return carry

    lax.fori_loop(0, n_groups * CNT_BLOCKS, finish, 0)


def _dsa_kernel(aq_ref, iq_ref, iwt_ref, ak_ref, avt_ref, ik_ref, bias_ref, o_ref,
                keybuf_ref, plane_ref, alive_ref, above_ref, kept_ref, sel_ref, qm_ref, m_ref,
                acc_ref, s0_ref, cm0_ref, s1_ref, cm1_ref, raw0_ref, raw1_ref, tmp_ref, *,
                k_sel):
    step = pl.program_id(1)
    n_q = pl.num_programs(1) - 1
    att = step - 1
    krow = lax.broadcasted_iota(jnp.int32, (KEY_ROWS, DQ), 0)
    qcol = lax.broadcasted_iota(jnp.int32, (KEY_ROWS, DQ), 1)
    qchunk = (step * DQ + qcol) // CHUNK

    iwt = iwt_ref[...] * (IDX_HEADS ** -0.5)

    def dots_stage(jb, slot):
        raw_ref = (raw0_ref, raw1_ref)[slot]
        k0 = pl.multiple_of(jb * NK, NK)
        for h in range(IDX_HEADS):
            ikh = ik_ref[pl.ds(k0, NK), (h % 2) * 128:(h % 2 + 1) * 128]
            iqp = iq_ref[:, (h // 2) * 128:(h // 2 + 1) * 128]
            raw_ref[h] = lax.dot_general(ikh, iqp, _NT, preferred_element_type=F32)

    def keys_stage(jb, slot, last):
        raw_ref = (raw0_ref, raw1_ref)[slot]
        k0 = pl.multiple_of(jb * NK, NK)
        for c in range(NK // KEY_ROWS):
            rows = slice(c * KEY_ROWS, (c + 1) * KEY_ROWS)
            acc = jnp.zeros((KEY_ROWS, DQ), F32)
            for h in range(IDX_HEADS):
                acc = acc + iwt[h:h + 1, :] * jnp.maximum(raw_ref[h, rows, :], 0.0)
            bits = lax.bitcast_convert_type(acc, jnp.int32)
            key = bits ^ ((bits >> 31) & 0x7FFFFFFF)
            key = jnp.where(key == -1, 0, key)
            ukey = key ^ INT_MIN
            if last:
                admissible = ((k0 + c * KEY_ROWS + krow) // CHUNK) <= qchunk
                ukey = jnp.where(admissible, ukey, 0)
            keybuf_ref[slot, rows, :] = ukey
        for half in range(DQ // 128):
            lanes = slice(half * 128, (half + 1) * 128)

            def load_row(r):
                return keybuf_ref[slot, 8 * r:8 * r + 8, lanes]

            def store_plane(i, v):
                plane_ref[31 - i, jb, :, lanes] = v

            _bit_transpose32(load_row, tmp_ref, store_plane)
        plane_ref[32, jb] = jnp.full((8, DQ), -1, jnp.int32)
        alive_ref[jb] = jnp.full((8, DQ), -1, jnp.int32)
        above_ref[jb] = jnp.zeros((8, DQ), jnp.int32)

    ones = jnp.ones((ONES_ROWS, NK), BF16)
    slots = ((s0_ref, cm0_ref), (s1_ref, cm1_ref))

    def logits_stage(jb, slot, bias_idx):
        s_ref, cm_ref = slots[slot]
        k0 = pl.multiple_of(jb * NK, NK)
        sel = sel_ref[jb]
        mask = jnp.concatenate(
            [jnp.where((sel << r) < 0, 0.0, NEG_BIG)
             for r in range(32)], axis=0).astype(BF16)
        for h in range(N_HEADS):
            kp = ak_ref[pl.ds(k0, NK), (h // 2) * 128:(h // 2 + 1) * 128]
            s = lax.dot_general(kp, qm_ref[h], _NT, preferred_element_type=F32)
            if bias_idx is not None:
                s = s + bias_ref[bias_idx, h]
            sb = s.astype(BF16) + mask
            s_ref[h] = sb
            cm_ref[h] = jnp.max(sb, axis=0, keepdims=True).astype(F32)

    def softmax_stage(jb, slot):
        s_ref, cm_ref = slots[slot]
        for h in range(N_HEADS):
            m_prev = m_ref[h]
            m_new = jnp.maximum(m_prev, cm_ref[h])
            alpha = jnp.exp2(m_prev - m_new)
            pe = jnp.exp2(s_ref[h] - m_new.astype(BF16))
            vt = jnp.concatenate([avt_ref[jb, h * HEAD_DIM:(h + 1) * HEAD_DIM, :], ones], axis=0)
            acc_ref[h] = alpha * acc_ref[h] + jnp.dot(vt, pe, preferred_element_type=F32)
            m_ref[h] = m_new

    def start_attention():
        lane = lax.broadcasted_iota(jnp.int32, (DQ, 128), 1)
        for p in range(N_PAIR):
            qp = aq_ref[:, p * 128:(p + 1) * 128].astype(F32)
            qm_ref[2 * p] = jnp.where(lane < HEAD_DIM, qp, 0.0).astype(BF16)
            qm_ref[2 * p + 1] = jnp.where(lane >= HEAD_DIM, qp, 0.0).astype(BF16)
        m_ref[...] = jnp.full(m_ref.shape, NEG_BIG, F32)
        acc_ref[...] = jnp.zeros(acc_ref.shape, F32)
        logits_stage(att, 0, 0)
        logits_stage(jnp.maximum(att - 1, 0), 1, 1)
        softmax_stage(att, 0)

    @pl.when(step == 0)
    def _():
        dots_stage(0, 0)
        keys_stage(0, 0, True)

    @pl.when(step == n_q)
    def _():
        start_attention()
        n_steps = att - 1

        def two_steps(u, carry):
            b = att - 1 - 2 * u
            logits_stage(b - 1, 0, None)
            softmax_stage(b, 1)
            logits_stage(b - 2, 1, None)
            softmax_stage(b - 1, 0)
            return carry

        lax.fori_loop(0, jnp.maximum(n_steps, 0) // 2, two_steps, 0)

        @pl.when(jnp.logical_and(n_steps >= 1, n_steps % 2 == 1))
        def _():
            logits_stage(0, 0, None)
            softmax_stage(1, 1)

        @pl.when(jnp.logical_and(att >= 1, att % 2 == 1))
        def _():
            softmax_stage(0, 1)

        @pl.when(jnp.logical_and(att >= 1, att % 2 == 0))
        def _():
            softmax_stage(0, 0)

    @pl.when(jnp.logical_and(step >= 1, step < n_q))
    def _():
        start_attention()
        dots_stage(0, 0)
        dots_stage(1, 1)
        keys_stage(0, 0, False)
        n_fused = jnp.maximum(att - 1, 0) // 2

        def fused_trip(u, carry):
            b = att - 1 - 2 * u
            j = 2 * u + 1
            logits_stage(b - 1, 0, None)
            dots_stage(j + 1, 0)
            softmax_stage(b, 1)
            keys_stage(j, 1, False)
            logits_stage(b - 2, 1, None)
            dots_stage(j + 2, 1)
            softmax_stage(b - 1, 0)
            keys_stage(j + 1, 0, False)
            return carry

        lax.fori_loop(0, n_fused, fused_trip, 0)

        @pl.when(att == 0)
        def _():
            keys_stage(1, 1, True)

        @pl.when(att % 2 == 1)
        def _():
            dots_stage(att + 1, 0)
            softmax_stage(0, 1)
            keys_stage(att, 1, False)
            keys_stage(att + 1, 0, True)

        @pl.when(jnp.logical_and(att >= 2, att % 2 == 0))
        def _():
            logits_stage(0, 0, None)
            dots_stage(att, 0)
            softmax_stage(1, 1)
            keys_stage(att - 1, 1, False)
            dots_stage(att + 1, 1)
            softmax_stage(0, 0)
            keys_stage(att, 0, False)
            keys_stage(att + 1, 1, True)

    @pl.when(step >= 1)
    def _():
        for p in range(N_PAIR):
            halves = []
            for h in (2 * p, 2 * p + 1):
                a = acc_ref[h]
                halves.append(a[:HEAD_DIM, :] / a[HEAD_DIM:HEAD_DIM + 1, :])
            o_ref[:, p * 128:(p + 1) * 128] = jnp.concatenate(halves, axis=0).T.astype(BF16)

    nkb = jnp.where(step < n_q, step + 1, 0)
    n_groups = (nkb + CNT_BLOCKS - 1) // CNT_BLOCKS

    def pad_block(jb, carry):
        for b in range(33):
            plane_ref[b, jb] = jnp.zeros((8, DQ), jnp.int32)
        alive_ref[jb] = jnp.zeros((8, DQ), jnp.int32)
        above_ref[jb] = jnp.zeros((8, DQ), jnp.int32)
        return carry

    lax.fori_loop(nkb, n_groups * CNT_BLOCKS, pad_block, 0)

    def select_pass(it, state):
        took_prev, n_above, thr_u = state
        b = 31 - it
        take_prev = took_prev != 0

        def body(g, cnts):
            cnts = list(cnts)
            for u in range(CNT_BLOCKS):
                jb = g * CNT_BLOCKS + u
                alive = alive_ref[jb]
                with_prev = alive & plane_ref[b + 1, jb]
                above_ref[jb] = above_ref[jb] | jnp.where(take_prev, 0, with_prev)
                alive = jnp.where(take_prev, with_prev, alive ^ with_prev)
                alive_ref[jb] = alive
                cnts[u] = cnts[u] + lax.population_count(alive & plane_ref[b, jb])
            return tuple(cnts)

        zeros = jnp.zeros((8, DQ), jnp.int32)
        cnts = lax.fori_loop(0, n_groups, body, (zeros,) * CNT_BLOCKS)
        n_one = jnp.sum(sum(cnts[1:], cnts[0]), axis=0, keepdims=True)
        take = (n_above + n_one) >= k_sel
        n_above = jnp.where(take, n_above, n_above + n_one)
        thr_u = jnp.where(take, thr_u | jnp.left_shift(jnp.int32(1), b), thr_u)
        return take.astype(jnp.int32), n_above, thr_u

    row0 = jnp.zeros((1, DQ), jnp.int32)
    took_last, n_above, thr_u = lax.fori_loop(0, 32, select_pass, (row0 + 1, row0, row0))

    def settle(g, cnts):
        cnts = list(cnts)
        for u in range(CNT_BLOCKS):
            jb = g * CNT_BLOCKS + u
            alive = alive_ref[jb]
            with_last = alive & plane_ref[0, jb]
            above_ref[jb] = above_ref[jb] | jnp.where(took_last != 0, 0, with_last)
            alive = jnp.where(took_last != 0, with_last, alive ^ with_last)
            alive_ref[jb] = alive
            cnts[u] = cnts[u] + lax.population_count(alive)
        return tuple(cnts)

    zeros = jnp.zeros((8, DQ), jnp.int32)
    cnts = lax.fori_loop(0, n_groups, settle, (zeros,) * CNT_BLOCKS)
    n_tied = jnp.sum(sum(cnts[1:], cnts[0]), axis=0, keepdims=True)
    need = k_sel - n_above
    real = thr_u != 0
    extra = jnp.logical_and(n_tied > need, real)

    @pl.when(jnp.max(extra.astype(jnp.int32)) > 0)
    def _():
        _keep_lowest_ties(alive_ref, kept_ref, need, n_groups)

    def finalize(jb, carry):
        sel_ref[jb] = above_ref[jb] | jnp.where(real, alive_ref[jb], 0)
        return carry

    lax.fori_loop(0, nkb, finalize, 0)


def _dsa(proj3, ik3, avt4, iwt, bias_tiles, k_sel):
    bsz, seq, _ = proj3.shape
    nq = seq // DQ
    assert seq % (NK * CNT_BLOCKS) == 0
    resident = dict(pipeline_mode=pl.Buffered(1))
    return pl.pallas_call(
        functools.partial(_dsa_kernel, k_sel=k_sel),
        name="dsa",
        grid=(bsz, nq + 1),
        in_specs=[
            pl.BlockSpec((None, DQ, 512), lambda b, s: (b, jnp.maximum(s - 1, 0), COL_AQ)),
            pl.BlockSpec((None, DQ, 512), lambda b, s: (b, jnp.minimum(s, nq - 1), COL_IQ)),
            pl.BlockSpec((IDX_HEADS, DQ), lambda b, s: (0, b * nq + jnp.minimum(s, nq - 1))),
            pl.BlockSpec((None, seq, 512), lambda b, i: (b, 0, COL_AK), **resident),
            pl.BlockSpec((None, seq // NK, 512, NK), lambda b, i: (b, 0, 0, 0), **resident),
            pl.BlockSpec((None, seq, N_IK), lambda b, i: (b, 0, 0), **resident),
            pl.BlockSpec(bias_tiles.shape, lambda b, i: (0, 0, 0, 0), **resident),
        ],
        out_specs=pl.BlockSpec((None, DQ, 512), lambda b, s: (b, jnp.maximum(s - 1, 0), 0)),
        out_shape=jax.ShapeDtypeStruct((bsz, seq, 512), BF16),
        scratch_shapes=[
            pltpu.VMEM((2, NK, DQ), jnp.int32),
            pltpu.VMEM((33, seq // NK, 8, DQ), jnp.int32),
            pltpu.VMEM((seq // NK, 8, DQ), jnp.int32),
            pltpu.VMEM((seq // NK, 8, DQ), jnp.int32),
            pltpu.VMEM((seq // NK, 8, DQ), jnp.int32),
            pltpu.VMEM((seq // NK, 8, DQ), jnp.int32),
            pltpu.VMEM((N_HEADS, DQ, 128), BF16),
            pltpu.VMEM((N_HEADS, 1, DQ), F32),
            pltpu.VMEM((N_HEADS, HEAD_DIM + ONES_ROWS, DQ), F32),
            pltpu.VMEM((N_HEADS, NK, DQ), BF16),
            pltpu.VMEM((N_HEADS, 1, DQ), F32),
            pltpu.VMEM((N_HEADS, NK, DQ), BF16),
            pltpu.VMEM((N_HEADS, 1, DQ), F32),
            pltpu.VMEM((IDX_HEADS, NK, DQ), F32),
            pltpu.VMEM((IDX_HEADS, NK, DQ), F32),
            pltpu.VMEM((16, 8, 128), jnp.int32),
        ],
        compiler_params=pltpu.CompilerParams(
            dimension_semantics=("arbitrary", "arbitrary"), vmem_limit_bytes=VMEM_LIMIT),
    )(proj3, proj3, iwt, proj3, avt4, ik3, bias_tiles)


SB_DEAD_MASS = 104.0 * LOG2E


def _sb_kernel(q_ref, k_ref, v_ref, o_ref, qm_ref, uu_ref, carry_ref, acc_ref, z_ref, sp_ref,
               later_ref):
    i = pl.program_id(1)
    diag = (i * QB + QB - 1) // NK

    @pl.when(jnp.logical_and(pl.program_id(0) == 0, i == 0))
    def _():
        kr = lax.broadcasted_iota(jnp.int32, (NK, NK), 0)
        kc = lax.broadcasted_iota(jnp.int32, (NK, NK), 1)
        uu_ref[...] = jnp.where(kr > kc, 1.0, 0.0).astype(BF16)

    _split_heads_into(qm_ref, q_ref[...])
    carry_ref[...] = jnp.zeros(carry_ref.shape, F32)
    acc_ref[...] = jnp.zeros(acc_ref.shape, F32)

    def block(jb, on_diagonal):
        k0 = pl.multiple_of(jb * NK, NK)
        if on_diagonal:
            row = lax.broadcasted_iota(jnp.int32, (2 * QB, NK), 0)
            col = lax.broadcasted_iota(jnp.int32, (2 * QB, NK), 1)
            causal = (k0 + col) < (i * QB + row % QB)
        for p in range(N_PAIR):
            kp = k_ref[pl.ds(k0, NK), p * 128:(p + 1) * 128]
            z_ref[p] = lax.dot_general(qm_ref[p], kp, _NT, preferred_element_type=F32)
        for p in range(N_PAIR):
            z = z_ref[p]
            neg_abs = lax.bitcast_convert_type(
                lax.bitcast_convert_type(z, jnp.int32) | INT_MIN, F32)
            sp = jnp.maximum(z, 0.0) + jnp.log(1.0 + jnp.exp2(neg_abs)) * LOG2E
            if on_diagonal:
                sp = jnp.where(causal, sp, 0.0)
            sp_ref[p] = sp
            later_ref[p] = jnp.dot(sp.astype(BF16), uu_ref[...], preferred_element_type=F32)
        for p in range(N_PAIR):
            vp = v_ref[pl.ds(k0, NK), p * 128:(p + 1) * 128]
            carry = carry_ref[p]
            sp = sp_ref[p]
            a = jnp.exp2(z_ref[p] - sp - later_ref[p] - carry)
            if on_diagonal:
                a = jnp.where(causal, a, 0.0)
            acc_ref[p] += jnp.dot(a.astype(BF16), vp, preferred_element_type=F32)
            carry_ref[p] = carry + jnp.sum(sp, axis=1, keepdims=True)

    block(diag, True)

    def alive():
        return (jnp.min(carry_ref[...]) <= SB_DEAD_MASS).astype(jnp.int32)

    def cond(state):
        jb, go = state
        return jnp.logical_and(jb >= 0, go > 0)

    def body(state):
        jb, _ = state
        block(jb, False)
        return jb - 1, alive()

    lax.while_loop(cond, body, (diag - 1, alive()))

    for p in range(N_PAIR):
        o_ref[:, p * 128:(p + 1) * 128] = _merge_pair(
            acc_ref[p, :QB, :], acc_ref[p, QB:, :]).astype(BF16)


def _stick_breaking(proj3):
    bsz, seq, _ = proj3.shape
    resident = dict(pipeline_mode=pl.Buffered(1))
    return pl.pallas_call(
        _sb_kernel,
        name="stick_breaking",
        grid=(bsz, seq // QB),
        in_specs=[
            pl.BlockSpec((None, QB, 512), lambda b, i: (b, i, COL_BQ)),
            pl.BlockSpec((None, seq, 512), lambda b, i: (b, 0, COL_BK), **resident),
            pl.BlockSpec((None, seq, 512), lambda b, i: (b, 0, COL_BV), **resident),
        ],
        out_specs=pl.BlockSpec((None, QB, 512), lambda b, i: (b, i, 0)),
        out_shape=jax.ShapeDtypeStruct((bsz, seq, 512), BF16),
        scratch_shapes=[
            pltpu.VMEM((N_PAIR, 2 * QB, 128), BF16),
            pltpu.VMEM((NK, NK), BF16),
            pltpu.VMEM((N_PAIR, 2 * QB, 1), F32),
            pltpu.VMEM((N_PAIR, 2 * QB, 128), F32),
            pltpu.VMEM((N_PAIR, 2 * QB, NK), F32),
            pltpu.VMEM((N_PAIR, 2 * QB, NK), F32),
            pltpu.VMEM((N_PAIR, 2 * QB, NK), F32),
        ],
        compiler_params=pltpu.CompilerParams(
            dimension_semantics=("arbitrary", "arbitrary"), vmem_limit_bytes=VMEM_LIMIT),
    )(proj3, proj3, proj3)


MERGE_ROWS = 512


def _merge_kernel(x_ref, ya_ref, yb_ref, cq_ref, g0_ref, g1_ref, g2_ref, mk_ref, mv_ref,
                  wa_ref, wb_ref, wc_ref, wo_ref, gp_ref, o_ref):
    n_rows = x_ref.shape[0]
    for r0 in range(0, n_rows, MERGE_ROWS):
        rows = slice(r0, r0 + MERGE_ROWS)
        heads = []
        for h in range(C_HEADS):
            sl = slice(h * C_HEAD_DIM, (h + 1) * C_HEAD_DIM)
            s = lax.dot_general(cq_ref[rows, sl], mk_ref[:, sl], _NT,
                                preferred_element_type=F32) * (C_HEAD_DIM ** -0.5)
            e = jnp.exp(s - jnp.max(s, axis=1, keepdims=True))
            p = e / jnp.sum(e, axis=1, keepdims=True)
            heads.append(jnp.dot(p.astype(BF16), mv_ref[:, sl], preferred_element_type=F32))
        yc_pre = jnp.concatenate(heads, axis=1).astype(BF16)
        ya = jnp.dot(ya_ref[rows, :], wa_ref[...], preferred_element_type=F32)
        yb = jnp.dot(yb_ref[rows, :], wb_ref[...], preferred_element_type=F32)
        yc = jnp.dot(yc_pre, wc_ref[...], preferred_element_type=F32)
        merged = (g0_ref[rows, :].astype(F32) * ya + g1_ref[rows, :].astype(F32) * yb
                  + g2_ref[rows, :].astype(F32) * yc)
        o = jnp.dot(merged.astype(BF16), wo_ref[...], preferred_element_type=F32)
        o_ref[rows, :] = x_ref[rows, :] + _rms(o, gp_ref[...])


def _merge(x2, ya2, yb2, proj2, mkv3, wa, wb, wc, wo, g_post, seq):
    n = x2.shape[0]
    tm = min(2 * MERGE_ROWS, seq)
    per_batch = seq // tm
    n_mem = mkv3.shape[1]
    c_dim = C_HEADS * C_HEAD_DIM
    const = lambda t: (0, 0)
    return pl.pallas_call(
        _merge_kernel,
        name="merge",
        grid=(n // tm,),
        in_specs=[
            pl.BlockSpec((tm, D_MODEL), lambda t: (t, 0)),
            pl.BlockSpec((tm, 512), lambda t: (t, 0)),
            pl.BlockSpec((tm, 512), lambda t: (t, 0)),
            pl.BlockSpec((tm, 512), lambda t: (t, COL_CQ)),
            pl.BlockSpec((tm, D_MODEL), lambda t: (t, 0)),
            pl.BlockSpec((tm, D_MODEL), lambda t: (t, 1)),
            pl.BlockSpec((tm, D_MODEL), lambda t: (t, 2)),
            pl.BlockSpec((None, n_mem, c_dim), lambda t: (t // per_batch, 0, 0)),
            pl.BlockSpec((None, n_mem, c_dim), lambda t: (t // per_batch, 0, 1)),
            pl.BlockSpec(wa.shape, const),
            pl.BlockSpec(wb.shape, const),
            pl.BlockSpec(wc.shape, const),
            pl.BlockSpec(wo.shape, const),
            pl.BlockSpec((1, D_MODEL), const),
        ],
        out_specs=pl.BlockSpec((tm, D_MODEL), lambda t: (t, 0)),
        out_shape=jax.ShapeDtypeStruct((n, D_MODEL), F32),
        compiler_params=pltpu.CompilerParams(
            dimension_semantics=("arbitrary",), vmem_limit_bytes=VMEM_LIMIT),
    )(x2, ya2, yb2, proj2, proj2, proj2, proj2, mkv3, mkv3, wa, wb, wc, wo, g_post)


FFN_ROWS = 512


def _ffn_kernel(x_ref, gpre_ref, wg_ref, wu_ref, wo_ref, gpost_ref, o_ref):
    for r0 in range(0, x_ref.shape[0], FFN_ROWS):
        rows = slice(r0, min(r0 + FFN_ROWS, x_ref.shape[0]))
        x = x_ref[rows, :]
        h = _rms(x, gpre_ref[...]).astype(BF16)
        g = jnp.dot(h, wg_ref[...], preferred_element_type=F32)
        u = jnp.dot(h, wu_ref[...], preferred_element_type=F32)
        act = (g * jax.nn.sigmoid(g) * u).astype(BF16)
        f = jnp.dot(act, wo_ref[...], preferred_element_type=F32)
        o_ref[rows, :] = x + _rms(f, gpost_ref[...])


def _ffn(x2, g_pre, wg, wu, wo, g_post):
    n = x2.shape[0]
    tm = min(2 * FFN_ROWS, n)
    const = lambda t: (0, 0)
    resident = dict(pipeline_mode=pl.Buffered(1))
    return pl.pallas_call(
        _ffn_kernel,
        name="ffn",
        grid=(n // tm,),
        in_specs=[
            pl.BlockSpec((tm, D_MODEL), lambda t: (t, 0)),
            pl.BlockSpec((1, D_MODEL), const),
            pl.BlockSpec(wg.shape, const, **resident),
            pl.BlockSpec(wu.shape, const, **resident),
            pl.BlockSpec(wo.shape, const, **resident),
            pl.BlockSpec((1, D_MODEL), const),
        ],
        out_specs=pl.BlockSpec((tm, D_MODEL), lambda t: (t, 0)),
        out_shape=jax.ShapeDtypeStruct((n, D_MODEL), F32),
        compiler_params=pltpu.CompilerParams(
            dimension_semantics=("arbitrary",), vmem_limit_bytes=VMEM_LIMIT),
    )(x2, g_pre, wg, wu, wo, g_post)


def _pack_w_in(w):
    sizes = (512, 512, 512, IDX_HEADS * 64, 64, IDX_HEADS, 512, 512, 512, 512,
             N_BRANCH * D_MODEL)
    aq, ak, av, iq, ik, iw, bq, bk, bv, cq, gates = jnp.split(w, np.cumsum(sizes)[:-1], axis=1)
    scale = HEAD_DIM ** -0.5
    scale2 = scale * LOG2E
    w_main = jnp.concatenate(
        [gates, aq * scale2, ak, iq * scale, bq * scale2, bk, bv, cq], axis=1).astype(BF16)
    z64 = jnp.zeros((D_MODEL, 64), F32)
    w_ik = jnp.concatenate([ik, z64, z64, ik], axis=1).astype(BF16)
    w_trans = jnp.concatenate(
        [av, iw, jnp.zeros((D_MODEL, N_TRANS - 512 - IDX_HEADS), F32)], axis=1).T.astype(BF16)
    return w_main, w_ik, w_trans


def kernel(x, mem, rel_bias, g_mix_pre, w_in, b_gate, g_mem, w_mem_kv, w_up_a, w_up_b, w_up_c,
           w_out, g_mix_post, g_ffn_pre, w_ffn_in, w_ffn_out, g_ffn_post):
    bsz, seq, _ = x.shape
    n_mem = mem.shape[1]
    k_sel = min(TOPK_MAX, seq // 4)
    bias_tiles = _bias_tiles(rel_bias)
    x2 = x.reshape(bsz * seq, D_MODEL)
    for l in range(w_in.shape[0]):
        w_main, w_ik, w_trans = _pack_w_in(w_in[l])
        proj2, ik2, avt, iwt = _project(x2, g_mix_pre[l][None, :], w_main, b_gate[l][None, :],
                                        w_ik, w_trans)
        proj3 = proj2.reshape(bsz, seq, N_MAIN)
        mkv = _memkv(mem.reshape(bsz * n_mem, D_MODEL), g_mem[l][None, :],
                     w_mem_kv[l].astype(BF16))
        ya = _dsa(proj3, ik2.reshape(bsz, seq, N_IK), avt.reshape(bsz, seq // NK, 512, NK),
                  iwt, bias_tiles, k_sel)
        yb = _stick_breaking(proj3)
        x2 = _merge(x2, ya.reshape(bsz * seq, 512), yb.reshape(bsz * seq, 512), proj2,
                    mkv.reshape(bsz, n_mem, 2 * C_HEADS * C_HEAD_DIM),
                    w_up_a[l].astype(BF16), w_up_b[l].astype(BF16), w_up_c[l].astype(BF16),
                    w_out[l].astype(BF16), g_mix_post[l][None, :], seq)
        d_ff = w_ffn_out.shape[1]
        w_ffn = w_ffn_in[l].astype(BF16)
        x2 = _ffn(x2, g_ffn_pre[l][None, :], w_ffn[:, :d_ff], w_ffn[:, d_ff:],
                  w_ffn_out[l].astype(BF16), g_ffn_post[l][None, :])
    return x2.reshape(bsz, seq, D_MODEL)
```

```python
import functools

import numpy as np
import jax
import jax.numpy as jnp
from jax import lax
from jax.experimental import pallas as pl
from jax.experimental.pallas import tpu as pltpu

D_MODEL = 1024
CHUNK = 64
HEAD_DIM = 64
N_HEADS = 8
IDX_HEADS = 8
TOPK_MAX = 256
C_HEADS = 4
C_HEAD_DIM = 128
N_BRANCH = 3
REL_BUCKETS = 32
EPS = 1e-6

F32 = jnp.float32
BF16 = jnp.bfloat16
INT_MIN = -2 ** 31
NEG_BIG = -1e30
LOG2E = 1.4426950408889634

QB = 256
DQ = 256
NK = 256
N_PAIR = N_HEADS // 2
ONES_ROWS = 16

N_GATE = N_BRANCH * D_MODEL
COL_AQ, COL_AK, COL_IQ, COL_BQ, COL_BK, COL_BV, COL_CQ = range(N_GATE // 512, N_GATE // 512 + 7)
N_MAIN = N_GATE + 7 * 512
N_IK = 256
N_TRANS = 512 + 16

V7X_VMEM_BYTES = 64 * 1024 * 1024
VMEM_LIMIT = V7X_VMEM_BYTES * 7 // 8

_NT = (((1,), (1,)), ((), ()))


def _rms(x, g):
    return x * lax.rsqrt(jnp.mean(x * x, axis=-1, keepdims=True) + EPS) * g


PROJ_ROWS = 512
PROJ_COLS = 512


def _proj_kernel(x_ref, g_ref, w_ref, b_ref, wik_ref, wt_ref, o_ref, ik_ref, avt_ref, iwt_ref):
    hb = _rms(x_ref[...], g_ref[...]).astype(BF16)
    ik_ref[...] = jnp.dot(hb, wik_ref[...], preferred_element_type=F32).astype(BF16)
    tr = lax.dot_general(wt_ref[...], hb, _NT, preferred_element_type=F32)
    for c in range(avt_ref.shape[0]):
        avt_ref[c] = tr[:512, c * NK:(c + 1) * NK].astype(BF16)
    iwt_ref[...] = tr[512:512 + IDX_HEADS, :]
    for c0 in range(0, N_MAIN, PROJ_COLS):
        cols = slice(c0, c0 + PROJ_COLS)
        acc = jnp.dot(hb, w_ref[:, cols], preferred_element_type=F32)
        if c0 < N_GATE:
            acc = 0.5 + 0.5 * jnp.tanh(0.5 * (acc + b_ref[:, cols]))
        o_ref[:, cols] = acc.astype(BF16)


def _project(x2, g, w_main, b_gate, w_ik, w_trans):
    n = x2.shape[0]
    tm = min(PROJ_ROWS, n)
    const = lambda i: (0, 0)
    resident = dict(pipeline_mode=pl.Buffered(1))
    return pl.pallas_call(
        _proj_kernel,
        name="in_proj",
        grid=(n // tm,),
        in_specs=[
            pl.BlockSpec((tm, D_MODEL), lambda i: (i, 0)),
            pl.BlockSpec((1, D_MODEL), const),
            pl.BlockSpec(w_main.shape, const, **resident),
            pl.BlockSpec(b_gate.shape, const),
            pl.BlockSpec(w_ik.shape, const, **resident),
            pl.BlockSpec(w_trans.shape, const, **resident),
        ],
        out_specs=[
            pl.BlockSpec((tm, N_MAIN), lambda i: (i, 0)),
            pl.BlockSpec((tm, N_IK), lambda i: (i, 0)),
            pl.BlockSpec((tm // NK, 512, NK), lambda i: (i, 0, 0)),
            pl.BlockSpec((IDX_HEADS, tm), lambda i: (0, i)),
        ],
        out_shape=[
            jax.ShapeDtypeStruct((n, N_MAIN), BF16),
            jax.ShapeDtypeStruct((n, N_IK), BF16),
            jax.ShapeDtypeStruct((n // NK, 512, NK), BF16),
            jax.ShapeDtypeStruct((IDX_HEADS, n), F32),
        ],
        compiler_params=pltpu.CompilerParams(
            dimension_semantics=("arbitrary",), vmem_limit_bytes=VMEM_LIMIT),
    )(x2, g, w_main, b_gate, w_ik, w_trans)


def _memkv_kernel(x_ref, g_ref, w_ref, o_ref):
    hb = _rms(x_ref[...], g_ref[...]).astype(BF16)
    o_ref[...] = jnp.dot(hb, w_ref[...], preferred_element_type=F32).astype(BF16)


def _memkv(mem2, g, w):
    n = mem2.shape[0]
    tm = min(512, n)
    return pl.pallas_call(
        _memkv_kernel,
        name="mem_kv",
        grid=(n // tm,),
        in_specs=[
            pl.BlockSpec((tm, D_MODEL), lambda i: (i, 0)),
            pl.BlockSpec((1, D_MODEL), lambda i: (0, 0)),
            pl.BlockSpec((D_MODEL, w.shape[1]), lambda i: (0, 0)),
        ],
        out_specs=pl.BlockSpec((tm, w.shape[1]), lambda i: (i, 0)),
        out_shape=jax.ShapeDtypeStruct((n, w.shape[1]), BF16),
        compiler_params=pltpu.CompilerParams(
            dimension_semantics=("arbitrary",), vmem_limit_bytes=VMEM_LIMIT),
    )(mem2, g, w)


BIAS_OFFSETS = (0, -NK)
_LOG_BUCKET_STARTS = (12, 16, 23, 32, 46, 64, 91)
FAR_BUCKET = 15
assert DQ == NK


def _bias_kernel(rb_ref, o_ref):
    key = lax.broadcasted_iota(jnp.int32, (NK, DQ), 0)
    qry = lax.broadcasted_iota(jnp.int32, (NK, DQ), 1)
    for c, off in enumerate(BIAS_OFFSETS):
        rel = key - qry + off
        n = jnp.abs(rel)
        large = jnp.full((NK, DQ), 8, jnp.int32)
        for start in _LOG_BUCKET_STARTS:
            large = large + jnp.where(n >= start, 1, 0)
        bucket = jnp.where(rel > 0, REL_BUCKETS // 2, 0) + jnp.where(n < 8, n, large)
        for h in range(N_HEADS):
            val = jnp.full((NK, DQ), rb_ref[0, h], F32)
            for b in range(1, REL_BUCKETS):
                val = jnp.where(bucket == b, rb_ref[b, h], val)
            o_ref[c, h] = (val - rb_ref[FAR_BUCKET, h]) * LOG2E


def _bias_tiles(rel_bias):
    return pl.pallas_call(
        _bias_kernel,
        name="rel_bias_tiles",
        in_specs=[pl.BlockSpec(memory_space=pltpu.SMEM)],
        out_specs=pl.BlockSpec(memory_space=pltpu.VMEM),
        out_shape=jax.ShapeDtypeStruct((len(BIAS_OFFSETS), N_HEADS, NK, DQ), F32),
    )(rel_bias)


def _split_heads_into(qm_ref, q):
    lane = lax.broadcasted_iota(jnp.int32, (QB, 128), 1)
    for p in range(N_PAIR):
        qp = q[:, p * 128:(p + 1) * 128].astype(F32)
        qm_ref[p, :QB, :] = jnp.where(lane < HEAD_DIM, qp, 0.0).astype(BF16)
        qm_ref[p, QB:, :] = jnp.where(lane >= HEAD_DIM, qp, 0.0).astype(BF16)


def _merge_pair(o_even, o_odd):
    lane = lax.broadcasted_iota(jnp.int32, (QB, 128), 1)
    return jnp.where(lane < HEAD_DIM, o_even, o_odd)


CNT_BLOCKS = 4
KEY_ROWS = 32
assert NK == 8 * 32


def _bit_transpose32(load_row, tmp_ref, store_row):
    def swap(a, b, j, m):
        t = (a ^ lax.shift_right_logical(b, jnp.int32(j))) & m
        return a ^ t, b ^ (t << j)

    lower = []
    for k in range(16):
        a, b = swap(load_row(k), load_row(k + 16), 16, 0x0000FFFF)
        lower.append(a)
        tmp_ref[k] = b
    for base in (0, 16):
        x = lower if base == 0 else [tmp_ref[k] for k in range(16)]
        j, m = 8, 0x00FF00FF
        while j:
            k = 0
            while k < 16:
                x[k], x[k + j] = swap(x[k], x[k + j], j, m)
                k = (k + j + 1) & ~j
            j >>= 1
            m ^= m << j
        for i in range(16):
            store_row(base + i, x[i])


def _keep_lowest_ties(alive_ref, kept_ref, need, n_groups):
    n_blocks = alive_ref.shape[0]
    sub = lax.broadcasted_iota(jnp.int32, (8, DQ), 0)
    zero_masks = []
    for c in reversed(range(max(n_blocks - 1, 1).bit_length())):
        zero_masks.append(
            lambda jb, c=c: jnp.where(((jb >> c) & 1) == 0, jnp.int32(-1), jnp.int32(0)))
    for word in (0xFFFF0000, 0xFF00FF00, 0xF0F0F0F0, 0xCCCCCCCC, 0xAAAAAAAA):
        zero_masks.append(lambda jb, word=word: jnp.int32(word - (1 << 32)))
    for c in (2, 1, 0):
        zero_masks.append(lambda jb, c=c: jnp.where(((sub >> c) & 1) == 0, -1, 0))

    remaining = need
    took_zero = None
    for p, zero_mask in enumerate(zero_masks):
        prev_mask = zero_masks[p - 1] if p else None

        def body(g, cnts, zero_mask=zero_mask, prev_mask=prev_mask, took_zero=took_zero):
            cnts = list(cnts)
            for u in range(CNT_BLOCKS):
                jb = g * CNT_BLOCKS + u
                alive = alive_ref[jb]
                if prev_mask is None:
                    kept_ref[jb] = jnp.zeros((8, DQ), jnp.int32)
                else:
                    lows = alive & prev_mask(jb)
                    kept_ref[jb] = kept_ref[jb] | jnp.where(took_zero, 0, lows)
                    alive = jnp.where(took_zero, lows, alive ^ lows)
                    alive_ref[jb] = alive
                cnts[u] = cnts[u] + lax.population_count(alive & zero_mask(jb))
            return tuple(cnts)

        zeros = jnp.zeros((8, DQ), jnp.int32)
        cnts = lax.fori_loop(0, n_groups, body, (zeros,) * CNT_BLOCKS)
        n_zero = jnp.sum(sum(cnts[1:], cnts[0]), axis=0, keepdims=True)
        took_zero = n_zero >= remaining
        remaining = jnp.where(took_zero, remaining, remaining - n_zero)

    last_mask = zero_masks[-1]

    def finish(jb, carry):
        alive = alive_ref[jb]
        alive_ref[jb] = kept_ref[jb] | jnp.where(took_zero, alive & last_mask(jb), alive)
        return carry

    lax.fori_loop(0, n_groups * CNT_BLOCKS, finish, 0)


def _dsa_kernel(aq_ref, iq_ref, iwt_ref, ak_ref, avt_ref, ik_ref, bias_ref, o_ref,
                keybuf_ref, plane_ref, alive_ref, above_ref, kept_ref, sel_ref, qm_ref, m_ref,
                acc_ref, s0_ref, cm0_ref, s1_ref, cm1_ref, raw0_ref, raw1_ref, tmp_ref, *,
                k_sel):
    step = pl.program_id(1)
    n_q = pl.num_programs(1) - 1
    att = step - 1
    krow = lax.broadcasted_iota(jnp.int32, (KEY_ROWS, DQ), 0)
    qcol = lax.broadcasted_iota(jnp.int32, (KEY_ROWS, DQ), 1)
    qchunk = (step * DQ + qcol) // CHUNK

    iwt = iwt_ref[...] * (IDX_HEADS ** -0.5)

    def dots_stage(jb, slot):
        raw_ref = (raw0_ref, raw1_ref)[slot]
        k0 = pl.multiple_of(jb * NK, NK)
        for h in range(IDX_HEADS):
            ikh = ik_ref[pl.ds(k0, NK), (h % 2) * 128:(h % 2 + 1) * 128]
            iqp = iq_ref[:, (h // 2) * 128:(h // 2 + 1) * 128]
            raw_ref[h] = lax.dot_general(ikh, iqp, _NT, preferred_element_type=F32)

    def keys_stage(jb, slot, last):
        raw_ref = (raw0_ref, raw1_ref)[slot]
        k0 = pl.multiple_of(jb * NK, NK)
        for c in range(NK // KEY_ROWS):
            rows = slice(c * KEY_ROWS, (c + 1) * KEY_ROWS)
            acc = jnp.zeros((KEY_ROWS, DQ), F32)
            for h in range(IDX_HEADS):
                acc = acc + iwt[h:h + 1, :] * jnp.maximum(raw_ref[h, rows, :], 0.0)
            bits = lax.bitcast_convert_type(acc, jnp.int32)
            key = bits ^ ((bits >> 31) & 0x7FFFFFFF)
            key = jnp.where(key == -1, 0, key)
            ukey = key ^ INT_MIN
            if last:
                admissible = ((k0 + c * KEY_ROWS + krow) // CHUNK) <= qchunk
                ukey = jnp.where(admissible, ukey, 0)
            keybuf_ref[slot, rows, :] = ukey
        for half in range(DQ // 128):
            lanes = slice(half * 128, (half + 1) * 128)

            def load_row(r):
                return keybuf_ref[slot, 8 * r:8 * r + 8, lanes]

            def store_plane(i, v):
                plane_ref[31 - i, jb, :, lanes] = v

            _bit_transpose32(load_row, tmp_ref, store_plane)
        plane_ref[32, jb] = jnp.full((8, DQ), -1, jnp.int32)
        alive_ref[jb] = jnp.full((8, DQ), -1, jnp.int32)
        above_ref[jb] = jnp.zeros((8, DQ), jnp.int32)

    ones = jnp.ones((ONES_ROWS, NK), BF16)
    slots = ((s0_ref, cm0_ref), (s1_ref, cm1_ref))

    def logits_stage(jb, slot, bias_idx):
        s_ref, cm_ref = slots[slot]
        k0 = pl.multiple_of(jb * NK, NK)
        sel = sel_ref[jb]
        mask = jnp.concatenate(
            [jnp.where((sel << r) < 0, 0.0, NEG_BIG)
             for r in range(32)], axis=0).astype(BF16)
        for h in range(N_HEADS):
            kp = ak_ref[pl.ds(k0, NK), (h // 2) * 128:(h // 2 + 1) * 128]
            s = lax.dot_general(kp, qm_ref[h], _NT, preferred_element_type=F32)
            if bias_idx is not None:
                s = s + bias_ref[bias_idx, h]
            sb = s.astype(BF16) + mask
            s_ref[h] = sb
            cm_ref[h] = jnp.max(sb, axis=0, keepdims=True).astype(F32)

    def softmax_stage(jb, slot):
        s_ref, cm_ref = slots[slot]
        for h in range(N_HEADS):
            m_prev = m_ref[h]
            m_new = jnp.maximum(m_prev, cm_ref[h])
            alpha = jnp.exp2(m_prev - m_new)
            pe = jnp.exp2(s_ref[h] - m_new.astype(BF16))
            vt = jnp.concatenate([avt_ref[jb, h * HEAD_DIM:(h + 1) * HEAD_DIM, :], ones], axis=0)
            acc_ref[h] = alpha * acc_ref[h] + jnp.dot(vt, pe, preferred_element_type=F32)
            m_ref[h] = m_new

    def start_attention():
        lane = lax.broadcasted_iota(jnp.int32, (DQ, 128), 1)
        for p in range(N_PAIR):
            qp = aq_ref[:, p * 128:(p + 1) * 128].astype(F32)
            qm_ref[2 * p] = jnp.where(lane < HEAD_DIM, qp, 0.0).astype(BF16)
            qm_ref[2 * p + 1] = jnp.where(lane >= HEAD_DIM, qp, 0.0).astype(BF16)
        m_ref[...] = jnp.full(m_ref.shape, NEG_BIG, F32)
        acc_ref[...] = jnp.zeros(acc_ref.shape, F32)
        logits_stage(att, 0, 0)
        logits_stage(jnp.maximum(att - 1, 0), 1, 1)
        softmax_stage(att, 0)

    @pl.when(step == 0)
    def _():
        dots_stage(0, 0)
        keys_stage(0, 0, True)

    @pl.when(step == n_q)
    def _():
        start_attention()
        n_steps = att - 1

        def two_steps(u, carry):
            b = att - 1 - 2 * u
            logits_stage(b - 1, 0, None)
            softmax_stage(b, 1)
            logits_stage(b - 2, 1, None)
            softmax_stage(b - 1, 0)
            return carry

        lax.fori_loop(0, jnp.maximum(n_steps, 0) // 2, two_steps, 0)

        @pl.when(jnp.logical_and(n_steps >= 1, n_steps % 2 == 1))
        def _():
            logits_stage(0, 0, None)
            softmax_stage(1, 1)

        @pl.when(jnp.logical_and(att >= 1, att % 2 == 1))
        def _():
            softmax_stage(0, 1)

        @pl.when(jnp.logical_and(att >= 1, att % 2 == 0))
        def _():
            softmax_stage(0, 0)

    @pl.when(jnp.logical_and(step >= 1, step < n_q))
    def _():
        start_attention()
        dots_stage(0, 0)
        dots_stage(1, 1)
        keys_stage(0, 0, False)
        n_fused = jnp.maximum(att - 1, 0) // 2

        def fused_trip(u, carry):
            b = att - 1 - 2 * u
            j = 2 * u + 1
            logits_stage(b - 1, 0, None)
            dots_stage(j + 1, 0)
            softmax_stage(b, 1)
            keys_stage(j, 1, False)
            logits_stage(b - 2, 1, None)
            dots_stage(j + 2, 1)
            softmax_stage(b - 1, 0)
            keys_stage(j + 1, 0, False)
            return carry

        lax.fori_loop(0, n_fused, fused_trip, 0)

        @pl.when(att == 0)
        def _():
            keys_stage(1, 1, True)

        @pl.when(att % 2 == 1)
        def _():
            dots_stage(att + 1, 0)
            softmax_stage(0, 1)
            keys_stage(att, 1, False)
            keys_stage(att + 1, 0, True)

        @pl.when(jnp.logical_and(att >= 2, att % 2 == 0))
        def _():
            logits_stage(0, 0, None)
            dots_stage(att, 0)
            softmax_stage(1, 1)
            keys_stage(att - 1, 1, False)
            dots_stage(att + 1, 1)
            softmax_stage(0, 0)
            keys_stage(att, 0, False)
            keys_stage(att + 1, 1, True)

    @pl.when(step >= 1)
    def _():
        for p in range(N_PAIR):
            halves = []
            for h in (2 * p, 2 * p + 1):
                a = acc_ref[h]
                halves.append(a[:HEAD_DIM, :] / a[HEAD_DIM:HEAD_DIM + 1, :])
            o_ref[:, p * 128:(p + 1) * 128] = jnp.concatenate(halves, axis=0).T.astype(BF16)

    nkb = jnp.where(step < n_q, step + 1, 0)
    n_groups = (nkb + CNT_BLOCKS - 1) // CNT_BLOCKS

    def pad_block(jb, carry):
        for b in range(33):
            plane_ref[b, jb] = jnp.zeros((8, DQ), jnp.int32)
        alive_ref[jb] = jnp.zeros((8, DQ), jnp.int32)
        above_ref[jb] = jnp.zeros((8, DQ), jnp.int32)
        return carry

    lax.fori_loop(nkb, n_groups * CNT_BLOCKS, pad_block, 0)

    def select_pass(it, state):
        took_prev, n_above, thr_u = state
        b = 31 - it
        take_prev = took_prev != 0

        def body(g, cnts):
            cnts = list(cnts)
            for u in range(CNT_BLOCKS):
                jb = g * CNT_BLOCKS + u
                alive = alive_ref[jb]
                with_prev = alive & plane_ref[b + 1, jb]
                above_ref[jb] = above_ref[jb] | jnp.where(take_prev, 0, with_prev)
                alive = jnp.where(take_prev, with_prev, alive ^ with_prev)
                alive_ref[jb] = alive
                cnts[u] = cnts[u] + lax.population_count(alive & plane_ref[b, jb])
            return tuple(cnts)

        zeros = jnp.zeros((8, DQ), jnp.int32)
        cnts = lax.fori_loop(0, n_groups, body, (zeros,) * CNT_BLOCKS)
        n_one = jnp.sum(sum(cnts[1:], cnts[0]), axis=0, keepdims=True)
        take = (n_above + n_one) >= k_sel
        n_above = jnp.where(take, n_above, n_above + n_one)
        thr_u = jnp.where(take, thr_u | jnp.left_shift(jnp.int32(1), b), thr_u)
        return take.astype(jnp.int32), n_above, thr_u

    row0 = jnp.zeros((1, DQ), jnp.int32)
    took_last, n_above, thr_u = lax.fori_loop(0, 32, select_pass, (row0 + 1, row0, row0))

    def settle(g, cnts):
        cnts = list(cnts)
        for u in range(CNT_BLOCKS):
            jb = g * CNT_BLOCKS + u
            alive = alive_ref[jb]
            with_last = alive & plane_ref[0, jb]
            above_ref[jb] = above_ref[jb] | jnp.where(took_last != 0, 0, with_last)
            alive = jnp.where(took_last != 0, with_last, alive ^ with_last)
            alive_ref[jb] = alive
            cnts[u] = cnts[u] + lax.population_count(alive)
        return tuple(cnts)

    zeros = jnp.zeros((8, DQ), jnp.int32)
    cnts = lax.fori_loop(0, n_groups, settle, (zeros,) * CNT_BLOCKS)
    n_tied = jnp.sum(sum(cnts[1:], cnts[0]), axis=0, keepdims=True)
    need = k_sel - n_above
    real = thr_u != 0
    extra = jnp.logical_and(n_tied > need, real)

    @pl.when(jnp.max(extra.astype(jnp.int32)) > 0)
    def _():
        _keep_lowest_ties(alive_ref, kept_ref, need, n_groups)

    def finalize(jb, carry):
        sel_ref[jb] = above_ref[jb] | jnp.where(real, alive_ref[jb], 0)
        return carry

    lax.fori_loop(0, nkb, finalize, 0)


def _dsa(proj3, ik3, avt4, iwt, bias_tiles, k_sel):
    bsz, seq, _ = proj3.shape
    nq = seq // DQ
    assert seq % (NK * CNT_BLOCKS) == 0
    resident = dict(pipeline_mode=pl.Buffered(1))
    return pl.pallas_call(
        functools.partial(_dsa_kernel, k_sel=k_sel),
        name="dsa",
        grid=(bsz, nq + 1),
        in_specs=[
            pl.BlockSpec((None, DQ, 512), lambda b, s: (b, jnp.maximum(s - 1, 0), COL_AQ)),
            pl.BlockSpec((None, DQ, 512), lambda b, s: (b, jnp.minimum(s, nq - 1), COL_IQ)),
            pl.BlockSpec((IDX_HEADS, DQ), lambda b, s: (0, b * nq + jnp.minimum(s, nq - 1))),
            pl.BlockSpec((None, seq, 512), lambda b, i: (b, 0, COL_AK), **resident),
            pl.BlockSpec((None, seq // NK, 512, NK), lambda b, i: (b, 0, 0, 0), **resident),
            pl.BlockSpec((None, seq, N_IK), lambda b, i: (b, 0, 0), **resident),
            pl.BlockSpec(bias_tiles.shape, lambda b, i: (0, 0, 0, 0), **resident),
        ],
        out_specs=pl.BlockSpec((None, DQ, 512), lambda b, s: (b, jnp.maximum(s - 1, 0), 0)),
        out_shape=jax.ShapeDtypeStruct((bsz, seq, 512), BF16),
        scratch_shapes=[
            pltpu.VMEM((2, NK, DQ), jnp.int32),
            pltpu.VMEM((33, seq // NK, 8, DQ), jnp.int32),
            pltpu.VMEM((seq // NK, 8, DQ), jnp.int32),
            pltpu.VMEM((seq // NK, 8, DQ), jnp.int32),
            pltpu.VMEM((seq // NK, 8, DQ), jnp.int32),
            pltpu.VMEM((seq // NK, 8, DQ), jnp.int32),
            pltpu.VMEM((N_HEADS, DQ, 128), BF16),
            pltpu.VMEM((N_HEADS, 1, DQ), F32),
            pltpu.VMEM((N_HEADS, HEAD_DIM + ONES_ROWS, DQ), F32),
            pltpu.VMEM((N_HEADS, NK, DQ), BF16),
            pltpu.VMEM((N_HEADS, 1, DQ), F32),
            pltpu.VMEM((N_HEADS, NK, DQ), BF16),
            pltpu.VMEM((N_HEADS, 1, DQ), F32),
            pltpu.VMEM((IDX_HEADS, NK, DQ), F32),
            pltpu.VMEM((IDX_HEADS, NK, DQ), F32),
            pltpu.VMEM((16, 8, 128), jnp.int32),
        ],
        compiler_params=pltpu.CompilerParams(
            dimension_semantics=("arbitrary", "arbitrary"), vmem_limit_bytes=VMEM_LIMIT),
    )(proj3, proj3, iwt, proj3, avt4, ik3, bias_tiles)


SB_DEAD_MASS = 104.0 * LOG2E


def _sb_kernel(q_ref, k_ref, v_ref, o_ref, qm_ref, uu_ref, carry_ref, acc_ref, z_ref, sp_ref,
               later_ref):
    i = pl.program_id(1)
    diag = (i * QB + QB - 1) // NK

    @pl.when(jnp.logical_and(pl.program_id(0) == 0, i == 0))
    def _():
        kr = lax.broadcasted_iota(jnp.int32, (NK, NK), 0)
        kc = lax.broadcasted_iota(jnp.int32, (NK, NK), 1)
        uu_ref[...] = jnp.where(kr > kc, 1.0, 0.0).astype(BF16)

    _split_heads_into(qm_ref, q_ref[...])
    carry_ref[...] = jnp.zeros(carry_ref.shape, F32)
    acc_ref[...] = jnp.zeros(acc_ref.shape, F32)

    def block(jb, on_diagonal):
        k0 = pl.multiple_of(jb * NK, NK)
        if on_diagonal:
            row = lax.broadcasted_iota(jnp.int32, (2 * QB, NK), 0)
            col = lax.broadcasted_iota(jnp.int32, (2 * QB, NK), 1)
            causal = (k0 + col) < (i * QB + row % QB)
        for p in range(N_PAIR):
            kp = k_ref[pl.ds(k0, NK), p * 128:(p + 1) * 128]
            z_ref[p] = lax.dot_general(qm_ref[p], kp, _NT, preferred_element_type=F32)
        for p in range(N_PAIR):
            z = z_ref[p]
            neg_abs = lax.bitcast_convert_type(
                lax.bitcast_convert_type(z, jnp.int32) | INT_MIN, F32)
            sp = jnp.maximum(z, 0.0) + jnp.log(1.0 + jnp.exp2(neg_abs)) * LOG2E
            if on_diagonal:
                sp = jnp.where(causal, sp, 0.0)
            sp_ref[p] = sp
            later_ref[p] = jnp.dot(sp.astype(BF16), uu_ref[...], preferred_element_type=F32)
        for p in range(N_PAIR):
            vp = v_ref[pl.ds(k0, NK), p * 128:(p + 1) * 128]
            carry = carry_ref[p]
            sp = sp_ref[p]
            a = jnp.exp2(z_ref[p] - sp - later_ref[p] - carry)
            if on_diagonal:
                a = jnp.where(causal, a, 0.0)
            acc_ref[p] += jnp.dot(a.astype(BF16), vp, preferred_element_type=F32)
            carry_ref[p] = carry + jnp.sum(sp, axis=1, keepdims=True)

    block(diag, True)

    def alive():
        return (jnp.min(carry_ref[...]) <= SB_DEAD_MASS).astype(jnp.int32)

    def cond(state):
        jb, go = state
        return jnp.logical_and(jb >= 0, go > 0)

    def body(state):
        jb, _ = state
        block(jb, False)
        return jb - 1, alive()

    lax.while_loop(cond, body, (diag - 1, alive()))

    for p in range(N_PAIR):
        o_ref[:, p * 128:(p + 1) * 128] = _merge_pair(
            acc_ref[p, :QB, :], acc_ref[p, QB:, :]).astype(BF16)


def _stick_breaking(proj3):
    bsz, seq, _ = proj3.shape
    resident = dict(pipeline_mode=pl.Buffered(1))
    return pl.pallas_call(
        _sb_kernel,
        name="stick_breaking",
        grid=(bsz, seq // QB),
        in_specs=[
            pl.BlockSpec((None, QB, 512), lambda b, i: (b, i, COL_BQ)),
            pl.BlockSpec((None, seq, 512), lambda b, i: (b, 0, COL_BK), **resident),
            pl.BlockSpec((None, seq, 512), lambda b, i: (b, 0, COL_BV), **resident),
        ],
        out_specs=pl.BlockSpec((None, QB, 512), lambda b, i: (b, i, 0)),
        out_shape=jax.ShapeDtypeStruct((bsz, seq, 512), BF16),
        scratch_shapes=[
            pltpu.VMEM((N_PAIR, 2 * QB, 128), BF16),
            pltpu.VMEM((NK, NK), BF16),
            pltpu.VMEM((N_PAIR, 2 * QB, 1), F32),
            pltpu.VMEM((N_PAIR, 2 * QB, 128), F32),
            pltpu.VMEM((N_PAIR, 2 * QB, NK), F32),
            pltpu.VMEM((N_PAIR, 2 * QB, NK), F32),
            pltpu.VMEM((N_PAIR, 2 * QB, NK), F32),
        ],
        compiler_params=pltpu.CompilerParams(
            dimension_semantics=("arbitrary", "arbitrary"), vmem_limit_bytes=VMEM_LIMIT),
    )(proj3, proj3, proj3)


MERGE_ROWS = 512


def _merge_kernel(x_ref, ya_ref, yb_ref, cq_ref, g0_ref, g1_ref, g2_ref, mk_ref, mv_ref,
                  wa_ref, wb_ref, wc_ref, wo_ref, gp_ref, o_ref):
    n_rows = x_ref.shape[0]
    for r0 in range(0, n_rows, MERGE_ROWS):
        rows = slice(r0, r0 + MERGE_ROWS)
        heads = []
        for h in range(C_HEADS):
            sl = slice(h * C_HEAD_DIM, (h + 1) * C_HEAD_DIM)
            s = lax.dot_general(cq_ref[rows, sl], mk_ref[:, sl], _NT,
                                preferred_element_type=F32) * (C_HEAD_DIM ** -0.5)
            e = jnp.exp(s - jnp.max(s, axis=1, keepdims=True))
            p = e / jnp.sum(e, axis=1, keepdims=True)
            heads.append(jnp.dot(p.astype(BF16), mv_ref[:, sl], preferred_element_type=F32))
        yc_pre = jnp.concatenate(heads, axis=1).astype(BF16)
        ya = jnp.dot(ya_ref[rows, :], wa_ref[...], preferred_element_type=F32)
        yb = jnp.dot(yb_ref[rows, :], wb_ref[...], preferred_element_type=F32)
        yc = jnp.dot(yc_pre, wc_ref[...], preferred_element_type=F32)
        merged = (g0_ref[rows, :].astype(F32) * ya + g1_ref[rows, :].astype(F32) * yb
                  + g2_ref[rows, :].astype(F32) * yc)
        o = jnp.dot(merged.astype(BF16), wo_ref[...], preferred_element_type=F32)
        o_ref[rows, :] = x_ref[rows, :] + _rms(o, gp_ref[...])


def _merge(x2, ya2, yb2, proj2, mkv3, wa, wb, wc, wo, g_post, seq):
    n = x2.shape[0]
    tm = min(2 * MERGE_ROWS, seq)
    per_batch = seq // tm
    n_mem = mkv3.shape[1]
    c_dim = C_HEADS * C_HEAD_DIM
    const = lambda t: (0, 0)
    return pl.pallas_call(
        _merge_kernel,
        name="merge",
        grid=(n // tm,),
        in_specs=[
            pl.BlockSpec((tm, D_MODEL), lambda t: (t, 0)),
            pl.BlockSpec((tm, 512), lambda t: (t, 0)),
            pl.BlockSpec((tm, 512), lambda t: (t, 0)),
            pl.BlockSpec((tm, 512), lambda t: (t, COL_CQ)),
            pl.BlockSpec((tm, D_MODEL), lambda t: (t, 0)),
            pl.BlockSpec((tm, D_MODEL), lambda t: (t, 1)),
            pl.BlockSpec((tm, D_MODEL), lambda t: (t, 2)),
            pl.BlockSpec((None, n_mem, c_dim), lambda t: (t // per_batch, 0, 0)),
            pl.BlockSpec((None, n_mem, c_dim), lambda t: (t // per_batch, 0, 1)),
            pl.BlockSpec(wa.shape, const),
            pl.BlockSpec(wb.shape, const),
            pl.BlockSpec(wc.shape, const),
            pl.BlockSpec(wo.shape, const),
            pl.BlockSpec((1, D_MODEL), const),
        ],
        out_specs=pl.BlockSpec((tm, D_MODEL), lambda t: (t, 0)),
        out_shape=jax.ShapeDtypeStruct((n, D_MODEL), F32),
        compiler_params=pltpu.CompilerParams(
            dimension_semantics=("arbitrary",), vmem_limit_bytes=VMEM_LIMIT),
    )(x2, ya2, yb2, proj2, proj2, proj2, proj2, mkv3, mkv3, wa, wb, wc, wo, g_post)


FFN_ROWS = 512


def _ffn_kernel(x_ref, gpre_ref, wg_ref, wu_ref, wo_ref, gpost_ref, o_ref):
    for r0 in range(0, x_ref.shape[0], FFN_ROWS):
        rows = slice(r0, min(r0 + FFN_ROWS, x_ref.shape[0]))
        x = x_ref[rows, :]
        h = _rms(x, gpre_ref[...]).astype(BF16)
        g = jnp.dot(h, wg_ref[...], preferred_element_type=F32)
        u = jnp.dot(h, wu_ref[...], preferred_element_type=F32)
        act = (g * jax.nn.sigmoid(g) * u).astype(BF16)
        f = jnp.dot(act, wo_ref[...], preferred_element_type=F32)
        o_ref[rows, :] = x + _rms(f, gpost_ref[...])


def _ffn(x2, g_pre, w_in, wo, g_post):
    n = x2.shape[0]
    d_ff = wo.shape[0]
    tm = min(2 * FFN_ROWS, n)
    const = lambda t: (0, 0)
    resident = dict(pipeline_mode=pl.Buffered(1))
    return pl.pallas_call(
        _ffn_kernel,
        name="ffn",
        grid=(n // tm,),
        in_specs=[
            pl.BlockSpec((tm, D_MODEL), lambda t: (t, 0)),
            pl.BlockSpec((1, D_MODEL), const),
            pl.BlockSpec((D_MODEL, d_ff), lambda t: (0, 0), **resident),
            pl.BlockSpec((D_MODEL, d_ff), lambda t: (0, 1), **resident),
            pl.BlockSpec(wo.shape, const, **resident),
            pl.BlockSpec((1, D_MODEL), const),
        ],
        out_specs=pl.BlockSpec((tm, D_MODEL), lambda t: (t, 0)),
        out_shape=jax.ShapeDtypeStruct((n, D_MODEL), F32),
        compiler_params=pltpu.CompilerParams(
            dimension_semantics=("arbitrary",), vmem_limit_bytes=VMEM_LIMIT),
    )(x2, g_pre, w_in, w_in, wo, g_post)


def _pack_w_in(w):
    sizes = (512, 512, 512, IDX_HEADS * 64, 64, IDX_HEADS, 512, 512, 512, 512,
             N_BRANCH * D_MODEL)
    aq, ak, av, iq, ik, iw, bq, bk, bv, cq, gates = jnp.split(w, np.cumsum(sizes)[:-1], axis=1)
    scale = HEAD_DIM ** -0.5
    scale2 = scale * LOG2E
    w_main = jnp.concatenate(
        [gates, aq * scale2, ak, iq * scale, bq * scale2, bk, bv, cq], axis=1).astype(BF16)
    z64 = jnp.zeros((D_MODEL, 64), F32)
    w_ik = jnp.concatenate([ik, z64, z64, ik], axis=1).astype(BF16)
    w_trans = jnp.concatenate(
        [av, iw, jnp.zeros((D_MODEL, N_TRANS - 512 - IDX_HEADS), F32)], axis=1).T.astype(BF16)
    return w_main, w_ik, w_trans


def kernel(x, mem, rel_bias, g_mix_pre, w_in, b_gate, g_mem, w_mem_kv, w_up_a, w_up_b, w_up_c,
           w_out, g_mix_post, g_ffn_pre, w_ffn_in, w_ffn_out, g_ffn_post):
    bsz, seq, _ = x.shape
    n_mem = mem.shape[1]
    k_sel = min(TOPK_MAX, seq // 4)
    bias_tiles = _bias_tiles(rel_bias)
    x2 = x.reshape(bsz * seq, D_MODEL)
    for l in range(w_in.shape[0]):
        w_main, w_ik, w_trans = _pack_w_in(w_in[l])
        proj2, ik2, avt, iwt = _project(x2, g_mix_pre[l][None, :], w_main, b_gate[l][None, :],
                                        w_ik, w_trans)
        proj3 = proj2.reshape(bsz, seq, N_MAIN)
        mkv = _memkv(mem.reshape(bsz * n_mem, D_MODEL), g_mem[l][None, :],
                     w_mem_kv[l].astype(BF16))
        ya = _dsa(proj3, ik2.reshape(bsz, seq, N_IK), avt.reshape(bsz, seq // NK, 512, NK),
                  iwt, bias_tiles, k_sel)
        yb = _stick_breaking(proj3)
        x2 = _merge(x2, ya.reshape(bsz * seq, 512), yb.reshape(bsz * seq, 512), proj2,
                    mkv.reshape(bsz, n_mem, 2 * C_HEADS * C_HEAD_DIM),
                    w_up_a[l].astype(BF16), w_up_b[l].astype(BF16), w_up_c[l].astype(BF16),
                    w_out[l].astype(BF16), g_mix_post[l][None, :], seq)
        x2 = _ffn(x2, g_ffn_pre[l][None, :], w_ffn_in[l].astype(BF16),
                  w_ffn_out[l].astype(BF16), g_ffn_post[l][None, :])
    return x2.reshape(bsz, seq, D_MODEL)
```

```python
import functools

import numpy as np
import jax
import jax.numpy as jnp
from jax import lax
from jax.experimental import pallas as pl
from jax.experimental.pallas import tpu as pltpu

D_MODEL = 1024
CHUNK = 64
HEAD_DIM = 64
N_HEADS = 8
IDX_HEADS = 8
TOPK_MAX = 256
C_HEADS = 4
C_HEAD_DIM = 128
N_BRANCH = 3
REL_BUCKETS = 32
EPS = 1e-6

F32 = jnp.float32
BF16 = jnp.bfloat16
INT_MIN = -2 ** 31
NEG_BIG = -1e30
LOG2E = 1.4426950408889634

QB = 256
DQ = 256
NK = 256
N_PAIR = N_HEADS // 2
ONES_ROWS = 16

N_GATE = N_BRANCH * D_MODEL
COL_AQ, COL_AK, COL_IQ, COL_BQ, COL_BK, COL_BV, COL_CQ = range(N_GATE // 512, N_GATE // 512 + 7)
N_MAIN = N_GATE + 7 * 512
N_IK = 256
N_TRANS = 512 + 16

V7X_VMEM_BYTES = 64 * 1024 * 1024
VMEM_LIMIT = V7X_VMEM_BYTES * 7 // 8

_NT = (((1,), (1,)), ((), ()))


def _rms(x, g):
    return x * lax.rsqrt(jnp.mean(x * x, axis=-1, keepdims=True) + EPS) * g


PROJ_ROWS = 512
PROJ_COLS = 512


def _proj_kernel(x_ref, g_ref, w_ref, b_ref, wik_ref, wt_ref, o_ref, ik_ref, avt_ref, iwt_ref):
    hb = _rms(x_ref[...], g_ref[...]).astype(BF16)
    ik_ref[...] = jnp.dot(hb, wik_ref[...], preferred_element_type=F32).astype(BF16)
    tr = lax.dot_general(wt_ref[...], hb, _NT, preferred_element_type=F32)
    for c in range(avt_ref.shape[0]):
        avt_ref[c] = tr[:512, c * NK:(c + 1) * NK].astype(BF16)
    iwt_ref[...] = tr[512:512 + IDX_HEADS, :]
    for c0 in range(0, N_MAIN, PROJ_COLS):
        cols = slice(c0, c0 + PROJ_COLS)
        acc = jnp.dot(hb, w_ref[:, cols], preferred_element_type=F32)
        if c0 < N_GATE:
            acc = 0.5 + 0.5 * jnp.tanh(0.5 * (acc + b_ref[:, cols]))
        o_ref[:, cols] = acc.astype(BF16)


def _project(x2, g, w_main, b_gate, w_ik, w_trans):
    n = x2.shape[0]
    tm = min(PROJ_ROWS, n)
    const = lambda i: (0, 0)
    resident = dict(pipeline_mode=pl.Buffered(1))
    return pl.pallas_call(
        _proj_kernel,
        name="in_proj",
        grid=(n // tm,),
        in_specs=[
            pl.BlockSpec((tm, D_MODEL), lambda i: (i, 0)),
            pl.BlockSpec((1, D_MODEL), const),
            pl.BlockSpec(w_main.shape, const, **resident),
            pl.BlockSpec(b_gate.shape, const),
            pl.BlockSpec(w_ik.shape, const, **resident),
            pl.BlockSpec(w_trans.shape, const, **resident),
        ],
        out_specs=[
            pl.BlockSpec((tm, N_MAIN), lambda i: (i, 0)),
            pl.BlockSpec((tm, N_IK), lambda i: (i, 0)),
            pl.BlockSpec((tm // NK, 512, NK), lambda i: (i, 0, 0)),
            pl.BlockSpec((IDX_HEADS, tm), lambda i: (0, i)),
        ],
        out_shape=[
            jax.ShapeDtypeStruct((n, N_MAIN), BF16),
            jax.ShapeDtypeStruct((n, N_IK), BF16),
            jax.ShapeDtypeStruct((n // NK, 512, NK), BF16),
            jax.ShapeDtypeStruct((IDX_HEADS, n), F32),
        ],
        compiler_params=pltpu.CompilerParams(
            dimension_semantics=("arbitrary",), vmem_limit_bytes=VMEM_LIMIT),
    )(x2, g, w_main, b_gate, w_ik, w_trans)


def _memkv_kernel(x_ref, g_ref, w_ref, o_ref):
    hb = _rms(x_ref[...], g_ref[...]).astype(BF16)
    o_ref[...] = jnp.dot(hb, w_ref[...], preferred_element_type=F32).astype(BF16)


def _memkv(mem2, g, w):
    n = mem2.shape[0]
    tm = min(512, n)
    return pl.pallas_call(
        _memkv_kernel,
        name="mem_kv",
        grid=(n // tm,),
        in_specs=[
            pl.BlockSpec((tm, D_MODEL), lambda i: (i, 0)),
            pl.BlockSpec((1, D_MODEL), lambda i: (0, 0)),
            pl.BlockSpec((D_MODEL, w.shape[1]), lambda i: (0, 0)),
        ],
        out_specs=pl.BlockSpec((tm, w.shape[1]), lambda i: (i, 0)),
        out_shape=jax.ShapeDtypeStruct((n, w.shape[1]), BF16),
        compiler_params=pltpu.CompilerParams(
            dimension_semantics=("arbitrary",), vmem_limit_bytes=VMEM_LIMIT),
    )(mem2, g, w)


BIAS_OFFSETS = (0, -NK)
_LOG_BUCKET_STARTS = (12, 16, 23, 32, 46, 64, 91)
FAR_BUCKET = 15
assert DQ == NK


def _bias_kernel(rb_ref, o_ref):
    key = lax.broadcasted_iota(jnp.int32, (NK, DQ), 0)
    qry = lax.broadcasted_iota(jnp.int32, (NK, DQ), 1)
    for c, off in enumerate(BIAS_OFFSETS):
        rel = key - qry + off
        n = jnp.abs(rel)
        large = jnp.full((NK, DQ), 8, jnp.int32)
        for start in _LOG_BUCKET_STARTS:
            large = large + jnp.where(n >= start, 1, 0)
        bucket = jnp.where(rel > 0, REL_BUCKETS // 2, 0) + jnp.where(n < 8, n, large)
        for h in range(N_HEADS):
            val = jnp.full((NK, DQ), rb_ref[0, h], F32)
            for b in range(1, REL_BUCKETS):
                val = jnp.where(bucket == b, rb_ref[b, h], val)
            o_ref[c, h] = (val - rb_ref[FAR_BUCKET, h]) * LOG2E


def _bias_tiles(rel_bias):
    return pl.pallas_call(
        _bias_kernel,
        name="rel_bias_tiles",
        in_specs=[pl.BlockSpec(memory_space=pltpu.SMEM)],
        out_specs=pl.BlockSpec(memory_space=pltpu.VMEM),
        out_shape=jax.ShapeDtypeStruct((len(BIAS_OFFSETS), N_HEADS, NK, DQ), F32),
    )(rel_bias)


def _split_heads_into(qm_ref, q):
    lane = lax.broadcasted_iota(jnp.int32, (QB, 128), 1)
    for p in range(N_PAIR):
        qp = q[:, p * 128:(p + 1) * 128].astype(F32)
        qm_ref[p, :QB, :] = jnp.where(lane < HEAD_DIM, qp, 0.0).astype(BF16)
        qm_ref[p, QB:, :] = jnp.where(lane >= HEAD_DIM, qp, 0.0).astype(BF16)


def _merge_pair(o_even, o_odd):
    lane = lax.broadcasted_iota(jnp.int32, (QB, 128), 1)
    return jnp.where(lane < HEAD_DIM, o_even, o_odd)


CNT_BLOCKS = 4
KEY_ROWS = 32
assert NK == 8 * 32


def _bit_transpose32(load_row, tmp_ref, store_row):
    def swap(a, b, j, m):
        t = (a ^ lax.shift_right_logical(b, jnp.int32(j))) & m
        return a ^ t, b ^ (t << j)

    lower = []
    for k in range(16):
        a, b = swap(load_row(k), load_row(k + 16), 16, 0x0000FFFF)
        lower.append(a)
        tmp_ref[k] = b
    for base in (0, 16):
        x = lower if base == 0 else [tmp_ref[k] for k in range(16)]
        j, m = 8, 0x00FF00FF
        while j:
            k = 0
            while k < 16:
                x[k], x[k + j] = swap(x[k], x[k + j], j, m)
                k = (k + j + 1) & ~j
            j >>= 1
            m ^= m << j
        for i in range(16):
            store_row(base + i, x[i])


def _keep_lowest_ties(alive_ref, kept_ref, need, n_groups):
    n_blocks = alive_ref.shape[0]
    sub = lax.broadcasted_iota(jnp.int32, (8, DQ), 0)
    zero_masks = []
    for c in reversed(range(max(n_blocks - 1, 1).bit_length())):
        zero_masks.append(
            lambda jb, c=c: jnp.where(((jb >> c) & 1) == 0, jnp.int32(-1), jnp.int32(0)))
    for word in (0xFFFF0000, 0xFF00FF00, 0xF0F0F0F0, 0xCCCCCCCC, 0xAAAAAAAA):
        zero_masks.append(lambda jb, word=word: jnp.int32(word - (1 << 32)))
    for c in (2, 1, 0):
        zero_masks.append(lambda jb, c=c: jnp.where(((sub >> c) & 1) == 0, -1, 0))

    remaining = need
    took_zero = None
    for p, zero_mask in enumerate(zero_masks):
        prev_mask = zero_masks[p - 1] if p else None

        def body(g, cnts, zero_mask=zero_mask, prev_mask=prev_mask, took_zero=took_zero):
            cnts = list(cnts)
            for u in range(CNT_BLOCKS):
                jb = g * CNT_BLOCKS + u
                alive = alive_ref[jb]
                if prev_mask is None:
                    kept_ref[jb] = jnp.zeros((8, DQ), jnp.int32)
                else:
                    lows = alive & prev_mask(jb)
                    kept_ref[jb] = kept_ref[jb] | jnp.where(took_zero, 0, lows)
                    alive = jnp.where(took_zero, lows, alive ^ lows)
                    alive_ref[jb] = alive
                cnts[u] = cnts[u] + lax.population_count(alive & zero_mask(jb))
            return tuple(cnts)

        zeros = jnp.zeros((8, DQ), jnp.int32)
        cnts = lax.fori_loop(0, n_groups, body, (zeros,) * CNT_BLOCKS)
        n_zero = jnp.sum(sum(cnts[1:], cnts[0]), axis=0, keepdims=True)
        took_zero = n_zero >= remaining
        remaining = jnp.where(took_zero, remaining, remaining - n_zero)

    last_mask = zero_masks[-1]

    def finish(jb, carry):
        alive = alive_ref[jb]
        alive_ref[jb] = kept_ref[jb] | jnp.where(took_zero, alive & last_mask(jb), alive)
        return carry

    lax.fori_loop(0, n_groups * CNT_BLOCKS, finish, 0)


def _dsa_kernel(aq_ref, iq_ref, iwt_ref, ak_ref, avt_ref, ik_ref, bias_ref, o_ref,
                keybuf_ref, plane_ref, alive_ref, above_ref, kept_ref, sel_ref, qm_ref, m_ref,
                acc_ref, s0_ref, cm0_ref, s1_ref, cm1_ref, raw0_ref, raw1_ref, tmp_ref, *,
                k_sel):
    step = pl.program_id(1)
    n_q = pl.num_programs(1) - 1
    att = step - 1
    krow = lax.broadcasted_iota(jnp.int32, (KEY_ROWS, DQ), 0)
    qcol = lax.broadcasted_iota(jnp.int32, (KEY_ROWS, DQ), 1)
    qchunk = (step * DQ + qcol) // CHUNK

    iwt = iwt_ref[...] * (IDX_HEADS ** -0.5)

    def dots_stage(jb, slot):
        raw_ref = (raw0_ref, raw1_ref)[slot]
        k0 = pl.multiple_of(jb * NK, NK)
        for h in range(IDX_HEADS):
            ikh = ik_ref[pl.ds(k0, NK), (h % 2) * 128:(h % 2 + 1) * 128]
            iqp = iq_ref[:, (h // 2) * 128:(h // 2 + 1) * 128]
            raw_ref[h] = lax.dot_general(ikh, iqp, _NT, preferred_element_type=F32)

    def keys_stage(jb, slot, last):
        raw_ref = (raw0_ref, raw1_ref)[slot]
        k0 = pl.multiple_of(jb * NK, NK)
        for c in range(NK // KEY_ROWS):
            rows = slice(c * KEY_ROWS, (c + 1) * KEY_ROWS)
            acc = jnp.zeros((KEY_ROWS, DQ), F32)
            for h in range(IDX_HEADS):
                acc = acc + iwt[h:h + 1, :] * jnp.maximum(raw_ref[h, rows, :], 0.0)
            bits = lax.bitcast_convert_type(acc, jnp.int32)
            sign = bits >> 31
            key = (bits ^ (sign & 0x7FFFFFFF)) - sign
            ukey = key ^ INT_MIN
            if last:
                admissible = ((k0 + c * KEY_ROWS + krow) // CHUNK) <= qchunk
                ukey = jnp.where(admissible, ukey, 0)
            keybuf_ref[slot, rows, :] = ukey
        for half in range(DQ // 128):
            lanes = slice(half * 128, (half + 1) * 128)

            def load_row(r):
                return keybuf_ref[slot, 8 * r:8 * r + 8, lanes]

            def store_plane(i, v):
                plane_ref[31 - i, jb, :, lanes] = v

            _bit_transpose32(load_row, tmp_ref, store_plane)
        plane_ref[32, jb] = jnp.full((8, DQ), -1, jnp.int32)
        alive_ref[jb] = jnp.full((8, DQ), -1, jnp.int32)
        above_ref[jb] = jnp.zeros((8, DQ), jnp.int32)

    ones = jnp.ones((ONES_ROWS, NK), BF16)
    slots = ((s0_ref, cm0_ref), (s1_ref, cm1_ref))

    def logits_stage(jb, slot, bias_idx):
        s_ref, cm_ref = slots[slot]
        k0 = pl.multiple_of(jb * NK, NK)
        sel = sel_ref[jb]
        mask = jnp.concatenate(
            [jnp.where((sel << r) < 0, 0.0, NEG_BIG)
             for r in range(32)], axis=0).astype(BF16)
        for h in range(N_HEADS):
            kp = ak_ref[pl.ds(k0, NK), (h // 2) * 128:(h // 2 + 1) * 128]
            s = lax.dot_general(kp, qm_ref[h], _NT, preferred_element_type=F32)
            if bias_idx is not None:
                s = s + bias_ref[bias_idx, h]
            sb = s.astype(BF16) + mask
            s_ref[h] = sb
            cm_ref[h] = jnp.max(sb, axis=0, keepdims=True).astype(F32)

    def softmax_stage(jb, slot):
        s_ref, cm_ref = slots[slot]
        for h in range(N_HEADS):
            m_prev = m_ref[h]
            m_new = jnp.maximum(m_prev, cm_ref[h])
            alpha = jnp.exp2(m_prev - m_new)
            pe = jnp.exp2(s_ref[h] - m_new.astype(BF16))
            vt = jnp.concatenate([avt_ref[jb, h * HEAD_DIM:(h + 1) * HEAD_DIM, :], ones], axis=0)
            acc_ref[h] = alpha * acc_ref[h] + jnp.dot(vt, pe, preferred_element_type=F32)
            m_ref[h] = m_new

    def start_attention():
        lane = lax.broadcasted_iota(jnp.int32, (DQ, 128), 1)
        for p in range(N_PAIR):
            qp = aq_ref[:, p * 128:(p + 1) * 128].astype(F32)
            qm_ref[2 * p] = jnp.where(lane < HEAD_DIM, qp, 0.0).astype(BF16)
            qm_ref[2 * p + 1] = jnp.where(lane >= HEAD_DIM, qp, 0.0).astype(BF16)
        m_ref[...] = jnp.full(m_ref.shape, NEG_BIG, F32)
        acc_ref[...] = jnp.zeros(acc_ref.shape, F32)
        logits_stage(att, 0, 0)
        logits_stage(jnp.maximum(att - 1, 0), 1, 1)
        softmax_stage(att, 0)

    @pl.when(step == 0)
    def _():
        dots_stage(0, 0)
        keys_stage(0, 0, True)

    @pl.when(step == n_q)
    def _():
        start_attention()
        n_steps = att - 1

        def two_steps(u, carry):
            b = att - 1 - 2 * u
            logits_stage(b - 1, 0, None)
            softmax_stage(b, 1)
            logits_stage(b - 2, 1, None)
            softmax_stage(b - 1, 0)
            return carry

        lax.fori_loop(0, jnp.maximum(n_steps, 0) // 2, two_steps, 0)

        @pl.when(jnp.logical_and(n_steps >= 1, n_steps % 2 == 1))
        def _():
            logits_stage(0, 0, None)
            softmax_stage(1, 1)

        @pl.when(jnp.logical_and(att >= 1, att % 2 == 1))
        def _():
            softmax_stage(0, 1)

        @pl.when(jnp.logical_and(att >= 1, att % 2 == 0))
        def _():
            softmax_stage(0, 0)

    @pl.when(jnp.logical_and(step >= 1, step < n_q))
    def _():
        start_attention()
        dots_stage(0, 0)
        dots_stage(1, 1)
        keys_stage(0, 0, False)
        n_fused = jnp.maximum(att - 1, 0) // 2

        def fused_trip(u, carry):
            b = att - 1 - 2 * u
            j = 2 * u + 1
            logits_stage(b - 1, 0, None)
            dots_stage(j + 1, 0)
            softmax_stage(b, 1)
            keys_stage(j, 1, False)
            logits_stage(b - 2, 1, None)
            dots_stage(j + 2, 1)
            softmax_stage(b - 1, 0)
            keys_stage(j + 1, 0, False)
            return carry

        lax.fori_loop(0, n_fused, fused_trip, 0)

        @pl.when(att == 0)
        def _():
            keys_stage(1, 1, True)

        @pl.when(att % 2 == 1)
        def _():
            dots_stage(att + 1, 0)
            softmax_stage(0, 1)
            keys_stage(att, 1, False)
            keys_stage(att + 1, 0, True)

        @pl.when(jnp.logical_and(att >= 2, att % 2 == 0))
        def _():
            logits_stage(0, 0, None)
            dots_stage(att, 0)
            softmax_stage(1, 1)
            keys_stage(att - 1, 1, False)
            dots_stage(att + 1, 1)
            softmax_stage(0, 0)
            keys_stage(att, 0, False)
            keys_stage(att + 1, 1, True)

    @pl.when(step >= 1)
    def _():
        for p in range(N_PAIR):
            halves = []
            for h in (2 * p, 2 * p + 1):
                a = acc_ref[h]
                halves.append(a[:HEAD_DIM, :] / a[HEAD_DIM:HEAD_DIM + 1, :])
            o_ref[:, p * 128:(p + 1) * 128] = jnp.concatenate(halves, axis=0).T.astype(BF16)

    nkb = jnp.where(step < n_q, step + 1, 0)
    n_groups = (nkb + CNT_BLOCKS - 1) // CNT_BLOCKS

    def pad_block(jb, carry):
        for b in range(33):
            plane_ref[b, jb] = jnp.zeros((8, DQ), jnp.int32)
        alive_ref[jb] = jnp.zeros((8, DQ), jnp.int32)
        above_ref[jb] = jnp.zeros((8, DQ), jnp.int32)
        return carry

    lax.fori_loop(nkb, n_groups * CNT_BLOCKS, pad_block, 0)

    def select_pass(it, state):
        took_prev, n_above, thr_u = state
        b = 31 - it
        take_prev = took_prev != 0

        def body(g, cnts):
            cnts = list(cnts)
            for u in range(CNT_BLOCKS):
                jb = g * CNT_BLOCKS + u
                alive = alive_ref[jb]
                with_prev = alive & plane_ref[b + 1, jb]
                above_ref[jb] = above_ref[jb] | jnp.where(take_prev, 0, with_prev)
                alive = jnp.where(take_prev, with_prev, alive ^ with_prev)
                alive_ref[jb] = alive
                cnts[u] = cnts[u] + lax.population_count(alive & plane_ref[b, jb])
            return tuple(cnts)

        zeros = jnp.zeros((8, DQ), jnp.int32)
        cnts = lax.fori_loop(0, n_groups, body, (zeros,) * CNT_BLOCKS)
        n_one = jnp.sum(sum(cnts[1:], cnts[0]), axis=0, keepdims=True)
        take = (n_above + n_one) >= k_sel
        n_above = jnp.where(take, n_above, n_above + n_one)
        thr_u = jnp.where(take, thr_u | jnp.left_shift(jnp.int32(1), b), thr_u)
        return take.astype(jnp.int32), n_above, thr_u

    row0 = jnp.zeros((1, DQ), jnp.int32)
    took_last, n_above, thr_u = lax.fori_loop(0, 32, select_pass, (row0 + 1, row0, row0))

    def settle(g, cnts):
        cnts = list(cnts)
        for u in range(CNT_BLOCKS):
            jb = g * CNT_BLOCKS + u
            alive = alive_ref[jb]
            with_last = alive & plane_ref[0, jb]
            above_ref[jb] = above_ref[jb] | jnp.where(took_last != 0, 0, with_last)
            alive = jnp.where(took_last != 0, with_last, alive ^ with_last)
            alive_ref[jb] = alive
            cnts[u] = cnts[u] + lax.population_count(alive)
        return tuple(cnts)

    zeros = jnp.zeros((8, DQ), jnp.int32)
    cnts = lax.fori_loop(0, n_groups, settle, (zeros,) * CNT_BLOCKS)
    n_tied = jnp.sum(sum(cnts[1:], cnts[0]), axis=0, keepdims=True)
    need = k_sel - n_above
    real = thr_u != 0
    extra = jnp.logical_and(n_tied > need, real)

    @pl.when(jnp.max(extra.astype(jnp.int32)) > 0)
    def _():
        _keep_lowest_ties(alive_ref, kept_ref, need, n_groups)

    def finalize(jb, carry):
        sel_ref[jb] = above_ref[jb] | jnp.where(real, alive_ref[jb], 0)
        return carry

    lax.fori_loop(0, nkb, finalize, 0)


def _dsa(proj3, ik3, avt4, iwt, bias_tiles, k_sel):
    bsz, seq, _ = proj3.shape
    nq = seq // DQ
    assert seq % (NK * CNT_BLOCKS) == 0
    resident = dict(pipeline_mode=pl.Buffered(1))
    return pl.pallas_call(
        functools.partial(_dsa_kernel, k_sel=k_sel),
        name="dsa",
        grid=(bsz, nq + 1),
        in_specs=[
            pl.BlockSpec((None, DQ, 512), lambda b, s: (b, jnp.maximum(s - 1, 0), COL_AQ)),
            pl.BlockSpec((None, DQ, 512), lambda b, s: (b, jnp.minimum(s, nq - 1), COL_IQ)),
            pl.BlockSpec((IDX_HEADS, DQ), lambda b, s: (0, b * nq + jnp.minimum(s, nq - 1))),
            pl.BlockSpec((None, seq, 512), lambda b, i: (b, 0, COL_AK), **resident),
            pl.BlockSpec((None, seq // NK, 512, NK), lambda b, i: (b, 0, 0, 0), **resident),
            pl.BlockSpec((None, seq, N_IK), lambda b, i: (b, 0, 0), **resident),
            pl.BlockSpec(bias_tiles.shape, lambda b, i: (0, 0, 0, 0), **resident),
        ],
        out_specs=pl.BlockSpec((None, DQ, 512), lambda b, s: (b, jnp.maximum(s - 1, 0), 0)),
        out_shape=jax.ShapeDtypeStruct((bsz, seq, 512), BF16),
        scratch_shapes=[
            pltpu.VMEM((2, NK, DQ), jnp.int32),
            pltpu.VMEM((33, seq // NK, 8, DQ), jnp.int32),
            pltpu.VMEM((seq // NK, 8, DQ), jnp.int32),
            pltpu.VMEM((seq // NK, 8, DQ), jnp.int32),
            pltpu.VMEM((seq // NK, 8, DQ), jnp.int32),
            pltpu.VMEM((seq // NK, 8, DQ), jnp.int32),
            pltpu.VMEM((N_HEADS, DQ, 128), BF16),
            pltpu.VMEM((N_HEADS, 1, DQ), F32),
            pltpu.VMEM((N_HEADS, HEAD_DIM + ONES_ROWS, DQ), F32),
            pltpu.VMEM((N_HEADS, NK, DQ), BF16),
            pltpu.VMEM((N_HEADS, 1, DQ), F32),
            pltpu.VMEM((N_HEADS, NK, DQ), BF16),
            pltpu.VMEM((N_HEADS, 1, DQ), F32),
            pltpu.VMEM((IDX_HEADS, NK, DQ), F32),
            pltpu.VMEM((IDX_HEADS, NK, DQ), F32),
            pltpu.VMEM((16, 8, 128), jnp.int32),
        ],
        compiler_params=pltpu.CompilerParams(
            dimension_semantics=("arbitrary", "arbitrary"), vmem_limit_bytes=VMEM_LIMIT),
    )(proj3, proj3, iwt, proj3, avt4, ik3, bias_tiles)


SB_DEAD_MASS = 104.0 * LOG2E


def _sb_kernel(q_ref, k_ref, v_ref, o_ref, qm_ref, uu_ref, carry_ref, acc_ref, z_ref, sp_ref,
               later_ref):
    i = pl.program_id(1)
    diag = (i * QB + QB - 1) // NK

    @pl.when(jnp.logical_and(pl.program_id(0) == 0, i == 0))
    def _():
        kr = lax.broadcasted_iota(jnp.int32, (NK, NK), 0)
        kc = lax.broadcasted_iota(jnp.int32, (NK, NK), 1)
        uu_ref[...] = jnp.where(kr > kc, 1.0, 0.0).astype(BF16)

    _split_heads_into(qm_ref, q_ref[...])
    carry_ref[...] = jnp.zeros(carry_ref.shape, F32)
    acc_ref[...] = jnp.zeros(acc_ref.shape, F32)

    def block(jb, on_diagonal):
        k0 = pl.multiple_of(jb * NK, NK)
        if on_diagonal:
            row = lax.broadcasted_iota(jnp.int32, (2 * QB, NK), 0)
            col = lax.broadcasted_iota(jnp.int32, (2 * QB, NK), 1)
            causal = (k0 + col) < (i * QB + row % QB)
        for p in range(N_PAIR):
            kp = k_ref[pl.ds(k0, NK), p * 128:(p + 1) * 128]
            z_ref[p] = lax.dot_general(qm_ref[p], kp, _NT, preferred_element_type=F32)
        for p in range(N_PAIR):
            z = z_ref[p]
            neg_abs = lax.bitcast_convert_type(
                lax.bitcast_convert_type(z, jnp.int32) | INT_MIN, F32)
            sp = jnp.maximum(z, 0.0) + jnp.log(1.0 + jnp.exp2(neg_abs)) * LOG2E
            if on_diagonal:
                sp = jnp.where(causal, sp, 0.0)
            sp_ref[p] = sp
            later_ref[p] = jnp.dot(sp.astype(BF16), uu_ref[...], preferred_element_type=F32)
        for p in range(N_PAIR):
            vp = v_ref[pl.ds(k0, NK), p * 128:(p + 1) * 128]
            carry = carry_ref[p]
            sp = sp_ref[p]
            a = jnp.exp2(z_ref[p] - sp - later_ref[p] - carry)
            if on_diagonal:
                a = jnp.where(causal, a, 0.0)
            acc_ref[p] += jnp.dot(a.astype(BF16), vp, preferred_element_type=F32)
            carry_ref[p] = carry + jnp.sum(sp, axis=1, keepdims=True)

    block(diag, True)

    def alive():
        return (jnp.min(carry_ref[...]) <= SB_DEAD_MASS).astype(jnp.int32)

    def cond(state):
        jb, go = state
        return jnp.logical_and(jb >= 0, go > 0)

    def body(state):
        jb, _ = state
        block(jb, False)
        return jb - 1, alive()

    lax.while_loop(cond, body, (diag - 1, alive()))

    for p in range(N_PAIR):
        o_ref[:, p * 128:(p + 1) * 128] = _merge_pair(
            acc_ref[p, :QB, :], acc_ref[p, QB:, :]).astype(BF16)


def _stick_breaking(proj3):
    bsz, seq, _ = proj3.shape
    resident = dict(pipeline_mode=pl.Buffered(1))
    return pl.pallas_call(
        _sb_kernel,
        name="stick_breaking",
        grid=(bsz, seq // QB),
        in_specs=[
            pl.BlockSpec((None, QB, 512), lambda b, i: (b, i, COL_BQ)),
            pl.BlockSpec((None, seq, 512), lambda b, i: (b, 0, COL_BK), **resident),
            pl.BlockSpec((None, seq, 512), lambda b, i: (b, 0, COL_BV), **resident),
        ],
        out_specs=pl.BlockSpec((None, QB, 512), lambda b, i: (b, i, 0)),
        out_shape=jax.ShapeDtypeStruct((bsz, seq, 512), BF16),
        scratch_shapes=[
            pltpu.VMEM((N_PAIR, 2 * QB, 128), BF16),
            pltpu.VMEM((NK, NK), BF16),
            pltpu.VMEM((N_PAIR, 2 * QB, 1), F32),
            pltpu.VMEM((N_PAIR, 2 * QB, 128), F32),
            pltpu.VMEM((N_PAIR, 2 * QB, NK), F32),
            pltpu.VMEM((N_PAIR, 2 * QB, NK), F32),
            pltpu.VMEM((N_PAIR, 2 * QB, NK), F32),
        ],
        compiler_params=pltpu.CompilerParams(
            dimension_semantics=("arbitrary", "arbitrary"), vmem_limit_bytes=VMEM_LIMIT),
    )(proj3, proj3, proj3)


MERGE_ROWS = 512


def _merge_kernel(x_ref, ya_ref, yb_ref, cq_ref, g0_ref, g1_ref, g2_ref, mk_ref, mv_ref,
                  wa_ref, wb_ref, wc_ref, wo_ref, gp_ref, o_ref):
    n_rows = x_ref.shape[0]
    for r0 in range(0, n_rows, MERGE_ROWS):
        rows = slice(r0, r0 + MERGE_ROWS)
        heads = []
        for h in range(C_HEADS):
            sl = slice(h * C_HEAD_DIM, (h + 1) * C_HEAD_DIM)
            s = lax.dot_general(cq_ref[rows, sl], mk_ref[:, sl], _NT,
                                preferred_element_type=F32) * (C_HEAD_DIM ** -0.5)
            e = jnp.exp(s - jnp.max(s, axis=1, keepdims=True))
            p = e / jnp.sum(e, axis=1, keepdims=True)
            heads.append(jnp.dot(p.astype(BF16), mv_ref[:, sl], preferred_element_type=F32))
        yc_pre = jnp.concatenate(heads, axis=1).astype(BF16)
        ya = jnp.dot(ya_ref[rows, :], wa_ref[...], preferred_element_type=F32)
        yb = jnp.dot(yb_ref[rows, :], wb_ref[...], preferred_element_type=F32)
        yc = jnp.dot(yc_pre, wc_ref[...], preferred_element_type=F32)
        merged = (g0_ref[rows, :].astype(F32) * ya + g1_ref[rows, :].astype(F32) * yb
                  + g2_ref[rows, :].astype(F32) * yc)
        o = jnp.dot(merged.astype(BF16), wo_ref[...], preferred_element_type=F32)
        o_ref[rows, :] = x_ref[rows, :] + _rms(o, gp_ref[...])


def _merge(x2, ya2, yb2, proj2, mkv3, wa, wb, wc, wo, g_post, seq):
    n = x2.shape[0]
    tm = min(2 * MERGE_ROWS, seq)
    per_batch = seq // tm
    n_mem = mkv3.shape[1]
    c_dim = C_HEADS * C_HEAD_DIM
    const = lambda t: (0, 0)
    return pl.pallas_call(
        _merge_kernel,
        name="merge",
        grid=(n // tm,),
        in_specs=[
            pl.BlockSpec((tm, D_MODEL), lambda t: (t, 0)),
            pl.BlockSpec((tm, 512), lambda t: (t, 0)),
            pl.BlockSpec((tm, 512), lambda t: (t, 0)),
            pl.BlockSpec((tm, 512), lambda t: (t, COL_CQ)),
            pl.BlockSpec((tm, D_MODEL), lambda t: (t, 0)),
            pl.BlockSpec((tm, D_MODEL), lambda t: (t, 1)),
            pl.BlockSpec((tm, D_MODEL), lambda t: (t, 2)),
            pl.BlockSpec((None, n_mem, c_dim), lambda t: (t // per_batch, 0, 0)),
            pl.BlockSpec((None, n_mem, c_dim), lambda t: (t // per_batch, 0, 1)),
            pl.BlockSpec(wa.shape, const),
            pl.BlockSpec(wb.shape, const),
            pl.BlockSpec(wc.shape, const),
            pl.BlockSpec(wo.shape, const),
            pl.BlockSpec((1, D_MODEL), const),
        ],
        out_specs=pl.BlockSpec((tm, D_MODEL), lambda t: (t, 0)),
        out_shape=jax.ShapeDtypeStruct((n, D_MODEL), F32),
        compiler_params=pltpu.CompilerParams(
            dimension_semantics=("arbitrary",), vmem_limit_bytes=VMEM_LIMIT),
    )(x2, ya2, yb2, proj2, proj2, proj2, proj2, mkv3, mkv3, wa, wb, wc, wo, g_post)


FFN_ROWS = 512


def _ffn_kernel(x_ref, gpre_ref, wg_ref, wu_ref, wo_ref, gpost_ref, o_ref):
    for r0 in range(0, x_ref.shape[0], FFN_ROWS):
        rows = slice(r0, min(r0 + FFN_ROWS, x_ref.shape[0]))
        x = x_ref[rows, :]
        h = _rms(x, gpre_ref[...]).astype(BF16)
        g = jnp.dot(h, wg_ref[...], preferred_element_type=F32)
        u = jnp.dot(h, wu_ref[...], preferred_element_type=F32)
        act = (g * jax.nn.sigmoid(g) * u).astype(BF16)
        f = jnp.dot(act, wo_ref[...], preferred_element_type=F32)
        o_ref[rows, :] = x + _rms(f, gpost_ref[...])


def _ffn(x2, g_pre, w_in, wo, g_post):
    n = x2.shape[0]
    d_ff = wo.shape[0]
    tm = min(2 * FFN_ROWS, n)
    const = lambda t: (0, 0)
    resident = dict(pipeline_mode=pl.Buffered(1))
    return pl.pallas_call(
        _ffn_kernel,
        name="ffn",
        grid=(n // tm,),
        in_specs=[
            pl.BlockSpec((tm, D_MODEL), lambda t: (t, 0)),
            pl.BlockSpec((1, D_MODEL), const),
            pl.BlockSpec((D_MODEL, d_ff), lambda t: (0, 0), **resident),
            pl.BlockSpec((D_MODEL, d_ff), lambda t: (0, 1), **resident),
            pl.BlockSpec(wo.shape, const, **resident),
            pl.BlockSpec((1, D_MODEL), const),
        ],
        out_specs=pl.BlockSpec((tm, D_MODEL), lambda t: (t, 0)),
        out_shape=jax.ShapeDtypeStruct((n, D_MODEL), F32),
        compiler_params=pltpu.CompilerParams(
            dimension_semantics=("arbitrary",), vmem_limit_bytes=VMEM_LIMIT),
    )(x2, g_pre, w_in, w_in, wo, g_post)


def _pack_w_in(w):
    sizes = (512, 512, 512, IDX_HEADS * 64, 64, IDX_HEADS, 512, 512, 512, 512,
             N_BRANCH * D_MODEL)
    aq, ak, av, iq, ik, iw, bq, bk, bv, cq, gates = jnp.split(w, np.cumsum(sizes)[:-1], axis=1)
    scale = HEAD_DIM ** -0.5
    scale2 = scale * LOG2E
    w_main = jnp.concatenate(
        [gates, aq * scale2, ak, iq * scale, bq * scale2, bk, bv, cq], axis=1).astype(BF16)
    z64 = jnp.zeros((D_MODEL, 64), F32)
    w_ik = jnp.concatenate([ik, z64, z64, ik], axis=1).astype(BF16)
    w_trans = jnp.concatenate(
        [av, iw, jnp.zeros((D_MODEL, N_TRANS - 512 - IDX_HEADS), F32)], axis=1).T.astype(BF16)
    return w_main, w_ik, w_trans


def kernel(x, mem, rel_bias, g_mix_pre, w_in, b_gate, g_mem, w_mem_kv, w_up_a, w_up_b, w_up_c,
           w_out, g_mix_post, g_ffn_pre, w_ffn_in, w_ffn_out, g_ffn_post):
    bsz, seq, _ = x.shape
    n_mem = mem.shape[1]
    k_sel = min(TOPK_MAX, seq // 4)
    bias_tiles = _bias_tiles(rel_bias)
    x2 = x.reshape(bsz * seq, D_MODEL)
    for l in range(w_in.shape[0]):
        w_main, w_ik, w_trans = _pack_w_in(w_in[l])
        proj2, ik2, avt, iwt = _project(x2, g_mix_pre[l][None, :], w_main, b_gate[l][None, :],
                                        w_ik, w_trans)
        proj3 = proj2.reshape(bsz, seq, N_MAIN)
        mkv = _memkv(mem.reshape(bsz * n_mem, D_MODEL), g_mem[l][None, :],
                     w_mem_kv[l].astype(BF16))
        ya = _dsa(proj3, ik2.reshape(bsz, seq, N_IK), avt.reshape(bsz, seq // NK, 512, NK),
                  iwt, bias_tiles, k_sel)
        yb = _stick_breaking(proj3)
        x2 = _merge(x2, ya.reshape(bsz * seq, 512), yb.reshape(bsz * seq, 512), proj2,
                    mkv.reshape(bsz, n_mem, 2 * C_HEADS * C_HEAD_DIM),
                    w_up_a[l].astype(BF16), w_up_b[l].astype(BF16), w_up_c[l].astype(BF16),
                    w_out[l].astype(BF16), g_mix_post[l][None, :], seq)
        x2 = _ffn(x2, g_ffn_pre[l][None, :], w_ffn_in[l].astype(BF16),
                  w_ffn_out[l].astype(BF16), g_ffn_post[l][None, :])
    return x2.reshape(bsz, seq, D_MODEL)
```

```python
import functools

import numpy as np
import jax
import jax.numpy as jnp
from jax import lax
from jax.experimental import pallas as pl
from jax.experimental.pallas import tpu as pltpu

D_MODEL = 1024
CHUNK = 64
HEAD_DIM = 64
N_HEADS = 8
IDX_HEADS = 8
TOPK_MAX = 256
C_HEADS = 4
C_HEAD_DIM = 128
N_BRANCH = 3
REL_BUCKETS = 32
EPS = 1e-6

F32 = jnp.float32
BF16 = jnp.bfloat16
INT_MIN = -2 ** 31
NEG_BIG = -1e30
LOG2E = 1.4426950408889634

QB = 256
DQ = 256
NK = 256
N_PAIR = N_HEADS // 2
ONES_ROWS = 16

N_GATE = N_BRANCH * D_MODEL
COL_AQ, COL_AK, COL_IQ, COL_BQ, COL_BK, COL_BV, COL_CQ = range(N_GATE // 512, N_GATE // 512 + 7)
N_MAIN = N_GATE + 7 * 512
N_IK = 256
N_TRANS = 512 + 16

V7X_VMEM_BYTES = 64 * 1024 * 1024
VMEM_LIMIT = V7X_VMEM_BYTES * 7 // 8

_NT = (((1,), (1,)), ((), ()))


def _rms(x, g):
    return x * lax.rsqrt(jnp.mean(x * x, axis=-1, keepdims=True) + EPS) * g


PROJ_ROWS = 512
PROJ_COLS = 512


def _proj_kernel(x_ref, g_ref, w_ref, b_ref, wik_ref, wt_ref, o_ref, ik_ref, avt_ref, iwt_ref):
    hb = _rms(x_ref[...], g_ref[...]).astype(BF16)
    ik_ref[...] = jnp.dot(hb, wik_ref[...], preferred_element_type=F32).astype(BF16)
    tr = lax.dot_general(wt_ref[...], hb, _NT, preferred_element_type=F32)
    for c in range(avt_ref.shape[0]):
        avt_ref[c] = tr[:512, c * NK:(c + 1) * NK].astype(BF16)
    iwt_ref[...] = tr[512:512 + IDX_HEADS, :]
    for c0 in range(0, N_MAIN, PROJ_COLS):
        cols = slice(c0, c0 + PROJ_COLS)
        acc = jnp.dot(hb, w_ref[:, cols], preferred_element_type=F32)
        if c0 < N_GATE:
            acc = 0.5 + 0.5 * jnp.tanh(0.5 * (acc + b_ref[:, cols]))
        o_ref[:, cols] = acc.astype(BF16)


def _project(x2, g, w_main, b_gate, w_ik, w_trans):
    n = x2.shape[0]
    tm = min(PROJ_ROWS, n)
    const = lambda i: (0, 0)
    resident = dict(pipeline_mode=pl.Buffered(1))
    return pl.pallas_call(
        _proj_kernel,
        name="in_proj",
        grid=(n // tm,),
        in_specs=[
            pl.BlockSpec((tm, D_MODEL), lambda i: (i, 0)),
            pl.BlockSpec((1, D_MODEL), const),
            pl.BlockSpec(w_main.shape, const, **resident),
            pl.BlockSpec(b_gate.shape, const),
            pl.BlockSpec(w_ik.shape, const, **resident),
            pl.BlockSpec(w_trans.shape, const, **resident),
        ],
        out_specs=[
            pl.BlockSpec((tm, N_MAIN), lambda i: (i, 0)),
            pl.BlockSpec((tm, N_IK), lambda i: (i, 0)),
            pl.BlockSpec((tm // NK, 512, NK), lambda i: (i, 0, 0)),
            pl.BlockSpec((IDX_HEADS, tm), lambda i: (0, i)),
        ],
        out_shape=[
            jax.ShapeDtypeStruct((n, N_MAIN), BF16),
            jax.ShapeDtypeStruct((n, N_IK), BF16),
            jax.ShapeDtypeStruct((n // NK, 512, NK), BF16),
            jax.ShapeDtypeStruct((IDX_HEADS, n), F32),
        ],
        compiler_params=pltpu.CompilerParams(
            dimension_semantics=("arbitrary",), vmem_limit_bytes=VMEM_LIMIT),
    )(x2, g, w_main, b_gate, w_ik, w_trans)


def _memkv_kernel(x_ref, g_ref, w_ref, o_ref):
    hb = _rms(x_ref[...], g_ref[...]).astype(BF16)
    o_ref[...] = jnp.dot(hb, w_ref[...], preferred_element_type=F32).astype(BF16)


def _memkv(mem2, g, w):
    n = mem2.shape[0]
    tm = min(512, n)
    return pl.pallas_call(
        _memkv_kernel,
        name="mem_kv",
        grid=(n // tm,),
        in_specs=[
            pl.BlockSpec((tm, D_MODEL), lambda i: (i, 0)),
            pl.BlockSpec((1, D_MODEL), lambda i: (0, 0)),
            pl.BlockSpec((D_MODEL, w.shape[1]), lambda i: (0, 0)),
        ],
        out_specs=pl.BlockSpec((tm, w.shape[1]), lambda i: (i, 0)),
        out_shape=jax.ShapeDtypeStruct((n, w.shape[1]), BF16),
        compiler_params=pltpu.CompilerParams(
            dimension_semantics=("arbitrary",), vmem_limit_bytes=VMEM_LIMIT),
    )(mem2, g, w)


BIAS_OFFSETS = (0, -NK)
_LOG_BUCKET_STARTS = (12, 16, 23, 32, 46, 64, 91)
FAR_BUCKET = 15
assert DQ == NK


def _bias_kernel(rb_ref, o_ref):
    key = lax.broadcasted_iota(jnp.int32, (NK, DQ), 0)
    qry = lax.broadcasted_iota(jnp.int32, (NK, DQ), 1)
    for c, off in enumerate(BIAS_OFFSETS):
        rel = key - qry + off
        n = jnp.abs(rel)
        large = jnp.full((NK, DQ), 8, jnp.int32)
        for start in _LOG_BUCKET_STARTS:
            large = large + jnp.where(n >= start, 1, 0)
        bucket = jnp.where(rel > 0, REL_BUCKETS // 2, 0) + jnp.where(n < 8, n, large)
        for h in range(N_HEADS):
            val = jnp.full((NK, DQ), rb_ref[0, h], F32)
            for b in range(1, REL_BUCKETS):
                val = jnp.where(bucket == b, rb_ref[b, h], val)
            o_ref[c, h] = (val - rb_ref[FAR_BUCKET, h]) * LOG2E


def _bias_tiles(rel_bias):
    return pl.pallas_call(
        _bias_kernel,
        name="rel_bias_tiles",
        in_specs=[pl.BlockSpec(memory_space=pltpu.SMEM)],
        out_specs=pl.BlockSpec(memory_space=pltpu.VMEM),
        out_shape=jax.ShapeDtypeStruct((len(BIAS_OFFSETS), N_HEADS, NK, DQ), F32),
    )(rel_bias)


def _split_heads_into(qm_ref, q):
    lane = lax.broadcasted_iota(jnp.int32, (QB, 128), 1)
    for p in range(N_PAIR):
        qp = q[:, p * 128:(p + 1) * 128].astype(F32)
        qm_ref[p, :QB, :] = jnp.where(lane < HEAD_DIM, qp, 0.0).astype(BF16)
        qm_ref[p, QB:, :] = jnp.where(lane >= HEAD_DIM, qp, 0.0).astype(BF16)


def _merge_pair(o_even, o_odd):
    lane = lax.broadcasted_iota(jnp.int32, (QB, 128), 1)
    return jnp.where(lane < HEAD_DIM, o_even, o_odd)


CNT_BLOCKS = 4
KEY_ROWS = 32
assert NK == 8 * 32


def _bit_transpose32(load_row, tmp_ref, store_row):
    def swap(a, b, j, m):
        t = (a ^ lax.shift_right_logical(b, jnp.int32(j))) & m
        return a ^ t, b ^ (t << j)

    lower = []
    for k in range(16):
        a, b = swap(load_row(k), load_row(k + 16), 16, 0x0000FFFF)
        lower.append(a)
        tmp_ref[k] = b
    for base in (0, 16):
        x = lower if base == 0 else [tmp_ref[k] for k in range(16)]
        j, m = 8, 0x00FF00FF
        while j:
            k = 0
            while k < 16:
                x[k], x[k + j] = swap(x[k], x[k + j], j, m)
                k = (k + j + 1) & ~j
            j >>= 1
            m ^= m << j
        for i in range(16):
            store_row(base + i, x[i])


def _keep_lowest_ties(alive_ref, kept_ref, need, n_groups):
    n_blocks = alive_ref.shape[0]
    sub = lax.broadcasted_iota(jnp.int32, (8, DQ), 0)
    zero_masks = []
    for c in reversed(range(max(n_blocks - 1, 1).bit_length())):
        zero_masks.append(
            lambda jb, c=c: jnp.where(((jb >> c) & 1) == 0, jnp.int32(-1), jnp.int32(0)))
    for word in (0xFFFF0000, 0xFF00FF00, 0xF0F0F0F0, 0xCCCCCCCC, 0xAAAAAAAA):
        zero_masks.append(lambda jb, word=word: jnp.int32(word - (1 << 32)))
    for c in (2, 1, 0):
        zero_masks.append(lambda jb, c=c: jnp.where(((sub >> c) & 1) == 0, -1, 0))

    remaining = need
    took_zero = None
    for p, zero_mask in enumerate(zero_masks):
        prev_mask = zero_masks[p - 1] if p else None

        def body(g, cnts, zero_mask=zero_mask, prev_mask=prev_mask, took_zero=took_zero):
            cnts = list(cnts)
            for u in range(CNT_BLOCKS):
                jb = g * CNT_BLOCKS + u
                alive = alive_ref[jb]
                if prev_mask is None:
                    kept_ref[jb] = jnp.zeros((8, DQ), jnp.int32)
                else:
                    lows = alive & prev_mask(jb)
                    kept_ref[jb] = kept_ref[jb] | jnp.where(took_zero, 0, lows)
                    alive = jnp.where(took_zero, lows, alive ^ lows)
                    alive_ref[jb] = alive
                cnts[u] = cnts[u] + lax.population_count(alive & zero_mask(jb))
            return tuple(cnts)

        zeros = jnp.zeros((8, DQ), jnp.int32)
        cnts = lax.fori_loop(0, n_groups, body, (zeros,) * CNT_BLOCKS)
        n_zero = jnp.sum(sum(cnts[1:], cnts[0]), axis=0, keepdims=True)
        took_zero = n_zero >= remaining
        remaining = jnp.where(took_zero, remaining, remaining - n_zero)

    last_mask = zero_masks[-1]

    def finish(jb, carry):
        alive = alive_ref[jb]
        alive_ref[jb] = kept_ref[jb] | jnp.where(took_zero, alive & last_mask(jb), alive)
        return carry

    lax.fori_loop(0, n_groups * CNT_BLOCKS, finish, 0)


def _dsa_kernel(aq_ref, iq_ref, iwt_ref, ak_ref, avt_ref, ik_ref, bias_ref, o_ref,
                keybuf_ref, plane_ref, alive_ref, above_ref, kept_ref, sel_ref, qm_ref, m_ref,
                acc_ref, s0_ref, cm0_ref, s1_ref, cm1_ref, raw0_ref, raw1_ref, tmp_ref, *,
                k_sel):
    step = pl.program_id(1)
    n_q = pl.num_programs(1) - 1
    att = step - 1
    krow = lax.broadcasted_iota(jnp.int32, (KEY_ROWS, DQ), 0)
    qcol = lax.broadcasted_iota(jnp.int32, (KEY_ROWS, DQ), 1)
    qchunk = (step * DQ + qcol) // CHUNK

    iwt = iwt_ref[...] * (IDX_HEADS ** -0.5)

    def dots_stage(jb, slot):
        raw_ref = (raw0_ref, raw1_ref)[slot]
        k0 = pl.multiple_of(jb * NK, NK)
        for h in range(IDX_HEADS):
            ikh = ik_ref[pl.ds(k0, NK), (h % 2) * 128:(h % 2 + 1) * 128]
            iqp = iq_ref[:, (h // 2) * 128:(h // 2 + 1) * 128]
            raw_ref[h] = lax.dot_general(ikh, iqp, _NT, preferred_element_type=F32)

    def keys_stage(jb, slot, last):
        raw_ref = (raw0_ref, raw1_ref)[slot]
        k0 = pl.multiple_of(jb * NK, NK)
        for c in range(NK // KEY_ROWS):
            rows = slice(c * KEY_ROWS, (c + 1) * KEY_ROWS)
            acc = jnp.zeros((KEY_ROWS, DQ), F32)
            for h in range(IDX_HEADS):
                acc = acc + iwt[h:h + 1, :] * jnp.maximum(raw_ref[h, rows, :], 0.0)
            bits = lax.bitcast_convert_type(acc, jnp.int32)
            sign = bits >> 31
            ukey = (bits ^ (sign | INT_MIN)) - sign
            if last:
                admissible = ((k0 + c * KEY_ROWS + krow) // CHUNK) <= qchunk
                ukey = jnp.where(admissible, ukey, 0)
            keybuf_ref[slot, rows, :] = ukey
        for half in range(DQ // 128):
            lanes = slice(half * 128, (half + 1) * 128)

            def load_row(r):
                return keybuf_ref[slot, 8 * r:8 * r + 8, lanes]

            def store_plane(i, v):
                plane_ref[31 - i, jb, :, lanes] = v

            _bit_transpose32(load_row, tmp_ref, store_plane)
        plane_ref[32, jb] = jnp.full((8, DQ), -1, jnp.int32)
        alive_ref[jb] = jnp.full((8, DQ), -1, jnp.int32)
        above_ref[jb] = jnp.zeros((8, DQ), jnp.int32)

    ones = jnp.ones((ONES_ROWS, NK), BF16)
    slots = ((s0_ref, cm0_ref), (s1_ref, cm1_ref))

    def logits_stage(jb, slot, bias_idx):
        s_ref, cm_ref = slots[slot]
        k0 = pl.multiple_of(jb * NK, NK)
        sel = sel_ref[jb]
        mask = jnp.concatenate(
            [jnp.where((sel << r) < 0, 0.0, NEG_BIG)
             for r in range(32)], axis=0).astype(BF16)
        for h in range(N_HEADS):
            kp = ak_ref[pl.ds(k0, NK), (h // 2) * 128:(h // 2 + 1) * 128]
            s = lax.dot_general(kp, qm_ref[h], _NT, preferred_element_type=F32)
            if bias_idx is not None:
                s = s + bias_ref[bias_idx, h]
            sb = s.astype(BF16) + mask
            s_ref[h] = sb
            cm_ref[h] = jnp.max(sb, axis=0, keepdims=True).astype(F32)

    def softmax_stage(jb, slot):
        s_ref, cm_ref = slots[slot]
        for h in range(N_HEADS):
            m_prev = m_ref[h]
            m_new = jnp.maximum(m_prev, cm_ref[h])
            alpha = jnp.exp2(m_prev - m_new)
            pe = jnp.exp2(s_ref[h] - m_new.astype(BF16))
            vt = jnp.concatenate([avt_ref[jb, h * HEAD_DIM:(h + 1) * HEAD_DIM, :], ones], axis=0)
            acc_ref[h] = alpha * acc_ref[h] + jnp.dot(vt, pe, preferred_element_type=F32)
            m_ref[h] = m_new

    def start_attention():
        lane = lax.broadcasted_iota(jnp.int32, (DQ, 128), 1)
        for p in range(N_PAIR):
            qp = aq_ref[:, p * 128:(p + 1) * 128].astype(F32)
            qm_ref[2 * p] = jnp.where(lane < HEAD_DIM, qp, 0.0).astype(BF16)
            qm_ref[2 * p + 1] = jnp.where(lane >= HEAD_DIM, qp, 0.0).astype(BF16)
        m_ref[...] = jnp.full(m_ref.shape, NEG_BIG, F32)
        acc_ref[...] = jnp.zeros(acc_ref.shape, F32)
        logits_stage(att, 0, 0)
        logits_stage(jnp.maximum(att - 1, 0), 1, 1)
        softmax_stage(att, 0)

    @pl.when(step == 0)
    def _():
        dots_stage(0, 0)
        keys_stage(0, 0, True)

    @pl.when(step == n_q)
    def _():
        start_attention()
        n_steps = att - 1

        def two_steps(u, carry):
            b = att - 1 - 2 * u
            logits_stage(b - 1, 0, None)
            softmax_stage(b, 1)
            logits_stage(b - 2, 1, None)
            softmax_stage(b - 1, 0)
            return carry

        lax.fori_loop(0, jnp.maximum(n_steps, 0) // 2, two_steps, 0)

        @pl.when(jnp.logical_and(n_steps >= 1, n_steps % 2 == 1))
        def _():
            logits_stage(0, 0, None)
            softmax_stage(1, 1)

        @pl.when(jnp.logical_and(att >= 1, att % 2 == 1))
        def _():
            softmax_stage(0, 1)

        @pl.when(jnp.logical_and(att >= 1, att % 2 == 0))
        def _():
            softmax_stage(0, 0)

    @pl.when(jnp.logical_and(step >= 1, step < n_q))
    def _():
        start_attention()
        dots_stage(0, 0)
        dots_stage(1, 1)
        keys_stage(0, 0, False)
        n_fused = jnp.maximum(att - 1, 0) // 2

        def fused_trip(u, carry):
            b = att - 1 - 2 * u
            j = 2 * u + 1
            logits_stage(b - 1, 0, None)
            dots_stage(j + 1, 0)
            softmax_stage(b, 1)
            keys_stage(j, 1, False)
            logits_stage(b - 2, 1, None)
            dots_stage(j + 2, 1)
            softmax_stage(b - 1, 0)
            keys_stage(j + 1, 0, False)
            return carry

        lax.fori_loop(0, n_fused, fused_trip, 0)

        @pl.when(att == 0)
        def _():
            keys_stage(1, 1, True)

        @pl.when(att % 2 == 1)
        def _():
            dots_stage(att + 1, 0)
            softmax_stage(0, 1)
            keys_stage(att, 1, False)
            keys_stage(att + 1, 0, True)

        @pl.when(jnp.logical_and(att >= 2, att % 2 == 0))
        def _():
            logits_stage(0, 0, None)
            dots_stage(att, 0)
            softmax_stage(1, 1)
            keys_stage(att - 1, 1, False)
            dots_stage(att + 1, 1)
            softmax_stage(0, 0)
            keys_stage(att, 0, False)
            keys_stage(att + 1, 1, True)

    @pl.when(step >= 1)
    def _():
        for p in range(N_PAIR):
            halves = []
            for h in (2 * p, 2 * p + 1):
                a = acc_ref[h]
                halves.append(a[:HEAD_DIM, :] / a[HEAD_DIM:HEAD_DIM + 1, :])
            o_ref[:, p * 128:(p + 1) * 128] = jnp.concatenate(halves, axis=0).T.astype(BF16)

    nkb = jnp.where(step < n_q, step + 1, 0)
    n_groups = (nkb + CNT_BLOCKS - 1) // CNT_BLOCKS

    def pad_block(jb, carry):
        for b in range(33):
            plane_ref[b, jb] = jnp.zeros((8, DQ), jnp.int32)
        alive_ref[jb] = jnp.zeros((8, DQ), jnp.int32)
        above_ref[jb] = jnp.zeros((8, DQ), jnp.int32)
        return carry

    lax.fori_loop(nkb, n_groups * CNT_BLOCKS, pad_block, 0)

    def select_pass(it, state):
        took_prev, n_above, thr_u = state
        b = 31 - it
        take_prev = took_prev != 0

        def body(g, cnts):
            cnts = list(cnts)
            for u in range(CNT_BLOCKS):
                jb = g * CNT_BLOCKS + u
                alive = alive_ref[jb]
                with_prev = alive & plane_ref[b + 1, jb]
                above_ref[jb] = above_ref[jb] | jnp.where(take_prev, 0, with_prev)
                alive = jnp.where(take_prev, with_prev, alive ^ with_prev)
                alive_ref[jb] = alive
                cnts[u] = cnts[u] + lax.population_count(alive & plane_ref[b, jb])
            return tuple(cnts)

        zeros = jnp.zeros((8, DQ), jnp.int32)
        cnts = lax.fori_loop(0, n_groups, body, (zeros,) * CNT_BLOCKS)
        n_one = jnp.sum(sum(cnts[1:], cnts[0]), axis=0, keepdims=True)
        take = (n_above + n_one) >= k_sel
        n_above = jnp.where(take, n_above, n_above + n_one)
        thr_u = jnp.where(take, thr_u | jnp.left_shift(jnp.int32(1), b), thr_u)
        return take.astype(jnp.int32), n_above, thr_u

    row0 = jnp.zeros((1, DQ), jnp.int32)
    took_last, n_above, thr_u = lax.fori_loop(0, 32, select_pass, (row0 + 1, row0, row0))

    def settle(g, cnts):
        cnts = list(cnts)
        for u in range(CNT_BLOCKS):
            jb = g * CNT_BLOCKS + u
            alive = alive_ref[jb]
            with_last = alive & plane_ref[0, jb]
            above_ref[jb] = above_ref[jb] | jnp.where(took_last != 0, 0, with_last)
            alive = jnp.where(took_last != 0, with_last, alive ^ with_last)
            alive_ref[jb] = alive
            cnts[u] = cnts[u] + lax.population_count(alive)
        return tuple(cnts)

    zeros = jnp.zeros((8, DQ), jnp.int32)
    cnts = lax.fori_loop(0, n_groups, settle, (zeros,) * CNT_BLOCKS)
    n_tied = jnp.sum(sum(cnts[1:], cnts[0]), axis=0, keepdims=True)
    need = k_sel - n_above
    real = thr_u != 0
    extra = jnp.logical_and(n_tied > need, real)

    @pl.when(jnp.max(extra.astype(jnp.int32)) > 0)
    def _():
        _keep_lowest_ties(alive_ref, kept_ref, need, n_groups)

    def finalize(jb, carry):
        sel_ref[jb] = above_ref[jb] | jnp.where(real, alive_ref[jb], 0)
        return carry

    lax.fori_loop(0, nkb, finalize, 0)


def _dsa(proj3, ik3, avt4, iwt, bias_tiles, k_sel):
    bsz, seq, _ = proj3.shape
    nq = seq // DQ
    assert seq % (NK * CNT_BLOCKS) == 0
    resident = dict(pipeline_mode=pl.Buffered(1))
    return pl.pallas_call(
        functools.partial(_dsa_kernel, k_sel=k_sel),
        name="dsa",
        grid=(bsz, nq + 1),
        in_specs=[
            pl.BlockSpec((None, DQ, 512), lambda b, s: (b, jnp.maximum(s - 1, 0), COL_AQ)),
            pl.BlockSpec((None, DQ, 512), lambda b, s: (b, jnp.minimum(s, nq - 1), COL_IQ)),
            pl.BlockSpec((IDX_HEADS, DQ), lambda b, s: (0, b * nq + jnp.minimum(s, nq - 1))),
            pl.BlockSpec((None, seq, 512), lambda b, i: (b, 0, COL_AK), **resident),
            pl.BlockSpec((None, seq // NK, 512, NK), lambda b, i: (b, 0, 0, 0), **resident),
            pl.BlockSpec((None, seq, N_IK), lambda b, i: (b, 0, 0), **resident),
            pl.BlockSpec(bias_tiles.shape, lambda b, i: (0, 0, 0, 0), **resident),
        ],
        out_specs=pl.BlockSpec((None, DQ, 512), lambda b, s: (b, jnp.maximum(s - 1, 0), 0)),
        out_shape=jax.ShapeDtypeStruct((bsz, seq, 512), BF16),
        scratch_shapes=[
            pltpu.VMEM((2, NK, DQ), jnp.int32),
            pltpu.VMEM((33, seq // NK, 8, DQ), jnp.int32),
            pltpu.VMEM((seq // NK, 8, DQ), jnp.int32),
            pltpu.VMEM((seq // NK, 8, DQ), jnp.int32),
            pltpu.VMEM((seq // NK, 8, DQ), jnp.int32),
            pltpu.VMEM((seq // NK, 8, DQ), jnp.int32),
            pltpu.VMEM((N_HEADS, DQ, 128), BF16),
            pltpu.VMEM((N_HEADS, 1, DQ), F32),
            pltpu.VMEM((N_HEADS, HEAD_DIM + ONES_ROWS, DQ), F32),
            pltpu.VMEM((N_HEADS, NK, DQ), BF16),
            pltpu.VMEM((N_HEADS, 1, DQ), F32),
            pltpu.VMEM((N_HEADS, NK, DQ), BF16),
            pltpu.VMEM((N_HEADS, 1, DQ), F32),
            pltpu.VMEM((IDX_HEADS, NK, DQ), F32),
            pltpu.VMEM((IDX_HEADS, NK, DQ), F32),
            pltpu.VMEM((16, 8, 128), jnp.int32),
        ],
        compiler_params=pltpu.CompilerParams(
            dimension_semantics=("arbitrary", "arbitrary"), vmem_limit_bytes=VMEM_LIMIT),
    )(proj3, proj3, iwt, proj3, avt4, ik3, bias_tiles)


SB_DEAD_MASS = 104.0 * LOG2E


def _sb_kernel(q_ref, k_ref, v_ref, o_ref, qm_ref, uu_ref, carry_ref, acc_ref, z_ref, sp_ref,
               later_ref):
    i = pl.program_id(1)
    diag = (i * QB + QB - 1) // NK

    @pl.when(jnp.logical_and(pl.program_id(0) == 0, i == 0))
    def _():
        kr = lax.broadcasted_iota(jnp.int32, (NK, NK), 0)
        kc = lax.broadcasted_iota(jnp.int32, (NK, NK), 1)
        uu_ref[...] = jnp.where(kr > kc, 1.0, 0.0).astype(BF16)

    _split_heads_into(qm_ref, q_ref[...])
    carry_ref[...] = jnp.zeros(carry_ref.shape, F32)
    acc_ref[...] = jnp.zeros(acc_ref.shape, F32)

    def block(jb, on_diagonal):
        k0 = pl.multiple_of(jb * NK, NK)
        if on_diagonal:
            row = lax.broadcasted_iota(jnp.int32, (2 * QB, NK), 0)
            col = lax.broadcasted_iota(jnp.int32, (2 * QB, NK), 1)
            causal = (k0 + col) < (i * QB + row % QB)
        for p in range(N_PAIR):
            kp = k_ref[pl.ds(k0, NK), p * 128:(p + 1) * 128]
            z_ref[p] = lax.dot_general(qm_ref[p], kp, _NT, preferred_element_type=F32)
        for p in range(N_PAIR):
            z = z_ref[p]
            neg_abs = lax.bitcast_convert_type(
                lax.bitcast_convert_type(z, jnp.int32) | INT_MIN, F32)
            sp = jnp.maximum(z, 0.0) + jnp.log(1.0 + jnp.exp2(neg_abs)) * LOG2E
            if on_diagonal:
                sp = jnp.where(causal, sp, 0.0)
            sp_ref[p] = sp
            later_ref[p] = jnp.dot(sp.astype(BF16), uu_ref[...], preferred_element_type=F32)
        for p in range(N_PAIR):
            vp = v_ref[pl.ds(k0, NK), p * 128:(p + 1) * 128]
            carry = carry_ref[p]
            sp = sp_ref[p]
            a = jnp.exp2(z_ref[p] - sp - later_ref[p] - carry)
            if on_diagonal:
                a = jnp.where(causal, a, 0.0)
            acc_ref[p] += jnp.dot(a.astype(BF16), vp, preferred_element_type=F32)
            carry_ref[p] = carry + jnp.sum(sp, axis=1, keepdims=True)

    block(diag, True)

    def alive():
        return (jnp.min(carry_ref[...]) <= SB_DEAD_MASS).astype(jnp.int32)

    def cond(state):
        jb, go = state
        return jnp.logical_and(jb >= 0, go > 0)

    def body(state):
        jb, _ = state
        block(jb, False)
        return jb - 1, alive()

    lax.while_loop(cond, body, (diag - 1, alive()))

    for p in range(N_PAIR):
        o_ref[:, p * 128:(p + 1) * 128] = _merge_pair(
            acc_ref[p, :QB, :], acc_ref[p, QB:, :]).astype(BF16)


def _stick_breaking(proj3):
    bsz, seq, _ = proj3.shape
    resident = dict(pipeline_mode=pl.Buffered(1))
    return pl.pallas_call(
        _sb_kernel,
        name="stick_breaking",
        grid=(bsz, seq // QB),
        in_specs=[
            pl.BlockSpec((None, QB, 512), lambda b, i: (b, i, COL_BQ)),
            pl.BlockSpec((None, seq, 512), lambda b, i: (b, 0, COL_BK), **resident),
            pl.BlockSpec((None, seq, 512), lambda b, i: (b, 0, COL_BV), **resident),
        ],
        out_specs=pl.BlockSpec((None, QB, 512), lambda b, i: (b, i, 0)),
        out_shape=jax.ShapeDtypeStruct((bsz, seq, 512), BF16),
        scratch_shapes=[
            pltpu.VMEM((N_PAIR, 2 * QB, 128), BF16),
            pltpu.VMEM((NK, NK), BF16),
            pltpu.VMEM((N_PAIR, 2 * QB, 1), F32),
            pltpu.VMEM((N_PAIR, 2 * QB, 128), F32),
            pltpu.VMEM((N_PAIR, 2 * QB, NK), F32),
            pltpu.VMEM((N_PAIR, 2 * QB, NK), F32),
            pltpu.VMEM((N_PAIR, 2 * QB, NK), F32),
        ],
        compiler_params=pltpu.CompilerParams(
            dimension_semantics=("arbitrary", "arbitrary"), vmem_limit_bytes=VMEM_LIMIT),
    )(proj3, proj3, proj3)


MERGE_ROWS = 512


def _merge_kernel(x_ref, ya_ref, yb_ref, cq_ref, g0_ref, g1_ref, g2_ref, mk_ref, mv_ref,
                  wa_ref, wb_ref, wc_ref, wo_ref, gp_ref, o_ref):
    n_rows = x_ref.shape[0]
    for r0 in range(0, n_rows, MERGE_ROWS):
        rows = slice(r0, r0 + MERGE_ROWS)
        heads = []
        for h in range(C_HEADS):
            sl = slice(h * C_HEAD_DIM, (h + 1) * C_HEAD_DIM)
            s = lax.dot_general(cq_ref[rows, sl], mk_ref[:, sl], _NT,
                                preferred_element_type=F32) * (C_HEAD_DIM ** -0.5)
            e = jnp.exp(s - jnp.max(s, axis=1, keepdims=True))
            p = e / jnp.sum(e, axis=1, keepdims=True)
            heads.append(jnp.dot(p.astype(BF16), mv_ref[:, sl], preferred_element_type=F32))
        yc_pre = jnp.concatenate(heads, axis=1).astype(BF16)
        ya = jnp.dot(ya_ref[rows, :], wa_ref[...], preferred_element_type=F32)
        yb = jnp.dot(yb_ref[rows, :], wb_ref[...], preferred_element_type=F32)
        yc = jnp.dot(yc_pre, wc_ref[...], preferred_element_type=F32)
        merged = (g0_ref[rows, :].astype(F32) * ya + g1_ref[rows, :].astype(F32) * yb
                  + g2_ref[rows, :].astype(F32) * yc)
        o = jnp.dot(merged.astype(BF16), wo_ref[...], preferred_element_type=F32)
        o_ref[rows, :] = x_ref[rows, :] + _rms(o, gp_ref[...])


def _merge(x2, ya2, yb2, proj2, mkv3, wa, wb, wc, wo, g_post, seq):
    n = x2.shape[0]
    tm = min(2 * MERGE_ROWS, seq)
    per_batch = seq // tm
    n_mem = mkv3.shape[1]
    c_dim = C_HEADS * C_HEAD_DIM
    const = lambda t: (0, 0)
    return pl.pallas_call(
        _merge_kernel,
        name="merge",
        grid=(n // tm,),
        in_specs=[
            pl.BlockSpec((tm, D_MODEL), lambda t: (t, 0)),
            pl.BlockSpec((tm, 512), lambda t: (t, 0)),
            pl.BlockSpec((tm, 512), lambda t: (t, 0)),
            pl.BlockSpec((tm, 512), lambda t: (t, COL_CQ)),
            pl.BlockSpec((tm, D_MODEL), lambda t: (t, 0)),
            pl.BlockSpec((tm, D_MODEL), lambda t: (t, 1)),
            pl.BlockSpec((tm, D_MODEL), lambda t: (t, 2)),
            pl.BlockSpec((None, n_mem, c_dim), lambda t: (t // per_batch, 0, 0)),
            pl.BlockSpec((None, n_mem, c_dim), lambda t: (t // per_batch, 0, 1)),
            pl.BlockSpec(wa.shape, const),
            pl.BlockSpec(wb.shape, const),
            pl.BlockSpec(wc.shape, const),
            pl.BlockSpec(wo.shape, const),
            pl.BlockSpec((1, D_MODEL), const),
        ],
        out_specs=pl.BlockSpec((tm, D_MODEL), lambda t: (t, 0)),
        out_shape=jax.ShapeDtypeStruct((n, D_MODEL), F32),
        compiler_params=pltpu.CompilerParams(
            dimension_semantics=("arbitrary",), vmem_limit_bytes=VMEM_LIMIT),
    )(x2, ya2, yb2, proj2, proj2, proj2, proj2, mkv3, mkv3, wa, wb, wc, wo, g_post)


FFN_ROWS = 512


def _ffn_kernel(x_ref, gpre_ref, wg_ref, wu_ref, wo_ref, gpost_ref, o_ref):
    for r0 in range(0, x_ref.shape[0], FFN_ROWS):
        rows = slice(r0, min(r0 + FFN_ROWS, x_ref.shape[0]))
        x = x_ref[rows, :]
        h = _rms(x, gpre_ref[...]).astype(BF16)
        g = jnp.dot(h, wg_ref[...], preferred_element_type=F32)
        u = jnp.dot(h, wu_ref[...], preferred_element_type=F32)
        act = (g * jax.nn.sigmoid(g) * u).astype(BF16)
        f = jnp.dot(act, wo_ref[...], preferred_element_type=F32)
        o_ref[rows, :] = x + _rms(f, gpost_ref[...])


def _ffn(x2, g_pre, w_in, wo, g_post):
    n = x2.shape[0]
    d_ff = wo.shape[0]
    tm = min(2 * FFN_ROWS, n)
    const = lambda t: (0, 0)
    resident = dict(pipeline_mode=pl.Buffered(1))
    return pl.pallas_call(
        _ffn_kernel,
        name="ffn",
        grid=(n // tm,),
        in_specs=[
            pl.BlockSpec((tm, D_MODEL), lambda t: (t, 0)),
            pl.BlockSpec((1, D_MODEL), const),
            pl.BlockSpec((D_MODEL, d_ff), lambda t: (0, 0), **resident),
            pl.BlockSpec((D_MODEL, d_ff), lambda t: (0, 1), **resident),
            pl.BlockSpec(wo.shape, const, **resident),
            pl.BlockSpec((1, D_MODEL), const),
        ],
        out_specs=pl.BlockSpec((tm, D_MODEL), lambda t: (t, 0)),
        out_shape=jax.ShapeDtypeStruct((n, D_MODEL), F32),
        compiler_params=pltpu.CompilerParams(
            dimension_semantics=("arbitrary",), vmem_limit_bytes=VMEM_LIMIT),
    )(x2, g_pre, w_in, w_in, wo, g_post)


def _pack_w_in(w):
    sizes = (512, 512, 512, IDX_HEADS * 64, 64, IDX_HEADS, 512, 512, 512, 512,
             N_BRANCH * D_MODEL)
    aq, ak, av, iq, ik, iw, bq, bk, bv, cq, gates = jnp.split(w, np.cumsum(sizes)[:-1], axis=1)
    scale = HEAD_DIM ** -0.5
    scale2 = scale * LOG2E
    w_main = jnp.concatenate(
        [gates, aq * scale2, ak, iq * scale, bq * scale2, bk, bv, cq], axis=1).astype(BF16)
    z64 = jnp.zeros((D_MODEL, 64), F32)
    w_ik = jnp.concatenate([ik, z64, z64, ik], axis=1).astype(BF16)
    w_trans = jnp.concatenate(
        [av, iw, jnp.zeros((D_MODEL, N_TRANS - 512 - IDX_HEADS), F32)], axis=1).T.astype(BF16)
    return w_main, w_ik, w_trans


def kernel(x, mem, rel_bias, g_mix_pre, w_in, b_gate, g_mem, w_mem_kv, w_up_a, w_up_b, w_up_c,
           w_out, g_mix_post, g_ffn_pre, w_ffn_in, w_ffn_out, g_ffn_post):
    bsz, seq, _ = x.shape
    n_mem = mem.shape[1]
    k_sel = min(TOPK_MAX, seq // 4)
    bias_tiles = _bias_tiles(rel_bias)
    x2 = x.reshape(bsz * seq, D_MODEL)
    for l in range(w_in.shape[0]):
        w_main, w_ik, w_trans = _pack_w_in(w_in[l])
        proj2, ik2, avt, iwt = _project(x2, g_mix_pre[l][None, :], w_main, b_gate[l][None, :],
                                        w_ik, w_trans)
        proj3 = proj2.reshape(bsz, seq, N_MAIN)
        mkv = _memkv(mem.reshape(bsz * n_mem, D_MODEL), g_mem[l][None, :],
                     w_mem_kv[l].astype(BF16))
        ya = _dsa(proj3, ik2.reshape(bsz, seq, N_IK), avt.reshape(bsz, seq // NK, 512, NK),
                  iwt, bias_tiles, k_sel)
        yb = _stick_breaking(proj3)
        x2 = _merge(x2, ya.reshape(bsz * seq, 512), yb.reshape(bsz * seq, 512), proj2,
                    mkv.reshape(bsz, n_mem, 2 * C_HEADS * C_HEAD_DIM),
                    w_up_a[l].astype(BF16), w_up_b[l].astype(BF16), w_up_c[l].astype(BF16),
                    w_out[l].astype(BF16), g_mix_post[l][None, :], seq)
        x2 = _ffn(x2, g_ffn_pre[l][None, :], w_ffn_in[l].astype(BF16),
                  w_ffn_out[l].astype(BF16), g_ffn_post[l][None, :])
    return x2.reshape(bsz, seq, D_MODEL)
```
